```python
import functools
import jax, jax.numpy as jnp
from jax import lax
import numpy as np

D_MODEL = 1024
BATCH = 4
SEQ = 4096
DEPTH = 1
DEC_BATCH = 32
DEC_SEQ = 1
PAST_LEN = 8192
PAGE_SIZE = 128

N_HEADS = 8
HEAD_DIM = 64
ATTN_WIDTH = N_HEADS * HEAD_DIM
MOBA_BLOCK = 256
MOBA_TOPK = 3
ROPE_THETA = 10000.0
Q_CHUNK = 128
POOL_WINDOWS = (2, 4, 8, 16)
POOL_GROUPS = 4
POOL_WIDTH = D_MODEL // 2
POOL_GROUP_WIDTH = POOL_WIDTH // POOL_GROUPS
POOL_STATE = max(POOL_WINDOWS) - 1
IN_WIDTH = 3 * ATTN_WIDTH + POOL_WIDTH + 2 * D_MODEL
N_EXPERT_GROUPS = 4
EXPERTS_PER_GROUP = 4
N_EXPERTS = N_EXPERT_GROUPS * EXPERTS_PER_GROUP
D_EXPERT = D_MODEL // 4
EXPERT_TOPK = 2
LN_EPS = 1e-5
ALPHA = (2 * DEPTH) ** 0.25
BETA = (8 * DEPTH) ** -0.25

kernel_name = 'moba_pool_hmoe_decode_step'


def layer_norm(x, g, b):
    xf = x.astype(jnp.float32)
    mu = jnp.mean(xf, axis=-1, keepdims=True)
    var = jnp.mean(jnp.square(xf - mu), axis=-1, keepdims=True)
    y = (xf - mu) * lax.rsqrt(var + LN_EPS) * g.astype(jnp.float32) + b.astype(jnp.float32)
    return y.astype(x.dtype)


def rope(x, pos):
    half = HEAD_DIM // 2
    inv_freq = 1.0 / (ROPE_THETA ** (jnp.arange(0, HEAD_DIM, 2, dtype=jnp.float32) / HEAD_DIM))
    ang = pos.astype(jnp.float32)[:, None] * inv_freq[None, :]
    cos = jnp.cos(ang)[None, :, None, :]
    sin = jnp.sin(ang)[None, :, None, :]
    xf = x.astype(jnp.float32)
    x1, x2 = xf[..., :half], xf[..., half:]
    return jnp.concatenate([x1 * cos - x2 * sin, x2 * cos + x1 * sin], axis=-1).astype(x.dtype)


def in_projection(x, w_in):
    b, l, _ = x.shape
    h = jnp.einsum('bld,de->ble', x, w_in)
    cuts = [ATTN_WIDTH, 2 * ATTN_WIDTH, 3 * ATTN_WIDTH, 3 * ATTN_WIDTH + POOL_WIDTH,
            3 * ATTN_WIDTH + POOL_WIDTH + D_MODEL]
    q, k, v, u, ga, gb = jnp.split(h, cuts, axis=-1)
    heads = (b, l, N_HEADS, HEAD_DIM)
    return q.reshape(heads), k.reshape(heads), v.reshape(heads), u, ga, gb


def to_blocks(k):
    b, t = k.shape[0], k.shape[1]
    nb = -(-t // MOBA_BLOCK)
    k = jnp.pad(k, ((0, 0), (0, nb * MOBA_BLOCK - t), (0, 0), (0, 0)))
    return k.reshape(b, nb, MOBA_BLOCK, N_HEADS, HEAD_DIM)


def block_means(kb):
    return jnp.mean(kb.astype(jnp.float32), axis=2)


def moba_attention(q, qpos, kb, vb, km):
    b, nq = q.shape[0], q.shape[1]
    nb = kb.shape[1]
    own = qpos // MOBA_BLOCK
    gate = jnp.einsum('bqhd,bnhd->bqhn', q.astype(jnp.float32), km)
    fully_past = jnp.arange(nb)[None, :] < own[:, None]
    gate = jnp.where(fully_past[None, :, None, :], gate, -jnp.inf)
    n_top = min(MOBA_TOPK, nb)
    _, top = lax.top_k(gate, n_top)
    own_b = jnp.broadcast_to(own[None, :, None, None], (b, nq, N_HEADS, 1))
    blk = jnp.concatenate([top.astype(jnp.int32), own_b.astype(jnp.int32)], axis=-1)
    b_ix = jnp.arange(b)[:, None, None, None]
    h_ix = jnp.arange(N_HEADS)[None, None, :, None]
    kg = kb[b_ix, blk, :, h_ix]
    vg = vb[b_ix, blk, :, h_ix]
    is_own = jnp.arange(n_top + 1) == n_top
    blk_ok = is_own[None, None, None, :] | (blk < own[None, :, None, None])
    kpos = blk[..., None] * MOBA_BLOCK + jnp.arange(MOBA_BLOCK)
    ok = blk_ok[..., None] & (kpos <= qpos[None, :, None, None, None])
    s = jnp.einsum('bqhd,bqhnkd->bqhnk', q, kg, preferred_element_type=jnp.float32) * (HEAD_DIM ** -0.5)
    s = jnp.where(ok, s, -jnp.inf)
    p = jax.nn.softmax(s.reshape(b, nq, N_HEADS, -1), axis=-1).reshape(s.shape)
    return jnp.einsum('bqhnk,bqhnkd->bqhd', p.astype(vg.dtype), vg)


def moba_prompt(q, k, v):
    b, s = q.shape[0], q.shape[1]
    kb, vb = to_blocks(k), to_blocks(v)
    km = block_means(kb)
    nc = s // Q_CHUNK
    qc = q.reshape(b, nc, Q_CHUNK, N_HEADS, HEAD_DIM).transpose(1, 0, 2, 3, 4)
    pc = jnp.arange(s, dtype=jnp.int32).reshape(nc, Q_CHUNK)
    out = lax.map(lambda a: moba_attention(a[0], a[1], kb, vb, km), (qc, pc))
    return out.transpose(1, 0, 2, 3, 4).reshape(b, s, N_HEADS, HEAD_DIM)


def moba_sample(q, k, v, cache_k, cache_v, page_table):
    db, l = q.shape[0], q.shape[1]
    past_k = cache_k[page_table].reshape(db, -1, N_HEADS, HEAD_DIM)
    past_v = cache_v[page_table].reshape(db, -1, N_HEADS, HEAD_DIM)
    past_len = past_k.shape[1]
    kb = to_blocks(jnp.concatenate([past_k, k.astype(past_k.dtype)], axis=1))
    vb = to_blocks(jnp.concatenate([past_v, v.astype(past_v.dtype)], axis=1))
    km = block_means(kb)
    qpos = past_len + jnp.arange(l, dtype=jnp.int32)
    return moba_attention(q, qpos, kb, vb, km)


def multiscale_pool(u, prefix, pos, w_pool, pool_scale):
    b, l, c = u.shape
    z = jnp.concatenate([prefix.astype(u.dtype), u], axis=1)
    cs = jnp.pad(jnp.cumsum(z.astype(jnp.float32), axis=1), ((0, 0), (1, 0), (0, 0)))
    means = []
    for g, w in enumerate(POOL_WINDOWS):
        c0, c1 = g * POOL_GROUP_WIDTH, (g + 1) * POOL_GROUP_WIDTH
        win_sum = (cs[:, POOL_STATE + 1:POOL_STATE + 1 + l, c0:c1]
                   - cs[:, POOL_STATE + 1 - w:POOL_STATE + 1 - w + l, c0:c1])
        count = jnp.minimum(w, pos + 1).astype(jnp.float32)[None, :, None]
        means.append(win_sum / count)
    diff = (jnp.concatenate(means, axis=-1) - u.astype(jnp.float32)).astype(u.dtype)
    d = diff.reshape(b, l, POOL_GROUPS, POOL_GROUP_WIDTH)
    mixed = jnp.einsum('blgc,gce->blge', d, w_pool).reshape(b, l, c) * pool_scale
    return mixed, z[:, -POOL_STATE:]


def hier_moe(x, w_group_router, b_group_router, w_expert_router, b_expert_router,
             w_e_gate, w_e_up, w_e_down):
    b, l, d = x.shape
    xt = x.reshape(b * l, d)
    n = xt.shape[0]
    rows = jnp.arange(n)
    g_logits = jnp.einsum('nd,dg->ng', xt, w_group_router).astype(jnp.float32) + b_group_router.astype(jnp.float32)
    g_prob = jax.nn.softmax(g_logits, axis=-1)
    g_sel = jnp.argmax(g_logits, axis=-1)
    g_w = g_prob[rows, g_sel][:, None]
    e_all = jnp.einsum('nd,gde->nge', xt, w_expert_router).astype(jnp.float32) + b_expert_router.astype(jnp.float32)
    e_logits = e_all[rows, g_sel]
    top_v, top_i = lax.top_k(e_logits, EXPERT_TOPK)
    wts = jax.nn.softmax(top_v, axis=-1) * g_w
    eid = g_sel[:, None] * EXPERTS_PER_GROUP + top_i
    combine = jnp.sum(jax.nn.one_hot(eid, N_EXPERTS, dtype=jnp.float32) * wts[..., None], axis=1)
    hg = jnp.einsum('nd,edf->nef', xt, w_e_gate)
    hu = jnp.einsum('nd,edf->nef', xt, w_e_up)
    h = jax.nn.silu(hg) * hu * combine[..., None].astype(x.dtype)
    y = jnp.einsum('nef,efd->nd', h, w_e_down)
    return y.reshape(b, l, d)


def decoder_layer(x, pos, attend, pool_prefix, w_in, w_pool, pool_scale, w_branch_a, w_branch_b,
                  b_gate, w_out, ln1_g, ln1_b, w_group_router, b_group_router, w_expert_router,
                  b_expert_router, w_e_gate, w_e_up, w_e_down, ln2_g, ln2_b):
    b, l, _ = x.shape
    q, k, v, u, ga, gb = in_projection(x, w_in)
    q = rope(q, pos)
    k = rope(k, pos)
    attn = attend(q, k, v).reshape(b, l, ATTN_WIDTH)
    pooled, new_pool = multiscale_pool(u, pool_prefix, pos, w_pool, pool_scale)
    gate_a = jax.nn.sigmoid(ga + b_gate[0])
    gate_b = jax.nn.sigmoid(gb + b_gate[1])
    merged = gate_a * (attn @ w_branch_a) + gate_b * (pooled @ w_branch_b)
    x1 = layer_norm(ALPHA * x + merged @ w_out, ln1_g, ln1_b)
    ffn = hier_moe(x1, w_group_router, b_group_router, w_expert_router, b_expert_router,
                   w_e_gate, w_e_up, w_e_down)
    y = layer_norm(ALPHA * x1 + ffn, ln2_g, ln2_b)
    return y, k, v, new_pool


def setup_inputs(seed: int = 0) -> dict:
    key = jax.random.key(seed)
    ks = jax.random.split(key, 26)
    f32 = jnp.float32
    n_pages = PAST_LEN // PAGE_SIZE
    n_used = DEC_BATCH * n_pages
    n_pool = (5 * n_used + 3) // 4
    perm = jax.random.permutation(ks[0], n_pool)
    page_table = perm[:n_used].reshape(DEC_BATCH, n_pages).astype(jnp.int32)

    def nrm(k, shape, scale):
        return jax.random.normal(k, shape, f32) * scale

    col_scale = jnp.concatenate([jnp.ones((2 * ATTN_WIDTH,), f32), jnp.full((ATTN_WIDTH,), BETA, f32),
                                 jnp.ones((POOL_WIDTH + 2 * D_MODEL,), f32)])
    return {
        'x_prompt': nrm(ks[1], (BATCH, SEQ, D_MODEL), 1.0),
        'x_sample': nrm(ks[2], (DEC_BATCH, DEC_SEQ, D_MODEL), 1.0),
        'cache_k': nrm(ks[3], (DEPTH, n_pool, PAGE_SIZE, N_HEADS, HEAD_DIM), 1.0),
        'cache_v': nrm(ks[4], (DEPTH, n_pool, PAGE_SIZE, N_HEADS, HEAD_DIM), 1.0),
        'state_pool': nrm(ks[5], (DEPTH, DEC_BATCH, POOL_STATE, POOL_WIDTH), 1.0),
        'page_table': page_table,
        'w_in': nrm(ks[6], (DEPTH, D_MODEL, IN_WIDTH), D_MODEL ** -0.5) * col_scale,
        'w_pool': nrm(ks[7], (DEPTH, POOL_GROUPS, POOL_GROUP_WIDTH, POOL_GROUP_WIDTH), POOL_GROUP_WIDTH ** -0.5),
        'pool_scale': 1.0 + nrm(ks[8], (DEPTH, POOL_WIDTH), 0.02),
        'w_branch_a': nrm(ks[9], (DEPTH, ATTN_WIDTH, D_MODEL), ATTN_WIDTH ** -0.5 * BETA),
        'w_branch_b': nrm(ks[10], (DEPTH, POOL_WIDTH, D_MODEL), POOL_WIDTH ** -0.5 * BETA),
        'b_gate': nrm(ks[11], (DEPTH, 2, D_MODEL), 0.02),
        'w_out': nrm(ks[12], (DEPTH, D_MODEL, D_MODEL), D_MODEL ** -0.5 * BETA),
        'ln1_g': 1.0 + nrm(ks[13], (DEPTH, D_MODEL), 0.02),
        'ln1_b': nrm(ks[14], (DEPTH, D_MODEL), 0.02),
        'w_group_router': nrm(ks[15], (DEPTH, D_MODEL, N_EXPERT_GROUPS), D_MODEL ** -0.5),
        'b_group_router': nrm(ks[16], (DEPTH, N_EXPERT_GROUPS), 0.01),
        'w_expert_router': nrm(ks[17], (DEPTH, N_EXPERT_GROUPS, D_MODEL, EXPERTS_PER_GROUP), D_MODEL ** -0.5),
        'b_expert_router': nrm(ks[18], (DEPTH, N_EXPERT_GROUPS, EXPERTS_PER_GROUP), 0.01),
        'w_e_gate': nrm(ks[19], (DEPTH, N_EXPERTS, D_MODEL, D_EXPERT), D_MODEL ** -0.5),
        'w_e_up': nrm(ks[20], (DEPTH, N_EXPERTS, D_MODEL, D_EXPERT), D_MODEL ** -0.5),
        'w_e_down': nrm(ks[21], (DEPTH, N_EXPERTS, D_EXPERT, D_MODEL), D_EXPERT ** -0.5 * BETA),
        'ln2_g': 1.0 + nrm(ks[22], (DEPTH, D_MODEL), 0.02),
        'ln2_b': nrm(ks[23], (DEPTH, D_MODEL), 0.02),
    }


def reference(x_prompt, x_sample, cache_k, cache_v, state_pool, page_table, w_in, w_pool, pool_scale,
              w_branch_a, w_branch_b, b_gate, w_out, ln1_g, ln1_b, w_group_router, b_group_router,
              w_expert_router, b_expert_router, w_e_gate, w_e_up, w_e_down, ln2_g, ln2_b):
    pos_p = jnp.arange(x_prompt.shape[1], dtype=jnp.int32)
    pos_s = PAST_LEN + jnp.arange(x_sample.shape[1], dtype=jnp.int32)
    prefix_p = jnp.zeros((x_prompt.shape[0], POOL_STATE, POOL_WIDTH), x_prompt.dtype)
    h_p, h_s = x_prompt, x_sample
    k_p, v_p, s_p, k_s, v_s, s_s = [], [], [], [], [], []
    for layer in range(DEPTH):
        lw = (w_in[layer], w_pool[layer], pool_scale[layer], w_branch_a[layer], w_branch_b[layer],
              b_gate[layer], w_out[layer], ln1_g[layer], ln1_b[layer], w_group_router[layer],
              b_group_router[layer], w_expert_router[layer], b_expert_router[layer], w_e_gate[layer],
              w_e_up[layer], w_e_down[layer], ln2_g[layer], ln2_b[layer])
        h_p, kl, vl, sl = decoder_layer(h_p, pos_p, moba_prompt, prefix_p, *lw)
        k_p.append(kl)
        v_p.append(vl)
        s_p.append(sl)
        attend_s = functools.partial(moba_sample, cache_k=cache_k[layer], cache_v=cache_v[layer],
                                     page_table=page_table)
        h_s, kl, vl, sl = decoder_layer(h_s, pos_s, attend_s, state_pool[layer], *lw)
        k_s.append(kl)
        v_s.append(vl)
        s_s.append(sl)
    return (h_p, h_s, jnp.stack(k_p), jnp.stack(v_p), jnp.stack(s_p), jnp.stack(k_s), jnp.stack(v_s), jnp.stack(s_s))
```

```python
import functools

import jax
import jax.numpy as jnp
from jax import lax
from jax.experimental import pallas as pl
from jax.experimental.pallas import tpu as pltpu

F32 = jnp.float32
BF16 = jnp.bfloat16
HIGHEST = lax.Precision.HIGHEST

N_HEADS = 8
HEAD_DIM = 64
ATTN_WIDTH = N_HEADS * HEAD_DIM
MOBA_BLOCK = 256
MOBA_TOPK = 3
ROPE_THETA = 10000.0
POOL_WINDOWS = (2, 4, 8, 16)
POOL_GROUP_WIDTH = 128
POOL_WIDTH = 512
POOL_STATE = 15
N_EXPERT_GROUPS = 4
EXPERTS_PER_GROUP = 4
N_EXPERTS = 16
LN_EPS = 1e-5
LANES = 128
HALO_ROWS = 16
MASK_BIAS = -1e30
VMEM_LIMIT = 56 * 1024 * 1024

NT_DIMS = (((1,), (1,)), ((), ()))


def _dot(a, b, precision=None):
    return jnp.dot(a, b, preferred_element_type=F32, precision=precision)


def _dot_nt(a, b, precision=None):
    return lax.dot_general(a, b, NT_DIMS, preferred_element_type=F32, precision=precision)


def _layer_norm(y, g, b):
    mu = jnp.mean(y, axis=-1, keepdims=True)
    var = jnp.mean(jnp.square(y - mu), axis=-1, keepdims=True)
    return (y - mu) * lax.rsqrt(var + LN_EPS) * g + b


def _sigmoid(x):
    return 1.0 / (1.0 + jnp.exp(-x))


def _rope_chunk(x, cos, sin_signed, first_half):
    partner = jnp.where(first_half, pltpu.roll(x, 96, 1), pltpu.roll(x, 32, 1))
    return x * cos + partner * sin_signed


def _route(logits):
    rows = logits.shape[0]
    lane = lax.broadcasted_iota(jnp.int32, (rows, LANES), 1)
    g = [logits[:, k:k + 1] for k in range(N_EXPERT_GROUPS)]
    gmax = jnp.maximum(jnp.maximum(g[0], g[1]), jnp.maximum(g[2], g[3]))
    den = sum(jnp.exp(gk - gmax) for gk in g)
    g_w = 1.0 / den
    is_g = []
    taken = jnp.zeros_like(gmax)
    for k in range(N_EXPERT_GROUPS):
        hit = jnp.where(g[k] == gmax, 1.0, 0.0) * (1.0 - taken)
        is_g.append(hit)
        taken = taken + hit
    e = []
    for k in range(EXPERTS_PER_GROUP):
        col = jnp.zeros_like(gmax)
        for gi in range(N_EXPERT_GROUPS):
            lane_id = N_EXPERT_GROUPS + gi * EXPERTS_PER_GROUP + k
            col = jnp.where(is_g[gi] > 0.5, logits[:, lane_id:lane_id + 1], col)
        e.append(col)
    v1 = jnp.maximum(jnp.maximum(e[0], e[1]), jnp.maximum(e[2], e[3]))
    first = []
    taken = jnp.zeros_like(v1)
    for k in range(EXPERTS_PER_GROUP):
        hit = jnp.where(e[k] == v1, 1.0, 0.0) * (1.0 - taken)
        first.append(hit)
        taken = taken + hit
    e2 = [jnp.where(first[k] > 0.5, -jnp.inf, e[k]) for k in range(EXPERTS_PER_GROUP)]
    v2 = jnp.maximum(jnp.maximum(e2[0], e2[1]), jnp.maximum(e2[2], e2[3]))
    second = []
    taken = jnp.zeros_like(v2)
    for k in range(EXPERTS_PER_GROUP):
        hit = jnp.where(e2[k] == v2, 1.0, 0.0) * (1.0 - taken)
        second.append(hit)
        taken = taken + hit
    t = jnp.exp(v2 - v1)
    w1 = 1.0 / (1.0 + t)
    w2 = t * w1
    comb = jnp.zeros((rows, LANES), F32)
    for gi in range(N_EXPERT_GROUPS):
        for k in range(EXPERTS_PER_GROUP):
            col = is_g[gi] * (first[k] * w1 + second[k] * w2) * g_w
            comb = jnp.where(lane == gi * EXPERTS_PER_GROUP + k, col, comb)
    return comb


def _qkvu_kernel(x_ref, w_ref, cos_ref, sin_ref, q_ref, k_ref, v_ref, u_ref, kb_ref, vb_ref,
                 km_ref, *, tm):
    xb = x_ref[...].astype(BF16)
    cos = cos_ref[...]
    sin = sin_ref[...]
    lane = lax.broadcasted_iota(jnp.int32, (tm, LANES), 1)
    first_half = (lane & 32) == 0
    hq = _dot(xb, w_ref[:, 0:ATTN_WIDTH])
    hk = _dot(xb, w_ref[:, ATTN_WIDTH:2 * ATTN_WIDTH])
    for c in range(ATTN_WIDTH // LANES):
        sl = slice(c * LANES, (c + 1) * LANES)
        qc = _rope_chunk(hq[:, sl], cos, sin, first_half)
        q_ref[:, sl] = (qc * (HEAD_DIM ** -0.5)).astype(BF16)
        kc = _rope_chunk(hk[:, sl], cos, sin, first_half)
        k_ref[:, sl] = kc
        kb_ref[:, sl] = kc.astype(BF16)
        for r in range(tm // MOBA_BLOCK):
            blk = kc[r * MOBA_BLOCK:(r + 1) * MOBA_BLOCK]
            km_ref[r:r + 1, sl] = jnp.sum(blk, axis=0, keepdims=True) * (1.0 / MOBA_BLOCK)
    hv = _dot(xb, w_ref[:, 2 * ATTN_WIDTH:3 * ATTN_WIDTH])
    v_ref[...] = hv
    vb_ref[...] = hv.astype(BF16)
    u_ref[...] = _dot(xb, w_ref[:, 3 * ATTN_WIDTH:])


def _qkvu(x2d, w_qkvu, cos, sin, seq, tm):
    n, d = x2d.shape
    s_tiles = seq // tm
    nblk = tm // MOBA_BLOCK
    row = lambda i: (i, 0)
    out_shape = (
        jax.ShapeDtypeStruct((n, ATTN_WIDTH), BF16),
        jax.ShapeDtypeStruct((n, ATTN_WIDTH), F32),
        jax.ShapeDtypeStruct((n, ATTN_WIDTH), F32),
        jax.ShapeDtypeStruct((n, POOL_WIDTH), F32),
        jax.ShapeDtypeStruct((n, ATTN_WIDTH), BF16),
        jax.ShapeDtypeStruct((n, ATTN_WIDTH), BF16),
        jax.ShapeDtypeStruct((n // tm, nblk, ATTN_WIDTH), F32),
    )
    return pl.pallas_call(
        functools.partial(_qkvu_kernel, tm=tm),
        out_shape=out_shape,
        grid=(n // tm,),
        in_specs=[
            pl.BlockSpec((tm, d), row),
            pl.BlockSpec((d, 4 * ATTN_WIDTH), lambda i: (0, 0)),
            pl.BlockSpec((tm, LANES), lambda i: (i % s_tiles, 0)),
            pl.BlockSpec((tm, LANES), lambda i: (i % s_tiles, 0)),
        ],
        out_specs=(
            pl.BlockSpec((tm, ATTN_WIDTH), row),
            pl.BlockSpec((tm, ATTN_WIDTH), row),
            pl.BlockSpec((tm, ATTN_WIDTH), row),
            pl.BlockSpec((tm, POOL_WIDTH), row),
            pl.BlockSpec((tm, ATTN_WIDTH), row),
            pl.BlockSpec((tm, ATTN_WIDTH), row),
            pl.BlockSpec((None, nblk, ATTN_WIDTH), lambda i: (i, 0, 0)),
        ),
        compiler_params=pltpu.CompilerParams(
            dimension_semantics=("arbitrary",), vmem_limit_bytes=VMEM_LIMIT),
        name="qkvu",
    )(x2d, w_qkvu, cos, sin)


def _attn_kernel(q_ref, k_ref, v_ref, km_ref, o_ref, qa_ref, m_ref, l_ref, acc_ref):
    i = pl.program_id(2)
    q = q_ref[...]
    lane = lax.broadcasted_iota(jnp.int32, (MOBA_BLOCK, LANES), 1)
    lane_f = lane.astype(F32)
    head0 = lane < HEAD_DIM
    past = lane < i
    km = km_ref[...]
    rowq = lax.broadcasted_iota(jnp.int32, (MOBA_BLOCK, MOBA_BLOCK), 0)
    colk = lax.broadcasted_iota(jnp.int32, (MOBA_BLOCK, MOBA_BLOCK), 1)
    causal = colk <= rowq
    own = pl.multiple_of(i * MOBA_BLOCK, MOBA_BLOCK)
    kd = k_ref[pl.ds(own, MOBA_BLOCK), :]
    vd = v_ref[pl.ds(own, MOBA_BLOCK), :]
    for h in range(2):
        qh = jnp.where(head0 if h == 0 else jnp.logical_not(head0), q, jnp.zeros_like(q))
        gate = _dot_nt(qh.astype(F32), km, HIGHEST)
        g = jnp.where(past, gate, -jnp.inf)
        sel = jnp.zeros((MOBA_BLOCK, LANES), F32)
        for _ in range(MOBA_TOPK):
            mx = jnp.max(g, axis=1, keepdims=True)
            idx = jnp.min(jnp.where(g == mx, lane_f, float(LANES)), axis=1, keepdims=True)
            pick = lane_f == idx
            sel = jnp.where(pick, 1.0, sel)
            g = jnp.where(pick, -jnp.inf, g)
        bias = jnp.where(jnp.logical_and(sel > 0.5, past), 0.0, MASK_BIAS).astype(BF16)
        qa_ref[h] = jnp.concatenate([qh, bias], axis=1)
        s = jnp.where(causal, _dot_nt(qh, kd), -jnp.inf)
        m = jnp.max(s, axis=1, keepdims=True)
        p = jnp.exp(s - m)
        m_ref[h] = m
        l_ref[h] = jnp.sum(p, axis=1, keepdims=True)
        acc_ref[h] = _dot(p.astype(BF16), vd)

    def body(j, carry):
        off = pl.multiple_of(j * MOBA_BLOCK, MOBA_BLOCK)
        kj = k_ref[pl.ds(off, MOBA_BLOCK), :]
        vj = v_ref[pl.ds(off, MOBA_BLOCK), :]
        onehot = jnp.where(lane == j, 1.0, 0.0).astype(BF16)
        ka = jnp.concatenate([kj, onehot], axis=1)
        for h in range(2):
            s = _dot_nt(qa_ref[h], ka)
            m_old = m_ref[h]
            m_new = jnp.maximum(m_old, jnp.max(s, axis=1, keepdims=True))
            alpha = jnp.exp(m_old - m_new)
            p = jnp.exp(s - m_new)
            l_ref[h] = alpha * l_ref[h] + jnp.sum(p, axis=1, keepdims=True)
            acc_ref[h] = alpha * acc_ref[h] + _dot(p.astype(BF16), vj)
            m_ref[h] = m_new
        return carry

    lax.fori_loop(0, i, body, 0)
    o = jnp.where(head0, acc_ref[0] / l_ref[0], acc_ref[1] / l_ref[1])
    o_ref[...] = o.astype(BF16)


def _attn(q, kb, vb, km_pad):
    b, s, _ = q.shape
    pairs = ATTN_WIDTH // LANES
    return pl.pallas_call(
        _attn_kernel,
        out_shape=jax.ShapeDtypeStruct((b, s, ATTN_WIDTH), BF16),
        grid=(b, pairs, s // MOBA_BLOCK),
        in_specs=[
            pl.BlockSpec((None, MOBA_BLOCK, LANES), lambda bi, p, i: (bi, i, p)),
            pl.BlockSpec((None, s, LANES), lambda bi, p, i: (bi, 0, p)),
            pl.BlockSpec((None, s, LANES), lambda bi, p, i: (bi, 0, p)),
            pl.BlockSpec((None, LANES, LANES), lambda bi, p, i: (bi, 0, p)),
        ],
        out_specs=pl.BlockSpec((None, MOBA_BLOCK, LANES), lambda bi, p, i: (bi, i, p)),
        scratch_shapes=[
            pltpu.VMEM((2, MOBA_BLOCK, 2 * LANES), BF16),
            pltpu.VMEM((2, MOBA_BLOCK, 1), F32),
            pltpu.VMEM((2, MOBA_BLOCK, 1), F32),
            pltpu.VMEM((2, MOBA_BLOCK, LANES), F32),
        ],
        compiler_params=pltpu.CompilerParams(
            dimension_semantics=("arbitrary", "arbitrary", "arbitrary"),
            vmem_limit_bytes=VMEM_LIMIT),
        name="moba_attn",
    )(q, kb, vb, km_pad)


def _pool_group(window_sum, u_g, count, w_pool_g, scale_g, precision=None, cast=None):
    d = window_sum / count - u_g
    if cast is not None:
        d = d.astype(cast)
    return _dot(d, w_pool_g, precision) * scale_g


def _mix_kernel(x_ref, attn_ref, u_ref, halo_ref, wg_ref, bg_ref, wpool_ref, pscale_ref, wa_ref,
                wb_ref, wout_ref, g1_ref, b1_ref, x1_ref, z_ref, *, tm, alpha):
    i = pl.program_id(1)
    x = x_ref[...]
    xb = x.astype(BF16)
    u = u_ref[...]
    z_ref[0:HALO_ROWS, :] = jnp.where(i > 0, halo_ref[...], 0.0)
    z_ref[HALO_ROWS:, :] = u
    pos1 = (i * tm + 1 + lax.broadcasted_iota(jnp.int32, (tm, 1), 0)).astype(F32)
    parts = []
    for g, w in enumerate(POOL_WINDOWS):
        sl = slice(g * POOL_GROUP_WIDTH, (g + 1) * POOL_GROUP_WIDTH)
        acc = u[:, sl]
        for back in range(1, w):
            acc = acc + z_ref[HALO_ROWS - back:HALO_ROWS - back + tm, sl]
        count = jnp.minimum(float(w), pos1)
        parts.append(_pool_group(acc, u[:, sl], count, wpool_ref[g], pscale_ref[:, sl], cast=BF16))
    pooled = jnp.concatenate(parts, axis=1).astype(BF16)
    d = x.shape[1]
    ga = _dot(xb, wg_ref[:, :d]) + bg_ref[0:1, :]
    gb = _dot(xb, wg_ref[:, d:]) + bg_ref[1:2, :]
    a = _dot(attn_ref[...], wa_ref[...])
    bb = _dot(pooled, wb_ref[...])
    merged = _sigmoid(ga) * a + _sigmoid(gb) * bb
    y = alpha * x + _dot(merged.astype(BF16), wout_ref[...])
    x1_ref[...] = _layer_norm(y, g1_ref[...], b1_ref[...])


def _mix(x, attn, u, w_gates, b_gate, w_pool, pool_scale, w_a, w_b, w_out, ln_g, ln_b, tm, alpha):
    b, s, d = x.shape
    halo_per_tile = tm // HALO_ROWS
    tile = lambda bi, i: (bi, i, 0)
    const2 = lambda bi, i: (0, 0)
    return pl.pallas_call(
        functools.partial(_mix_kernel, tm=tm, alpha=alpha),
        out_shape=jax.ShapeDtypeStruct((b, s, d), F32),
        grid=(b, s // tm),
        in_specs=[
            pl.BlockSpec((None, tm, d), tile),
            pl.BlockSpec((None, tm, ATTN_WIDTH), tile),
            pl.BlockSpec((None, tm, POOL_WIDTH), tile),
            pl.BlockSpec((None, HALO_ROWS, POOL_WIDTH),
                         lambda bi, i: (bi, jnp.maximum(i * halo_per_tile - 1, 0), 0)),
            pl.BlockSpec(w_gates.shape, const2),
            pl.BlockSpec(b_gate.shape, const2),
            pl.BlockSpec(w_pool.shape, lambda bi, i: (0, 0, 0)),
            pl.BlockSpec(pool_scale.shape, const2),
            pl.BlockSpec(w_a.shape, const2),
            pl.BlockSpec(w_b.shape, const2),
            pl.BlockSpec(w_out.shape, const2),
            pl.BlockSpec(ln_g.shape, const2),
            pl.BlockSpec(ln_b.shape, const2),
        ],
        out_specs=pl.BlockSpec((None, tm, d), tile),
        scratch_shapes=[pltpu.VMEM((HALO_ROWS + tm, POOL_WIDTH), F32)],
        compiler_params=pltpu.CompilerParams(
            dimension_semantics=("arbitrary", "arbitrary"), vmem_limit_bytes=VMEM_LIMIT),
        name="mix",
    )(x, attn, u, u, w_gates, b_gate, w_pool, pool_scale, w_a, w_b, w_out, ln_g, ln_b)


def _moe_kernel(x1_ref, wr_ref, br_ref, wg_ref, wu_ref, wd_ref, g2_ref, b2_ref, y_ref,
                xb_ref, comb_ref, acc_ref, *, alpha):
    e = pl.program_id(1)
    tm = x1_ref.shape[0]
    lane = lax.broadcasted_iota(jnp.int32, (tm, LANES), 1)

    @pl.when(e == 0)
    def _():
        x1 = x1_ref[...]
        xb_ref[...] = x1.astype(BF16)
        logits = _dot(x1, wr_ref[...], HIGHEST) + br_ref[...]
        comb_ref[...] = _route(logits)
        acc_ref[...] = jnp.zeros_like(acc_ref)

    xb = xb_ref[...]
    hg = _dot(xb, wg_ref[...])
    hu = _dot(xb, wu_ref[...])
    c_e = jnp.sum(jnp.where(lane == e, comb_ref[...], 0.0), axis=1, keepdims=True)
    h = hg * _sigmoid(hg) * hu * c_e
    acc_ref[...] += _dot(h.astype(BF16), wd_ref[...])

    @pl.when(e == N_EXPERTS - 1)
    def _():
        y = alpha * x1_ref[...] + acc_ref[...]
        y_ref[...] = _layer_norm(y, g2_ref[...], b2_ref[...])


def _moe(x1, w_r, b_r, w_g, w_u, w_d, ln_g, ln_b, tm, alpha):
    n, d = x1.shape
    f = w_g.shape[2]
    tile = lambda i, e: (i, 0)
    const2 = lambda i, e: (0, 0)
    return pl.pallas_call(
        functools.partial(_moe_kernel, alpha=alpha),
        out_shape=jax.ShapeDtypeStruct((n, d), F32),
        grid=(n // tm, N_EXPERTS),
        in_specs=[
            pl.BlockSpec((tm, d), tile),
            pl.BlockSpec(w_r.shape, const2),
            pl.BlockSpec(b_r.shape, const2),
            pl.BlockSpec((None, d, f), lambda i, e: (e, 0, 0)),
            pl.BlockSpec((None, d, f), lambda i, e: (e, 0, 0)),
            pl.BlockSpec((None, f, d), lambda i, e: (e, 0, 0)),
            pl.BlockSpec(ln_g.shape, const2),
            pl.BlockSpec(ln_b.shape, const2),
        ],
        out_specs=pl.BlockSpec((tm, d), tile),
        scratch_shapes=[
            pltpu.VMEM((tm, d), BF16),
            pltpu.VMEM((tm, LANES), F32),
            pltpu.VMEM((tm, d), F32),
        ],
        compiler_params=pltpu.CompilerParams(
            dimension_semantics=("arbitrary", "arbitrary"), vmem_limit_bytes=VMEM_LIMIT),
        name="moe",
    )(x1, w_r, b_r, w_g, w_u, w_d, ln_g, ln_b)


def _s_proj_kernel(x_ref, w_ref, cos_ref, sin_ref, h_ref):
    c = pl.program_id(0)
    h = _dot(x_ref[...], w_ref[...], HIGHEST)
    rows = h.shape[0]
    lane = lax.broadcasted_iota(jnp.int32, (rows, LANES), 1)
    first_half = (lane & 32) == 0
    rotary = c < 2
    for j in range(h.shape[1] // LANES):
        sl = slice(j * LANES, (j + 1) * LANES)
        hc = h[:, sl]
        h_ref[:, sl] = jnp.where(rotary, _rope_chunk(hc, cos_ref[...], sin_ref[...], first_half), hc)


def _s_proj(x, w_in, cos, sin):
    rows, d = x.shape
    width = w_in.shape[1]
    chunk = ATTN_WIDTH
    return pl.pallas_call(
        _s_proj_kernel,
        out_shape=jax.ShapeDtypeStruct((rows, width), F32),
        grid=(width // chunk,),
        in_specs=[
            pl.BlockSpec((rows, d), lambda c: (0, 0)),
            pl.BlockSpec((d, chunk), lambda c: (0, c)),
            pl.BlockSpec((1, LANES), lambda c: (0, 0)),
            pl.BlockSpec((1, LANES), lambda c: (0, 0)),
        ],
        out_specs=pl.BlockSpec((rows, chunk), lambda c: (0, c)),
        compiler_params=pltpu.CompilerParams(
            dimension_semantics=("arbitrary",), vmem_limit_bytes=VMEM_LIMIT),
        name="s_proj",
    )(x, w_in, cos, sin)


PAGES_PER_STEP = 16


def _s_means_kernel(pt_ref, q_ref, *refs, page_size, n_blocks):
    page_refs = refs[:PAGES_PER_STEP]
    sel_ref = refs[PAGES_PER_STEP]
    km_ref = refs[PAGES_PER_STEP + 1]
    c = pl.program_id(1)
    pages_per_block = MOBA_BLOCK // page_size
    blocks_per_step = PAGES_PER_STEP // pages_per_block
    for r in range(blocks_per_step):
        tot = jnp.zeros((1, ATTN_WIDTH), F32)
        for pp in range(pages_per_block):
            tot = tot + jnp.sum(page_refs[r * pages_per_block + pp][...], axis=0, keepdims=True)
        km_ref[pl.ds(c * blocks_per_step + r, 1), :] = tot * (1.0 / MOBA_BLOCK)

    @pl.when(c == pl.num_programs(1) - 1)
    def _():
        prod = km_ref[...] * q_ref[...]
        d_id = lax.broadcasted_iota(jnp.int32, (ATTN_WIDTH, LANES), 0)
        h_id = lax.broadcasted_iota(jnp.int32, (ATTN_WIDTH, LANES), 1)
        seg = jnp.where(d_id // HEAD_DIM == h_id, 1.0, 0.0)
        g = _dot(prod, seg, HIGHEST)
        row_f = lax.broadcasted_iota(jnp.int32, (n_blocks, LANES), 0).astype(F32)
        out_row = lax.broadcasted_iota(jnp.int32, (8, LANES), 0)
        out = jnp.zeros((8, LANES), F32)
        for t in range(MOBA_TOPK):
            mx = jnp.max(g, axis=0, keepdims=True)
            idx = jnp.min(jnp.where(g == mx, row_f, float(n_blocks)), axis=0, keepdims=True)
            out = jnp.where(out_row == t, idx, out)
            g = jnp.where(row_f == idx, -jnp.inf, g)
        sel_ref[...] = out.astype(jnp.int32)


def _s_means(page_table, q3, cache_k_pages, n_blocks):
    db, n_pages = page_table.shape
    page_size = cache_k_pages.shape[1]

    def page_spec(r):
        return pl.BlockSpec((None, page_size, ATTN_WIDTH),
                            lambda b, c, pt: (pt[b, c * PAGES_PER_STEP + r], 0, 0))

    return pl.pallas_call(
        functools.partial(_s_means_kernel, page_size=page_size, n_blocks=n_blocks),
        out_shape=jax.ShapeDtypeStruct((db, 8, LANES), jnp.int32),
        grid_spec=pltpu.PrefetchScalarGridSpec(
            num_scalar_prefetch=1,
            grid=(db, n_pages // PAGES_PER_STEP),
            in_specs=[pl.BlockSpec((None, 1, ATTN_WIDTH), lambda b, c, pt: (b, 0, 0))]
            + [page_spec(r) for r in range(PAGES_PER_STEP)],
            out_specs=pl.BlockSpec((None, 8, LANES), lambda b, c, pt: (b, 0, 0)),
            scratch_shapes=[pltpu.VMEM((n_blocks, ATTN_WIDTH), F32)],
        ),
        compiler_params=pltpu.CompilerParams(
            dimension_semantics=("arbitrary", "arbitrary"), vmem_limit_bytes=VMEM_LIMIT),
        name="s_means",
    )(page_table, q3, *([cache_k_pages] * PAGES_PER_STEP))


def _s_attn_kernel(pt_ref, sel_ref, q_ref, kn_ref, vn_ref, k0_ref, k1_ref, v0_ref, v1_ref, o_ref,
                   m_ref, l_ref, acc_ref):
    h = pl.program_id(1)
    t = pl.program_id(2)
    lane = lax.broadcasted_iota(jnp.int32, (1, ATTN_WIDTH), 1)
    in_head = (lane // HEAD_DIM) == h
    qh = jnp.where(in_head, q_ref[...] * (HEAD_DIM ** -0.5), 0.0)

    @pl.when(jnp.logical_and(h == 0, t == 0))
    def _():
        o_ref[...] = jnp.zeros_like(o_ref)

    @pl.when(t == 0)
    def _():
        m_ref[...] = jnp.sum(qh * kn_ref[...], axis=1, keepdims=True)
        l_ref[...] = jnp.ones_like(l_ref)
        acc_ref[...] = vn_ref[...]

    s0 = jnp.sum(k0_ref[...] * qh, axis=1, keepdims=True)
    s1 = jnp.sum(k1_ref[...] * qh, axis=1, keepdims=True)
    m_old = m_ref[...]
    m_new = jnp.maximum(m_old, jnp.maximum(jnp.max(s0, axis=0, keepdims=True),
                                           jnp.max(s1, axis=0, keepdims=True)))
    alpha = jnp.exp(m_old - m_new)
    p0 = jnp.exp(s0 - m_new)
    p1 = jnp.exp(s1 - m_new)
    l_ref[...] = alpha * l_ref[...] + jnp.sum(p0, axis=0, keepdims=True) + jnp.sum(p1, axis=0, keepdims=True)
    pv = jnp.sum(p0 * v0_ref[...], axis=0, keepdims=True) + jnp.sum(p1 * v1_ref[...], axis=0, keepdims=True)
    acc_ref[...] = alpha * acc_ref[...] + pv
    m_ref[...] = m_new

    @pl.when(t == MOBA_TOPK - 1)
    def _():
        o_ref[...] = jnp.where(in_head, acc_ref[...] / l_ref[...], o_ref[...])


def _s_attn(page_table, sel, q3, kn3, vn3, cache_k_pages, cache_v_pages):
    db = page_table.shape[0]
    page_size = cache_k_pages.shape[1]
    pages_per_block = MOBA_BLOCK // page_size
    assert pages_per_block == 2

    def page_spec(r):
        return pl.BlockSpec(
            (None, page_size, ATTN_WIDTH),
            lambda b, h, t, pt, sl: (pt[b, sl[b, t * N_HEADS + h] * pages_per_block + r], 0, 0))

    per_b = pl.BlockSpec((None, 1, ATTN_WIDTH), lambda b, h, t, pt, sl: (b, 0, 0))
    return pl.pallas_call(
        _s_attn_kernel,
        out_shape=jax.ShapeDtypeStruct((db, 1, ATTN_WIDTH), F32),
        grid_spec=pltpu.PrefetchScalarGridSpec(
            num_scalar_prefetch=2,
            grid=(db, N_HEADS, MOBA_TOPK),
            in_specs=[per_b, per_b, per_b, page_spec(0), page_spec(1), page_spec(0), page_spec(1)],
            out_specs=per_b,
            scratch_shapes=[pltpu.VMEM((1, 1), F32), pltpu.VMEM((1, 1), F32),
                            pltpu.VMEM((1, ATTN_WIDTH), F32)],
        ),
        compiler_params=pltpu.CompilerParams(
            dimension_semantics=("arbitrary", "arbitrary", "arbitrary"),
            vmem_limit_bytes=VMEM_LIMIT),
        name="s_attn",
    )(page_table, sel, q3, kn3, vn3, cache_k_pages, cache_k_pages, cache_v_pages, cache_v_pages)


def _s_tail_kernel(x_ref, attn_ref, u_ref, ga_ref, gb_ref, sp_ref, bg_ref, wpool_ref, pscale_ref,
                   wa_ref, wb_ref, wout_ref, g1_ref, b1_ref, wr_ref, br_ref, wg_ref, wu_ref,
                   wd_ref, g2_ref, b2_ref, y_ref, x1_ref, comb_ref, acc_ref, *, alpha):
    e = pl.program_id(0)
    rows = x_ref.shape[0]
    lane = lax.broadcasted_iota(jnp.int32, (rows, LANES), 1)

    @pl.when(e == 0)
    def _():
        u = u_ref[...]
        parts = []
        for g, w in enumerate(POOL_WINDOWS):
            sl = slice(g * POOL_GROUP_WIDTH, (g + 1) * POOL_GROUP_WIDTH)
            acc = u[:, sl]
            for back in range(1, w):
                acc = acc + sp_ref[POOL_STATE - back][:, sl]
            parts.append(_pool_group(acc, u[:, sl], float(w), wpool_ref[g], pscale_ref[:, sl],
                                     precision=HIGHEST))
        pooled = jnp.concatenate(parts, axis=1)
        a = _dot(attn_ref[...], wa_ref[...], HIGHEST)
        bb = _dot(pooled, wb_ref[...], HIGHEST)
        merged = _sigmoid(ga_ref[...] + bg_ref[0:1, :]) * a + _sigmoid(gb_ref[...] + bg_ref[1:2, :]) * bb
        y = alpha * x_ref[...] + _dot(merged, wout_ref[...], HIGHEST)
        x1 = _layer_norm(y, g1_ref[...], b1_ref[...])
        x1_ref[...] = x1
        comb_ref[...] = _route(_dot(x1, wr_ref[...], HIGHEST) + br_ref[...])
        acc_ref[...] = jnp.zeros_like(acc_ref)

    x1 = x1_ref[...]
    hg = _dot(x1, wg_ref[...], HIGHEST)
    hu = _dot(x1, wu_ref[...], HIGHEST)
    c_e = jnp.sum(jnp.where(lane == e, comb_ref[...], 0.0), axis=1, keepdims=True)
    h = hg * _sigmoid(hg) * hu * c_e
    acc_ref[...] += _dot(h, wd_ref[...], HIGHEST)

    @pl.when(e == N_EXPERTS - 1)
    def _():
        y_ref[...] = _layer_norm(alpha * x1_ref[...] + acc_ref[...], g2_ref[...], b2_ref[...])


def _s_tail(x, attn, u, ga, gb, sp_t, b_gate, w_pool, pool_scale, w_a, w_b, w_out, ln1_g, ln1_b,
            w_r, b_r, w_g, w_u, w_d, ln2_g, ln2_b, alpha):
    rows, d = x.shape
    f = w_g.shape[2]
    const2 = lambda e: (0, 0)
    const3 = lambda e: (0, 0, 0)
    full = lambda a: pl.BlockSpec(a.shape, const2 if a.ndim == 2 else const3)
    per_e = lambda blk: pl.BlockSpec((None,) + blk, lambda e: (e, 0, 0))
    small = [x, attn, u, ga, gb, sp_t, b_gate, w_pool, pool_scale, w_a, w_b, w_out, ln1_g, ln1_b,
             w_r, b_r]
    return pl.pallas_call(
        functools.partial(_s_tail_kernel, alpha=alpha),
        out_shape=jax.ShapeDtypeStruct((rows, d), F32),
        grid=(N_EXPERTS,),
        in_specs=[full(a) for a in small]
        + [per_e((d, f)), per_e((d, f)), per_e((f, d)), full(ln2_g), full(ln2_b)],
        out_specs=pl.BlockSpec((rows, d), const2),
        scratch_shapes=[pltpu.VMEM((rows, d), F32), pltpu.VMEM((rows, LANES), F32),
                        pltpu.VMEM((rows, d), F32)],
        compiler_params=pltpu.CompilerParams(
            dimension_semantics=("arbitrary",), vmem_limit_bytes=VMEM_LIMIT),
        name="s_tail",
    )(*small, w_g, w_u, w_d, ln2_g, ln2_b)


def _rope_tables(pos):
    inv_freq = 1.0 / (ROPE_THETA ** (jnp.arange(0, HEAD_DIM, 2, dtype=F32) / HEAD_DIM))
    ang = pos.astype(F32)[:, None] * inv_freq[None, :]
    cos = jnp.cos(ang)
    sin = jnp.sin(ang)
    cos_t = jnp.tile(cos, (1, LANES // (HEAD_DIM // 2)))
    sin_t = jnp.tile(jnp.concatenate([-sin, sin], axis=1), (1, LANES // HEAD_DIM))
    return cos_t, sin_t


def kernel(x_prompt, x_sample, cache_k, cache_v, state_pool, page_table, w_in, w_pool, pool_scale,
           w_branch_a, w_branch_b, b_gate, w_out, ln1_g, ln1_b, w_group_router, b_group_router,
           w_expert_router, b_expert_router, w_e_gate, w_e_up, w_e_down, ln2_g, ln2_b):
    depth = w_in.shape[0]
    assert depth == 1 and x_sample.shape[1] == 1
    alpha = (2 * depth) ** 0.25
    b, s, d = x_prompt.shape
    db = x_sample.shape[0]
    n_pages = page_table.shape[1]
    page_size = cache_k.shape[2]
    past_len = n_pages * page_size
    n_blocks = past_len // MOBA_BLOCK
    assert past_len % MOBA_BLOCK == 0 and s % MOBA_BLOCK == 0

    w_in0 = w_in[0]
    qkvu_cols = 3 * ATTN_WIDTH + POOL_WIDTH
    row2 = lambda a: a.reshape(1, -1)
    w_r = jnp.concatenate(
        [w_group_router[0], jnp.transpose(w_expert_router[0], (1, 0, 2)).reshape(d, N_EXPERTS),
         jnp.zeros((d, LANES - N_EXPERT_GROUPS - N_EXPERTS), F32)], axis=1)
    b_r = jnp.concatenate(
        [b_group_router[0], b_expert_router[0].reshape(-1),
         jnp.zeros((LANES - N_EXPERT_GROUPS - N_EXPERTS,), F32)]).reshape(1, LANES)

    cos_p, sin_p = _rope_tables(jnp.arange(s, dtype=jnp.int32))
    tm_a = 512
    q, k, v, u, kb, vb, km = _qkvu(x_prompt.reshape(b * s, d), w_in0[:, :qkvu_cols].astype(BF16),
                                   cos_p, sin_p, s, tm_a)
    km = km.reshape(b, s // MOBA_BLOCK, ATTN_WIDTH)
    km_pad = jnp.pad(km, ((0, 0), (0, LANES - s // MOBA_BLOCK), (0, 0)))
    attn = _attn(q.reshape(b, s, ATTN_WIDTH), kb.reshape(b, s, ATTN_WIDTH),
                 vb.reshape(b, s, ATTN_WIDTH), km_pad)
    u3 = u.reshape(b, s, POOL_WIDTH)
    x1 = _mix(x_prompt, attn, u3, w_in0[:, qkvu_cols:].astype(BF16), b_gate[0],
              w_pool[0].astype(BF16), row2(pool_scale[0]), w_branch_a[0].astype(BF16),
              w_branch_b[0].astype(BF16), w_out[0].astype(BF16), row2(ln1_g[0]), row2(ln1_b[0]),
              256, alpha)
    y_p = _moe(x1.reshape(b * s, d), w_r, b_r, w_e_gate[0].astype(BF16), w_e_up[0].astype(BF16),
               w_e_down[0].astype(BF16), row2(ln2_g[0]), row2(ln2_b[0]), 1024, alpha)

    cos_s, sin_s = _rope_tables(jnp.full((1,), past_len, jnp.int32))
    x_s = x_sample.reshape(db, d)
    h_s = _s_proj(x_s, w_in0, cos_s, sin_s)
    q_s = h_s[:, :ATTN_WIDTH]
    k_s = h_s[:, ATTN_WIDTH:2 * ATTN_WIDTH]
    v_s = h_s[:, 2 * ATTN_WIDTH:3 * ATTN_WIDTH]
    u_s = h_s[:, 3 * ATTN_WIDTH:qkvu_cols]
    ga_s = h_s[:, qkvu_cols:qkvu_cols + d]
    gb_s = h_s[:, qkvu_cols + d:]
    ck = cache_k[0].reshape(cache_k.shape[1], page_size, ATTN_WIDTH)
    cv = cache_v[0].reshape(cache_v.shape[1], page_size, ATTN_WIDTH)
    q3 = q_s.reshape(db, 1, ATTN_WIDTH)
    sel = _s_means(page_table, q3, ck, n_blocks)
    sel2 = sel[:, :MOBA_TOPK, :N_HEADS].reshape(db, MOBA_TOPK * N_HEADS)
    attn_s = _s_attn(page_table, sel2, q3, k_s.reshape(db, 1, ATTN_WIDTH),
                     v_s.reshape(db, 1, ATTN_WIDTH), ck, cv).reshape(db, ATTN_WIDTH)
    sp = state_pool[0]
    y_s = _s_tail(x_s, attn_s, u_s, ga_s, gb_s, jnp.transpose(sp, (1, 0, 2)), b_gate[0], w_pool[0],
                  row2(pool_scale[0]), w_branch_a[0], w_branch_b[0], w_out[0], row2(ln1_g[0]),
                  row2(ln1_b[0]), w_r, b_r, w_e_gate[0], w_e_up[0], w_e_down[0], row2(ln2_g[0]),
                  row2(ln2_b[0]), alpha)

    heads = (N_HEADS, HEAD_DIM)
    return (
        y_p.reshape(b, s, d),
        y_s.reshape(db, 1, d),
        k.reshape((1, b, s) + heads),
        v.reshape((1, b, s) + heads),
        u3[:, s - POOL_STATE:, :][None],
        k_s.reshape((1, db, 1) + heads),
        v_s.reshape((1, db, 1) + heads),
        jnp.concatenate([sp[:, 1:, :], u_s[:, None, :]], axis=1)[None],
    )
```

```python
import functools

import jax
import jax.numpy as jnp
from jax import lax
from jax.experimental import pallas as pl
from jax.experimental.pallas import tpu as pltpu

F32 = jnp.float32
BF16 = jnp.bfloat16
HIGHEST = lax.Precision.HIGHEST

N_HEADS = 8
HEAD_DIM = 64
ATTN_WIDTH = N_HEADS * HEAD_DIM
MOBA_BLOCK = 256
MOBA_TOPK = 3
ROPE_THETA = 10000.0
POOL_WINDOWS = (2, 4, 8, 16)
POOL_GROUP_WIDTH = 128
POOL_WIDTH = 512
POOL_STATE = 15
N_EXPERT_GROUPS = 4
EXPERTS_PER_GROUP = 4
N_EXPERTS = 16
LN_EPS = 1e-5
LANES = 128
HALO_ROWS = 16
MASK_BIAS = -1e30
VMEM_LIMIT = 56 * 1024 * 1024

NT_DIMS = (((1,), (1,)), ((), ()))


def _dot(a, b, precision=None):
    return jnp.dot(a, b, preferred_element_type=F32, precision=precision)


def _dot_nt(a, b, precision=None):
    return lax.dot_general(a, b, NT_DIMS, preferred_element_type=F32, precision=precision)


def _layer_norm(y, g, b):
    mu = jnp.mean(y, axis=-1, keepdims=True)
    var = jnp.mean(jnp.square(y - mu), axis=-1, keepdims=True)
    return (y - mu) * lax.rsqrt(var + LN_EPS) * g + b


def _sigmoid(x):
    return 1.0 / (1.0 + jnp.exp(-x))


def _rope_chunk(x, cos, sin_signed, first_half):
    partner = jnp.where(first_half, pltpu.roll(x, 96, 1), pltpu.roll(x, 32, 1))
    return x * cos + partner * sin_signed


def _route(logits):
    rows = logits.shape[0]
    lane = lax.broadcasted_iota(jnp.int32, (rows, LANES), 1)
    g = [logits[:, k:k + 1] for k in range(N_EXPERT_GROUPS)]
    gmax = jnp.maximum(jnp.maximum(g[0], g[1]), jnp.maximum(g[2], g[3]))
    den = sum(jnp.exp(gk - gmax) for gk in g)
    g_w = 1.0 / den
    is_g = []
    taken = jnp.zeros_like(gmax)
    for k in range(N_EXPERT_GROUPS):
        hit = jnp.where(g[k] == gmax, 1.0, 0.0) * (1.0 - taken)
        is_g.append(hit)
        taken = taken + hit
    e = []
    for k in range(EXPERTS_PER_GROUP):
        col = jnp.zeros_like(gmax)
        for gi in range(N_EXPERT_GROUPS):
            lane_id = N_EXPERT_GROUPS + gi * EXPERTS_PER_GROUP + k
            col = jnp.where(is_g[gi] > 0.5, logits[:, lane_id:lane_id + 1], col)
        e.append(col)
    v1 = jnp.maximum(jnp.maximum(e[0], e[1]), jnp.maximum(e[2], e[3]))
    first = []
    taken = jnp.zeros_like(v1)
    for k in range(EXPERTS_PER_GROUP):
        hit = jnp.where(e[k] == v1, 1.0, 0.0) * (1.0 - taken)
        first.append(hit)
        taken = taken + hit
    e2 = [jnp.where(first[k] > 0.5, -jnp.inf, e[k]) for k in range(EXPERTS_PER_GROUP)]
    v2 = jnp.maximum(jnp.maximum(e2[0], e2[1]), jnp.maximum(e2[2], e2[3]))
    second = []
    taken = jnp.zeros_like(v2)
    for k in range(EXPERTS_PER_GROUP):
        hit = jnp.where(e2[k] == v2, 1.0, 0.0) * (1.0 - taken)
        second.append(hit)
        taken = taken + hit
    t = jnp.exp(v2 - v1)
    w1 = 1.0 / (1.0 + t)
    w2 = t * w1
    comb = jnp.zeros((rows, LANES), F32)
    for gi in range(N_EXPERT_GROUPS):
        for k in range(EXPERTS_PER_GROUP):
            col = is_g[gi] * (first[k] * w1 + second[k] * w2) * g_w
            comb = jnp.where(lane == gi * EXPERTS_PER_GROUP + k, col, comb)
    return comb


Q_SCALE = 1.4426950408889634 * HEAD_DIM ** -0.5


def _qkvu_kernel(x_ref, w_ref, cos_ref, sin_ref, qt_ref, k_ref, v_ref, u_ref, kb_ref, vt_ref,
                 km_ref, *, tm):
    xb = x_ref[...].astype(BF16)
    cos = cos_ref[...]
    sin = sin_ref[...]
    lane = lax.broadcasted_iota(jnp.int32, (tm, LANES), 1)
    first_half = (lane & 32) == 0
    hq = _dot(xb, w_ref[:, 0:ATTN_WIDTH])
    hk = _dot(xb, w_ref[:, ATTN_WIDTH:2 * ATTN_WIDTH])
    hv = _dot(xb, w_ref[:, 2 * ATTN_WIDTH:3 * ATTN_WIDTH])
    v_ref[...] = hv
    for c in range(ATTN_WIDTH // LANES):
        sl = slice(c * LANES, (c + 1) * LANES)
        qc = _rope_chunk(hq[:, sl], cos, sin, first_half) * Q_SCALE
        kc = _rope_chunk(hk[:, sl], cos, sin, first_half)
        k_ref[:, sl] = kc
        kb_ref[:, sl] = kc.astype(BF16)
        for r in range(tm // MOBA_BLOCK):
            rows = slice(r * MOBA_BLOCK, (r + 1) * MOBA_BLOCK)
            km_ref[r:r + 1, sl] = jnp.sum(kc[rows], axis=0, keepdims=True) * (1.0 / MOBA_BLOCK)
            qt_ref[r, sl, :] = qc[rows].T.astype(BF16)
            vt_ref[r, sl, :] = hv[rows, sl].T.astype(BF16)
    u_ref[...] = _dot(xb, w_ref[:, 3 * ATTN_WIDTH:])


def _qkvu(x2d, w_qkvu, cos, sin, seq, tm):
    n, d = x2d.shape
    s_tiles = seq // tm
    nblk = tm // MOBA_BLOCK
    row = lambda i: (i, 0)
    blocked = lambda i: (i, 0, 0)
    out_shape = (
        jax.ShapeDtypeStruct((n // MOBA_BLOCK, ATTN_WIDTH, MOBA_BLOCK), BF16),
        jax.ShapeDtypeStruct((n, ATTN_WIDTH), F32),
        jax.ShapeDtypeStruct((n, ATTN_WIDTH), F32),
        jax.ShapeDtypeStruct((n, POOL_WIDTH), F32),
        jax.ShapeDtypeStruct((n, ATTN_WIDTH), BF16),
        jax.ShapeDtypeStruct((n // MOBA_BLOCK, ATTN_WIDTH, MOBA_BLOCK), BF16),
        jax.ShapeDtypeStruct((n // tm, nblk, ATTN_WIDTH), F32),
    )
    return pl.pallas_call(
        functools.partial(_qkvu_kernel, tm=tm),
        out_shape=out_shape,
        grid=(n // tm,),
        in_specs=[
            pl.BlockSpec((tm, d), row),
            pl.BlockSpec((d, 4 * ATTN_WIDTH), lambda i: (0, 0)),
            pl.BlockSpec((tm, LANES), lambda i: (i % s_tiles, 0)),
            pl.BlockSpec((tm, LANES), lambda i: (i % s_tiles, 0)),
        ],
        out_specs=(
            pl.BlockSpec((nblk, ATTN_WIDTH, MOBA_BLOCK), blocked),
            pl.BlockSpec((tm, ATTN_WIDTH), row),
            pl.BlockSpec((tm, ATTN_WIDTH), row),
            pl.BlockSpec((tm, POOL_WIDTH), row),
            pl.BlockSpec((tm, ATTN_WIDTH), row),
            pl.BlockSpec((nblk, ATTN_WIDTH, MOBA_BLOCK), blocked),
            pl.BlockSpec((None, nblk, ATTN_WIDTH), blocked),
        ),
        compiler_params=pltpu.CompilerParams(
            dimension_semantics=("arbitrary",), vmem_limit_bytes=VMEM_LIMIT),
        name="qkvu",
    )(x2d, w_qkvu, cos, sin)


BIAS_ROWS = 128
SUM_ROWS = 16


ATTN_PAIRS = 4


def _attn_kernel(qt_ref, k_ref, vt_ref, km_ref, o_ref, qa_ref, acc_ref):
    i = pl.program_id(2)
    n_past = km_ref.shape[0]
    n_heads = 2 * ATTN_PAIRS
    feat = lax.broadcasted_iota(jnp.int32, (LANES, MOBA_BLOCK), 0)
    head0 = feat < HEAD_DIM
    blk_id = lax.broadcasted_iota(jnp.int32, (n_past, MOBA_BLOCK), 0)
    blk_f = blk_id.astype(F32)
    past = blk_id < i
    key_id = lax.broadcasted_iota(jnp.int32, (MOBA_BLOCK, MOBA_BLOCK), 0)
    qry_id = lax.broadcasted_iota(jnp.int32, (MOBA_BLOCK, MOBA_BLOCK), 1)
    causal = key_id <= qry_id
    lane = lax.broadcasted_iota(jnp.int32, (MOBA_BLOCK, LANES), 1)
    ones_rows = jnp.ones((SUM_ROWS, MOBA_BLOCK), BF16)
    bias_pad = jnp.zeros((BIAS_ROWS - n_past, MOBA_BLOCK), BF16)
    own = pl.multiple_of(i * MOBA_BLOCK, MOBA_BLOCK)

    def slab(pair):
        return slice(pair * LANES, (pair + 1) * LANES)

    def vt_aug(blk, pair):
        return jnp.concatenate([vt_ref[blk, slab(pair), :], ones_rows], axis=0)

    def softmax_pv(ss, ms, blk):
        out = []
        for h in range(n_heads):
            m_new = jnp.maximum(ms[h], jnp.max(ss[h], axis=0, keepdims=True))
            alpha = jnp.exp2(ms[h] - m_new)
            p = jnp.exp2(ss[h] - m_new)
            acc_ref[h] = alpha * acc_ref[h] + _dot(vt_aug(blk, h // 2), p.astype(BF16))
            out.append(m_new)
        return tuple(out)

    ss = []
    for h in range(n_heads):
        pair = h // 2
        qt = qt_ref[slab(pair), :]
        qh = jnp.where(head0 if h % 2 == 0 else jnp.logical_not(head0), qt, jnp.zeros_like(qt))
        gate = _dot(km_ref[:, slab(pair)], qh.astype(F32), HIGHEST)
        g = jnp.where(past, gate, -jnp.inf)
        sel = jnp.zeros(g.shape, F32)
        for _ in range(MOBA_TOPK):
            mx = jnp.max(g, axis=0, keepdims=True)
            idx = jnp.min(jnp.where(g == mx, blk_f, float(n_past)), axis=0, keepdims=True)
            pick = blk_f == idx
            sel = jnp.where(pick, 1.0, sel)
            g = jnp.where(pick, -jnp.inf, g)
        bias = jnp.where(jnp.logical_and(sel > 0.5, past), 0.0, MASK_BIAS).astype(BF16)
        qa_ref[h] = jnp.concatenate([qh, bias, bias_pad], axis=0)
        kd = k_ref[pl.ds(own, MOBA_BLOCK), slab(pair)]
        ss.append(jnp.where(causal, _dot(kd, qh), -jnp.inf))
        acc_ref[h] = jnp.zeros(acc_ref.shape[1:], F32)
    m0 = jnp.full((1, MOBA_BLOCK), MASK_BIAS, F32)
    ms = softmax_pv(ss, (m0,) * n_heads, i)

    def body(j, ms):
        off = pl.multiple_of(j * MOBA_BLOCK, MOBA_BLOCK)
        onehot = jnp.where(lane == j, 1.0, 0.0).astype(BF16)
        ss = []
        for pair in range(ATTN_PAIRS):
            ka = jnp.concatenate([k_ref[pl.ds(off, MOBA_BLOCK), slab(pair)], onehot], axis=1)
            ss += [_dot(ka, qa_ref[2 * pair]), _dot(ka, qa_ref[2 * pair + 1])]
        return softmax_pv(ss, ms, j)

    lax.fori_loop(0, i, body, ms)
    for pair in range(ATTN_PAIRS):
        a0 = acc_ref[2 * pair]
        a1 = acc_ref[2 * pair + 1]
        ot = jnp.where(head0, a0[:LANES] / a0[LANES:LANES + 1], a1[:LANES] / a1[LANES:LANES + 1])
        o_ref[:, slab(pair)] = ot.T.astype(BF16)


def _attn(qt, kb, vt, km):
    b, s, _ = kb.shape
    nb = s // MOBA_BLOCK
    width = ATTN_PAIRS * LANES
    return pl.pallas_call(
        _attn_kernel,
        out_shape=jax.ShapeDtypeStruct((b, s, ATTN_WIDTH), BF16),
        grid=(b, ATTN_WIDTH // width, nb),
        in_specs=[
            pl.BlockSpec((None, width, MOBA_BLOCK), lambda bi, p, i: (bi * nb + i, p, 0)),
            pl.BlockSpec((None, s, width), lambda bi, p, i: (bi, 0, p)),
            pl.BlockSpec((None, nb, width, MOBA_BLOCK), lambda bi, p, i: (bi, 0, p, 0)),
            pl.BlockSpec((None, nb, width), lambda bi, p, i: (bi, 0, p)),
        ],
        out_specs=pl.BlockSpec((None, MOBA_BLOCK, width), lambda bi, p, i: (bi, i, p)),
        scratch_shapes=[
            pltpu.VMEM((2 * ATTN_PAIRS, LANES + BIAS_ROWS, MOBA_BLOCK), BF16),
            pltpu.VMEM((2 * ATTN_PAIRS, LANES + SUM_ROWS, MOBA_BLOCK), F32),
        ],
        compiler_params=pltpu.CompilerParams(
            dimension_semantics=("arbitrary", "arbitrary", "arbitrary"),
            vmem_limit_bytes=VMEM_LIMIT),
        name="moba_attn",
    )(qt, kb, vt, km)


def _pool_group(window_sum, u_g, count, w_pool_g, scale_g, precision=None, cast=None):
    d = window_sum / count - u_g
    if cast is not None:
        d = d.astype(cast)
    return _dot(d, w_pool_g, precision) * scale_g


def _mix_kernel(x_ref, attn_ref, u_ref, halo_ref, wg_ref, bg_ref, wpool_ref, pscale_ref, wa_ref,
                wb_ref, wout_ref, g1_ref, b1_ref, x1_ref, z_ref, *, tm, alpha):
    i = pl.program_id(1)
    x = x_ref[...]
    xb = x.astype(BF16)
    u = u_ref[...]
    z_ref[0:HALO_ROWS, :] = jnp.where(i > 0, halo_ref[...], 0.0)
    z_ref[HALO_ROWS:, :] = u
    pos1 = (i * tm + 1 + lax.broadcasted_iota(jnp.int32, (tm, 1), 0)).astype(F32)
    parts = []
    for g, w in enumerate(POOL_WINDOWS):
        sl = slice(g * POOL_GROUP_WIDTH, (g + 1) * POOL_GROUP_WIDTH)
        acc = u[:, sl]
        for back in range(1, w):
            acc = acc + z_ref[HALO_ROWS - back:HALO_ROWS - back + tm, sl]
        count = jnp.minimum(float(w), pos1)
        parts.append(_pool_group(acc, u[:, sl], count, wpool_ref[g], pscale_ref[:, sl], cast=BF16))
    pooled = jnp.concatenate(parts, axis=1).astype(BF16)
    d = x.shape[1]
    ga = _dot(xb, wg_ref[:, :d]) + bg_ref[0:1, :]
    gb = _dot(xb, wg_ref[:, d:]) + bg_ref[1:2, :]
    a = _dot(attn_ref[...], wa_ref[...])
    bb = _dot(pooled, wb_ref[...])
    merged = _sigmoid(ga) * a + _sigmoid(gb) * bb
    y = alpha * x + _dot(merged.astype(BF16), wout_ref[...])
    x1_ref[...] = _layer_norm(y, g1_ref[...], b1_ref[...])


def _mix(x, attn, u, w_gates, b_gate, w_pool, pool_scale, w_a, w_b, w_out, ln_g, ln_b, tm, alpha):
    b, s, d = x.shape
    halo_per_tile = tm // HALO_ROWS
    tile = lambda bi, i: (bi, i, 0)
    const2 = lambda bi, i: (0, 0)
    return pl.pallas_call(
        functools.partial(_mix_kernel, tm=tm, alpha=alpha),
        out_shape=jax.ShapeDtypeStruct((b, s, d), F32),
        grid=(b, s // tm),
        in_specs=[
            pl.BlockSpec((None, tm, d), tile),
            pl.BlockSpec((None, tm, ATTN_WIDTH), tile),
            pl.BlockSpec((None, tm, POOL_WIDTH), tile),
            pl.BlockSpec((None, HALO_ROWS, POOL_WIDTH),
                         lambda bi, i: (bi, jnp.maximum(i * halo_per_tile - 1, 0), 0)),
            pl.BlockSpec(w_gates.shape, const2),
            pl.BlockSpec(b_gate.shape, const2),
            pl.BlockSpec(w_pool.shape, lambda bi, i: (0, 0, 0)),
            pl.BlockSpec(pool_scale.shape, const2),
            pl.BlockSpec(w_a.shape, const2),
            pl.BlockSpec(w_b.shape, const2),
            pl.BlockSpec(w_out.shape, const2),
            pl.BlockSpec(ln_g.shape, const2),
            pl.BlockSpec(ln_b.shape, const2),
        ],
        out_specs=pl.BlockSpec((None, tm, d), tile),
        scratch_shapes=[pltpu.VMEM((HALO_ROWS + tm, POOL_WIDTH), F32)],
        compiler_params=pltpu.CompilerParams(
            dimension_semantics=("arbitrary", "arbitrary"), vmem_limit_bytes=VMEM_LIMIT),
        name="mix",
    )(x, attn, u, u, w_gates, b_gate, w_pool, pool_scale, w_a, w_b, w_out, ln_g, ln_b)


def _moe_kernel(x1_ref, wr_ref, br_ref, wg_ref, wu_ref, wd_ref, g2_ref, b2_ref, y_ref,
                xb_ref, comb_ref, acc_ref, *, alpha):
    e = pl.program_id(1)
    tm = x1_ref.shape[0]
    lane = lax.broadcasted_iota(jnp.int32, (tm, LANES), 1)

    @pl.when(e == 0)
    def _():
        x1 = x1_ref[...]
        xb_ref[...] = x1.astype(BF16)
        logits = _dot(x1, wr_ref[...], HIGHEST) + br_ref[...]
        comb_ref[...] = _route(logits)
        acc_ref[...] = jnp.zeros_like(acc_ref)

    xb = xb_ref[...]
    hg = _dot(xb, wg_ref[...])
    hu = _dot(xb, wu_ref[...])
    c_e = jnp.sum(jnp.where(lane == e, comb_ref[...], 0.0), axis=1, keepdims=True)
    h = hg * _sigmoid(hg) * hu * c_e
    acc_ref[...] += _dot(h.astype(BF16), wd_ref[...])

    @pl.when(e == N_EXPERTS - 1)
    def _():
        y = alpha * x1_ref[...] + acc_ref[...]
        y_ref[...] = _layer_norm(y, g2_ref[...], b2_ref[...])


def _moe(x1, w_r, b_r, w_g, w_u, w_d, ln_g, ln_b, tm, alpha):
    n, d = x1.shape
    f = w_g.shape[2]
    tile = lambda i, e: (i, 0)
    const2 = lambda i, e: (0, 0)
    return pl.pallas_call(
        functools.partial(_moe_kernel, alpha=alpha),
        out_shape=jax.ShapeDtypeStruct((n, d), F32),
        grid=(n // tm, N_EXPERTS),
        in_specs=[
            pl.BlockSpec((tm, d), tile),
            pl.BlockSpec(w_r.shape, const2),
            pl.BlockSpec(b_r.shape, const2),
            pl.BlockSpec((None, d, f), lambda i, e: (e, 0, 0)),
            pl.BlockSpec((None, d, f), lambda i, e: (e, 0, 0)),
            pl.BlockSpec((None, f, d), lambda i, e: (e, 0, 0)),
            pl.BlockSpec(ln_g.shape, const2),
            pl.BlockSpec(ln_b.shape, const2),
        ],
        out_specs=pl.BlockSpec((tm, d), tile),
        scratch_shapes=[
            pltpu.VMEM((tm, d), BF16),
            pltpu.VMEM((tm, LANES), F32),
            pltpu.VMEM((tm, d), F32),
        ],
        compiler_params=pltpu.CompilerParams(
            dimension_semantics=("arbitrary", "arbitrary"), vmem_limit_bytes=VMEM_LIMIT),
        name="moe",
    )(x1, w_r, b_r, w_g, w_u, w_d, ln_g, ln_b)


def _s_proj_kernel(x_ref, w_ref, cos_ref, sin_ref, h_ref):
    c = pl.program_id(0)
    h = _dot(x_ref[...], w_ref[...], HIGHEST)
    rows = h.shape[0]
    lane = lax.broadcasted_iota(jnp.int32, (rows, LANES), 1)
    first_half = (lane & 32) == 0
    rotary = c < 2
    for j in range(h.shape[1] // LANES):
        sl = slice(j * LANES, (j + 1) * LANES)
        hc = h[:, sl]
        h_ref[:, sl] = jnp.where(rotary, _rope_chunk(hc, cos_ref[...], sin_ref[...], first_half), hc)


def _s_proj(x, w_in, cos, sin):
    rows, d = x.shape
    width = w_in.shape[1]
    chunk = ATTN_WIDTH
    return pl.pallas_call(
        _s_proj_kernel,
        out_shape=jax.ShapeDtypeStruct((rows, width), F32),
        grid=(width // chunk,),
        in_specs=[
            pl.BlockSpec((rows, d), lambda c: (0, 0)),
            pl.BlockSpec((d, chunk), lambda c: (0, c)),
            pl.BlockSpec((1, LANES), lambda c: (0, 0)),
            pl.BlockSpec((1, LANES), lambda c: (0, 0)),
        ],
        out_specs=pl.BlockSpec((rows, chunk), lambda c: (0, c)),
        compiler_params=pltpu.CompilerParams(
            dimension_semantics=("arbitrary",), vmem_limit_bytes=VMEM_LIMIT),
        name="s_proj",
    )(x, w_in, cos, sin)


PAGES_PER_STEP = 16


def _s_means_kernel(pt_ref, q_ref, *refs, page_size, n_blocks):
    page_refs = refs[:PAGES_PER_STEP]
    sel_ref = refs[PAGES_PER_STEP]
    km_ref = refs[PAGES_PER_STEP + 1]
    c = pl.program_id(1)
    pages_per_block = MOBA_BLOCK // page_size
    blocks_per_step = PAGES_PER_STEP // pages_per_block
    for r in range(blocks_per_step):
        tot = jnp.zeros((1, ATTN_WIDTH), F32)
        for pp in range(pages_per_block):
            tot = tot + jnp.sum(page_refs[r * pages_per_block + pp][...], axis=0, keepdims=True)
        km_ref[pl.ds(c * blocks_per_step + r, 1), :] = tot * (1.0 / MOBA_BLOCK)

    @pl.when(c == pl.num_programs(1) - 1)
    def _():
        prod = km_ref[...] * q_ref[...]
        d_id = lax.broadcasted_iota(jnp.int32, (ATTN_WIDTH, LANES), 0)
        h_id = lax.broadcasted_iota(jnp.int32, (ATTN_WIDTH, LANES), 1)
        seg = jnp.where(d_id // HEAD_DIM == h_id, 1.0, 0.0)
        g = _dot(prod, seg, HIGHEST)
        row_f = lax.broadcasted_iota(jnp.int32, (n_blocks, LANES), 0).astype(F32)
        out_row = lax.broadcasted_iota(jnp.int32, (8, LANES), 0)
        out = jnp.zeros((8, LANES), F32)
        for t in range(MOBA_TOPK):
            mx = jnp.max(g, axis=0, keepdims=True)
            idx = jnp.min(jnp.where(g == mx, row_f, float(n_blocks)), axis=0, keepdims=True)
            out = jnp.where(out_row == t, idx, out)
            g = jnp.where(row_f == idx, -jnp.inf, g)
        sel_ref[...] = out.astype(jnp.int32)


def _s_means(page_table, q3, cache_k_pages, n_blocks):
    db, n_pages = page_table.shape
    page_size = cache_k_pages.shape[1]

    def page_spec(r):
        return pl.BlockSpec((None, page_size, ATTN_WIDTH),
                            lambda b, c, pt: (pt[b, c * PAGES_PER_STEP + r], 0, 0))

    return pl.pallas_call(
        functools.partial(_s_means_kernel, page_size=page_size, n_blocks=n_blocks),
        out_shape=jax.ShapeDtypeStruct((db, 8, LANES), jnp.int32),
        grid_spec=pltpu.PrefetchScalarGridSpec(
            num_scalar_prefetch=1,
            grid=(db, n_pages // PAGES_PER_STEP),
            in_specs=[pl.BlockSpec((None, 1, ATTN_WIDTH), lambda b, c, pt: (b, 0, 0))]
            + [page_spec(r) for r in range(PAGES_PER_STEP)],
            out_specs=pl.BlockSpec((None, 8, LANES), lambda b, c, pt: (b, 0, 0)),
            scratch_shapes=[pltpu.VMEM((n_blocks, ATTN_WIDTH), F32)],
        ),
        compiler_params=pltpu.CompilerParams(
            dimension_semantics=("arbitrary", "arbitrary"), vmem_limit_bytes=VMEM_LIMIT),
        name="s_means",
    )(page_table, q3, *([cache_k_pages] * PAGES_PER_STEP))


def _s_attn_kernel(pt_ref, sel_ref, q_ref, kn_ref, vn_ref, k0_ref, k1_ref, v0_ref, v1_ref, o_ref,
                   m_ref, l_ref, acc_ref):
    h = pl.program_id(1)
    t = pl.program_id(2)
    lane = lax.broadcasted_iota(jnp.int32, (1, ATTN_WIDTH), 1)
    in_head = (lane // HEAD_DIM) == h
    qh = jnp.where(in_head, q_ref[...] * (HEAD_DIM ** -0.5), 0.0)

    @pl.when(jnp.logical_and(h == 0, t == 0))
    def _():
        o_ref[...] = jnp.zeros_like(o_ref)

    @pl.when(t == 0)
    def _():
        m_ref[...] = jnp.sum(qh * kn_ref[...], axis=1, keepdims=True)
        l_ref[...] = jnp.ones_like(l_ref)
        acc_ref[...] = vn_ref[...]

    s0 = jnp.sum(k0_ref[...] * qh, axis=1, keepdims=True)
    s1 = jnp.sum(k1_ref[...] * qh, axis=1, keepdims=True)
    m_old = m_ref[...]
    m_new = jnp.maximum(m_old, jnp.maximum(jnp.max(s0, axis=0, keepdims=True),
                                           jnp.max(s1, axis=0, keepdims=True)))
    alpha = jnp.exp(m_old - m_new)
    p0 = jnp.exp(s0 - m_new)
    p1 = jnp.exp(s1 - m_new)
    l_ref[...] = alpha * l_ref[...] + jnp.sum(p0, axis=0, keepdims=True) + jnp.sum(p1, axis=0, keepdims=True)
    pv = jnp.sum(p0 * v0_ref[...], axis=0, keepdims=True) + jnp.sum(p1 * v1_ref[...], axis=0, keepdims=True)
    acc_ref[...] = alpha * acc_ref[...] + pv
    m_ref[...] = m_new

    @pl.when(t == MOBA_TOPK - 1)
    def _():
        o_ref[...] = jnp.where(in_head, acc_ref[...] / l_ref[...], o_ref[...])


def _s_attn(page_table, sel, q3, kn3, vn3, cache_k_pages, cache_v_pages):
    db = page_table.shape[0]
    page_size = cache_k_pages.shape[1]
    pages_per_block = MOBA_BLOCK // page_size
    assert pages_per_block == 2

    def page_spec(r):
        return pl.BlockSpec(
            (None, page_size, ATTN_WIDTH),
            lambda b, h, t, pt, sl: (pt[b, sl[b, t * N_HEADS + h] * pages_per_block + r], 0, 0))

    per_b = pl.BlockSpec((None, 1, ATTN_WIDTH), lambda b, h, t, pt, sl: (b, 0, 0))
    return pl.pallas_call(
        _s_attn_kernel,
        out_shape=jax.ShapeDtypeStruct((db, 1, ATTN_WIDTH), F32),
        grid_spec=pltpu.PrefetchScalarGridSpec(
            num_scalar_prefetch=2,
            grid=(db, N_HEADS, MOBA_TOPK),
            in_specs=[per_b, per_b, per_b, page_spec(0), page_spec(1), page_spec(0), page_spec(1)],
            out_specs=per_b,
            scratch_shapes=[pltpu.VMEM((1, 1), F32), pltpu.VMEM((1, 1), F32),
                            pltpu.VMEM((1, ATTN_WIDTH), F32)],
        ),
        compiler_params=pltpu.CompilerParams(
            dimension_semantics=("arbitrary", "arbitrary", "arbitrary"),
            vmem_limit_bytes=VMEM_LIMIT),
        name="s_attn",
    )(page_table, sel, q3, kn3, vn3, cache_k_pages, cache_k_pages, cache_v_pages, cache_v_pages)


def _s_tail_kernel(x_ref, attn_ref, u_ref, ga_ref, gb_ref, sp_ref, bg_ref, wpool_ref, pscale_ref,
                   wa_ref, wb_ref, wout_ref, g1_ref, b1_ref, wr_ref, br_ref, wg_ref, wu_ref,
                   wd_ref, g2_ref, b2_ref, y_ref, x1_ref, comb_ref, acc_ref, *, alpha):
    e = pl.program_id(0)
    rows = x_ref.shape[0]
    lane = lax.broadcasted_iota(jnp.int32, (rows, LANES), 1)

    @pl.when(e == 0)
    def _():
        u = u_ref[...]
        parts = []
        for g, w in enumerate(POOL_WINDOWS):
            sl = slice(g * POOL_GROUP_WIDTH, (g + 1) * POOL_GROUP_WIDTH)
            acc = u[:, sl]
            for back in range(1, w):
                acc = acc + sp_ref[POOL_STATE - back][:, sl]
            parts.append(_pool_group(acc, u[:, sl], float(w), wpool_ref[g], pscale_ref[:, sl],
                                     precision=HIGHEST))
        pooled = jnp.concatenate(parts, axis=1)
        a = _dot(attn_ref[...], wa_ref[...], HIGHEST)
        bb = _dot(pooled, wb_ref[...], HIGHEST)
        merged = _sigmoid(ga_ref[...] + bg_ref[0:1, :]) * a + _sigmoid(gb_ref[...] + bg_ref[1:2, :]) * bb
        y = alpha * x_ref[...] + _dot(merged, wout_ref[...], HIGHEST)
        x1 = _layer_norm(y, g1_ref[...], b1_ref[...])
        x1_ref[...] = x1
        comb_ref[...] = _route(_dot(x1, wr_ref[...], HIGHEST) + br_ref[...])
        acc_ref[...] = jnp.zeros_like(acc_ref)

    x1 = x1_ref[...]
    hg = _dot(x1, wg_ref[...], HIGHEST)
    hu = _dot(x1, wu_ref[...], HIGHEST)
    c_e = jnp.sum(jnp.where(lane == e, comb_ref[...], 0.0), axis=1, keepdims=True)
    h = hg * _sigmoid(hg) * hu * c_e
    acc_ref[...] += _dot(h, wd_ref[...], HIGHEST)

    @pl.when(e == N_EXPERTS - 1)
    def _():
        y_ref[...] = _layer_norm(alpha * x1_ref[...] + acc_ref[...], g2_ref[...], b2_ref[...])


def _s_tail(x, attn, u, ga, gb, sp_t, b_gate, w_pool, pool_scale, w_a, w_b, w_out, ln1_g, ln1_b,
            w_r, b_r, w_g, w_u, w_d, ln2_g, ln2_b, alpha):
    rows, d = x.shape
    f = w_g.shape[2]
    const2 = lambda e: (0, 0)
    const3 = lambda e: (0, 0, 0)
    full = lambda a: pl.BlockSpec(a.shape, const2 if a.ndim == 2 else const3)
    per_e = lambda blk: pl.BlockSpec((None,) + blk, lambda e: (e, 0, 0))
    small = [x, attn, u, ga, gb, sp_t, b_gate, w_pool, pool_scale, w_a, w_b, w_out, ln1_g, ln1_b,
             w_r, b_r]
    return pl.pallas_call(
        functools.partial(_s_tail_kernel, alpha=alpha),
        out_shape=jax.ShapeDtypeStruct((rows, d), F32),
        grid=(N_EXPERTS,),
        in_specs=[full(a) for a in small]
        + [per_e((d, f)), per_e((d, f)), per_e((f, d)), full(ln2_g), full(ln2_b)],
        out_specs=pl.BlockSpec((rows, d), const2),
        scratch_shapes=[pltpu.VMEM((rows, d), F32), pltpu.VMEM((rows, LANES), F32),
                        pltpu.VMEM((rows, d), F32)],
        compiler_params=pltpu.CompilerParams(
            dimension_semantics=("arbitrary",), vmem_limit_bytes=VMEM_LIMIT),
        name="s_tail",
    )(*small, w_g, w_u, w_d, ln2_g, ln2_b)


def _rope_tables(pos):
    inv_freq = 1.0 / (ROPE_THETA ** (jnp.arange(0, HEAD_DIM, 2, dtype=F32) / HEAD_DIM))
    ang = pos.astype(F32)[:, None] * inv_freq[None, :]
    cos = jnp.cos(ang)
    sin = jnp.sin(ang)
    cos_t = jnp.tile(cos, (1, LANES // (HEAD_DIM // 2)))
    sin_t = jnp.tile(jnp.concatenate([-sin, sin], axis=1), (1, LANES // HEAD_DIM))
    return cos_t, sin_t


def kernel(x_prompt, x_sample, cache_k, cache_v, state_pool, page_table, w_in, w_pool, pool_scale,
           w_branch_a, w_branch_b, b_gate, w_out, ln1_g, ln1_b, w_group_router, b_group_router,
           w_expert_router, b_expert_router, w_e_gate, w_e_up, w_e_down, ln2_g, ln2_b):
    depth = w_in.shape[0]
    assert depth == 1 and x_sample.shape[1] == 1
    alpha = (2 * depth) ** 0.25
    b, s, d = x_prompt.shape
    db = x_sample.shape[0]
    n_pages = page_table.shape[1]
    page_size = cache_k.shape[2]
    past_len = n_pages * page_size
    n_blocks = past_len // MOBA_BLOCK
    assert past_len % MOBA_BLOCK == 0 and s % MOBA_BLOCK == 0

    w_in0 = w_in[0]
    qkvu_cols = 3 * ATTN_WIDTH + POOL_WIDTH
    row2 = lambda a: a.reshape(1, -1)
    w_r = jnp.concatenate(
        [w_group_router[0], jnp.transpose(w_expert_router[0], (1, 0, 2)).reshape(d, N_EXPERTS),
         jnp.zeros((d, LANES - N_EXPERT_GROUPS - N_EXPERTS), F32)], axis=1)
    b_r = jnp.concatenate(
        [b_group_router[0], b_expert_router[0].reshape(-1),
         jnp.zeros((LANES - N_EXPERT_GROUPS - N_EXPERTS,), F32)]).reshape(1, LANES)

    cos_p, sin_p = _rope_tables(jnp.arange(s, dtype=jnp.int32))
    tm_a = 512
    qt, k, v, u, kb, vt, km = _qkvu(x_prompt.reshape(b * s, d), w_in0[:, :qkvu_cols].astype(BF16),
                                    cos_p, sin_p, s, tm_a)
    nb = s // MOBA_BLOCK
    attn = _attn(qt, kb.reshape(b, s, ATTN_WIDTH), vt.reshape(b, nb, ATTN_WIDTH, MOBA_BLOCK),
                 km.reshape(b, nb, ATTN_WIDTH))
    u3 = u.reshape(b, s, POOL_WIDTH)
    x1 = _mix(x_prompt, attn, u3, w_in0[:, qkvu_cols:].astype(BF16), b_gate[0],
              w_pool[0].astype(BF16), row2(pool_scale[0]), w_branch_a[0].astype(BF16),
              w_branch_b[0].astype(BF16), w_out[0].astype(BF16), row2(ln1_g[0]), row2(ln1_b[0]),
              256, alpha)
    y_p = _moe(x1.reshape(b * s, d), w_r, b_r, w_e_gate[0].astype(BF16), w_e_up[0].astype(BF16),
               w_e_down[0].astype(BF16), row2(ln2_g[0]), row2(ln2_b[0]), 1024, alpha)

    cos_s, sin_s = _rope_tables(jnp.full((1,), past_len, jnp.int32))
    x_s = x_sample.reshape(db, d)
    h_s = _s_proj(x_s, w_in0, cos_s, sin_s)
    q_s = h_s[:, :ATTN_WIDTH]
    k_s = h_s[:, ATTN_WIDTH:2 * ATTN_WIDTH]
    v_s = h_s[:, 2 * ATTN_WIDTH:3 * ATTN_WIDTH]
    u_s = h_s[:, 3 * ATTN_WIDTH:qkvu_cols]
    ga_s = h_s[:, qkvu_cols:qkvu_cols + d]
    gb_s = h_s[:, qkvu_cols + d:]
    ck = cache_k[0].reshape(cache_k.shape[1], page_size, ATTN_WIDTH)
    cv = cache_v[0].reshape(cache_v.shape[1], page_size, ATTN_WIDTH)
    q3 = q_s.reshape(db, 1, ATTN_WIDTH)
    sel = _s_means(page_table, q3, ck, n_blocks)
    sel2 = sel[:, :MOBA_TOPK, :N_HEADS].reshape(db, MOBA_TOPK * N_HEADS)
    attn_s = _s_attn(page_table, sel2, q3, k_s.reshape(db, 1, ATTN_WIDTH),
                     v_s.reshape(db, 1, ATTN_WIDTH), ck, cv).reshape(db, ATTN_WIDTH)
    sp = state_pool[0]
    y_s = _s_tail(x_s, attn_s, u_s, ga_s, gb_s, jnp.transpose(sp, (1, 0, 2)), b_gate[0], w_pool[0],
                  row2(pool_scale[0]), w_branch_a[0], w_branch_b[0], w_out[0], row2(ln1_g[0]),
                  row2(ln1_b[0]), w_r, b_r, w_e_gate[0], w_e_up[0], w_e_down[0], row2(ln2_g[0]),
                  row2(ln2_b[0]), alpha)

    heads = (N_HEADS, HEAD_DIM)
    return (
        y_p.reshape(b, s, d),
        y_s.reshape(db, 1, d),
        k.reshape((1, b, s) + heads),
        v.reshape((1, b, s) + heads),
        u3[:, s - POOL_STATE:, :][None],
        k_s.reshape((1, db, 1) + heads),
        v_s.reshape((1, db, 1) + heads),
        jnp.concatenate([sp[:, 1:, :], u_s[:, None, :]], axis=1)[None],
    )
```

```python
import functools

import jax
import jax.numpy as jnp
from jax import lax
from jax.experimental import pallas as pl
from jax.experimental.pallas import tpu as pltpu

F32 = jnp.float32
BF16 = jnp.bfloat16
HIGHEST = lax.Precision.HIGHEST

N_HEADS = 8
HEAD_DIM = 64
ATTN_WIDTH = N_HEADS * HEAD_DIM
MOBA_BLOCK = 256
MOBA_TOPK = 3
ROPE_THETA = 10000.0
POOL_WINDOWS = (2, 4, 8, 16)
POOL_GROUP_WIDTH = 128
POOL_WIDTH = 512
POOL_STATE = 15
N_EXPERT_GROUPS = 4
EXPERTS_PER_GROUP = 4
N_EXPERTS = 16
LN_EPS = 1e-5
LANES = 128
HALO_ROWS = 16
MASK_BIAS = -1e30
VMEM_LIMIT = 56 * 1024 * 1024

NT_DIMS = (((1,), (1,)), ((), ()))


def _dot(a, b, precision=None):
    return jnp.dot(a, b, preferred_element_type=F32, precision=precision)


def _dot_nt(a, b, precision=None):
    return lax.dot_general(a, b, NT_DIMS, preferred_element_type=F32, precision=precision)


def _layer_norm(y, g, b):
    mu = jnp.mean(y, axis=-1, keepdims=True)
    var = jnp.mean(jnp.square(y - mu), axis=-1, keepdims=True)
    return (y - mu) * lax.rsqrt(var + LN_EPS) * g + b


def _sigmoid(x):
    return 1.0 / (1.0 + jnp.exp(-x))


def _rope_chunk(x, cos, sin_signed, first_half):
    partner = jnp.where(first_half, pltpu.roll(x, 96, 1), pltpu.roll(x, 32, 1))
    return x * cos + partner * sin_signed


def _route(logits):
    rows = logits.shape[0]
    lane = lax.broadcasted_iota(jnp.int32, (rows, LANES), 1)
    g = [logits[:, k:k + 1] for k in range(N_EXPERT_GROUPS)]
    gmax = jnp.maximum(jnp.maximum(g[0], g[1]), jnp.maximum(g[2], g[3]))
    den = sum(jnp.exp(gk - gmax) for gk in g)
    g_w = 1.0 / den
    is_g = []
    taken = jnp.zeros_like(gmax)
    for k in range(N_EXPERT_GROUPS):
        hit = jnp.where(g[k] == gmax, 1.0, 0.0) * (1.0 - taken)
        is_g.append(hit)
        taken = taken + hit
    e = []
    for k in range(EXPERTS_PER_GROUP):
        col = jnp.zeros_like(gmax)
        for gi in range(N_EXPERT_GROUPS):
            lane_id = N_EXPERT_GROUPS + gi * EXPERTS_PER_GROUP + k
            col = jnp.where(is_g[gi] > 0.5, logits[:, lane_id:lane_id + 1], col)
        e.append(col)
    v1 = jnp.maximum(jnp.maximum(e[0], e[1]), jnp.maximum(e[2], e[3]))
    first = []
    taken = jnp.zeros_like(v1)
    for k in range(EXPERTS_PER_GROUP):
        hit = jnp.where(e[k] == v1, 1.0, 0.0) * (1.0 - taken)
        first.append(hit)
        taken = taken + hit
    e2 = [jnp.where(first[k] > 0.5, -jnp.inf, e[k]) for k in range(EXPERTS_PER_GROUP)]
    v2 = jnp.maximum(jnp.maximum(e2[0], e2[1]), jnp.maximum(e2[2], e2[3]))
    second = []
    taken = jnp.zeros_like(v2)
    for k in range(EXPERTS_PER_GROUP):
        hit = jnp.where(e2[k] == v2, 1.0, 0.0) * (1.0 - taken)
        second.append(hit)
        taken = taken + hit
    t = jnp.exp(v2 - v1)
    w1 = 1.0 / (1.0 + t)
    w2 = t * w1
    comb = jnp.zeros((rows, LANES), F32)
    for gi in range(N_EXPERT_GROUPS):
        for k in range(EXPERTS_PER_GROUP):
            col = is_g[gi] * (first[k] * w1 + second[k] * w2) * g_w
            comb = jnp.where(lane == gi * EXPERTS_PER_GROUP + k, col, comb)
    return comb


Q_SCALE = 1.4426950408889634 * HEAD_DIM ** -0.5


def _qkvu_kernel(x_ref, w_ref, cos_ref, sin_ref, qt_ref, k_ref, v_ref, u_ref, kb_ref, vt_ref,
                 km_ref, *, tm):
    xb = x_ref[...].astype(BF16)
    cos = cos_ref[...]
    sin = sin_ref[...]
    lane = lax.broadcasted_iota(jnp.int32, (tm, LANES), 1)
    first_half = (lane & 32) == 0
    hq = _dot(xb, w_ref[:, 0:ATTN_WIDTH])
    hk = _dot(xb, w_ref[:, ATTN_WIDTH:2 * ATTN_WIDTH])
    hv = _dot(xb, w_ref[:, 2 * ATTN_WIDTH:3 * ATTN_WIDTH])
    v_ref[...] = hv
    for c in range(ATTN_WIDTH // LANES):
        sl = slice(c * LANES, (c + 1) * LANES)
        qc = _rope_chunk(hq[:, sl], cos, sin, first_half) * Q_SCALE
        kc = _rope_chunk(hk[:, sl], cos, sin, first_half)
        k_ref[:, sl] = kc
        kb_ref[:, sl] = kc.astype(BF16)
        for r in range(tm // MOBA_BLOCK):
            rows = slice(r * MOBA_BLOCK, (r + 1) * MOBA_BLOCK)
            km_ref[r:r + 1, sl] = jnp.sum(kc[rows], axis=0, keepdims=True) * (1.0 / MOBA_BLOCK)
            qt_ref[r, sl, :] = qc[rows].T.astype(BF16)
            vt_ref[r, sl, :] = hv[rows, sl].T.astype(BF16)
    u_ref[...] = _dot(xb, w_ref[:, 3 * ATTN_WIDTH:])


def _qkvu(x2d, w_qkvu, cos, sin, seq, tm):
    n, d = x2d.shape
    s_tiles = seq // tm
    nblk = tm // MOBA_BLOCK
    row = lambda i: (i, 0)
    blocked = lambda i: (i, 0, 0)
    out_shape = (
        jax.ShapeDtypeStruct((n // MOBA_BLOCK, ATTN_WIDTH, MOBA_BLOCK), BF16),
        jax.ShapeDtypeStruct((n, ATTN_WIDTH), F32),
        jax.ShapeDtypeStruct((n, ATTN_WIDTH), F32),
        jax.ShapeDtypeStruct((n, POOL_WIDTH), F32),
        jax.ShapeDtypeStruct((n, ATTN_WIDTH), BF16),
        jax.ShapeDtypeStruct((n // MOBA_BLOCK, ATTN_WIDTH, MOBA_BLOCK), BF16),
        jax.ShapeDtypeStruct((n // tm, nblk, ATTN_WIDTH), F32),
    )
    return pl.pallas_call(
        functools.partial(_qkvu_kernel, tm=tm),
        out_shape=out_shape,
        grid=(n // tm,),
        in_specs=[
            pl.BlockSpec((tm, d), row),
            pl.BlockSpec((d, 4 * ATTN_WIDTH), lambda i: (0, 0)),
            pl.BlockSpec((tm, LANES), lambda i: (i % s_tiles, 0)),
            pl.BlockSpec((tm, LANES), lambda i: (i % s_tiles, 0)),
        ],
        out_specs=(
            pl.BlockSpec((nblk, ATTN_WIDTH, MOBA_BLOCK), blocked),
            pl.BlockSpec((tm, ATTN_WIDTH), row),
            pl.BlockSpec((tm, ATTN_WIDTH), row),
            pl.BlockSpec((tm, POOL_WIDTH), row),
            pl.BlockSpec((tm, ATTN_WIDTH), row),
            pl.BlockSpec((nblk, ATTN_WIDTH, MOBA_BLOCK), blocked),
            pl.BlockSpec((None, nblk, ATTN_WIDTH), blocked),
        ),
        compiler_params=pltpu.CompilerParams(
            dimension_semantics=("arbitrary",), vmem_limit_bytes=VMEM_LIMIT),
        name="qkvu",
    )(x2d, w_qkvu, cos, sin)


BIAS_ROWS = 128
SUM_ROWS = 16


ATTN_PAIRS = 4


def _attn_kernel(qt_ref, k_ref, vt_ref, km_ref, o_ref, qa_ref, acc_ref):
    i = pl.program_id(2)
    n_past = km_ref.shape[0]
    n_heads = 2 * ATTN_PAIRS
    feat = lax.broadcasted_iota(jnp.int32, (LANES, MOBA_BLOCK), 0)
    head0 = feat < HEAD_DIM
    blk_id = lax.broadcasted_iota(jnp.int32, (n_past, MOBA_BLOCK), 0)
    blk_f = blk_id.astype(F32)
    past = blk_id < i
    key_id = lax.broadcasted_iota(jnp.int32, (MOBA_BLOCK, MOBA_BLOCK), 0)
    qry_id = lax.broadcasted_iota(jnp.int32, (MOBA_BLOCK, MOBA_BLOCK), 1)
    causal = key_id <= qry_id
    lane = lax.broadcasted_iota(jnp.int32, (MOBA_BLOCK, LANES), 1)
    ones_rows = jnp.ones((SUM_ROWS, MOBA_BLOCK), BF16)
    bias_pad = jnp.zeros((BIAS_ROWS - n_past, MOBA_BLOCK), BF16)
    own = pl.multiple_of(i * MOBA_BLOCK, MOBA_BLOCK)

    def slab(pair):
        return slice(pair * LANES, (pair + 1) * LANES)

    def vt_aug(blk, pair):
        return jnp.concatenate([vt_ref[blk, slab(pair), :], ones_rows], axis=0)

    def softmax_pv(ss, ms, blk):
        out = []
        for h in range(n_heads):
            m_new = jnp.maximum(ms[h], jnp.max(ss[h], axis=0, keepdims=True))
            alpha = jnp.exp2(ms[h] - m_new)
            p = jnp.exp2(ss[h] - m_new)
            acc_ref[h] = alpha * acc_ref[h] + _dot(vt_aug(blk, h // 2), p.astype(BF16))
            out.append(m_new)
        return tuple(out)

    ss = []
    for h in range(n_heads):
        pair = h // 2
        qt = qt_ref[slab(pair), :]
        qh = jnp.where(head0 if h % 2 == 0 else jnp.logical_not(head0), qt, jnp.zeros_like(qt))
        gate = _dot(km_ref[:, slab(pair)], qh.astype(F32), HIGHEST)
        g = jnp.where(past, gate, -jnp.inf)
        sel = jnp.zeros(g.shape, F32)
        for _ in range(MOBA_TOPK):
            mx = jnp.max(g, axis=0, keepdims=True)
            idx = jnp.min(jnp.where(g == mx, blk_f, float(n_past)), axis=0, keepdims=True)
            pick = blk_f == idx
            sel = jnp.where(pick, 1.0, sel)
            g = jnp.where(pick, -jnp.inf, g)
        bias = jnp.where(jnp.logical_and(sel > 0.5, past), 0.0, MASK_BIAS).astype(BF16)
        qa_ref[h] = jnp.concatenate([qh, bias, bias_pad], axis=0)
        kd = k_ref[pl.ds(own, MOBA_BLOCK), slab(pair)]
        ss.append(jnp.where(causal, _dot(kd, qh), -jnp.inf))
        acc_ref[h] = jnp.zeros(acc_ref.shape[1:], F32)
    m0 = jnp.full((1, MOBA_BLOCK), MASK_BIAS, F32)
    ms = softmax_pv(ss, (m0,) * n_heads, i)

    def body(j, ms):
        off = pl.multiple_of(j * MOBA_BLOCK, MOBA_BLOCK)
        onehot = jnp.where(lane == j, 1.0, 0.0).astype(BF16)
        ss = []
        for pair in range(ATTN_PAIRS):
            ka = jnp.concatenate([k_ref[pl.ds(off, MOBA_BLOCK), slab(pair)], onehot], axis=1)
            ss += [_dot(ka, qa_ref[2 * pair]), _dot(ka, qa_ref[2 * pair + 1])]
        return softmax_pv(ss, ms, j)

    lax.fori_loop(0, i, body, ms)
    for pair in range(ATTN_PAIRS):
        a0 = acc_ref[2 * pair]
        a1 = acc_ref[2 * pair + 1]
        ot = jnp.where(head0, a0[:LANES] / a0[LANES:LANES + 1], a1[:LANES] / a1[LANES:LANES + 1])
        o_ref[:, slab(pair)] = ot.T.astype(BF16)


def _attn(qt, kb, vt, km):
    b, s, _ = kb.shape
    nb = s // MOBA_BLOCK
    width = ATTN_PAIRS * LANES
    return pl.pallas_call(
        _attn_kernel,
        out_shape=jax.ShapeDtypeStruct((b, s, ATTN_WIDTH), BF16),
        grid=(b, ATTN_WIDTH // width, nb),
        in_specs=[
            pl.BlockSpec((None, width, MOBA_BLOCK), lambda bi, p, i: (bi * nb + i, p, 0)),
            pl.BlockSpec((None, s, width), lambda bi, p, i: (bi, 0, p)),
            pl.BlockSpec((None, nb, width, MOBA_BLOCK), lambda bi, p, i: (bi, 0, p, 0)),
            pl.BlockSpec((None, nb, width), lambda bi, p, i: (bi, 0, p)),
        ],
        out_specs=pl.BlockSpec((None, MOBA_BLOCK, width), lambda bi, p, i: (bi, i, p)),
        scratch_shapes=[
            pltpu.VMEM((2 * ATTN_PAIRS, LANES + BIAS_ROWS, MOBA_BLOCK), BF16),
            pltpu.VMEM((2 * ATTN_PAIRS, LANES + SUM_ROWS, MOBA_BLOCK), F32),
        ],
        compiler_params=pltpu.CompilerParams(
            dimension_semantics=("arbitrary", "arbitrary", "arbitrary"),
            vmem_limit_bytes=VMEM_LIMIT),
        name="moba_attn",
    )(qt, kb, vt, km)


def _pool_group(window_sum, u_g, count, w_pool_g, scale_g, precision=None, cast=None):
    d = window_sum / count - u_g
    if cast is not None:
        d = d.astype(cast)
    return _dot(d, w_pool_g, precision) * scale_g


def _mix_kernel(x_ref, attn_ref, u_ref, halo_ref, wg_ref, bg_ref, wpool_ref, pscale_ref, wa_ref,
                wb_ref, wout_ref, g1_ref, b1_ref, x1_ref, z_ref, *, tm, alpha):
    i = pl.program_id(1)
    x = x_ref[...]
    xb = x.astype(BF16)
    u = u_ref[...]
    z_ref[0:HALO_ROWS, :] = jnp.where(i > 0, halo_ref[...], 0.0)
    z_ref[HALO_ROWS:, :] = u
    pos1 = (i * tm + 1 + lax.broadcasted_iota(jnp.int32, (tm, 1), 0)).astype(F32)
    parts = []
    for g, w in enumerate(POOL_WINDOWS):
        sl = slice(g * POOL_GROUP_WIDTH, (g + 1) * POOL_GROUP_WIDTH)
        acc = u[:, sl]
        for back in range(1, w):
            acc = acc + z_ref[HALO_ROWS - back:HALO_ROWS - back + tm, sl]
        count = jnp.minimum(float(w), pos1)
        parts.append(_pool_group(acc, u[:, sl], count, wpool_ref[g], pscale_ref[:, sl], cast=BF16))
    pooled = jnp.concatenate(parts, axis=1).astype(BF16)
    d = x.shape[1]
    ga = _dot(xb, wg_ref[:, :d]) + bg_ref[0:1, :]
    gb = _dot(xb, wg_ref[:, d:]) + bg_ref[1:2, :]
    a = _dot(attn_ref[...], wa_ref[...])
    bb = _dot(pooled, wb_ref[...])
    merged = _sigmoid(ga) * a + _sigmoid(gb) * bb
    y = alpha * x + _dot(merged.astype(BF16), wout_ref[...])
    x1_ref[...] = _layer_norm(y, g1_ref[...], b1_ref[...])


def _mix(x, attn, u, w_gates, b_gate, w_pool, pool_scale, w_a, w_b, w_out, ln_g, ln_b, tm, alpha):
    b, s, d = x.shape
    halo_per_tile = tm // HALO_ROWS
    tile = lambda bi, i: (bi, i, 0)
    const2 = lambda bi, i: (0, 0)
    return pl.pallas_call(
        functools.partial(_mix_kernel, tm=tm, alpha=alpha),
        out_shape=jax.ShapeDtypeStruct((b, s, d), F32),
        grid=(b, s // tm),
        in_specs=[
            pl.BlockSpec((None, tm, d), tile),
            pl.BlockSpec((None, tm, ATTN_WIDTH), tile),
            pl.BlockSpec((None, tm, POOL_WIDTH), tile),
            pl.BlockSpec((None, HALO_ROWS, POOL_WIDTH),
                         lambda bi, i: (bi, jnp.maximum(i * halo_per_tile - 1, 0), 0)),
            pl.BlockSpec(w_gates.shape, const2),
            pl.BlockSpec(b_gate.shape, const2),
            pl.BlockSpec(w_pool.shape, lambda bi, i: (0, 0, 0)),
            pl.BlockSpec(pool_scale.shape, const2),
            pl.BlockSpec(w_a.shape, const2),
            pl.BlockSpec(w_b.shape, const2),
            pl.BlockSpec(w_out.shape, const2),
            pl.BlockSpec(ln_g.shape, const2),
            pl.BlockSpec(ln_b.shape, const2),
        ],
        out_specs=pl.BlockSpec((None, tm, d), tile),
        scratch_shapes=[pltpu.VMEM((HALO_ROWS + tm, POOL_WIDTH), F32)],
        compiler_params=pltpu.CompilerParams(
            dimension_semantics=("arbitrary", "arbitrary"), vmem_limit_bytes=VMEM_LIMIT),
        name="mix",
    )(x, attn, u, u, w_gates, b_gate, w_pool, pool_scale, w_a, w_b, w_out, ln_g, ln_b)


def _moe_kernel(x1_ref, wr_ref, br_ref, wg_ref, wu_ref, wd_ref, g2_ref, b2_ref, y_ref,
                xb_ref, comb_ref, acc_ref, *, alpha):
    e = pl.program_id(1)
    tm = x1_ref.shape[0]
    lane = lax.broadcasted_iota(jnp.int32, (tm, LANES), 1)

    @pl.when(e == 0)
    def _():
        x1 = x1_ref[...]
        xb_ref[...] = x1.astype(BF16)
        logits = _dot(x1, wr_ref[...], HIGHEST) + br_ref[...]
        comb_ref[...] = _route(logits)
        acc_ref[...] = jnp.zeros_like(acc_ref)

    xb = xb_ref[...]
    hg = _dot(xb, wg_ref[...])
    hu = _dot(xb, wu_ref[...])
    c_e = jnp.sum(jnp.where(lane == e, comb_ref[...], 0.0), axis=1, keepdims=True)
    h = hg * _sigmoid(hg) * hu * c_e
    acc_ref[...] += _dot(h.astype(BF16), wd_ref[...])

    @pl.when(e == N_EXPERTS - 1)
    def _():
        y = alpha * x1_ref[...] + acc_ref[...]
        y_ref[...] = _layer_norm(y, g2_ref[...], b2_ref[...])


def _moe(x1, w_r, b_r, w_g, w_u, w_d, ln_g, ln_b, tm, alpha):
    n, d = x1.shape
    f = w_g.shape[2]
    tile = lambda i, e: (i, 0)
    const2 = lambda i, e: (0, 0)
    return pl.pallas_call(
        functools.partial(_moe_kernel, alpha=alpha),
        out_shape=jax.ShapeDtypeStruct((n, d), F32),
        grid=(n // tm, N_EXPERTS),
        in_specs=[
            pl.BlockSpec((tm, d), tile),
            pl.BlockSpec(w_r.shape, const2),
            pl.BlockSpec(b_r.shape, const2),
            pl.BlockSpec((None, d, f), lambda i, e: (e, 0, 0)),
            pl.BlockSpec((None, d, f), lambda i, e: (e, 0, 0)),
            pl.BlockSpec((None, f, d), lambda i, e: (e, 0, 0)),
            pl.BlockSpec(ln_g.shape, const2),
            pl.BlockSpec(ln_b.shape, const2),
        ],
        out_specs=pl.BlockSpec((tm, d), tile),
        scratch_shapes=[
            pltpu.VMEM((tm, d), BF16),
            pltpu.VMEM((tm, LANES), F32),
            pltpu.VMEM((tm, d), F32),
        ],
        compiler_params=pltpu.CompilerParams(
            dimension_semantics=("arbitrary", "arbitrary"), vmem_limit_bytes=VMEM_LIMIT),
        name="moe",
    )(x1, w_r, b_r, w_g, w_u, w_d, ln_g, ln_b)


def _s_proj_kernel(x_ref, w_ref, cos_ref, sin_ref, h_ref):
    c = pl.program_id(0)
    h = _dot(x_ref[...], w_ref[...], HIGHEST)
    rows = h.shape[0]
    lane = lax.broadcasted_iota(jnp.int32, (rows, LANES), 1)
    first_half = (lane & 32) == 0
    rotary = c < 2
    for j in range(h.shape[1] // LANES):
        sl = slice(j * LANES, (j + 1) * LANES)
        hc = h[:, sl]
        h_ref[:, sl] = jnp.where(rotary, _rope_chunk(hc, cos_ref[...], sin_ref[...], first_half), hc)


def _s_proj(x, w_in, cos, sin):
    rows, d = x.shape
    width = w_in.shape[1]
    chunk = ATTN_WIDTH
    return pl.pallas_call(
        _s_proj_kernel,
        out_shape=jax.ShapeDtypeStruct((rows, width), F32),
        grid=(width // chunk,),
        in_specs=[
            pl.BlockSpec((rows, d), lambda c: (0, 0)),
            pl.BlockSpec((d, chunk), lambda c: (0, c)),
            pl.BlockSpec((1, LANES), lambda c: (0, 0)),
            pl.BlockSpec((1, LANES), lambda c: (0, 0)),
        ],
        out_specs=pl.BlockSpec((rows, chunk), lambda c: (0, c)),
        compiler_params=pltpu.CompilerParams(
            dimension_semantics=("arbitrary",), vmem_limit_bytes=VMEM_LIMIT),
        name="s_proj",
    )(x, w_in, cos, sin)


PAGES_PER_STEP = 8


def _s_means_kernel(pt_ref, q_ref, *refs, page_size, n_blocks):
    page_refs = refs[:PAGES_PER_STEP]
    sel_ref = refs[PAGES_PER_STEP]
    km_ref = refs[PAGES_PER_STEP + 1]
    c = pl.program_id(1)
    pages_per_block = MOBA_BLOCK // page_size
    blocks_per_step = PAGES_PER_STEP // pages_per_block
    for r in range(blocks_per_step):
        tot = jnp.zeros((N_HEADS, HEAD_DIM), F32)
        for pp in range(pages_per_block):
            tot = tot + jnp.sum(page_refs[r * pages_per_block + pp][...], axis=0)
        km_ref[c * blocks_per_step + r] = tot * (1.0 / MOBA_BLOCK)

    @pl.when(c == pl.num_programs(1) - 1)
    def _():
        g = jnp.sum(km_ref[...] * q_ref[...], axis=2, keepdims=True)
        blk_f = lax.broadcasted_iota(jnp.int32, g.shape, 0).astype(F32)
        for t in range(MOBA_TOPK):
            mx = jnp.max(g, axis=0, keepdims=True)
            idx = jnp.min(jnp.where(g == mx, blk_f, float(n_blocks)), axis=0, keepdims=True)
            sel_ref[t:t + 1] = idx.astype(jnp.int32)
            g = jnp.where(blk_f == idx, -jnp.inf, g)


def _s_means(page_table, q3, cache_k_pages, n_blocks):
    db, n_pages = page_table.shape
    page_size = cache_k_pages.shape[1]
    page_block = (None, page_size, N_HEADS, HEAD_DIM)

    def page_spec(r):
        return pl.BlockSpec(page_block, lambda b, c, pt: (pt[b, c * PAGES_PER_STEP + r], 0, 0, 0))

    return pl.pallas_call(
        functools.partial(_s_means_kernel, page_size=page_size, n_blocks=n_blocks),
        out_shape=jax.ShapeDtypeStruct((db, MOBA_TOPK, N_HEADS, 1), jnp.int32),
        grid_spec=pltpu.PrefetchScalarGridSpec(
            num_scalar_prefetch=1,
            grid=(db, n_pages // PAGES_PER_STEP),
            in_specs=[pl.BlockSpec((None, N_HEADS, HEAD_DIM), lambda b, c, pt: (b, 0, 0))]
            + [page_spec(r) for r in range(PAGES_PER_STEP)],
            out_specs=pl.BlockSpec((None, MOBA_TOPK, N_HEADS, 1), lambda b, c, pt: (b, 0, 0, 0)),
            scratch_shapes=[pltpu.VMEM((n_blocks, N_HEADS, HEAD_DIM), F32)],
        ),
        compiler_params=pltpu.CompilerParams(
            dimension_semantics=("arbitrary", "arbitrary"), vmem_limit_bytes=VMEM_LIMIT),
        name="s_means",
    )(page_table, q3, *([cache_k_pages] * PAGES_PER_STEP))


def _s_attn_kernel(pt_ref, sel_ref, q_ref, kn_ref, vn_ref, ck_ref, cv_ref, o_ref, kbuf, vbuf, sem,
                   *, pages_per_block):
    b = pl.program_id(0)
    slices_per_seq = N_HEADS * MOBA_TOPK * pages_per_block

    def slot_index(slot, h, t, r):
        return slot * slices_per_seq + (h * MOBA_TOPK + t) * pages_per_block + r

    def copies(seq, slot):
        out = []
        for h in range(N_HEADS):
            for t in range(MOBA_TOPK):
                blk = sel_ref[seq, t * N_HEADS + h]
                for r in range(pages_per_block):
                    page = pt_ref[seq, blk * pages_per_block + r]
                    dst = slot_index(slot, h, t, r)
                    out.append(pltpu.make_async_copy(ck_ref.at[page, :, h, :], kbuf.at[dst],
                                                     sem.at[slot]))
                    out.append(pltpu.make_async_copy(cv_ref.at[page, :, h, :], vbuf.at[dst],
                                                     sem.at[slot]))
        return out

    @pl.when(b == 0)
    def _():
        for cp in copies(0, 0):
            cp.start()

    @pl.when(b + 1 < pl.num_programs(0))
    def _():
        for cp in copies(b + 1, (b + 1) % 2):
            cp.start()

    slot = b % 2
    for cp in copies(b, slot):
        cp.wait()

    scale = HEAD_DIM ** -0.5
    for h in range(N_HEADS):
        qh = q_ref[h:h + 1, :] * scale
        s_self = jnp.sum(qh * kn_ref[h:h + 1, :], axis=1, keepdims=True)
        scores = []
        mx = s_self
        for t in range(MOBA_TOPK):
            for r in range(pages_per_block):
                sc = jnp.sum(kbuf[slot_index(slot, h, t, r)] * qh, axis=1, keepdims=True)
                scores.append(sc)
                mx = jnp.maximum(mx, jnp.max(sc, axis=0, keepdims=True))
        p_self = jnp.exp(s_self - mx)
        den = p_self
        acc = p_self * vn_ref[h:h + 1, :]
        n = 0
        for t in range(MOBA_TOPK):
            for r in range(pages_per_block):
                p = jnp.exp(scores[n] - mx)
                n += 1
                den = den + jnp.sum(p, axis=0, keepdims=True)
                acc = acc + jnp.sum(p * vbuf[slot_index(slot, h, t, r)], axis=0, keepdims=True)
        o_ref[h:h + 1, :] = acc / den


def _s_attn(page_table, sel, q3, kn3, vn3, cache_k_pages, cache_v_pages):
    db = page_table.shape[0]
    page_size = cache_k_pages.shape[1]
    pages_per_block = MOBA_BLOCK // page_size
    n_slices = 2 * N_HEADS * MOBA_TOPK * pages_per_block
    per_b = pl.BlockSpec((None, N_HEADS, HEAD_DIM), lambda b, pt, sl: (b, 0, 0))
    return pl.pallas_call(
        functools.partial(_s_attn_kernel, pages_per_block=pages_per_block),
        out_shape=jax.ShapeDtypeStruct((db, N_HEADS, HEAD_DIM), F32),
        grid_spec=pltpu.PrefetchScalarGridSpec(
            num_scalar_prefetch=2,
            grid=(db,),
            in_specs=[per_b, per_b, per_b, pl.BlockSpec(memory_space=pl.ANY),
                      pl.BlockSpec(memory_space=pl.ANY)],
            out_specs=per_b,
            scratch_shapes=[pltpu.VMEM((n_slices, page_size, HEAD_DIM), F32),
                            pltpu.VMEM((n_slices, page_size, HEAD_DIM), F32),
                            pltpu.SemaphoreType.DMA((2,))],
        ),
        compiler_params=pltpu.CompilerParams(
            dimension_semantics=("arbitrary",), vmem_limit_bytes=VMEM_LIMIT),
        name="s_attn",
    )(page_table, sel, q3, kn3, vn3, cache_k_pages, cache_v_pages)


def _s_tail_kernel(x_ref, attn_ref, u_ref, ga_ref, gb_ref, sp_ref, bg_ref, wpool_ref, pscale_ref,
                   wa_ref, wb_ref, wout_ref, g1_ref, b1_ref, wr_ref, br_ref, wg_ref, wu_ref,
                   wd_ref, g2_ref, b2_ref, y_ref, x1_ref, comb_ref, acc_ref, *, alpha):
    e = pl.program_id(0)
    rows = x_ref.shape[0]
    lane = lax.broadcasted_iota(jnp.int32, (rows, LANES), 1)

    @pl.when(e == 0)
    def _():
        u = u_ref[...]
        parts = []
        for g, w in enumerate(POOL_WINDOWS):
            sl = slice(g * POOL_GROUP_WIDTH, (g + 1) * POOL_GROUP_WIDTH)
            acc = u[:, sl]
            for back in range(1, w):
                acc = acc + sp_ref[POOL_STATE - back][:, sl]
            parts.append(_pool_group(acc, u[:, sl], float(w), wpool_ref[g], pscale_ref[:, sl],
                                     precision=HIGHEST))
        pooled = jnp.concatenate(parts, axis=1)
        a = _dot(attn_ref[...], wa_ref[...], HIGHEST)
        bb = _dot(pooled, wb_ref[...], HIGHEST)
        merged = _sigmoid(ga_ref[...] + bg_ref[0:1, :]) * a + _sigmoid(gb_ref[...] + bg_ref[1:2, :]) * bb
        y = alpha * x_ref[...] + _dot(merged, wout_ref[...], HIGHEST)
        x1 = _layer_norm(y, g1_ref[...], b1_ref[...])
        x1_ref[...] = x1
        comb_ref[...] = _route(_dot(x1, wr_ref[...], HIGHEST) + br_ref[...])
        acc_ref[...] = jnp.zeros_like(acc_ref)

    x1 = x1_ref[...]
    hg = _dot(x1, wg_ref[...], HIGHEST)
    hu = _dot(x1, wu_ref[...], HIGHEST)
    c_e = jnp.sum(jnp.where(lane == e, comb_ref[...], 0.0), axis=1, keepdims=True)
    h = hg * _sigmoid(hg) * hu * c_e
    acc_ref[...] += _dot(h, wd_ref[...], HIGHEST)

    @pl.when(e == N_EXPERTS - 1)
    def _():
        y_ref[...] = _layer_norm(alpha * x1_ref[...] + acc_ref[...], g2_ref[...], b2_ref[...])


def _s_tail(x, attn, u, ga, gb, sp_t, b_gate, w_pool, pool_scale, w_a, w_b, w_out, ln1_g, ln1_b,
            w_r, b_r, w_g, w_u, w_d, ln2_g, ln2_b, alpha):
    rows, d = x.shape
    f = w_g.shape[2]
    const2 = lambda e: (0, 0)
    const3 = lambda e: (0, 0, 0)
    full = lambda a: pl.BlockSpec(a.shape, const2 if a.ndim == 2 else const3)
    per_e = lambda blk: pl.BlockSpec((None,) + blk, lambda e: (e, 0, 0))
    small = [x, attn, u, ga, gb, sp_t, b_gate, w_pool, pool_scale, w_a, w_b, w_out, ln1_g, ln1_b,
             w_r, b_r]
    return pl.pallas_call(
        functools.partial(_s_tail_kernel, alpha=alpha),
        out_shape=jax.ShapeDtypeStruct((rows, d), F32),
        grid=(N_EXPERTS,),
        in_specs=[full(a) for a in small]
        + [per_e((d, f)), per_e((d, f)), per_e((f, d)), full(ln2_g), full(ln2_b)],
        out_specs=pl.BlockSpec((rows, d), const2),
        scratch_shapes=[pltpu.VMEM((rows, d), F32), pltpu.VMEM((rows, LANES), F32),
                        pltpu.VMEM((rows, d), F32)],
        compiler_params=pltpu.CompilerParams(
            dimension_semantics=("arbitrary",), vmem_limit_bytes=VMEM_LIMIT),
        name="s_tail",
    )(*small, w_g, w_u, w_d, ln2_g, ln2_b)


def _rope_tables(pos):
    inv_freq = 1.0 / (ROPE_THETA ** (jnp.arange(0, HEAD_DIM, 2, dtype=F32) / HEAD_DIM))
    ang = pos.astype(F32)[:, None] * inv_freq[None, :]
    cos = jnp.cos(ang)
    sin = jnp.sin(ang)
    cos_t = jnp.tile(cos, (1, LANES // (HEAD_DIM // 2)))
    sin_t = jnp.tile(jnp.concatenate([-sin, sin], axis=1), (1, LANES // HEAD_DIM))
    return cos_t, sin_t


def kernel(x_prompt, x_sample, cache_k, cache_v, state_pool, page_table, w_in, w_pool, pool_scale,
           w_branch_a, w_branch_b, b_gate, w_out, ln1_g, ln1_b, w_group_router, b_group_router,
           w_expert_router, b_expert_router, w_e_gate, w_e_up, w_e_down, ln2_g, ln2_b):
    depth = w_in.shape[0]
    assert depth == 1 and x_sample.shape[1] == 1
    alpha = (2 * depth) ** 0.25
    b, s, d = x_prompt.shape
    db = x_sample.shape[0]
    n_pages = page_table.shape[1]
    page_size = cache_k.shape[2]
    past_len = n_pages * page_size
    n_blocks = past_len // MOBA_BLOCK
    assert past_len % MOBA_BLOCK == 0 and s % MOBA_BLOCK == 0

    w_in0 = w_in[0]
    qkvu_cols = 3 * ATTN_WIDTH + POOL_WIDTH
    row2 = lambda a: a.reshape(1, -1)
    w_r = jnp.concatenate(
        [w_group_router[0], jnp.transpose(w_expert_router[0], (1, 0, 2)).reshape(d, N_EXPERTS),
         jnp.zeros((d, LANES - N_EXPERT_GROUPS - N_EXPERTS), F32)], axis=1)
    b_r = jnp.concatenate(
        [b_group_router[0], b_expert_router[0].reshape(-1),
         jnp.zeros((LANES - N_EXPERT_GROUPS - N_EXPERTS,), F32)]).reshape(1, LANES)

    cos_p, sin_p = _rope_tables(jnp.arange(s, dtype=jnp.int32))
    tm_a = 512
    qt, k, v, u, kb, vt, km = _qkvu(x_prompt.reshape(b * s, d), w_in0[:, :qkvu_cols].astype(BF16),
                                    cos_p, sin_p, s, tm_a)
    nb = s // MOBA_BLOCK
    attn = _attn(qt, kb.reshape(b, s, ATTN_WIDTH), vt.reshape(b, nb, ATTN_WIDTH, MOBA_BLOCK),
                 km.reshape(b, nb, ATTN_WIDTH))
    u3 = u.reshape(b, s, POOL_WIDTH)
    x1 = _mix(x_prompt, attn, u3, w_in0[:, qkvu_cols:].astype(BF16), b_gate[0],
              w_pool[0].astype(BF16), row2(pool_scale[0]), w_branch_a[0].astype(BF16),
              w_branch_b[0].astype(BF16), w_out[0].astype(BF16), row2(ln1_g[0]), row2(ln1_b[0]),
              256, alpha)
    y_p = _moe(x1.reshape(b * s, d), w_r, b_r, w_e_gate[0].astype(BF16), w_e_up[0].astype(BF16),
               w_e_down[0].astype(BF16), row2(ln2_g[0]), row2(ln2_b[0]), 1024, alpha)

    cos_s, sin_s = _rope_tables(jnp.full((1,), past_len, jnp.int32))
    x_s = x_sample.reshape(db, d)
    h_s = _s_proj(x_s, w_in0, cos_s, sin_s)
    q_s = h_s[:, :ATTN_WIDTH]
    k_s = h_s[:, ATTN_WIDTH:2 * ATTN_WIDTH]
    v_s = h_s[:, 2 * ATTN_WIDTH:3 * ATTN_WIDTH]
    u_s = h_s[:, 3 * ATTN_WIDTH:qkvu_cols]
    ga_s = h_s[:, qkvu_cols:qkvu_cols + d]
    gb_s = h_s[:, qkvu_cols + d:]
    heads = (N_HEADS, HEAD_DIM)
    q3 = q_s.reshape((db,) + heads)
    sel = _s_means(page_table, q3, cache_k[0], n_blocks)
    sel2 = sel.reshape(db, MOBA_TOPK * N_HEADS)
    attn_s = _s_attn(page_table, sel2, q3, k_s.reshape((db,) + heads), v_s.reshape((db,) + heads),
                     cache_k[0], cache_v[0]).reshape(db, ATTN_WIDTH)
    sp = state_pool[0]
    y_s = _s_tail(x_s, attn_s, u_s, ga_s, gb_s, jnp.transpose(sp, (1, 0, 2)), b_gate[0], w_pool[0],
                  row2(pool_scale[0]), w_branch_a[0], w_branch_b[0], w_out[0], row2(ln1_g[0]),
                  row2(ln1_b[0]), w_r, b_r, w_e_gate[0], w_e_up[0], w_e_down[0], row2(ln2_g[0]),
                  row2(ln2_b[0]), alpha)

    heads = (N_HEADS, HEAD_DIM)
    return (
        y_p.reshape(b, s, d),
        y_s.reshape(db, 1, d),
        k.reshape((1, b, s) + heads),
        v.reshape((1, b, s) + heads),
        u3[:, s - POOL_STATE:, :][None],
        k_s.reshape((1, db, 1) + heads),
        v_s.reshape((1, db, 1) + heads),
        jnp.concatenate([sp[:, 1:, :], u_s[:, None, :]], axis=1)[None],
    )
```

```python
import functools

import jax
import jax.numpy as jnp
from jax import lax
from jax.experimental import pallas as pl
from jax.experimental.pallas import tpu as pltpu

F32 = jnp.float32
BF16 = jnp.bfloat16
HIGHEST = lax.Precision.HIGHEST

N_HEADS = 8
HEAD_DIM = 64
ATTN_WIDTH = N_HEADS * HEAD_DIM
MOBA_BLOCK = 256
MOBA_TOPK = 3
ROPE_THETA = 10000.0
POOL_WINDOWS = (2, 4, 8, 16)
POOL_GROUP_WIDTH = 128
POOL_WIDTH = 512
POOL_STATE = 15
N_EXPERT_GROUPS = 4
EXPERTS_PER_GROUP = 4
N_EXPERTS = 16
LN_EPS = 1e-5
LANES = 128
HALO_ROWS = 16
MASK_BIAS = -1e30
VMEM_LIMIT = 56 * 1024 * 1024

NT_DIMS = (((1,), (1,)), ((), ()))


def _dot(a, b, precision=None):
    return jnp.dot(a, b, preferred_element_type=F32, precision=precision)


def _dot_nt(a, b, precision=None):
    return lax.dot_general(a, b, NT_DIMS, preferred_element_type=F32, precision=precision)


def _layer_norm(y, g, b):
    mu = jnp.mean(y, axis=-1, keepdims=True)
    var = jnp.mean(jnp.square(y - mu), axis=-1, keepdims=True)
    return (y - mu) * lax.rsqrt(var + LN_EPS) * g + b


def _sigmoid(x):
    return 1.0 / (1.0 + jnp.exp(-x))


def _rope_chunk(x, cos, sin_signed, first_half):
    partner = jnp.where(first_half, pltpu.roll(x, 96, 1), pltpu.roll(x, 32, 1))
    return x * cos + partner * sin_signed


def _route(logits):
    rows = logits.shape[0]
    lane = lax.broadcasted_iota(jnp.int32, (rows, LANES), 1)
    g = [logits[:, k:k + 1] for k in range(N_EXPERT_GROUPS)]
    gmax = jnp.maximum(jnp.maximum(g[0], g[1]), jnp.maximum(g[2], g[3]))
    den = sum(jnp.exp(gk - gmax) for gk in g)
    g_w = 1.0 / den
    is_g = []
    taken = jnp.zeros_like(gmax)
    for k in range(N_EXPERT_GROUPS):
        hit = jnp.where(g[k] == gmax, 1.0, 0.0) * (1.0 - taken)
        is_g.append(hit)
        taken = taken + hit
    e = []
    for k in range(EXPERTS_PER_GROUP):
        col = jnp.zeros_like(gmax)
        for gi in range(N_EXPERT_GROUPS):
            lane_id = N_EXPERT_GROUPS + gi * EXPERTS_PER_GROUP + k
            col = jnp.where(is_g[gi] > 0.5, logits[:, lane_id:lane_id + 1], col)
        e.append(col)
    v1 = jnp.maximum(jnp.maximum(e[0], e[1]), jnp.maximum(e[2], e[3]))
    first = []
    taken = jnp.zeros_like(v1)
    for k in range(EXPERTS_PER_GROUP):
        hit = jnp.where(e[k] == v1, 1.0, 0.0) * (1.0 - taken)
        first.append(hit)
        taken = taken + hit
    e2 = [jnp.where(first[k] > 0.5, -jnp.inf, e[k]) for k in range(EXPERTS_PER_GROUP)]
    v2 = jnp.maximum(jnp.maximum(e2[0], e2[1]), jnp.maximum(e2[2], e2[3]))
    second = []
    taken = jnp.zeros_like(v2)
    for k in range(EXPERTS_PER_GROUP):
        hit = jnp.where(e2[k] == v2, 1.0, 0.0) * (1.0 - taken)
        second.append(hit)
        taken = taken + hit
    t = jnp.exp(v2 - v1)
    w1 = 1.0 / (1.0 + t)
    w2 = t * w1
    comb = jnp.zeros((rows, LANES), F32)
    for gi in range(N_EXPERT_GROUPS):
        for k in range(EXPERTS_PER_GROUP):
            col = is_g[gi] * (first[k] * w1 + second[k] * w2) * g_w
            comb = jnp.where(lane == gi * EXPERTS_PER_GROUP + k, col, comb)
    return comb


Q_SCALE = 1.4426950408889634 * HEAD_DIM ** -0.5


def _qkvu_kernel(x_ref, w_ref, cos_ref, sin_ref, qt_ref, k_ref, v_ref, u_ref, kb_ref, vt_ref,
                 km_ref, *, tm):
    xb = x_ref[...].astype(BF16)
    cos = cos_ref[...]
    sin = sin_ref[...]
    lane = lax.broadcasted_iota(jnp.int32, (tm, LANES), 1)
    first_half = (lane & 32) == 0
    hq = _dot(xb, w_ref[:, 0:ATTN_WIDTH])
    hk = _dot(xb, w_ref[:, ATTN_WIDTH:2 * ATTN_WIDTH])
    hv = _dot(xb, w_ref[:, 2 * ATTN_WIDTH:3 * ATTN_WIDTH])
    for c in range(ATTN_WIDTH // LANES):
        sl = slice(c * LANES, (c + 1) * LANES)
        qc = _rope_chunk(hq[:, sl], cos, sin, first_half) * Q_SCALE
        kc = _rope_chunk(hk[:, sl], cos, sin, first_half)
        kb_ref[:, sl] = kc.astype(BF16)
        k_ref[sl, :] = kc.T
        vct = hv[:, sl].T
        v_ref[sl, :] = vct
        qct = qc.T.astype(BF16)
        for r in range(tm // MOBA_BLOCK):
            rows = slice(r * MOBA_BLOCK, (r + 1) * MOBA_BLOCK)
            km_ref[r:r + 1, sl] = jnp.sum(kc[rows], axis=0, keepdims=True) * (1.0 / MOBA_BLOCK)
            qt_ref[r, sl, :] = qct[:, rows]
            vt_ref[r, sl, :] = vct[:, rows].astype(BF16)
    u_ref[...] = _dot(xb, w_ref[:, 3 * ATTN_WIDTH:])


def _qkvu(x2d, w_qkvu, cos, sin, seq, tm):
    n, d = x2d.shape
    s_tiles = seq // tm
    nblk = tm // MOBA_BLOCK
    row = lambda i: (i, 0)
    blocked = lambda i: (i, 0, 0)
    featmajor = lambda i: (i // s_tiles, 0, i % s_tiles)
    out_shape = (
        jax.ShapeDtypeStruct((n // MOBA_BLOCK, ATTN_WIDTH, MOBA_BLOCK), BF16),
        jax.ShapeDtypeStruct((n // seq, ATTN_WIDTH, seq), F32),
        jax.ShapeDtypeStruct((n // seq, ATTN_WIDTH, seq), F32),
        jax.ShapeDtypeStruct((n, POOL_WIDTH), F32),
        jax.ShapeDtypeStruct((n, ATTN_WIDTH), BF16),
        jax.ShapeDtypeStruct((n // MOBA_BLOCK, ATTN_WIDTH, MOBA_BLOCK), BF16),
        jax.ShapeDtypeStruct((n // tm, nblk, ATTN_WIDTH), F32),
    )
    return pl.pallas_call(
        functools.partial(_qkvu_kernel, tm=tm),
        out_shape=out_shape,
        grid=(n // tm,),
        in_specs=[
            pl.BlockSpec((tm, d), row),
            pl.BlockSpec((d, 4 * ATTN_WIDTH), lambda i: (0, 0)),
            pl.BlockSpec((tm, LANES), lambda i: (i % s_tiles, 0)),
            pl.BlockSpec((tm, LANES), lambda i: (i % s_tiles, 0)),
        ],
        out_specs=(
            pl.BlockSpec((nblk, ATTN_WIDTH, MOBA_BLOCK), blocked),
            pl.BlockSpec((None, ATTN_WIDTH, tm), featmajor),
            pl.BlockSpec((None, ATTN_WIDTH, tm), featmajor),
            pl.BlockSpec((tm, POOL_WIDTH), row),
            pl.BlockSpec((tm, ATTN_WIDTH), row),
            pl.BlockSpec((nblk, ATTN_WIDTH, MOBA_BLOCK), blocked),
            pl.BlockSpec((None, nblk, ATTN_WIDTH), blocked),
        ),
        compiler_params=pltpu.CompilerParams(
            dimension_semantics=("arbitrary",), vmem_limit_bytes=VMEM_LIMIT),
        name="qkvu",
    )(x2d, w_qkvu, cos, sin)


BIAS_ROWS = 128
SUM_ROWS = 16


ATTN_PAIRS = 4


def _attn_kernel(qt_ref, k_ref, vt_ref, km_ref, o_ref, qa_ref, acc_ref):
    i = pl.program_id(2)
    n_past = km_ref.shape[0]
    n_heads = 2 * ATTN_PAIRS
    feat = lax.broadcasted_iota(jnp.int32, (LANES, MOBA_BLOCK), 0)
    head0 = feat < HEAD_DIM
    blk_id = lax.broadcasted_iota(jnp.int32, (n_past, MOBA_BLOCK), 0)
    blk_f = blk_id.astype(F32)
    past = blk_id < i
    key_id = lax.broadcasted_iota(jnp.int32, (MOBA_BLOCK, MOBA_BLOCK), 0)
    qry_id = lax.broadcasted_iota(jnp.int32, (MOBA_BLOCK, MOBA_BLOCK), 1)
    causal = key_id <= qry_id
    lane = lax.broadcasted_iota(jnp.int32, (MOBA_BLOCK, LANES), 1)
    ones_rows = jnp.ones((SUM_ROWS, MOBA_BLOCK), BF16)
    bias_pad = jnp.zeros((BIAS_ROWS - n_past, MOBA_BLOCK), BF16)
    own = pl.multiple_of(i * MOBA_BLOCK, MOBA_BLOCK)

    def slab(pair):
        return slice(pair * LANES, (pair + 1) * LANES)

    def vt_aug(blk, pair):
        return jnp.concatenate([vt_ref[blk, slab(pair), :], ones_rows], axis=0)

    def softmax_pv(ss, ms, blk):
        out = []
        for h in range(n_heads):
            m_new = jnp.maximum(ms[h], jnp.max(ss[h], axis=0, keepdims=True))
            alpha = jnp.exp2(ms[h] - m_new)
            p = jnp.exp2(ss[h] - m_new)
            acc_ref[h] = alpha * acc_ref[h] + _dot(vt_aug(blk, h // 2), p.astype(BF16))
            out.append(m_new)
        return tuple(out)

    ss = []
    for h in range(n_heads):
        pair = h // 2
        qt = qt_ref[slab(pair), :]
        qh = jnp.where(head0 if h % 2 == 0 else jnp.logical_not(head0), qt, jnp.zeros_like(qt))
        gate = _dot(km_ref[:, slab(pair)], qh.astype(F32), HIGHEST)
        g = jnp.where(past, gate, -jnp.inf)
        sel = jnp.zeros(g.shape, F32)
        for _ in range(MOBA_TOPK):
            mx = jnp.max(g, axis=0, keepdims=True)
            idx = jnp.min(jnp.where(g == mx, blk_f, float(n_past)), axis=0, keepdims=True)
            pick = blk_f == idx
            sel = jnp.where(pick, 1.0, sel)
            g = jnp.where(pick, -jnp.inf, g)
        bias = jnp.where(jnp.logical_and(sel > 0.5, past), 0.0, MASK_BIAS).astype(BF16)
        qa_ref[h] = jnp.concatenate([qh, bias, bias_pad], axis=0)
        kd = k_ref[pl.ds(own, MOBA_BLOCK), slab(pair)]
        ss.append(jnp.where(causal, _dot(kd, qh), -jnp.inf))
        acc_ref[h] = jnp.zeros(acc_ref.shape[1:], F32)
    m0 = jnp.full((1, MOBA_BLOCK), MASK_BIAS, F32)
    ms = softmax_pv(ss, (m0,) * n_heads, i)

    def body(j, ms):
        off = pl.multiple_of(j * MOBA_BLOCK, MOBA_BLOCK)
        onehot = jnp.where(lane == j, 1.0, 0.0).astype(BF16)
        ss = []
        for pair in range(ATTN_PAIRS):
            ka = jnp.concatenate([k_ref[pl.ds(off, MOBA_BLOCK), slab(pair)], onehot], axis=1)
            ss += [_dot(ka, qa_ref[2 * pair]), _dot(ka, qa_ref[2 * pair + 1])]
        return softmax_pv(ss, ms, j)

    lax.fori_loop(0, i, body, ms)
    for pair in range(ATTN_PAIRS):
        a0 = acc_ref[2 * pair]
        a1 = acc_ref[2 * pair + 1]
        ot = jnp.where(head0, a0[:LANES] / a0[LANES:LANES + 1], a1[:LANES] / a1[LANES:LANES + 1])
        o_ref[:, slab(pair)] = ot.T.astype(BF16)


def _attn(qt, kb, vt, km):
    b, s, _ = kb.shape
    nb = s // MOBA_BLOCK
    width = ATTN_PAIRS * LANES
    return pl.pallas_call(
        _attn_kernel,
        out_shape=jax.ShapeDtypeStruct((b, s, ATTN_WIDTH), BF16),
        grid=(b, ATTN_WIDTH // width, nb),
        in_specs=[
            pl.BlockSpec((None, width, MOBA_BLOCK), lambda bi, p, i: (bi * nb + i, p, 0)),
            pl.BlockSpec((None, s, width), lambda bi, p, i: (bi, 0, p)),
            pl.BlockSpec((None, nb, width, MOBA_BLOCK), lambda bi, p, i: (bi, 0, p, 0)),
            pl.BlockSpec((None, nb, width), lambda bi, p, i: (bi, 0, p)),
        ],
        out_specs=pl.BlockSpec((None, MOBA_BLOCK, width), lambda bi, p, i: (bi, i, p)),
        scratch_shapes=[
            pltpu.VMEM((2 * ATTN_PAIRS, LANES + BIAS_ROWS, MOBA_BLOCK), BF16),
            pltpu.VMEM((2 * ATTN_PAIRS, LANES + SUM_ROWS, MOBA_BLOCK), F32),
        ],
        compiler_params=pltpu.CompilerParams(
            dimension_semantics=("arbitrary", "arbitrary", "arbitrary"),
            vmem_limit_bytes=VMEM_LIMIT),
        name="moba_attn",
    )(qt, kb, vt, km)


def _pool_group(window_sum, u_g, count, w_pool_g, scale_g, precision=None, cast=None):
    d = window_sum / count - u_g
    if cast is not None:
        d = d.astype(cast)
    return _dot(d, w_pool_g, precision) * scale_g


def _mix_kernel(x_ref, attn_ref, u_ref, halo_ref, wg_ref, bg_ref, wpool_ref, pscale_ref, wa_ref,
                wb_ref, wout_ref, g1_ref, b1_ref, x1_ref, z_ref, *, tm, alpha):
    i = pl.program_id(1)
    x = x_ref[...]
    xb = x.astype(BF16)
    u = u_ref[...]
    z_ref[0:HALO_ROWS, :] = jnp.where(i > 0, halo_ref[...], 0.0)
    z_ref[HALO_ROWS:, :] = u
    pos1 = (i * tm + 1 + lax.broadcasted_iota(jnp.int32, (tm, 1), 0)).astype(F32)
    parts = []
    for g, w in enumerate(POOL_WINDOWS):
        sl = slice(g * POOL_GROUP_WIDTH, (g + 1) * POOL_GROUP_WIDTH)
        acc = u[:, sl]
        for back in range(1, w):
            acc = acc + z_ref[HALO_ROWS - back:HALO_ROWS - back + tm, sl]
        count = jnp.minimum(float(w), pos1)
        parts.append(_pool_group(acc, u[:, sl], count, wpool_ref[g], pscale_ref[:, sl], cast=BF16))
    pooled = jnp.concatenate(parts, axis=1).astype(BF16)
    d = x.shape[1]
    ga = _dot(xb, wg_ref[:, :d]) + bg_ref[0:1, :]
    gb = _dot(xb, wg_ref[:, d:]) + bg_ref[1:2, :]
    a = _dot(attn_ref[...], wa_ref[...])
    bb = _dot(pooled, wb_ref[...])
    merged = _sigmoid(ga) * a + _sigmoid(gb) * bb
    y = alpha * x + _dot(merged.astype(BF16), wout_ref[...])
    x1_ref[...] = _layer_norm(y, g1_ref[...], b1_ref[...])


def _mix(x, attn, u, w_gates, b_gate, w_pool, pool_scale, w_a, w_b, w_out, ln_g, ln_b, tm, alpha):
    b, s, d = x.shape
    halo_per_tile = tm // HALO_ROWS
    tile = lambda bi, i: (bi, i, 0)
    const2 = lambda bi, i: (0, 0)
    return pl.pallas_call(
        functools.partial(_mix_kernel, tm=tm, alpha=alpha),
        out_shape=jax.ShapeDtypeStruct((b, s, d), F32),
        grid=(b, s // tm),
        in_specs=[
            pl.BlockSpec((None, tm, d), tile),
            pl.BlockSpec((None, tm, ATTN_WIDTH), tile),
            pl.BlockSpec((None, tm, POOL_WIDTH), tile),
            pl.BlockSpec((None, HALO_ROWS, POOL_WIDTH),
                         lambda bi, i: (bi, jnp.maximum(i * halo_per_tile - 1, 0), 0)),
            pl.BlockSpec(w_gates.shape, const2),
            pl.BlockSpec(b_gate.shape, const2),
            pl.BlockSpec(w_pool.shape, lambda bi, i: (0, 0, 0)),
            pl.BlockSpec(pool_scale.shape, const2),
            pl.BlockSpec(w_a.shape, const2),
            pl.BlockSpec(w_b.shape, const2),
            pl.BlockSpec(w_out.shape, const2),
            pl.BlockSpec(ln_g.shape, const2),
            pl.BlockSpec(ln_b.shape, const2),
        ],
        out_specs=pl.BlockSpec((None, tm, d), tile),
        scratch_shapes=[pltpu.VMEM((HALO_ROWS + tm, POOL_WIDTH), F32)],
        compiler_params=pltpu.CompilerParams(
            dimension_semantics=("arbitrary", "arbitrary"), vmem_limit_bytes=VMEM_LIMIT),
        name="mix",
    )(x, attn, u, u, w_gates, b_gate, w_pool, pool_scale, w_a, w_b, w_out, ln_g, ln_b)


def _moe_kernel(x1_ref, wr_ref, br_ref, wg_ref, wu_ref, wd_ref, g2_ref, b2_ref, y_ref,
                xb_ref, comb_ref, acc_ref, *, alpha):
    e = pl.program_id(1)
    tm = x1_ref.shape[0]
    lane = lax.broadcasted_iota(jnp.int32, (tm, LANES), 1)

    @pl.when(e == 0)
    def _():
        x1 = x1_ref[...]
        xb_ref[...] = x1.astype(BF16)
        logits = _dot(x1, wr_ref[...], HIGHEST) + br_ref[...]
        comb_ref[...] = _route(logits)
        acc_ref[...] = jnp.zeros_like(acc_ref)

    xb = xb_ref[...]
    hg = _dot(xb, wg_ref[...])
    hu = _dot(xb, wu_ref[...])
    c_e = jnp.sum(jnp.where(lane == e, comb_ref[...], 0.0), axis=1, keepdims=True)
    h = hg * _sigmoid(hg) * hu * c_e
    acc_ref[...] += _dot(h.astype(BF16), wd_ref[...])

    @pl.when(e == N_EXPERTS - 1)
    def _():
        y = alpha * x1_ref[...] + acc_ref[...]
        y_ref[...] = _layer_norm(y, g2_ref[...], b2_ref[...])


def _moe(x1, w_r, b_r, w_g, w_u, w_d, ln_g, ln_b, tm, alpha):
    n, d = x1.shape
    f = w_g.shape[2]
    tile = lambda i, e: (i, 0)
    const2 = lambda i, e: (0, 0)
    return pl.pallas_call(
        functools.partial(_moe_kernel, alpha=alpha),
        out_shape=jax.ShapeDtypeStruct((n, d), F32),
        grid=(n // tm, N_EXPERTS),
        in_specs=[
            pl.BlockSpec((tm, d), tile),
            pl.BlockSpec(w_r.shape, const2),
            pl.BlockSpec(b_r.shape, const2),
            pl.BlockSpec((None, d, f), lambda i, e: (e, 0, 0)),
            pl.BlockSpec((None, d, f), lambda i, e: (e, 0, 0)),
            pl.BlockSpec((None, f, d), lambda i, e: (e, 0, 0)),
            pl.BlockSpec(ln_g.shape, const2),
            pl.BlockSpec(ln_b.shape, const2),
        ],
        out_specs=pl.BlockSpec((tm, d), tile),
        scratch_shapes=[
            pltpu.VMEM((tm, d), BF16),
            pltpu.VMEM((tm, LANES), F32),
            pltpu.VMEM((tm, d), F32),
        ],
        compiler_params=pltpu.CompilerParams(
            dimension_semantics=("arbitrary", "arbitrary"), vmem_limit_bytes=VMEM_LIMIT),
        name="moe",
    )(x1, w_r, b_r, w_g, w_u, w_d, ln_g, ln_b)


def _s_proj_kernel(x_ref, w_ref, cos_ref, sin_ref, h_ref):
    c = pl.program_id(0)
    h = _dot(x_ref[...], w_ref[...], HIGHEST)
    rows = h.shape[0]
    lane = lax.broadcasted_iota(jnp.int32, (rows, LANES), 1)
    first_half = (lane & 32) == 0
    rotary = c < 2
    for j in range(h.shape[1] // LANES):
        sl = slice(j * LANES, (j + 1) * LANES)
        hc = h[:, sl]
        h_ref[:, sl] = jnp.where(rotary, _rope_chunk(hc, cos_ref[...], sin_ref[...], first_half), hc)


def _s_proj(x, w_in, cos, sin):
    rows, d = x.shape
    width = w_in.shape[1]
    chunk = ATTN_WIDTH
    return pl.pallas_call(
        _s_proj_kernel,
        out_shape=jax.ShapeDtypeStruct((rows, width), F32),
        grid=(width // chunk,),
        in_specs=[
            pl.BlockSpec((rows, d), lambda c: (0, 0)),
            pl.BlockSpec((d, chunk), lambda c: (0, c)),
            pl.BlockSpec((1, LANES), lambda c: (0, 0)),
            pl.BlockSpec((1, LANES), lambda c: (0, 0)),
        ],
        out_specs=pl.BlockSpec((rows, chunk), lambda c: (0, c)),
        compiler_params=pltpu.CompilerParams(
            dimension_semantics=("arbitrary",), vmem_limit_bytes=VMEM_LIMIT),
        name="s_proj",
    )(x, w_in, cos, sin)


PAGES_PER_STEP = 8


def _s_scores_kernel(pt_ref, q_ref, *refs, page_size, n_blocks):
    page_refs = refs[:PAGES_PER_STEP]
    sc_ref, sel_ref, gs_ref = refs[PAGES_PER_STEP:]
    c = pl.program_id(1)
    pages_per_block = MOBA_BLOCK // page_size
    blocks_per_step = PAGES_PER_STEP // pages_per_block
    scale = HEAD_DIM ** -0.5
    for h in range(N_HEADS):
        qh = q_ref[h] * scale
        for r in range(blocks_per_step):
            tot = jnp.zeros((1, page_size), F32)
            for pp in range(pages_per_block):
                p = r * pages_per_block + pp
                s = jnp.sum(page_refs[p][h] * qh, axis=0, keepdims=True)
                sc_ref[p, h:h + 1, :] = s
                tot = tot + s
            gs_ref[c * blocks_per_step + r, h:h + 1, :] = jnp.sum(tot, axis=1, keepdims=True)

    @pl.when(c == pl.num_programs(1) - 1)
    def _():
        g = gs_ref[...]
        blk_f = lax.broadcasted_iota(jnp.int32, g.shape, 0).astype(F32)
        for t in range(MOBA_TOPK):
            mx = jnp.max(g, axis=0, keepdims=True)
            idx = jnp.min(jnp.where(g == mx, blk_f, float(n_blocks)), axis=0, keepdims=True)
            sel_ref[t:t + 1] = idx.astype(jnp.int32)
            g = jnp.where(blk_f == idx, -jnp.inf, g)


def _s_scores(page_table, q4, cache_kt, n_blocks):
    db, n_pages = page_table.shape
    page_size = cache_kt.shape[3]
    page_block = (None, N_HEADS, HEAD_DIM, page_size)

    def page_spec(r):
        return pl.BlockSpec(page_block, lambda b, c, pt: (pt[b, c * PAGES_PER_STEP + r], 0, 0, 0))

    return pl.pallas_call(
        functools.partial(_s_scores_kernel, page_size=page_size, n_blocks=n_blocks),
        out_shape=(jax.ShapeDtypeStruct((db, n_pages, N_HEADS, page_size), F32),
                   jax.ShapeDtypeStruct((db, MOBA_TOPK, N_HEADS, 1), jnp.int32)),
        grid_spec=pltpu.PrefetchScalarGridSpec(
            num_scalar_prefetch=1,
            grid=(db, n_pages // PAGES_PER_STEP),
            in_specs=[pl.BlockSpec((None, N_HEADS, HEAD_DIM, 1), lambda b, c, pt: (b, 0, 0, 0))]
            + [page_spec(r) for r in range(PAGES_PER_STEP)],
            out_specs=(pl.BlockSpec((None, PAGES_PER_STEP, N_HEADS, page_size),
                                    lambda b, c, pt: (b, c, 0, 0)),
                       pl.BlockSpec((None, MOBA_TOPK, N_HEADS, 1), lambda b, c, pt: (b, 0, 0, 0))),
            scratch_shapes=[pltpu.VMEM((n_blocks, N_HEADS, 1), F32)],
        ),
        compiler_params=pltpu.CompilerParams(
            dimension_semantics=("arbitrary", "arbitrary"), vmem_limit_bytes=VMEM_LIMIT),
        name="s_scores",
    )(page_table, q4, *([cache_kt] * PAGES_PER_STEP))


def _s_attn_kernel(pt_ref, sel_ref, sc_ref, q_ref, kn_ref, vn_ref, cv_ref, o_ref, vbuf, sem,
                   *, pages_per_block):
    b = pl.program_id(0)
    slices_per_seq = N_HEADS * MOBA_TOPK * pages_per_block

    def slot_index(slot, h, t, r):
        return slot * slices_per_seq + (h * MOBA_TOPK + t) * pages_per_block + r

    def copies(seq, slot):
        out = []
        for h in range(N_HEADS):
            for t in range(MOBA_TOPK):
                blk = sel_ref[seq, t * N_HEADS + h]
                for r in range(pages_per_block):
                    page = pt_ref[seq, blk * pages_per_block + r]
                    out.append(pltpu.make_async_copy(
                        cv_ref.at[page, h], vbuf.at[slot_index(slot, h, t, r)], sem.at[slot]))
        return out

    @pl.when(b == 0)
    def _():
        for cp in copies(0, 0):
            cp.start()

    @pl.when(b + 1 < pl.num_programs(0))
    def _():
        for cp in copies(b + 1, (b + 1) % 2):
            cp.start()

    slot = b % 2
    for cp in copies(b, slot):
        cp.wait()

    scale = HEAD_DIM ** -0.5
    for h in range(N_HEADS):
        s_self = jnp.sum(q_ref[h] * kn_ref[h], axis=0, keepdims=True) * scale
        scores = []
        mx = s_self
        for t in range(MOBA_TOPK):
            blk = sel_ref[b, t * N_HEADS + h]
            for r in range(pages_per_block):
                sc = sc_ref[blk * pages_per_block + r, h:h + 1, :]
                scores.append(sc)
                mx = jnp.maximum(mx, jnp.max(sc, axis=1, keepdims=True))
        p_self = jnp.exp(s_self - mx)
        den = p_self
        acc = p_self * vn_ref[h]
        n = 0
        for t in range(MOBA_TOPK):
            for r in range(pages_per_block):
                p = jnp.exp(scores[n] - mx)
                n += 1
                den = den + jnp.sum(p, axis=1, keepdims=True)
                acc = acc + jnp.sum(vbuf[slot_index(slot, h, t, r)] * p, axis=1, keepdims=True)
        o_ref[h] = acc / den


def _s_attn(page_table, sel, scores, q4, kn4, vn4, cache_vt):
    db, n_pages = page_table.shape
    page_size = cache_vt.shape[3]
    pages_per_block = MOBA_BLOCK // page_size
    n_slices = 2 * N_HEADS * MOBA_TOPK * pages_per_block
    per_b = pl.BlockSpec((None, N_HEADS, HEAD_DIM, 1), lambda b, pt, sl: (b, 0, 0, 0))
    return pl.pallas_call(
        functools.partial(_s_attn_kernel, pages_per_block=pages_per_block),
        out_shape=jax.ShapeDtypeStruct((db, N_HEADS, HEAD_DIM, 1), F32),
        grid_spec=pltpu.PrefetchScalarGridSpec(
            num_scalar_prefetch=2,
            grid=(db,),
            in_specs=[pl.BlockSpec((None, n_pages, N_HEADS, page_size),
                                   lambda b, pt, sl: (b, 0, 0, 0)),
                      per_b, per_b, per_b, pl.BlockSpec(memory_space=pl.ANY)],
            out_specs=per_b,
            scratch_shapes=[pltpu.VMEM((n_slices, HEAD_DIM, page_size), F32),
                            pltpu.SemaphoreType.DMA((2,))],
        ),
        compiler_params=pltpu.CompilerParams(
            dimension_semantics=("arbitrary",), vmem_limit_bytes=VMEM_LIMIT),
        name="s_attn",
    )(page_table, sel, scores, q4, kn4, vn4, cache_vt)


def _s_tail_kernel(x_ref, attn_ref, u_ref, ga_ref, gb_ref, sp_ref, bg_ref, wpool_ref, pscale_ref,
                   wa_ref, wb_ref, wout_ref, g1_ref, b1_ref, wr_ref, br_ref, wg_ref, wu_ref,
                   wd_ref, g2_ref, b2_ref, y_ref, x1_ref, comb_ref, acc_ref, *, alpha):
    e = pl.program_id(0)
    rows = x_ref.shape[0]
    lane = lax.broadcasted_iota(jnp.int32, (rows, LANES), 1)

    @pl.when(e == 0)
    def _():
        u = u_ref[...]
        parts = []
        for g, w in enumerate(POOL_WINDOWS):
            sl = slice(g * POOL_GROUP_WIDTH, (g + 1) * POOL_GROUP_WIDTH)
            acc = u[:, sl]
            for back in range(1, w):
                acc = acc + sp_ref[POOL_STATE - back][:, sl]
            parts.append(_pool_group(acc, u[:, sl], float(w), wpool_ref[g], pscale_ref[:, sl],
                                     precision=HIGHEST))
        pooled = jnp.concatenate(parts, axis=1)
        a = _dot(attn_ref[...], wa_ref[...], HIGHEST)
        bb = _dot(pooled, wb_ref[...], HIGHEST)
        merged = _sigmoid(ga_ref[...] + bg_ref[0:1, :]) * a + _sigmoid(gb_ref[...] + bg_ref[1:2, :]) * bb
        y = alpha * x_ref[...] + _dot(merged, wout_ref[...], HIGHEST)
        x1 = _layer_norm(y, g1_ref[...], b1_ref[...])
        x1_ref[...] = x1
        comb_ref[...] = _route(_dot(x1, wr_ref[...], HIGHEST) + br_ref[...])
        acc_ref[...] = jnp.zeros_like(acc_ref)

    x1 = x1_ref[...]
    hg = _dot(x1, wg_ref[...], HIGHEST)
    hu = _dot(x1, wu_ref[...], HIGHEST)
    c_e = jnp.sum(jnp.where(lane == e, comb_ref[...], 0.0), axis=1, keepdims=True)
    h = hg * _sigmoid(hg) * hu * c_e
    acc_ref[...] += _dot(h, wd_ref[...], HIGHEST)

    @pl.when(e == N_EXPERTS - 1)
    def _():
        y_ref[...] = _layer_norm(alpha * x1_ref[...] + acc_ref[...], g2_ref[...], b2_ref[...])


def _s_tail(x, attn, u, ga, gb, sp_t, b_gate, w_pool, pool_scale, w_a, w_b, w_out, ln1_g, ln1_b,
            w_r, b_r, w_g, w_u, w_d, ln2_g, ln2_b, alpha):
    rows, d = x.shape
    f = w_g.shape[2]
    const2 = lambda e: (0, 0)
    const3 = lambda e: (0, 0, 0)
    full = lambda a: pl.BlockSpec(a.shape, const2 if a.ndim == 2 else const3)
    per_e = lambda blk: pl.BlockSpec((None,) + blk, lambda e: (e, 0, 0))
    small = [x, attn, u, ga, gb, sp_t, b_gate, w_pool, pool_scale, w_a, w_b, w_out, ln1_g, ln1_b,
             w_r, b_r]
    return pl.pallas_call(
        functools.partial(_s_tail_kernel, alpha=alpha),
        out_shape=jax.ShapeDtypeStruct((rows, d), F32),
        grid=(N_EXPERTS,),
        in_specs=[full(a) for a in small]
        + [per_e((d, f)), per_e((d, f)), per_e((f, d)), full(ln2_g), full(ln2_b)],
        out_specs=pl.BlockSpec((rows, d), const2),
        scratch_shapes=[pltpu.VMEM((rows, d), F32), pltpu.VMEM((rows, LANES), F32),
                        pltpu.VMEM((rows, d), F32)],
        compiler_params=pltpu.CompilerParams(
            dimension_semantics=("arbitrary",), vmem_limit_bytes=VMEM_LIMIT),
        name="s_tail",
    )(*small, w_g, w_u, w_d, ln2_g, ln2_b)


def _rope_tables(pos):
    inv_freq = 1.0 / (ROPE_THETA ** (jnp.arange(0, HEAD_DIM, 2, dtype=F32) / HEAD_DIM))
    ang = pos.astype(F32)[:, None] * inv_freq[None, :]
    cos = jnp.cos(ang)
    sin = jnp.sin(ang)
    cos_t = jnp.tile(cos, (1, LANES // (HEAD_DIM // 2)))
    sin_t = jnp.tile(jnp.concatenate([-sin, sin], axis=1), (1, LANES // HEAD_DIM))
    return cos_t, sin_t


def kernel(x_prompt, x_sample, cache_k, cache_v, state_pool, page_table, w_in, w_pool, pool_scale,
           w_branch_a, w_branch_b, b_gate, w_out, ln1_g, ln1_b, w_group_router, b_group_router,
           w_expert_router, b_expert_router, w_e_gate, w_e_up, w_e_down, ln2_g, ln2_b):
    depth = w_in.shape[0]
    assert depth == 1 and x_sample.shape[1] == 1
    alpha = (2 * depth) ** 0.25
    b, s, d = x_prompt.shape
    db = x_sample.shape[0]
    n_pages = page_table.shape[1]
    page_size = cache_k.shape[2]
    past_len = n_pages * page_size
    n_blocks = past_len // MOBA_BLOCK
    assert past_len % MOBA_BLOCK == 0 and s % MOBA_BLOCK == 0

    w_in0 = w_in[0]
    qkvu_cols = 3 * ATTN_WIDTH + POOL_WIDTH
    row2 = lambda a: a.reshape(1, -1)
    w_r = jnp.concatenate(
        [w_group_router[0], jnp.transpose(w_expert_router[0], (1, 0, 2)).reshape(d, N_EXPERTS),
         jnp.zeros((d, LANES - N_EXPERT_GROUPS - N_EXPERTS), F32)], axis=1)
    b_r = jnp.concatenate(
        [b_group_router[0], b_expert_router[0].reshape(-1),
         jnp.zeros((LANES - N_EXPERT_GROUPS - N_EXPERTS,), F32)]).reshape(1, LANES)

    cos_p, sin_p = _rope_tables(jnp.arange(s, dtype=jnp.int32))
    tm_a = 512
    qt, k, v, u, kb, vt, km = _qkvu(x_prompt.reshape(b * s, d), w_in0[:, :qkvu_cols].astype(BF16),
                                    cos_p, sin_p, s, tm_a)
    nb = s // MOBA_BLOCK
    attn = _attn(qt, kb.reshape(b, s, ATTN_WIDTH), vt.reshape(b, nb, ATTN_WIDTH, MOBA_BLOCK),
                 km.reshape(b, nb, ATTN_WIDTH))
    u3 = u.reshape(b, s, POOL_WIDTH)
    x1 = _mix(x_prompt, attn, u3, w_in0[:, qkvu_cols:].astype(BF16), b_gate[0],
              w_pool[0].astype(BF16), row2(pool_scale[0]), w_branch_a[0].astype(BF16),
              w_branch_b[0].astype(BF16), w_out[0].astype(BF16), row2(ln1_g[0]), row2(ln1_b[0]),
              256, alpha)
    y_p = _moe(x1.reshape(b * s, d), w_r, b_r, w_e_gate[0].astype(BF16), w_e_up[0].astype(BF16),
               w_e_down[0].astype(BF16), row2(ln2_g[0]), row2(ln2_b[0]), 1024, alpha)

    cos_s, sin_s = _rope_tables(jnp.full((1,), past_len, jnp.int32))
    x_s = x_sample.reshape(db, d)
    h_s = _s_proj(x_s, w_in0, cos_s, sin_s)
    q_s = h_s[:, :ATTN_WIDTH]
    k_s = h_s[:, ATTN_WIDTH:2 * ATTN_WIDTH]
    v_s = h_s[:, 2 * ATTN_WIDTH:3 * ATTN_WIDTH]
    u_s = h_s[:, 3 * ATTN_WIDTH:qkvu_cols]
    ga_s = h_s[:, qkvu_cols:qkvu_cols + d]
    gb_s = h_s[:, qkvu_cols + d:]
    cache_kt = jnp.transpose(cache_k[0], (0, 2, 3, 1))
    cache_vt = jnp.transpose(cache_v[0], (0, 2, 3, 1))
    col4 = lambda a: a.reshape(db, N_HEADS, HEAD_DIM, 1)
    scores, sel = _s_scores(page_table, col4(q_s), cache_kt, n_blocks)
    sel2 = sel.reshape(db, MOBA_TOPK * N_HEADS)
    attn_s = _s_attn(page_table, sel2, scores, col4(q_s), col4(k_s), col4(v_s),
                     cache_vt).reshape(db, ATTN_WIDTH)
    sp = state_pool[0]
    y_s = _s_tail(x_s, attn_s, u_s, ga_s, gb_s, jnp.transpose(sp, (1, 0, 2)), b_gate[0], w_pool[0],
                  row2(pool_scale[0]), w_branch_a[0], w_branch_b[0], w_out[0], row2(ln1_g[0]),
                  row2(ln1_b[0]), w_r, b_r, w_e_gate[0], w_e_up[0], w_e_down[0], row2(ln2_g[0]),
                  row2(ln2_b[0]), alpha)

    heads = (N_HEADS, HEAD_DIM)
    return (
        y_p.reshape(b, s, d),
        y_s.reshape(db, 1, d),
        jnp.transpose(k.reshape((b,) + heads + (s,)), (0, 3, 1, 2))[None],
        jnp.transpose(v.reshape((b,) + heads + (s,)), (0, 3, 1, 2))[None],
        u3[:, s - POOL_STATE:, :][None],
        k_s.reshape((1, db, 1) + heads),
        v_s.reshape((1, db, 1) + heads),
        jnp.concatenate([sp[:, 1:, :], u_s[:, None, :]], axis=1)[None],
    )
```

```python
import functools

import jax
import jax.numpy as jnp
from jax import lax
from jax.experimental import pallas as pl
from jax.experimental.pallas import tpu as pltpu

F32 = jnp.float32
BF16 = jnp.bfloat16
HIGHEST = lax.Precision.HIGHEST

N_HEADS = 8
HEAD_DIM = 64
ATTN_WIDTH = N_HEADS * HEAD_DIM
MOBA_BLOCK = 256
MOBA_TOPK = 3
ROPE_THETA = 10000.0
POOL_WINDOWS = (2, 4, 8, 16)
POOL_GROUP_WIDTH = 128
POOL_WIDTH = 512
POOL_STATE = 15
N_EXPERT_GROUPS = 4
EXPERTS_PER_GROUP = 4
N_EXPERTS = 16
LN_EPS = 1e-5
LANES = 128
HALO_ROWS = 16
MASK_BIAS = -1e30
VMEM_LIMIT = 56 * 1024 * 1024

NT_DIMS = (((1,), (1,)), ((), ()))


def _dot(a, b, precision=None):
    return jnp.dot(a, b, preferred_element_type=F32, precision=precision)


def _dot_nt(a, b, precision=None):
    return lax.dot_general(a, b, NT_DIMS, preferred_element_type=F32, precision=precision)


def _layer_norm(y, g, b):
    mu = jnp.mean(y, axis=-1, keepdims=True)
    var = jnp.mean(jnp.square(y - mu), axis=-1, keepdims=True)
    return (y - mu) * lax.rsqrt(var + LN_EPS) * g + b


def _sigmoid(x):
    return 1.0 / (1.0 + jnp.exp(-x))


def _rope_chunk(x, cos, sin_signed, first_half):
    partner = jnp.where(first_half, pltpu.roll(x, 96, 1), pltpu.roll(x, 32, 1))
    return x * cos + partner * sin_signed


def _route_weights(logit):
    g = [logit(k) for k in range(N_EXPERT_GROUPS)]
    gmax = jnp.maximum(jnp.maximum(g[0], g[1]), jnp.maximum(g[2], g[3]))
    den = sum(jnp.exp(gk - gmax) for gk in g)
    g_w = 1.0 / den
    is_g = []
    taken = jnp.zeros_like(gmax)
    for k in range(N_EXPERT_GROUPS):
        hit = jnp.where(g[k] == gmax, 1.0, 0.0) * (1.0 - taken)
        is_g.append(hit)
        taken = taken + hit
    e = []
    for k in range(EXPERTS_PER_GROUP):
        col = jnp.zeros_like(gmax)
        for gi in range(N_EXPERT_GROUPS):
            col = jnp.where(is_g[gi] > 0.5, logit(N_EXPERT_GROUPS + gi * EXPERTS_PER_GROUP + k), col)
        e.append(col)
    v1 = jnp.maximum(jnp.maximum(e[0], e[1]), jnp.maximum(e[2], e[3]))
    first = []
    taken = jnp.zeros_like(v1)
    for k in range(EXPERTS_PER_GROUP):
        hit = jnp.where(e[k] == v1, 1.0, 0.0) * (1.0 - taken)
        first.append(hit)
        taken = taken + hit
    e2 = [jnp.where(first[k] > 0.5, -jnp.inf, e[k]) for k in range(EXPERTS_PER_GROUP)]
    v2 = jnp.maximum(jnp.maximum(e2[0], e2[1]), jnp.maximum(e2[2], e2[3]))
    second = []
    taken = jnp.zeros_like(v2)
    for k in range(EXPERTS_PER_GROUP):
        hit = jnp.where(e2[k] == v2, 1.0, 0.0) * (1.0 - taken)
        second.append(hit)
        taken = taken + hit
    t = jnp.exp(v2 - v1)
    w1 = 1.0 / (1.0 + t)
    w2 = t * w1
    return [is_g[gi] * (first[k] * w1 + second[k] * w2) * g_w
            for gi in range(N_EXPERT_GROUPS) for k in range(EXPERTS_PER_GROUP)]


def _route(logits):
    lane = lax.broadcasted_iota(jnp.int32, logits.shape, 1)
    comb = jnp.zeros(logits.shape, F32)
    for e, col in enumerate(_route_weights(lambda k: logits[:, k:k + 1])):
        comb = jnp.where(lane == e, col, comb)
    return comb


def _route_rows(logits_t):
    tokens = logits_t.shape[1]
    row = lax.broadcasted_iota(jnp.int32, (N_EXPERTS, tokens), 0)
    comb = jnp.zeros((N_EXPERTS, tokens), F32)
    for e, r in enumerate(_route_weights(lambda k: logits_t[k:k + 1, :])):
        comb = jnp.where(row == e, r, comb)
    return jnp.concatenate([comb, jnp.zeros((logits_t.shape[0] - N_EXPERTS, tokens), F32)], axis=0)


Q_SCALE = 1.4426950408889634 * HEAD_DIM ** -0.5


def _qkvu_kernel(x_ref, w_ref, cos_ref, sin_ref, qt_ref, k_ref, v_ref, u_ref, kb_ref, vt_ref,
                 km_ref, *, tm):
    xb = x_ref[...].astype(BF16)
    cos = cos_ref[...]
    sin = sin_ref[...]
    lane = lax.broadcasted_iota(jnp.int32, (tm, LANES), 1)
    first_half = (lane & 32) == 0
    hq = _dot(xb, w_ref[:, 0:ATTN_WIDTH])
    hk = _dot(xb, w_ref[:, ATTN_WIDTH:2 * ATTN_WIDTH])
    hv = _dot(xb, w_ref[:, 2 * ATTN_WIDTH:3 * ATTN_WIDTH])
    for c in range(ATTN_WIDTH // LANES):
        sl = slice(c * LANES, (c + 1) * LANES)
        qc = _rope_chunk(hq[:, sl], cos, sin, first_half) * Q_SCALE
        kc = _rope_chunk(hk[:, sl], cos, sin, first_half)
        kb_ref[:, sl] = kc.astype(BF16)
        k_ref[sl, :] = kc.T
        vct = hv[:, sl].T
        v_ref[sl, :] = vct
        qct = qc.T.astype(BF16)
        for r in range(tm // MOBA_BLOCK):
            rows = slice(r * MOBA_BLOCK, (r + 1) * MOBA_BLOCK)
            km_ref[r:r + 1, sl] = jnp.sum(kc[rows], axis=0, keepdims=True) * (1.0 / MOBA_BLOCK)
            qt_ref[r, sl, :] = qct[:, rows]
            vt_ref[r, sl, :] = vct[:, rows].astype(BF16)
    u_ref[...] = _dot(xb, w_ref[:, 3 * ATTN_WIDTH:])


def _qkvu(x2d, w_qkvu, cos, sin, seq, tm):
    n, d = x2d.shape
    s_tiles = seq // tm
    nblk = tm // MOBA_BLOCK
    row = lambda i: (i, 0)
    blocked = lambda i: (i, 0, 0)
    featmajor = lambda i: (i // s_tiles, 0, i % s_tiles)
    out_shape = (
        jax.ShapeDtypeStruct((n // MOBA_BLOCK, ATTN_WIDTH, MOBA_BLOCK), BF16),
        jax.ShapeDtypeStruct((n // seq, ATTN_WIDTH, seq), F32),
        jax.ShapeDtypeStruct((n // seq, ATTN_WIDTH, seq), F32),
        jax.ShapeDtypeStruct((n, POOL_WIDTH), F32),
        jax.ShapeDtypeStruct((n, ATTN_WIDTH), BF16),
        jax.ShapeDtypeStruct((n // MOBA_BLOCK, ATTN_WIDTH, MOBA_BLOCK), BF16),
        jax.ShapeDtypeStruct((n // tm, nblk, ATTN_WIDTH), F32),
    )
    return pl.pallas_call(
        functools.partial(_qkvu_kernel, tm=tm),
        out_shape=out_shape,
        grid=(n // tm,),
        in_specs=[
            pl.BlockSpec((tm, d), row),
            pl.BlockSpec((d, 4 * ATTN_WIDTH), lambda i: (0, 0)),
            pl.BlockSpec((tm, LANES), lambda i: (i % s_tiles, 0)),
            pl.BlockSpec((tm, LANES), lambda i: (i % s_tiles, 0)),
        ],
        out_specs=(
            pl.BlockSpec((nblk, ATTN_WIDTH, MOBA_BLOCK), blocked),
            pl.BlockSpec((None, ATTN_WIDTH, tm), featmajor),
            pl.BlockSpec((None, ATTN_WIDTH, tm), featmajor),
            pl.BlockSpec((tm, POOL_WIDTH), row),
            pl.BlockSpec((tm, ATTN_WIDTH), row),
            pl.BlockSpec((nblk, ATTN_WIDTH, MOBA_BLOCK), blocked),
            pl.BlockSpec((None, nblk, ATTN_WIDTH), blocked),
        ),
        compiler_params=pltpu.CompilerParams(
            dimension_semantics=("arbitrary",), vmem_limit_bytes=VMEM_LIMIT),
        name="qkvu",
    )(x2d, w_qkvu, cos, sin)


BIAS_ROWS = 128
SUM_ROWS = 16


ATTN_PAIRS = 4


def _attn_kernel(qt_ref, k_ref, vt_ref, km_ref, o_ref, qa_ref, acc_ref):
    i = pl.program_id(2)
    n_past = km_ref.shape[0]
    n_heads = 2 * ATTN_PAIRS
    feat = lax.broadcasted_iota(jnp.int32, (LANES, MOBA_BLOCK), 0)
    head0 = feat < HEAD_DIM
    blk_id = lax.broadcasted_iota(jnp.int32, (n_past, MOBA_BLOCK), 0)
    blk_f = blk_id.astype(F32)
    past = blk_id < i
    key_id = lax.broadcasted_iota(jnp.int32, (MOBA_BLOCK, MOBA_BLOCK), 0)
    qry_id = lax.broadcasted_iota(jnp.int32, (MOBA_BLOCK, MOBA_BLOCK), 1)
    causal = key_id <= qry_id
    lane = lax.broadcasted_iota(jnp.int32, (MOBA_BLOCK, LANES), 1)
    ones_rows = jnp.ones((SUM_ROWS, MOBA_BLOCK), BF16)
    bias_pad = jnp.zeros((BIAS_ROWS - n_past, MOBA_BLOCK), BF16)
    own = pl.multiple_of(i * MOBA_BLOCK, MOBA_BLOCK)

    def slab(pair):
        return slice(pair * LANES, (pair + 1) * LANES)

    def vt_aug(blk, pair):
        return jnp.concatenate([vt_ref[blk, slab(pair), :], ones_rows], axis=0)

    def softmax_pv(ss, ms, blk):
        out = []
        for h in range(n_heads):
            m_new = jnp.maximum(ms[h], jnp.max(ss[h], axis=0, keepdims=True))
            alpha = jnp.exp2(ms[h] - m_new)
            p = jnp.exp2(ss[h] - m_new)
            acc_ref[h] = alpha * acc_ref[h] + _dot(vt_aug(blk, h // 2), p.astype(BF16))
            out.append(m_new)
        return tuple(out)

    ss = []
    for h in range(n_heads):
        pair = h // 2
        qt = qt_ref[slab(pair), :]
        qh = jnp.where(head0 if h % 2 == 0 else jnp.logical_not(head0), qt, jnp.zeros_like(qt))
        gate = _dot(km_ref[:, slab(pair)], qh.astype(F32), HIGHEST)
        g = jnp.where(past, gate, -jnp.inf)
        sel = jnp.zeros(g.shape, F32)
        for _ in range(MOBA_TOPK):
            mx = jnp.max(g, axis=0, keepdims=True)
            idx = jnp.min(jnp.where(g == mx, blk_f, float(n_past)), axis=0, keepdims=True)
            pick = blk_f == idx
            sel = jnp.where(pick, 1.0, sel)
            g = jnp.where(pick, -jnp.inf, g)
        bias = jnp.where(jnp.logical_and(sel > 0.5, past), 0.0, MASK_BIAS).astype(BF16)
        qa_ref[h] = jnp.concatenate([qh, bias, bias_pad], axis=0)
        kd = k_ref[pl.ds(own, MOBA_BLOCK), slab(pair)]
        ss.append(jnp.where(causal, _dot(kd, qh), -jnp.inf))
        acc_ref[h] = jnp.zeros(acc_ref.shape[1:], F32)
    m0 = jnp.full((1, MOBA_BLOCK), MASK_BIAS, F32)
    ms = softmax_pv(ss, (m0,) * n_heads, i)

    def body(j, ms):
        off = pl.multiple_of(j * MOBA_BLOCK, MOBA_BLOCK)
        onehot = jnp.where(lane == j, 1.0, 0.0).astype(BF16)
        ss = []
        for pair in range(ATTN_PAIRS):
            ka = jnp.concatenate([k_ref[pl.ds(off, MOBA_BLOCK), slab(pair)], onehot], axis=1)
            ss += [_dot(ka, qa_ref[2 * pair]), _dot(ka, qa_ref[2 * pair + 1])]
        return softmax_pv(ss, ms, j)

    lax.fori_loop(0, i, body, ms)
    for pair in range(ATTN_PAIRS):
        a0 = acc_ref[2 * pair]
        a1 = acc_ref[2 * pair + 1]
        ot = jnp.where(head0, a0[:LANES] / a0[LANES:LANES + 1], a1[:LANES] / a1[LANES:LANES + 1])
        o_ref[:, slab(pair)] = ot.T.astype(BF16)


def _attn(qt, kb, vt, km):
    b, s, _ = kb.shape
    nb = s // MOBA_BLOCK
    width = ATTN_PAIRS * LANES
    return pl.pallas_call(
        _attn_kernel,
        out_shape=jax.ShapeDtypeStruct((b, s, ATTN_WIDTH), BF16),
        grid=(b, ATTN_WIDTH // width, nb),
        in_specs=[
            pl.BlockSpec((None, width, MOBA_BLOCK), lambda bi, p, i: (bi * nb + i, p, 0)),
            pl.BlockSpec((None, s, width), lambda bi, p, i: (bi, 0, p)),
            pl.BlockSpec((None, nb, width, MOBA_BLOCK), lambda bi, p, i: (bi, 0, p, 0)),
            pl.BlockSpec((None, nb, width), lambda bi, p, i: (bi, 0, p)),
        ],
        out_specs=pl.BlockSpec((None, MOBA_BLOCK, width), lambda bi, p, i: (bi, i, p)),
        scratch_shapes=[
            pltpu.VMEM((2 * ATTN_PAIRS, LANES + BIAS_ROWS, MOBA_BLOCK), BF16),
            pltpu.VMEM((2 * ATTN_PAIRS, LANES + SUM_ROWS, MOBA_BLOCK), F32),
        ],
        compiler_params=pltpu.CompilerParams(
            dimension_semantics=("arbitrary", "arbitrary", "arbitrary"),
            vmem_limit_bytes=VMEM_LIMIT),
        name="moba_attn",
    )(qt, kb, vt, km)


def _pool_group(window_sum, u_g, count, w_pool_g, scale_g, precision=None, cast=None):
    d = window_sum / count - u_g
    if cast is not None:
        d = d.astype(cast)
    return _dot(d, w_pool_g, precision) * scale_g


def _mix_kernel(x_ref, attn_ref, u_ref, halo_ref, wg_ref, bg_ref, wpool_ref, pscale_ref, wa_ref,
                wb_ref, wout_ref, g1_ref, b1_ref, x1_ref, z_ref, *, tm, alpha):
    i = pl.program_id(1)
    x = x_ref[...]
    xb = x.astype(BF16)
    u = u_ref[...]
    z_ref[0:HALO_ROWS, :] = jnp.where(i > 0, halo_ref[...], 0.0)
    z_ref[HALO_ROWS:, :] = u
    pos1 = (i * tm + 1 + lax.broadcasted_iota(jnp.int32, (tm, 1), 0)).astype(F32)
    parts = []
    for g, w in enumerate(POOL_WINDOWS):
        sl = slice(g * POOL_GROUP_WIDTH, (g + 1) * POOL_GROUP_WIDTH)
        acc = u[:, sl]
        for back in range(1, w):
            acc = acc + z_ref[HALO_ROWS - back:HALO_ROWS - back + tm, sl]
        count = jnp.minimum(float(w), pos1)
        parts.append(_pool_group(acc, u[:, sl], count, wpool_ref[g], pscale_ref[:, sl], cast=BF16))
    pooled = jnp.concatenate(parts, axis=1).astype(BF16)
    d = x.shape[1]
    ga = _dot(xb, wg_ref[:, :d]) + bg_ref[0:1, :]
    gb = _dot(xb, wg_ref[:, d:]) + bg_ref[1:2, :]
    a = _dot(attn_ref[...], wa_ref[...])
    bb = _dot(pooled, wb_ref[...])
    merged = _sigmoid(ga) * a + _sigmoid(gb) * bb
    y = alpha * x + _dot(merged.astype(BF16), wout_ref[...])
    x1_ref[...] = _layer_norm(y, g1_ref[...], b1_ref[...])


def _mix(x, attn, u, w_gates, b_gate, w_pool, pool_scale, w_a, w_b, w_out, ln_g, ln_b, tm, alpha):
    b, s, d = x.shape
    halo_per_tile = tm // HALO_ROWS
    tile = lambda bi, i: (bi, i, 0)
    const2 = lambda bi, i: (0, 0)
    return pl.pallas_call(
        functools.partial(_mix_kernel, tm=tm, alpha=alpha),
        out_shape=jax.ShapeDtypeStruct((b, s, d), F32),
        grid=(b, s // tm),
        in_specs=[
            pl.BlockSpec((None, tm, d), tile),
            pl.BlockSpec((None, tm, ATTN_WIDTH), tile),
            pl.BlockSpec((None, tm, POOL_WIDTH), tile),
            pl.BlockSpec((None, HALO_ROWS, POOL_WIDTH),
                         lambda bi, i: (bi, jnp.maximum(i * halo_per_tile - 1, 0), 0)),
            pl.BlockSpec(w_gates.shape, const2),
            pl.BlockSpec(b_gate.shape, const2),
            pl.BlockSpec(w_pool.shape, lambda bi, i: (0, 0, 0)),
            pl.BlockSpec(pool_scale.shape, const2),
            pl.BlockSpec(w_a.shape, const2),
            pl.BlockSpec(w_b.shape, const2),
            pl.BlockSpec(w_out.shape, const2),
            pl.BlockSpec(ln_g.shape, const2),
            pl.BlockSpec(ln_b.shape, const2),
        ],
        out_specs=pl.BlockSpec((None, tm, d), tile),
        scratch_shapes=[pltpu.VMEM((HALO_ROWS + tm, POOL_WIDTH), F32)],
        compiler_params=pltpu.CompilerParams(
            dimension_semantics=("arbitrary", "arbitrary"), vmem_limit_bytes=VMEM_LIMIT),
        name="mix",
    )(x, attn, u, u, w_gates, b_gate, w_pool, pool_scale, w_a, w_b, w_out, ln_g, ln_b)


def _moe_kernel(x1_ref, wr_ref, br_ref, wg_ref, wu_ref, wd_ref, g2_ref, b2_ref, y_ref,
                xb_ref, comb_ref, acc_ref, *, alpha):
    e = pl.program_id(1)
    tm = x1_ref.shape[0]
    lane = lax.broadcasted_iota(jnp.int32, (tm, LANES), 1)

    @pl.when(e == 0)
    def _():
        x1 = x1_ref[...]
        xb_ref[...] = x1.astype(BF16)
        logits = _dot(x1, wr_ref[...], HIGHEST) + br_ref[...]
        comb_ref[...] = _route_rows(logits.T).T
        acc_ref[...] = jnp.zeros_like(acc_ref)

    xb = xb_ref[...]
    hg = _dot(xb, wg_ref[...])
    hu = _dot(xb, wu_ref[...])
    c_e = jnp.sum(jnp.where(lane == e, comb_ref[...], 0.0), axis=1, keepdims=True)
    h = hg * _sigmoid(hg) * hu * c_e
    acc_ref[...] += _dot(h.astype(BF16), wd_ref[...])

    @pl.when(e == N_EXPERTS - 1)
    def _():
        y = alpha * x1_ref[...] + acc_ref[...]
        y_ref[...] = _layer_norm(y, g2_ref[...], b2_ref[...])


def _moe(x1, w_r, b_r, w_g, w_u, w_d, ln_g, ln_b, tm, alpha):
    n, d = x1.shape
    f = w_g.shape[2]
    tile = lambda i, e: (i, 0)
    const2 = lambda i, e: (0, 0)
    return pl.pallas_call(
        functools.partial(_moe_kernel, alpha=alpha),
        out_shape=jax.ShapeDtypeStruct((n, d), F32),
        grid=(n // tm, N_EXPERTS),
        in_specs=[
            pl.BlockSpec((tm, d), tile),
            pl.BlockSpec(w_r.shape, const2),
            pl.BlockSpec(b_r.shape, const2),
            pl.BlockSpec((None, d, f), lambda i, e: (e, 0, 0)),
            pl.BlockSpec((None, d, f), lambda i, e: (e, 0, 0)),
            pl.BlockSpec((None, f, d), lambda i, e: (e, 0, 0)),
            pl.BlockSpec(ln_g.shape, const2),
            pl.BlockSpec(ln_b.shape, const2),
        ],
        out_specs=pl.BlockSpec((tm, d), tile),
        scratch_shapes=[
            pltpu.VMEM((tm, d), BF16),
            pltpu.VMEM((tm, LANES), F32),
            pltpu.VMEM((tm, d), F32),
        ],
        compiler_params=pltpu.CompilerParams(
            dimension_semantics=("arbitrary", "arbitrary"), vmem_limit_bytes=VMEM_LIMIT),
        name="moe",
    )(x1, w_r, b_r, w_g, w_u, w_d, ln_g, ln_b)


def _s_proj_kernel(x_ref, w_ref, cos_ref, sin_ref, h_ref):
    c = pl.program_id(0)
    h = _dot(x_ref[...], w_ref[...], HIGHEST)
    rows = h.shape[0]
    lane = lax.broadcasted_iota(jnp.int32, (rows, LANES), 1)
    first_half = (lane & 32) == 0
    rotary = c < 2
    for j in range(h.shape[1] // LANES):
        sl = slice(j * LANES, (j + 1) * LANES)
        hc = h[:, sl]
        h_ref[:, sl] = jnp.where(rotary, _rope_chunk(hc, cos_ref[...], sin_ref[...], first_half), hc)


def _s_proj(x, w_in, cos, sin):
    rows, d = x.shape
    width = w_in.shape[1]
    chunk = ATTN_WIDTH
    return pl.pallas_call(
        _s_proj_kernel,
        out_shape=jax.ShapeDtypeStruct((rows, width), F32),
        grid=(width // chunk,),
        in_specs=[
            pl.BlockSpec((rows, d), lambda c: (0, 0)),
            pl.BlockSpec((d, chunk), lambda c: (0, c)),
            pl.BlockSpec((1, LANES), lambda c: (0, 0)),
            pl.BlockSpec((1, LANES), lambda c: (0, 0)),
        ],
        out_specs=pl.BlockSpec((rows, chunk), lambda c: (0, c)),
        compiler_params=pltpu.CompilerParams(
            dimension_semantics=("arbitrary",), vmem_limit_bytes=VMEM_LIMIT),
        name="s_proj",
    )(x, w_in, cos, sin)


PAGES_PER_STEP = 8


def _s_scores_kernel(pt_ref, q_ref, *refs, page_size, n_blocks):
    page_refs = refs[:PAGES_PER_STEP]
    sc_ref, sel_ref, gs_ref = refs[PAGES_PER_STEP:]
    c = pl.program_id(1)
    pages_per_block = MOBA_BLOCK // page_size
    blocks_per_step = PAGES_PER_STEP // pages_per_block
    scale = HEAD_DIM ** -0.5
    for h in range(N_HEADS):
        qh = q_ref[h] * scale
        for r in range(blocks_per_step):
            tot = jnp.zeros((1, page_size), F32)
            for pp in range(pages_per_block):
                p = r * pages_per_block + pp
                s = jnp.sum(page_refs[p][h] * qh, axis=0, keepdims=True)
                sc_ref[p, h:h + 1, :] = s
                tot = tot + s
            gs_ref[c * blocks_per_step + r, h:h + 1, :] = jnp.sum(tot, axis=1, keepdims=True)

    @pl.when(c == pl.num_programs(1) - 1)
    def _():
        g = gs_ref[...]
        blk_f = lax.broadcasted_iota(jnp.int32, g.shape, 0).astype(F32)
        for t in range(MOBA_TOPK):
            mx = jnp.max(g, axis=0, keepdims=True)
            idx = jnp.min(jnp.where(g == mx, blk_f, float(n_blocks)), axis=0, keepdims=True)
            sel_ref[t:t + 1] = idx.astype(jnp.int32)
            g = jnp.where(blk_f == idx, -jnp.inf, g)


def _s_scores(page_table, q4, cache_kt, n_blocks):
    db, n_pages = page_table.shape
    page_size = cache_kt.shape[3]
    page_block = (None, N_HEADS, HEAD_DIM, page_size)

    def page_spec(r):
        return pl.BlockSpec(page_block, lambda b, c, pt: (pt[b, c * PAGES_PER_STEP + r], 0, 0, 0))

    return pl.pallas_call(
        functools.partial(_s_scores_kernel, page_size=page_size, n_blocks=n_blocks),
        out_shape=(jax.ShapeDtypeStruct((db, n_pages, N_HEADS, page_size), F32),
                   jax.ShapeDtypeStruct((db, MOBA_TOPK, N_HEADS, 1), jnp.int32)),
        grid_spec=pltpu.PrefetchScalarGridSpec(
            num_scalar_prefetch=1,
            grid=(db, n_pages // PAGES_PER_STEP),
            in_specs=[pl.BlockSpec((None, N_HEADS, HEAD_DIM, 1), lambda b, c, pt: (b, 0, 0, 0))]
            + [page_spec(r) for r in range(PAGES_PER_STEP)],
            out_specs=(pl.BlockSpec((None, PAGES_PER_STEP, N_HEADS, page_size),
                                    lambda b, c, pt: (b, c, 0, 0)),
                       pl.BlockSpec((None, MOBA_TOPK, N_HEADS, 1), lambda b, c, pt: (b, 0, 0, 0))),
            scratch_shapes=[pltpu.VMEM((n_blocks, N_HEADS, 1), F32)],
        ),
        compiler_params=pltpu.CompilerParams(
            dimension_semantics=("arbitrary", "arbitrary"), vmem_limit_bytes=VMEM_LIMIT),
        name="s_scores",
    )(page_table, q4, *([cache_kt] * PAGES_PER_STEP))


def _s_attn_kernel(pt_ref, sel_ref, sc_ref, q_ref, kn_ref, vn_ref, cv_ref, o_ref, vbuf, sem,
                   *, pages_per_block):
    b = pl.program_id(0)
    slices_per_seq = N_HEADS * MOBA_TOPK * pages_per_block

    def slot_index(slot, h, t, r):
        return slot * slices_per_seq + (h * MOBA_TOPK + t) * pages_per_block + r

    def copies(seq, slot):
        out = []
        for h in range(N_HEADS):
            for t in range(MOBA_TOPK):
                blk = sel_ref[seq, t * N_HEADS + h]
                for r in range(pages_per_block):
                    page = pt_ref[seq, blk * pages_per_block + r]
                    out.append(pltpu.make_async_copy(
                        cv_ref.at[page, h], vbuf.at[slot_index(slot, h, t, r)], sem.at[slot]))
        return out

    @pl.when(b == 0)
    def _():
        for cp in copies(0, 0):
            cp.start()

    @pl.when(b + 1 < pl.num_programs(0))
    def _():
        for cp in copies(b + 1, (b + 1) % 2):
            cp.start()

    slot = b % 2
    for cp in copies(b, slot):
        cp.wait()

    scale = HEAD_DIM ** -0.5
    for h in range(N_HEADS):
        s_self = jnp.sum(q_ref[h] * kn_ref[h], axis=0, keepdims=True) * scale
        scores = []
        mx = s_self
        for t in range(MOBA_TOPK):
            blk = sel_ref[b, t * N_HEADS + h]
            for r in range(pages_per_block):
                sc = sc_ref[blk * pages_per_block + r, h:h + 1, :]
                scores.append(sc)
                mx = jnp.maximum(mx, jnp.max(sc, axis=1, keepdims=True))
        p_self = jnp.exp(s_self - mx)
        den = p_self
        acc = p_self * vn_ref[h]
        n = 0
        for t in range(MOBA_TOPK):
            for r in range(pages_per_block):
                p = jnp.exp(scores[n] - mx)
                n += 1
                den = den + jnp.sum(p, axis=1, keepdims=True)
                acc = acc + jnp.sum(vbuf[slot_index(slot, h, t, r)] * p, axis=1, keepdims=True)
        o_ref[h] = acc / den


def _s_attn(page_table, sel, scores, q4, kn4, vn4, cache_vt):
    db, n_pages = page_table.shape
    page_size = cache_vt.shape[3]
    pages_per_block = MOBA_BLOCK // page_size
    n_slices = 2 * N_HEADS * MOBA_TOPK * pages_per_block
    per_b = pl.BlockSpec((None, N_HEADS, HEAD_DIM, 1), lambda b, pt, sl: (b, 0, 0, 0))
    return pl.pallas_call(
        functools.partial(_s_attn_kernel, pages_per_block=pages_per_block),
        out_shape=jax.ShapeDtypeStruct((db, N_HEADS, HEAD_DIM, 1), F32),
        grid_spec=pltpu.PrefetchScalarGridSpec(
            num_scalar_prefetch=2,
            grid=(db,),
            in_specs=[pl.BlockSpec((None, n_pages, N_HEADS, page_size),
                                   lambda b, pt, sl: (b, 0, 0, 0)),
                      per_b, per_b, per_b, pl.BlockSpec(memory_space=pl.ANY)],
            out_specs=per_b,
            scratch_shapes=[pltpu.VMEM((n_slices, HEAD_DIM, page_size), F32),
                            pltpu.SemaphoreType.DMA((2,))],
        ),
        compiler_params=pltpu.CompilerParams(
            dimension_semantics=("arbitrary",), vmem_limit_bytes=VMEM_LIMIT),
        name="s_attn",
    )(page_table, sel, scores, q4, kn4, vn4, cache_vt)


def _s_tail_kernel(x_ref, attn_ref, u_ref, ga_ref, gb_ref, sp_ref, bg_ref, wpool_ref, pscale_ref,
                   wa_ref, wb_ref, wout_ref, g1_ref, b1_ref, wr_ref, br_ref, wg_ref, wu_ref,
                   wd_ref, g2_ref, b2_ref, y_ref, x1_ref, comb_ref, acc_ref, *, alpha):
    e = pl.program_id(0)
    rows = x_ref.shape[0]
    lane = lax.broadcasted_iota(jnp.int32, (rows, LANES), 1)

    @pl.when(e == 0)
    def _():
        u = u_ref[...]
        parts = []
        for g, w in enumerate(POOL_WINDOWS):
            sl = slice(g * POOL_GROUP_WIDTH, (g + 1) * POOL_GROUP_WIDTH)
            acc = u[:, sl]
            for back in range(1, w):
                acc = acc + sp_ref[POOL_STATE - back][:, sl]
            parts.append(_pool_group(acc, u[:, sl], float(w), wpool_ref[g], pscale_ref[:, sl],
                                     precision=HIGHEST))
        pooled = jnp.concatenate(parts, axis=1)
        a = _dot(attn_ref[...], wa_ref[...], HIGHEST)
        bb = _dot(pooled, wb_ref[...], HIGHEST)
        merged = _sigmoid(ga_ref[...] + bg_ref[0:1, :]) * a + _sigmoid(gb_ref[...] + bg_ref[1:2, :]) * bb
        y = alpha * x_ref[...] + _dot(merged, wout_ref[...], HIGHEST)
        x1 = _layer_norm(y, g1_ref[...], b1_ref[...])
        x1_ref[...] = x1
        comb_ref[...] = _route(_dot(x1, wr_ref[...], HIGHEST) + br_ref[...])
        acc_ref[...] = jnp.zeros_like(acc_ref)

    x1 = x1_ref[...]
    hg = _dot(x1, wg_ref[...], HIGHEST)
    hu = _dot(x1, wu_ref[...], HIGHEST)
    c_e = jnp.sum(jnp.where(lane == e, comb_ref[...], 0.0), axis=1, keepdims=True)
    h = hg * _sigmoid(hg) * hu * c_e
    acc_ref[...] += _dot(h, wd_ref[...], HIGHEST)

    @pl.when(e == N_EXPERTS - 1)
    def _():
        y_ref[...] = _layer_norm(alpha * x1_ref[...] + acc_ref[...], g2_ref[...], b2_ref[...])


def _s_tail(x, attn, u, ga, gb, sp_t, b_gate, w_pool, pool_scale, w_a, w_b, w_out, ln1_g, ln1_b,
            w_r, b_r, w_g, w_u, w_d, ln2_g, ln2_b, alpha):
    rows, d = x.shape
    f = w_g.shape[2]
    const2 = lambda e: (0, 0)
    const3 = lambda e: (0, 0, 0)
    full = lambda a: pl.BlockSpec(a.shape, const2 if a.ndim == 2 else const3)
    per_e = lambda blk: pl.BlockSpec((None,) + blk, lambda e: (e, 0, 0))
    small = [x, attn, u, ga, gb, sp_t, b_gate, w_pool, pool_scale, w_a, w_b, w_out, ln1_g, ln1_b,
             w_r, b_r]
    return pl.pallas_call(
        functools.partial(_s_tail_kernel, alpha=alpha),
        out_shape=jax.ShapeDtypeStruct((rows, d), F32),
        grid=(N_EXPERTS,),
        in_specs=[full(a) for a in small]
        + [per_e((d, f)), per_e((d, f)), per_e((f, d)), full(ln2_g), full(ln2_b)],
        out_specs=pl.BlockSpec((rows, d), const2),
        scratch_shapes=[pltpu.VMEM((rows, d), F32), pltpu.VMEM((rows, LANES), F32),
                        pltpu.VMEM((rows, d), F32)],
        compiler_params=pltpu.CompilerParams(
            dimension_semantics=("arbitrary",), vmem_limit_bytes=VMEM_LIMIT),
        name="s_tail",
    )(*small, w_g, w_u, w_d, ln2_g, ln2_b)


def _rope_tables(pos):
    inv_freq = 1.0 / (ROPE_THETA ** (jnp.arange(0, HEAD_DIM, 2, dtype=F32) / HEAD_DIM))
    ang = pos.astype(F32)[:, None] * inv_freq[None, :]
    cos = jnp.cos(ang)
    sin = jnp.sin(ang)
    cos_t = jnp.tile(cos, (1, LANES // (HEAD_DIM // 2)))
    sin_t = jnp.tile(jnp.concatenate([-sin, sin], axis=1), (1, LANES // HEAD_DIM))
    return cos_t, sin_t


def kernel(x_prompt, x_sample, cache_k, cache_v, state_pool, page_table, w_in, w_pool, pool_scale,
           w_branch_a, w_branch_b, b_gate, w_out, ln1_g, ln1_b, w_group_router, b_group_router,
           w_expert_router, b_expert_router, w_e_gate, w_e_up, w_e_down, ln2_g, ln2_b):
    depth = w_in.shape[0]
    assert depth == 1 and x_sample.shape[1] == 1
    alpha = (2 * depth) ** 0.25
    b, s, d = x_prompt.shape
    db = x_sample.shape[0]
    n_pages = page_table.shape[1]
    page_size = cache_k.shape[2]
    past_len = n_pages * page_size
    n_blocks = past_len // MOBA_BLOCK
    assert past_len % MOBA_BLOCK == 0 and s % MOBA_BLOCK == 0

    w_in0 = w_in[0]
    qkvu_cols = 3 * ATTN_WIDTH + POOL_WIDTH
    row2 = lambda a: a.reshape(1, -1)
    w_r = jnp.concatenate(
        [w_group_router[0], jnp.transpose(w_expert_router[0], (1, 0, 2)).reshape(d, N_EXPERTS),
         jnp.zeros((d, LANES - N_EXPERT_GROUPS - N_EXPERTS), F32)], axis=1)
    b_r = jnp.concatenate(
        [b_group_router[0], b_expert_router[0].reshape(-1),
         jnp.zeros((LANES - N_EXPERT_GROUPS - N_EXPERTS,), F32)]).reshape(1, LANES)

    cos_p, sin_p = _rope_tables(jnp.arange(s, dtype=jnp.int32))
    tm_a = 512
    qt, k, v, u, kb, vt, km = _qkvu(x_prompt.reshape(b * s, d), w_in0[:, :qkvu_cols].astype(BF16),
                                    cos_p, sin_p, s, tm_a)
    nb = s // MOBA_BLOCK
    attn = _attn(qt, kb.reshape(b, s, ATTN_WIDTH), vt.reshape(b, nb, ATTN_WIDTH, MOBA_BLOCK),
                 km.reshape(b, nb, ATTN_WIDTH))
    u3 = u.reshape(b, s, POOL_WIDTH)
    x1 = _mix(x_prompt, attn, u3, w_in0[:, qkvu_cols:].astype(BF16), b_gate[0],
              w_pool[0].astype(BF16), row2(pool_scale[0]), w_branch_a[0].astype(BF16),
              w_branch_b[0].astype(BF16), w_out[0].astype(BF16), row2(ln1_g[0]), row2(ln1_b[0]),
              256, alpha)
    y_p = _moe(x1.reshape(b * s, d), w_r, b_r, w_e_gate[0].astype(BF16), w_e_up[0].astype(BF16),
               w_e_down[0].astype(BF16), row2(ln2_g[0]), row2(ln2_b[0]), 1024, alpha)

    cos_s, sin_s = _rope_tables(jnp.full((1,), past_len, jnp.int32))
    x_s = x_sample.reshape(db, d)
    h_s = _s_proj(x_s, w_in0, cos_s, sin_s)
    q_s = h_s[:, :ATTN_WIDTH]
    k_s = h_s[:, ATTN_WIDTH:2 * ATTN_WIDTH]
    v_s = h_s[:, 2 * ATTN_WIDTH:3 * ATTN_WIDTH]
    u_s = h_s[:, 3 * ATTN_WIDTH:qkvu_cols]
    ga_s = h_s[:, qkvu_cols:qkvu_cols + d]
    gb_s = h_s[:, qkvu_cols + d:]
    cache_kt = jnp.transpose(cache_k[0], (0, 2, 3, 1))
    cache_vt = jnp.transpose(cache_v[0], (0, 2, 3, 1))
    col4 = lambda a: a.reshape(db, N_HEADS, HEAD_DIM, 1)
    scores, sel = _s_scores(page_table, col4(q_s), cache_kt, n_blocks)
    sel2 = sel.reshape(db, MOBA_TOPK * N_HEADS)
    attn_s = _s_attn(page_table, sel2, scores, col4(q_s), col4(k_s), col4(v_s),
                     cache_vt).reshape(db, ATTN_WIDTH)
    sp = state_pool[0]
    y_s = _s_tail(x_s, attn_s, u_s, ga_s, gb_s, jnp.transpose(sp, (1, 0, 2)), b_gate[0], w_pool[0],
                  row2(pool_scale[0]), w_branch_a[0], w_branch_b[0], w_out[0], row2(ln1_g[0]),
                  row2(ln1_b[0]), w_r, b_r, w_e_gate[0], w_e_up[0], w_e_down[0], row2(ln2_g[0]),
                  row2(ln2_b[0]), alpha)

    heads = (N_HEADS, HEAD_DIM)
    return (
        y_p.reshape(b, s, d),
        y_s.reshape(db, 1, d),
        jnp.transpose(k.reshape((b,) + heads + (s,)), (0, 3, 1, 2))[None],
        jnp.transpose(v.reshape((b,) + heads + (s,)), (0, 3, 1, 2))[None],
        u3[:, s - POOL_STATE:, :][None],
        k_s.reshape((1, db, 1) + heads),
        v_s.reshape((1, db, 1) + heads),
        jnp.concatenate([sp[:, 1:, :], u_s[:, None, :]], axis=1)[None],
    )
```

```python
import functools

import jax
import jax.numpy as jnp
from jax import lax
from jax.experimental import pallas as pl
from jax.experimental.pallas import tpu as pltpu

F32 = jnp.float32
BF16 = jnp.bfloat16
HIGHEST = lax.Precision.HIGHEST

N_HEADS = 8
HEAD_DIM = 64
ATTN_WIDTH = N_HEADS * HEAD_DIM
MOBA_BLOCK = 256
MOBA_TOPK = 3
ROPE_THETA = 10000.0
POOL_WINDOWS = (2, 4, 8, 16)
POOL_GROUP_WIDTH = 128
POOL_WIDTH = 512
POOL_STATE = 15
N_EXPERT_GROUPS = 4
EXPERTS_PER_GROUP = 4
N_EXPERTS = 16
LN_EPS = 1e-5
LANES = 128
HALO_ROWS = 16
MASK_BIAS = -1e30
VMEM_LIMIT = 56 * 1024 * 1024

NT_DIMS = (((1,), (1,)), ((), ()))


def _dot(a, b, precision=None):
    return jnp.dot(a, b, preferred_element_type=F32, precision=precision)


def _dot_nt(a, b, precision=None):
    return lax.dot_general(a, b, NT_DIMS, preferred_element_type=F32, precision=precision)


def _layer_norm(y, g, b):
    mu = jnp.mean(y, axis=-1, keepdims=True)
    var = jnp.mean(jnp.square(y - mu), axis=-1, keepdims=True)
    return (y - mu) * lax.rsqrt(var + LN_EPS) * g + b


def _sigmoid(x):
    return 1.0 / (1.0 + jnp.exp(-x))


def _rope_chunk(x, cos, sin_signed, first_half):
    partner = jnp.where(first_half, pltpu.roll(x, 96, 1), pltpu.roll(x, 32, 1))
    return x * cos + partner * sin_signed


def _route_weights(logit):
    g = [logit(k) for k in range(N_EXPERT_GROUPS)]
    gmax = jnp.maximum(jnp.maximum(g[0], g[1]), jnp.maximum(g[2], g[3]))
    den = sum(jnp.exp(gk - gmax) for gk in g)
    g_w = 1.0 / den
    is_g = []
    taken = jnp.zeros_like(gmax)
    for k in range(N_EXPERT_GROUPS):
        hit = jnp.where(g[k] == gmax, 1.0, 0.0) * (1.0 - taken)
        is_g.append(hit)
        taken = taken + hit
    e = []
    for k in range(EXPERTS_PER_GROUP):
        col = jnp.zeros_like(gmax)
        for gi in range(N_EXPERT_GROUPS):
            col = jnp.where(is_g[gi] > 0.5, logit(N_EXPERT_GROUPS + gi * EXPERTS_PER_GROUP + k), col)
        e.append(col)
    v1 = jnp.maximum(jnp.maximum(e[0], e[1]), jnp.maximum(e[2], e[3]))
    first = []
    taken = jnp.zeros_like(v1)
    for k in range(EXPERTS_PER_GROUP):
        hit = jnp.where(e[k] == v1, 1.0, 0.0) * (1.0 - taken)
        first.append(hit)
        taken = taken + hit
    e2 = [jnp.where(first[k] > 0.5, -jnp.inf, e[k]) for k in range(EXPERTS_PER_GROUP)]
    v2 = jnp.maximum(jnp.maximum(e2[0], e2[1]), jnp.maximum(e2[2], e2[3]))
    second = []
    taken = jnp.zeros_like(v2)
    for k in range(EXPERTS_PER_GROUP):
        hit = jnp.where(e2[k] == v2, 1.0, 0.0) * (1.0 - taken)
        second.append(hit)
        taken = taken + hit
    t = jnp.exp(v2 - v1)
    w1 = 1.0 / (1.0 + t)
    w2 = t * w1
    return [is_g[gi] * (first[k] * w1 + second[k] * w2) * g_w
            for gi in range(N_EXPERT_GROUPS) for k in range(EXPERTS_PER_GROUP)]


def _route(logits):
    lane = lax.broadcasted_iota(jnp.int32, logits.shape, 1)
    comb = jnp.zeros(logits.shape, F32)
    for e, col in enumerate(_route_weights(lambda k: logits[:, k:k + 1])):
        comb = jnp.where(lane == e, col, comb)
    return comb


def _route_rows(logits_t):
    tokens = logits_t.shape[1]
    row = lax.broadcasted_iota(jnp.int32, (N_EXPERTS, tokens), 0)
    comb = jnp.zeros((N_EXPERTS, tokens), F32)
    for e, r in enumerate(_route_weights(lambda k: logits_t[k:k + 1, :])):
        comb = jnp.where(row == e, r, comb)
    return jnp.concatenate([comb, jnp.zeros((logits_t.shape[0] - N_EXPERTS, tokens), F32)], axis=0)


Q_SCALE = 1.4426950408889634 * HEAD_DIM ** -0.5


def _qkvu_kernel(x_ref, w_ref, cos_ref, sin_ref, qt_ref, k_ref, v_ref, u_ref, kb_ref, vt_ref,
                 km_ref, *, tm):
    xb = x_ref[...].astype(BF16)
    cos = cos_ref[...]
    sin = sin_ref[...]
    lane = lax.broadcasted_iota(jnp.int32, (tm, LANES), 1)
    first_half = (lane & 32) == 0
    hq = _dot(xb, w_ref[:, 0:ATTN_WIDTH])
    hk = _dot(xb, w_ref[:, ATTN_WIDTH:2 * ATTN_WIDTH])
    hv = _dot(xb, w_ref[:, 2 * ATTN_WIDTH:3 * ATTN_WIDTH])
    for c in range(ATTN_WIDTH // LANES):
        sl = slice(c * LANES, (c + 1) * LANES)
        qc = _rope_chunk(hq[:, sl], cos, sin, first_half) * Q_SCALE
        kc = _rope_chunk(hk[:, sl], cos, sin, first_half)
        kb_ref[:, sl] = kc.astype(BF16)
        k_ref[sl, :] = kc.T
        vct = hv[:, sl].T
        v_ref[sl, :] = vct
        qct = qc.T.astype(BF16)
        for r in range(tm // MOBA_BLOCK):
            rows = slice(r * MOBA_BLOCK, (r + 1) * MOBA_BLOCK)
            km_ref[r:r + 1, sl] = jnp.sum(kc[rows], axis=0, keepdims=True) * (1.0 / MOBA_BLOCK)
            qt_ref[r, sl, :] = qct[:, rows]
            vt_ref[r, sl, :] = vct[:, rows].astype(BF16)
    u_ref[...] = _dot(xb, w_ref[:, 3 * ATTN_WIDTH:])


def _qkvu(x2d, w_qkvu, cos, sin, seq, tm):
    n, d = x2d.shape
    s_tiles = seq // tm
    nblk = tm // MOBA_BLOCK
    row = lambda i: (i, 0)
    blocked = lambda i: (i, 0, 0)
    featmajor = lambda i: (i // s_tiles, 0, i % s_tiles)
    out_shape = (
        jax.ShapeDtypeStruct((n // MOBA_BLOCK, ATTN_WIDTH, MOBA_BLOCK), BF16),
        jax.ShapeDtypeStruct((n // seq, ATTN_WIDTH, seq), F32),
        jax.ShapeDtypeStruct((n // seq, ATTN_WIDTH, seq), F32),
        jax.ShapeDtypeStruct((n, POOL_WIDTH), F32),
        jax.ShapeDtypeStruct((n, ATTN_WIDTH), BF16),
        jax.ShapeDtypeStruct((n // MOBA_BLOCK, ATTN_WIDTH, MOBA_BLOCK), BF16),
        jax.ShapeDtypeStruct((n // tm, nblk, ATTN_WIDTH), F32),
    )
    return pl.pallas_call(
        functools.partial(_qkvu_kernel, tm=tm),
        out_shape=out_shape,
        grid=(n // tm,),
        in_specs=[
            pl.BlockSpec((tm, d), row),
            pl.BlockSpec((d, 4 * ATTN_WIDTH), lambda i: (0, 0)),
            pl.BlockSpec((tm, LANES), lambda i: (i % s_tiles, 0)),
            pl.BlockSpec((tm, LANES), lambda i: (i % s_tiles, 0)),
        ],
        out_specs=(
            pl.BlockSpec((nblk, ATTN_WIDTH, MOBA_BLOCK), blocked),
            pl.BlockSpec((None, ATTN_WIDTH, tm), featmajor),
            pl.BlockSpec((None, ATTN_WIDTH, tm), featmajor),
            pl.BlockSpec((tm, POOL_WIDTH), row),
            pl.BlockSpec((tm, ATTN_WIDTH), row),
            pl.BlockSpec((nblk, ATTN_WIDTH, MOBA_BLOCK), blocked),
            pl.BlockSpec((None, nblk, ATTN_WIDTH), blocked),
        ),
        compiler_params=pltpu.CompilerParams(
            dimension_semantics=("arbitrary",), vmem_limit_bytes=VMEM_LIMIT),
        name="qkvu",
    )(x2d, w_qkvu, cos, sin)


BIAS_ROWS = 128
SUM_ROWS = 16


ATTN_PAIRS = 4


def _attn_kernel(qt_ref, k_ref, vt_ref, km_ref, o_ref, qa_ref, acc_ref):
    i = pl.program_id(2)
    n_past = km_ref.shape[0]
    n_heads = 2 * ATTN_PAIRS
    feat = lax.broadcasted_iota(jnp.int32, (LANES, MOBA_BLOCK), 0)
    head0 = feat < HEAD_DIM
    blk_id = lax.broadcasted_iota(jnp.int32, (n_past, MOBA_BLOCK), 0)
    blk_f = blk_id.astype(F32)
    past = blk_id < i
    key_id = lax.broadcasted_iota(jnp.int32, (MOBA_BLOCK, MOBA_BLOCK), 0)
    qry_id = lax.broadcasted_iota(jnp.int32, (MOBA_BLOCK, MOBA_BLOCK), 1)
    causal = key_id <= qry_id
    lane = lax.broadcasted_iota(jnp.int32, (MOBA_BLOCK, LANES), 1)
    ones_rows = jnp.ones((SUM_ROWS, MOBA_BLOCK), BF16)
    bias_pad = jnp.zeros((BIAS_ROWS - n_past, MOBA_BLOCK), BF16)
    own = pl.multiple_of(i * MOBA_BLOCK, MOBA_BLOCK)

    def slab(pair):
        return slice(pair * LANES, (pair + 1) * LANES)

    def vt_aug(blk, pair):
        return jnp.concatenate([vt_ref[blk, slab(pair), :], ones_rows], axis=0)

    def softmax_pv(ss, ms, blk):
        out = []
        for h in range(n_heads):
            m_new = jnp.maximum(ms[h], jnp.max(ss[h], axis=0, keepdims=True))
            alpha = jnp.exp2(ms[h] - m_new)
            p = jnp.exp2(ss[h] - m_new)
            acc_ref[h] = alpha * acc_ref[h] + _dot(vt_aug(blk, h // 2), p.astype(BF16))
            out.append(m_new)
        return tuple(out)

    ss = []
    for h in range(n_heads):
        pair = h // 2
        qt = qt_ref[slab(pair), :]
        qh = jnp.where(head0 if h % 2 == 0 else jnp.logical_not(head0), qt, jnp.zeros_like(qt))
        gate = _dot(km_ref[:, slab(pair)], qh.astype(F32), HIGHEST)
        g = jnp.where(past, gate, -jnp.inf)
        sel = jnp.zeros(g.shape, F32)
        for _ in range(MOBA_TOPK):
            mx = jnp.max(g, axis=0, keepdims=True)
            idx = jnp.min(jnp.where(g == mx, blk_f, float(n_past)), axis=0, keepdims=True)
            pick = blk_f == idx
            sel = jnp.where(pick, 1.0, sel)
            g = jnp.where(pick, -jnp.inf, g)
        bias = jnp.where(jnp.logical_and(sel > 0.5, past), 0.0, MASK_BIAS).astype(BF16)
        qa_ref[h] = jnp.concatenate([qh, bias, bias_pad], axis=0)
        kd = k_ref[pl.ds(own, MOBA_BLOCK), slab(pair)]
        ss.append(jnp.where(causal, _dot(kd, qh), -jnp.inf))
        acc_ref[h] = jnp.zeros(acc_ref.shape[1:], F32)
    m0 = jnp.full((1, MOBA_BLOCK), MASK_BIAS, F32)
    ms = softmax_pv(ss, (m0,) * n_heads, i)

    def body(j, ms):
        off = pl.multiple_of(j * MOBA_BLOCK, MOBA_BLOCK)
        onehot = jnp.where(lane == j, 1.0, 0.0).astype(BF16)
        ss = []
        for pair in range(ATTN_PAIRS):
            ka = jnp.concatenate([k_ref[pl.ds(off, MOBA_BLOCK), slab(pair)], onehot], axis=1)
            ss += [_dot(ka, qa_ref[2 * pair]), _dot(ka, qa_ref[2 * pair + 1])]
        return softmax_pv(ss, ms, j)

    lax.fori_loop(0, i, body, ms)
    for pair in range(ATTN_PAIRS):
        a0 = acc_ref[2 * pair]
        a1 = acc_ref[2 * pair + 1]
        ot = jnp.where(head0, a0[:LANES] / a0[LANES:LANES + 1], a1[:LANES] / a1[LANES:LANES + 1])
        o_ref[:, slab(pair)] = ot.T.astype(BF16)


def _attn(qt, kb, vt, km):
    b, s, _ = kb.shape
    nb = s // MOBA_BLOCK
    width = ATTN_PAIRS * LANES
    return pl.pallas_call(
        _attn_kernel,
        out_shape=jax.ShapeDtypeStruct((b, s, ATTN_WIDTH), BF16),
        grid=(b, ATTN_WIDTH // width, nb),
        in_specs=[
            pl.BlockSpec((None, width, MOBA_BLOCK), lambda bi, p, i: (bi * nb + i, p, 0)),
            pl.BlockSpec((None, s, width), lambda bi, p, i: (bi, 0, p)),
            pl.BlockSpec((None, nb, width, MOBA_BLOCK), lambda bi, p, i: (bi, 0, p, 0)),
            pl.BlockSpec((None, nb, width), lambda bi, p, i: (bi, 0, p)),
        ],
        out_specs=pl.BlockSpec((None, MOBA_BLOCK, width), lambda bi, p, i: (bi, i, p)),
        scratch_shapes=[
            pltpu.VMEM((2 * ATTN_PAIRS, LANES + BIAS_ROWS, MOBA_BLOCK), BF16),
            pltpu.VMEM((2 * ATTN_PAIRS, LANES + SUM_ROWS, MOBA_BLOCK), F32),
        ],
        compiler_params=pltpu.CompilerParams(
            dimension_semantics=("arbitrary", "arbitrary", "arbitrary"),
            vmem_limit_bytes=VMEM_LIMIT),
        name="moba_attn",
    )(qt, kb, vt, km)


def _pool_group(window_sum, u_g, count, w_pool_g, scale_g, precision=None, cast=None):
    d = window_sum / count - u_g
    if cast is not None:
        d = d.astype(cast)
    return _dot(d, w_pool_g, precision) * scale_g


def _mix_kernel(x_ref, attn_ref, u_ref, halo_ref, wg_ref, bg_ref, wpool_ref, pscale_ref, wa_ref,
                wb_ref, wout_ref, g1_ref, b1_ref, x1_ref, z_ref, *, tm, alpha):
    i = pl.program_id(1)
    x = x_ref[...]
    xb = x.astype(BF16)
    u = u_ref[...]
    z_ref[0:HALO_ROWS, :] = jnp.where(i > 0, halo_ref[...], 0.0)
    z_ref[HALO_ROWS:, :] = u
    pos1 = (i * tm + 1 + lax.broadcasted_iota(jnp.int32, (tm, 1), 0)).astype(F32)
    parts = []
    for g, w in enumerate(POOL_WINDOWS):
        sl = slice(g * POOL_GROUP_WIDTH, (g + 1) * POOL_GROUP_WIDTH)
        acc = u[:, sl]
        for back in range(1, w):
            acc = acc + z_ref[HALO_ROWS - back:HALO_ROWS - back + tm, sl]
        count = jnp.minimum(float(w), pos1)
        parts.append(_pool_group(acc, u[:, sl], count, wpool_ref[g], pscale_ref[:, sl], cast=BF16))
    pooled = jnp.concatenate(parts, axis=1).astype(BF16)
    d = x.shape[1]
    ga = _dot(xb, wg_ref[:, :d]) + bg_ref[0:1, :]
    gb = _dot(xb, wg_ref[:, d:]) + bg_ref[1:2, :]
    a = _dot(attn_ref[...], wa_ref[...])
    bb = _dot(pooled, wb_ref[...])
    merged = _sigmoid(ga) * a + _sigmoid(gb) * bb
    y = alpha * x + _dot(merged.astype(BF16), wout_ref[...])
    x1_ref[...] = _layer_norm(y, g1_ref[...], b1_ref[...])


def _mix(x, attn, u, w_gates, b_gate, w_pool, pool_scale, w_a, w_b, w_out, ln_g, ln_b, tm, alpha):
    b, s, d = x.shape
    halo_per_tile = tm // HALO_ROWS
    tile = lambda bi, i: (bi, i, 0)
    const2 = lambda bi, i: (0, 0)
    return pl.pallas_call(
        functools.partial(_mix_kernel, tm=tm, alpha=alpha),
        out_shape=jax.ShapeDtypeStruct((b, s, d), F32),
        grid=(b, s // tm),
        in_specs=[
            pl.BlockSpec((None, tm, d), tile),
            pl.BlockSpec((None, tm, ATTN_WIDTH), tile),
            pl.BlockSpec((None, tm, POOL_WIDTH), tile),
            pl.BlockSpec((None, HALO_ROWS, POOL_WIDTH),
                         lambda bi, i: (bi, jnp.maximum(i * halo_per_tile - 1, 0), 0)),
            pl.BlockSpec(w_gates.shape, const2),
            pl.BlockSpec(b_gate.shape, const2),
            pl.BlockSpec(w_pool.shape, lambda bi, i: (0, 0, 0)),
            pl.BlockSpec(pool_scale.shape, const2),
            pl.BlockSpec(w_a.shape, const2),
            pl.BlockSpec(w_b.shape, const2),
            pl.BlockSpec(w_out.shape, const2),
            pl.BlockSpec(ln_g.shape, const2),
            pl.BlockSpec(ln_b.shape, const2),
        ],
        out_specs=pl.BlockSpec((None, tm, d), tile),
        scratch_shapes=[pltpu.VMEM((HALO_ROWS + tm, POOL_WIDTH), F32)],
        compiler_params=pltpu.CompilerParams(
            dimension_semantics=("arbitrary", "arbitrary"), vmem_limit_bytes=VMEM_LIMIT),
        name="mix",
    )(x, attn, u, u, w_gates, b_gate, w_pool, pool_scale, w_a, w_b, w_out, ln_g, ln_b)


def _moe_kernel(x1_ref, wr_ref, br_ref, wg_ref, wu_ref, wd_ref, g2_ref, b2_ref, y_ref,
                xb_ref, comb_ref, acc_ref, *, alpha):
    e = pl.program_id(1)
    tm = x1_ref.shape[0]
    lane = lax.broadcasted_iota(jnp.int32, (tm, LANES), 1)

    @pl.when(e == 0)
    def _():
        x1 = x1_ref[...]
        xb_ref[...] = x1.astype(BF16)
        logits = _dot(x1, wr_ref[...], HIGHEST) + br_ref[...]
        comb_ref[...] = _route_rows(logits.T).T
        acc_ref[...] = jnp.zeros_like(acc_ref)

    xb = xb_ref[...]
    hg = _dot(xb, wg_ref[...])
    hu = _dot(xb, wu_ref[...])
    c_e = jnp.sum(jnp.where(lane == e, comb_ref[...], 0.0), axis=1, keepdims=True)
    h = hg * _sigmoid(hg) * hu * c_e
    acc_ref[...] += _dot(h.astype(BF16), wd_ref[...])

    @pl.when(e == N_EXPERTS - 1)
    def _():
        y = alpha * x1_ref[...] + acc_ref[...]
        y_ref[...] = _layer_norm(y, g2_ref[...], b2_ref[...])


def _moe(x1, w_r, b_r, w_g, w_u, w_d, ln_g, ln_b, tm, alpha):
    n, d = x1.shape
    f = w_g.shape[2]
    tile = lambda i, e: (i, 0)
    const2 = lambda i, e: (0, 0)
    return pl.pallas_call(
        functools.partial(_moe_kernel, alpha=alpha),
        out_shape=jax.ShapeDtypeStruct((n, d), F32),
        grid=(n // tm, N_EXPERTS),
        in_specs=[
            pl.BlockSpec((tm, d), tile),
            pl.BlockSpec(w_r.shape, const2),
            pl.BlockSpec(b_r.shape, const2),
            pl.BlockSpec((None, d, f), lambda i, e: (e, 0, 0)),
            pl.BlockSpec((None, d, f), lambda i, e: (e, 0, 0)),
            pl.BlockSpec((None, f, d), lambda i, e: (e, 0, 0)),
            pl.BlockSpec(ln_g.shape, const2),
            pl.BlockSpec(ln_b.shape, const2),
        ],
        out_specs=pl.BlockSpec((tm, d), tile),
        scratch_shapes=[
            pltpu.VMEM((tm, d), BF16),
            pltpu.VMEM((tm, LANES), F32),
            pltpu.VMEM((tm, d), F32),
        ],
        compiler_params=pltpu.CompilerParams(
            dimension_semantics=("arbitrary", "arbitrary"), vmem_limit_bytes=VMEM_LIMIT),
        name="moe",
    )(x1, w_r, b_r, w_g, w_u, w_d, ln_g, ln_b)


def _s_proj_kernel(x_ref, w_ref, cos_ref, sin_ref, h_ref):
    c = pl.program_id(0)
    h = _dot(x_ref[...], w_ref[...], HIGHEST)
    rows = h.shape[0]
    lane = lax.broadcasted_iota(jnp.int32, (rows, LANES), 1)
    first_half = (lane & 32) == 0
    rotary = c < 2
    for j in range(h.shape[1] // LANES):
        sl = slice(j * LANES, (j + 1) * LANES)
        hc = h[:, sl]
        h_ref[:, sl] = jnp.where(rotary, _rope_chunk(hc, cos_ref[...], sin_ref[...], first_half), hc)


def _s_proj(x, w_in, cos, sin):
    rows, d = x.shape
    width = w_in.shape[1]
    chunk = ATTN_WIDTH
    return pl.pallas_call(
        _s_proj_kernel,
        out_shape=jax.ShapeDtypeStruct((rows, width), F32),
        grid=(width // chunk,),
        in_specs=[
            pl.BlockSpec((rows, d), lambda c: (0, 0)),
            pl.BlockSpec((d, chunk), lambda c: (0, c)),
            pl.BlockSpec((1, LANES), lambda c: (0, 0)),
            pl.BlockSpec((1, LANES), lambda c: (0, 0)),
        ],
        out_specs=pl.BlockSpec((rows, chunk), lambda c: (0, c)),
        compiler_params=pltpu.CompilerParams(
            dimension_semantics=("arbitrary",), vmem_limit_bytes=VMEM_LIMIT),
        name="s_proj",
    )(x, w_in, cos, sin)


PAGES_PER_STEP = 32


def _s_scores_kernel(pt_ref, q_ref, *refs, page_size, n_blocks):
    page_refs = refs[:PAGES_PER_STEP]
    sc_ref, sel_ref, gs_ref = refs[PAGES_PER_STEP:]
    c = pl.program_id(1)
    pages_per_block = MOBA_BLOCK // page_size
    blocks_per_step = PAGES_PER_STEP // pages_per_block
    q = q_ref[...] * (HEAD_DIM ** -0.5)
    for r in range(blocks_per_step):
        tot = jnp.zeros((N_HEADS, page_size), F32)
        for pp in range(pages_per_block):
            p = r * pages_per_block + pp
            s = jnp.sum(page_refs[p][...] * q, axis=1)
            sc_ref[p] = s
            tot = tot + s
        gs_ref[c * blocks_per_step + r] = jnp.sum(tot, axis=1, keepdims=True)

    @pl.when(c == pl.num_programs(1) - 1)
    def _():
        g = gs_ref[...]
        blk_f = lax.broadcasted_iota(jnp.int32, g.shape, 0).astype(F32)
        for t in range(MOBA_TOPK):
            mx = jnp.max(g, axis=0, keepdims=True)
            idx = jnp.min(jnp.where(g == mx, blk_f, float(n_blocks)), axis=0, keepdims=True)
            sel_ref[t:t + 1] = idx.astype(jnp.int32)
            g = jnp.where(blk_f == idx, -jnp.inf, g)


def _s_scores(page_table, q4, cache_kt, n_blocks):
    db, n_pages = page_table.shape
    page_size = cache_kt.shape[3]
    page_block = (None, N_HEADS, HEAD_DIM, page_size)

    def page_spec(r):
        return pl.BlockSpec(page_block, lambda b, c, pt: (pt[b, c * PAGES_PER_STEP + r], 0, 0, 0))

    return pl.pallas_call(
        functools.partial(_s_scores_kernel, page_size=page_size, n_blocks=n_blocks),
        out_shape=(jax.ShapeDtypeStruct((db, n_pages, N_HEADS, page_size), F32),
                   jax.ShapeDtypeStruct((db, MOBA_TOPK, N_HEADS, 1), jnp.int32)),
        grid_spec=pltpu.PrefetchScalarGridSpec(
            num_scalar_prefetch=1,
            grid=(db, n_pages // PAGES_PER_STEP),
            in_specs=[pl.BlockSpec((None, N_HEADS, HEAD_DIM, 1), lambda b, c, pt: (b, 0, 0, 0))]
            + [page_spec(r) for r in range(PAGES_PER_STEP)],
            out_specs=(pl.BlockSpec((None, PAGES_PER_STEP, N_HEADS, page_size),
                                    lambda b, c, pt: (b, c, 0, 0)),
                       pl.BlockSpec((None, MOBA_TOPK, N_HEADS, 1), lambda b, c, pt: (b, 0, 0, 0))),
            scratch_shapes=[pltpu.VMEM((n_blocks, N_HEADS, 1), F32)],
        ),
        compiler_params=pltpu.CompilerParams(
            dimension_semantics=("arbitrary", "arbitrary"), vmem_limit_bytes=VMEM_LIMIT),
        name="s_scores",
    )(page_table, q4, *([cache_kt] * PAGES_PER_STEP))


def _s_attn_kernel(pt_ref, sel_ref, sc_ref, q_ref, kn_ref, vn_ref, cv_ref, o_ref, vbuf, sem,
                   *, pages_per_block):
    b = pl.program_id(0)
    slices_per_seq = N_HEADS * MOBA_TOPK * pages_per_block

    def slot_index(slot, h, t, r):
        return slot * slices_per_seq + (h * MOBA_TOPK + t) * pages_per_block + r

    def copies(seq, slot):
        out = []
        for h in range(N_HEADS):
            for t in range(MOBA_TOPK):
                blk = sel_ref[seq, t * N_HEADS + h]
                for r in range(pages_per_block):
                    page = pt_ref[seq, blk * pages_per_block + r]
                    out.append(pltpu.make_async_copy(
                        cv_ref.at[page, h], vbuf.at[slot_index(slot, h, t, r)], sem.at[slot]))
        return out

    @pl.when(b == 0)
    def _():
        for cp in copies(0, 0):
            cp.start()

    @pl.when(b + 1 < pl.num_programs(0))
    def _():
        for cp in copies(b + 1, (b + 1) % 2):
            cp.start()

    slot = b % 2
    for cp in copies(b, slot):
        cp.wait()

    scale = HEAD_DIM ** -0.5
    for h in range(N_HEADS):
        s_self = jnp.sum(q_ref[h] * kn_ref[h], axis=0, keepdims=True) * scale
        scores = []
        mx = s_self
        for t in range(MOBA_TOPK):
            blk = sel_ref[b, t * N_HEADS + h]
            for r in range(pages_per_block):
                sc = sc_ref[blk * pages_per_block + r, h:h + 1, :]
                scores.append(sc)
                mx = jnp.maximum(mx, jnp.max(sc, axis=1, keepdims=True))
        p_self = jnp.exp(s_self - mx)
        den = p_self
        acc = p_self * vn_ref[h]
        n = 0
        for t in range(MOBA_TOPK):
            for r in range(pages_per_block):
                p = jnp.exp(scores[n] - mx)
                n += 1
                den = den + jnp.sum(p, axis=1, keepdims=True)
                acc = acc + jnp.sum(vbuf[slot_index(slot, h, t, r)] * p, axis=1, keepdims=True)
        o_ref[h] = acc / den


def _s_attn(page_table, sel, scores, q4, kn4, vn4, cache_vt):
    db, n_pages = page_table.shape
    page_size = cache_vt.shape[3]
    pages_per_block = MOBA_BLOCK // page_size
    n_slices = 2 * N_HEADS * MOBA_TOPK * pages_per_block
    per_b = pl.BlockSpec((None, N_HEADS, HEAD_DIM, 1), lambda b, pt, sl: (b, 0, 0, 0))
    return pl.pallas_call(
        functools.partial(_s_attn_kernel, pages_per_block=pages_per_block),
        out_shape=jax.ShapeDtypeStruct((db, N_HEADS, HEAD_DIM, 1), F32),
        grid_spec=pltpu.PrefetchScalarGridSpec(
            num_scalar_prefetch=2,
            grid=(db,),
            in_specs=[pl.BlockSpec((None, n_pages, N_HEADS, page_size),
                                   lambda b, pt, sl: (b, 0, 0, 0)),
                      per_b, per_b, per_b, pl.BlockSpec(memory_space=pl.ANY)],
            out_specs=per_b,
            scratch_shapes=[pltpu.VMEM((n_slices, HEAD_DIM, page_size), F32),
                            pltpu.SemaphoreType.DMA((2,))],
        ),
        compiler_params=pltpu.CompilerParams(
            dimension_semantics=("arbitrary",), vmem_limit_bytes=VMEM_LIMIT),
        name="s_attn",
    )(page_table, sel, scores, q4, kn4, vn4, cache_vt)


def _s_tail_kernel(x_ref, attn_ref, u_ref, ga_ref, gb_ref, sp_ref, bg_ref, wpool_ref, pscale_ref,
                   wa_ref, wb_ref, wout_ref, g1_ref, b1_ref, wr_ref, br_ref, wg_ref, wu_ref,
                   wd_ref, g2_ref, b2_ref, y_ref, x1_ref, comb_ref, acc_ref, *, alpha):
    e = pl.program_id(0)
    rows = x_ref.shape[0]
    lane = lax.broadcasted_iota(jnp.int32, (rows, LANES), 1)

    @pl.when(e == 0)
    def _():
        u = u_ref[...]
        parts = []
        for g, w in enumerate(POOL_WINDOWS):
            sl = slice(g * POOL_GROUP_WIDTH, (g + 1) * POOL_GROUP_WIDTH)
            acc = u[:, sl]
            for back in range(1, w):
                acc = acc + sp_ref[POOL_STATE - back][:, sl]
            parts.append(_pool_group(acc, u[:, sl], float(w), wpool_ref[g], pscale_ref[:, sl],
                                     precision=HIGHEST))
        pooled = jnp.concatenate(parts, axis=1)
        a = _dot(attn_ref[...], wa_ref[...], HIGHEST)
        bb = _dot(pooled, wb_ref[...], HIGHEST)
        merged = _sigmoid(ga_ref[...] + bg_ref[0:1, :]) * a + _sigmoid(gb_ref[...] + bg_ref[1:2, :]) * bb
        y = alpha * x_ref[...] + _dot(merged, wout_ref[...], HIGHEST)
        x1 = _layer_norm(y, g1_ref[...], b1_ref[...])
        x1_ref[...] = x1
        comb_ref[...] = _route(_dot(x1, wr_ref[...], HIGHEST) + br_ref[...])
        acc_ref[...] = jnp.zeros_like(acc_ref)

    x1 = x1_ref[...]
    hg = _dot(x1, wg_ref[...], HIGHEST)
    hu = _dot(x1, wu_ref[...], HIGHEST)
    c_e = jnp.sum(jnp.where(lane == e, comb_ref[...], 0.0), axis=1, keepdims=True)
    h = hg * _sigmoid(hg) * hu * c_e
    acc_ref[...] += _dot(h, wd_ref[...], HIGHEST)

    @pl.when(e == N_EXPERTS - 1)
    def _():
        y_ref[...] = _layer_norm(alpha * x1_ref[...] + acc_ref[...], g2_ref[...], b2_ref[...])


def _s_tail(x, attn, u, ga, gb, sp_t, b_gate, w_pool, pool_scale, w_a, w_b, w_out, ln1_g, ln1_b,
            w_r, b_r, w_g, w_u, w_d, ln2_g, ln2_b, alpha):
    rows, d = x.shape
    f = w_g.shape[2]
    const2 = lambda e: (0, 0)
    const3 = lambda e: (0, 0, 0)
    full = lambda a: pl.BlockSpec(a.shape, const2 if a.ndim == 2 else const3)
    per_e = lambda blk: pl.BlockSpec((None,) + blk, lambda e: (e, 0, 0))
    small = [x, attn, u, ga, gb, sp_t, b_gate, w_pool, pool_scale, w_a, w_b, w_out, ln1_g, ln1_b,
             w_r, b_r]
    return pl.pallas_call(
        functools.partial(_s_tail_kernel, alpha=alpha),
        out_shape=jax.ShapeDtypeStruct((rows, d), F32),
        grid=(N_EXPERTS,),
        in_specs=[full(a) for a in small]
        + [per_e((d, f)), per_e((d, f)), per_e((f, d)), full(ln2_g), full(ln2_b)],
        out_specs=pl.BlockSpec((rows, d), const2),
        scratch_shapes=[pltpu.VMEM((rows, d), F32), pltpu.VMEM((rows, LANES), F32),
                        pltpu.VMEM((rows, d), F32)],
        compiler_params=pltpu.CompilerParams(
            dimension_semantics=("arbitrary",), vmem_limit_bytes=VMEM_LIMIT),
        name="s_tail",
    )(*small, w_g, w_u, w_d, ln2_g, ln2_b)


def _rope_tables(pos):
    inv_freq = 1.0 / (ROPE_THETA ** (jnp.arange(0, HEAD_DIM, 2, dtype=F32) / HEAD_DIM))
    ang = pos.astype(F32)[:, None] * inv_freq[None, :]
    cos = jnp.cos(ang)
    sin = jnp.sin(ang)
    cos_t = jnp.tile(cos, (1, LANES // (HEAD_DIM // 2)))
    sin_t = jnp.tile(jnp.concatenate([-sin, sin], axis=1), (1, LANES // HEAD_DIM))
    return cos_t, sin_t


def kernel(x_prompt, x_sample, cache_k, cache_v, state_pool, page_table, w_in, w_pool, pool_scale,
           w_branch_a, w_branch_b, b_gate, w_out, ln1_g, ln1_b, w_group_router, b_group_router,
           w_expert_router, b_expert_router, w_e_gate, w_e_up, w_e_down, ln2_g, ln2_b):
    depth = w_in.shape[0]
    assert depth == 1 and x_sample.shape[1] == 1
    alpha = (2 * depth) ** 0.25
    b, s, d = x_prompt.shape
    db = x_sample.shape[0]
    n_pages = page_table.shape[1]
    page_size = cache_k.shape[2]
    past_len = n_pages * page_size
    n_blocks = past_len // MOBA_BLOCK
    assert past_len % MOBA_BLOCK == 0 and s % MOBA_BLOCK == 0

    w_in0 = w_in[0]
    qkvu_cols = 3 * ATTN_WIDTH + POOL_WIDTH
    row2 = lambda a: a.reshape(1, -1)
    w_r = jnp.concatenate(
        [w_group_router[0], jnp.transpose(w_expert_router[0], (1, 0, 2)).reshape(d, N_EXPERTS),
         jnp.zeros((d, LANES - N_EXPERT_GROUPS - N_EXPERTS), F32)], axis=1)
    b_r = jnp.concatenate(
        [b_group_router[0], b_expert_router[0].reshape(-1),
         jnp.zeros((LANES - N_EXPERT_GROUPS - N_EXPERTS,), F32)]).reshape(1, LANES)

    cos_p, sin_p = _rope_tables(jnp.arange(s, dtype=jnp.int32))
    tm_a = 512
    qt, k, v, u, kb, vt, km = _qkvu(x_prompt.reshape(b * s, d), w_in0[:, :qkvu_cols].astype(BF16),
                                    cos_p, sin_p, s, tm_a)
    nb = s // MOBA_BLOCK
    attn = _attn(qt, kb.reshape(b, s, ATTN_WIDTH), vt.reshape(b, nb, ATTN_WIDTH, MOBA_BLOCK),
                 km.reshape(b, nb, ATTN_WIDTH))
    u3 = u.reshape(b, s, POOL_WIDTH)
    x1 = _mix(x_prompt, attn, u3, w_in0[:, qkvu_cols:].astype(BF16), b_gate[0],
              w_pool[0].astype(BF16), row2(pool_scale[0]), w_branch_a[0].astype(BF16),
              w_branch_b[0].astype(BF16), w_out[0].astype(BF16), row2(ln1_g[0]), row2(ln1_b[0]),
              256, alpha)
    y_p = _moe(x1.reshape(b * s, d), w_r, b_r, w_e_gate[0].astype(BF16), w_e_up[0].astype(BF16),
               w_e_down[0].astype(BF16), row2(ln2_g[0]), row2(ln2_b[0]), 1024, alpha)

    cos_s, sin_s = _rope_tables(jnp.full((1,), past_len, jnp.int32))
    x_s = x_sample.reshape(db, d)
    h_s = _s_proj(x_s, w_in0, cos_s, sin_s)
    q_s = h_s[:, :ATTN_WIDTH]
    k_s = h_s[:, ATTN_WIDTH:2 * ATTN_WIDTH]
    v_s = h_s[:, 2 * ATTN_WIDTH:3 * ATTN_WIDTH]
    u_s = h_s[:, 3 * ATTN_WIDTH:qkvu_cols]
    ga_s = h_s[:, qkvu_cols:qkvu_cols + d]
    gb_s = h_s[:, qkvu_cols + d:]
    cache_kt = jnp.transpose(cache_k[0], (0, 2, 3, 1))
    cache_vt = jnp.transpose(cache_v[0], (0, 2, 3, 1))
    col4 = lambda a: a.reshape(db, N_HEADS, HEAD_DIM, 1)
    scores, sel = _s_scores(page_table, col4(q_s), cache_kt, n_blocks)
    sel2 = sel.reshape(db, MOBA_TOPK * N_HEADS)
    attn_s = _s_attn(page_table, sel2, scores, col4(q_s), col4(k_s), col4(v_s),
                     cache_vt).reshape(db, ATTN_WIDTH)
    sp = state_pool[0]
    y_s = _s_tail(x_s, attn_s, u_s, ga_s, gb_s, jnp.transpose(sp, (1, 0, 2)), b_gate[0], w_pool[0],
                  row2(pool_scale[0]), w_branch_a[0], w_branch_b[0], w_out[0], row2(ln1_g[0]),
                  row2(ln1_b[0]), w_r, b_r, w_e_gate[0], w_e_up[0], w_e_down[0], row2(ln2_g[0]),
                  row2(ln2_b[0]), alpha)

    heads = (N_HEADS, HEAD_DIM)
    return (
        y_p.reshape(b, s, d),
        y_s.reshape(db, 1, d),
        jnp.transpose(k.reshape((b,) + heads + (s,)), (0, 3, 1, 2))[None],
        jnp.transpose(v.reshape((b,) + heads + (s,)), (0, 3, 1, 2))[None],
        u3[:, s - POOL_STATE:, :][None],
        k_s.reshape((1, db, 1) + heads),
        v_s.reshape((1, db, 1) + heads),
        jnp.concatenate([sp[:, 1:, :], u_s[:, None, :]], axis=1)[None],
    )
```

```python
import functools

import jax
import jax.numpy as jnp
from jax import lax
from jax.experimental import pallas as pl
from jax.experimental.pallas import tpu as pltpu

F32 = jnp.float32
BF16 = jnp.bfloat16
HIGHEST = lax.Precision.HIGHEST

N_HEADS = 8
HEAD_DIM = 64
ATTN_WIDTH = N_HEADS * HEAD_DIM
MOBA_BLOCK = 256
MOBA_TOPK = 3
ROPE_THETA = 10000.0
POOL_WINDOWS = (2, 4, 8, 16)
POOL_GROUP_WIDTH = 128
POOL_WIDTH = 512
POOL_STATE = 15
N_EXPERT_GROUPS = 4
EXPERTS_PER_GROUP = 4
N_EXPERTS = 16
LN_EPS = 1e-5
LANES = 128
HALO_ROWS = 16
MASK_BIAS = -1e30
VMEM_LIMIT = 56 * 1024 * 1024

NT_DIMS = (((1,), (1,)), ((), ()))


def _dot(a, b, precision=None):
    return jnp.dot(a, b, preferred_element_type=F32, precision=precision)


def _dot_nt(a, b, precision=None):
    return lax.dot_general(a, b, NT_DIMS, preferred_element_type=F32, precision=precision)


def _layer_norm(y, g, b):
    mu = jnp.mean(y, axis=-1, keepdims=True)
    var = jnp.mean(jnp.square(y - mu), axis=-1, keepdims=True)
    return (y - mu) * lax.rsqrt(var + LN_EPS) * g + b


def _sigmoid(x):
    return 1.0 / (1.0 + jnp.exp(-x))


def _rope_chunk(x, cos, sin_signed, first_half):
    partner = jnp.where(first_half, pltpu.roll(x, 96, 1), pltpu.roll(x, 32, 1))
    return x * cos + partner * sin_signed


def _route_weights(logit):
    g = [logit(k) for k in range(N_EXPERT_GROUPS)]
    gmax = jnp.maximum(jnp.maximum(g[0], g[1]), jnp.maximum(g[2], g[3]))
    den = sum(jnp.exp(gk - gmax) for gk in g)
    g_w = 1.0 / den
    is_g = []
    taken = jnp.zeros_like(gmax)
    for k in range(N_EXPERT_GROUPS):
        hit = jnp.where(g[k] == gmax, 1.0, 0.0) * (1.0 - taken)
        is_g.append(hit)
        taken = taken + hit
    e = []
    for k in range(EXPERTS_PER_GROUP):
        col = jnp.zeros_like(gmax)
        for gi in range(N_EXPERT_GROUPS):
            col = jnp.where(is_g[gi] > 0.5, logit(N_EXPERT_GROUPS + gi * EXPERTS_PER_GROUP + k), col)
        e.append(col)
    v1 = jnp.maximum(jnp.maximum(e[0], e[1]), jnp.maximum(e[2], e[3]))
    first = []
    taken = jnp.zeros_like(v1)
    for k in range(EXPERTS_PER_GROUP):
        hit = jnp.where(e[k] == v1, 1.0, 0.0) * (1.0 - taken)
        first.append(hit)
        taken = taken + hit
    e2 = [jnp.where(first[k] > 0.5, -jnp.inf, e[k]) for k in range(EXPERTS_PER_GROUP)]
    v2 = jnp.maximum(jnp.maximum(e2[0], e2[1]), jnp.maximum(e2[2], e2[3]))
    second = []
    taken = jnp.zeros_like(v2)
    for k in range(EXPERTS_PER_GROUP):
        hit = jnp.where(e2[k] == v2, 1.0, 0.0) * (1.0 - taken)
        second.append(hit)
        taken = taken + hit
    t = jnp.exp(v2 - v1)
    w1 = 1.0 / (1.0 + t)
    w2 = t * w1
    weights = [is_g[gi] * (first[k] * w1 + second[k] * w2) * g_w
               for gi in range(N_EXPERT_GROUPS) for k in range(EXPERTS_PER_GROUP)]
    group = sum(float(gi) * is_g[gi] for gi in range(1, N_EXPERT_GROUPS))
    return weights, group


def _route(logits):
    lane = lax.broadcasted_iota(jnp.int32, logits.shape, 1)
    comb = jnp.zeros(logits.shape, F32)
    for e, col in enumerate(_route_weights(lambda k: logits[:, k:k + 1])[0]):
        comb = jnp.where(lane == e, col, comb)
    return comb


def _route_rows(logits_t):
    tokens = logits_t.shape[1]
    row = lax.broadcasted_iota(jnp.int32, (N_EXPERTS, tokens), 0)
    comb = jnp.zeros((N_EXPERTS, tokens), F32)
    weights, group = _route_weights(lambda k: logits_t[k:k + 1, :])
    for e, r in enumerate(weights):
        comb = jnp.where(row == e, r, comb)
    pad = jnp.zeros((logits_t.shape[0] - N_EXPERTS, tokens), F32)
    return jnp.concatenate([comb, pad], axis=0), group


Q_SCALE = 1.4426950408889634 * HEAD_DIM ** -0.5


def _qkvu_kernel(x_ref, w_ref, cos_ref, sin_ref, qt_ref, k_ref, v_ref, u_ref, kb_ref, vt_ref,
                 km_ref, *, tm):
    xb = x_ref[...].astype(BF16)
    cos = cos_ref[...]
    sin = sin_ref[...]
    lane = lax.broadcasted_iota(jnp.int32, (tm, LANES), 1)
    first_half = (lane & 32) == 0
    hq = _dot(xb, w_ref[:, 0:ATTN_WIDTH])
    hk = _dot(xb, w_ref[:, ATTN_WIDTH:2 * ATTN_WIDTH])
    hv = _dot(xb, w_ref[:, 2 * ATTN_WIDTH:3 * ATTN_WIDTH])
    for c in range(ATTN_WIDTH // LANES):
        sl = slice(c * LANES, (c + 1) * LANES)
        qc = _rope_chunk(hq[:, sl], cos, sin, first_half) * Q_SCALE
        kc = _rope_chunk(hk[:, sl], cos, sin, first_half)
        kb_ref[:, sl] = kc.astype(BF16)
        k_ref[sl, :] = kc.T
        vct = hv[:, sl].T
        v_ref[sl, :] = vct
        qct = qc.T.astype(BF16)
        for r in range(tm // MOBA_BLOCK):
            rows = slice(r * MOBA_BLOCK, (r + 1) * MOBA_BLOCK)
            km_ref[r:r + 1, sl] = jnp.sum(kc[rows], axis=0, keepdims=True) * (1.0 / MOBA_BLOCK)
            qt_ref[r, sl, :] = qct[:, rows]
            vt_ref[r, sl, :] = vct[:, rows].astype(BF16)
    u_ref[...] = _dot(xb, w_ref[:, 3 * ATTN_WIDTH:])


def _qkvu(x2d, w_qkvu, cos, sin, seq, tm):
    n, d = x2d.shape
    s_tiles = seq // tm
    nblk = tm // MOBA_BLOCK
    row = lambda i: (i, 0)
    blocked = lambda i: (i, 0, 0)
    featmajor = lambda i: (i // s_tiles, 0, i % s_tiles)
    out_shape = (
        jax.ShapeDtypeStruct((n // MOBA_BLOCK, ATTN_WIDTH, MOBA_BLOCK), BF16),
        jax.ShapeDtypeStruct((n // seq, ATTN_WIDTH, seq), F32),
        jax.ShapeDtypeStruct((n // seq, ATTN_WIDTH, seq), F32),
        jax.ShapeDtypeStruct((n, POOL_WIDTH), F32),
        jax.ShapeDtypeStruct((n, ATTN_WIDTH), BF16),
        jax.ShapeDtypeStruct((n // MOBA_BLOCK, ATTN_WIDTH, MOBA_BLOCK), BF16),
        jax.ShapeDtypeStruct((n // tm, nblk, ATTN_WIDTH), F32),
    )
    return pl.pallas_call(
        functools.partial(_qkvu_kernel, tm=tm),
        out_shape=out_shape,
        grid=(n // tm,),
        in_specs=[
            pl.BlockSpec((tm, d), row),
            pl.BlockSpec((d, 4 * ATTN_WIDTH), lambda i: (0, 0)),
            pl.BlockSpec((tm, LANES), lambda i: (i % s_tiles, 0)),
            pl.BlockSpec((tm, LANES), lambda i: (i % s_tiles, 0)),
        ],
        out_specs=(
            pl.BlockSpec((nblk, ATTN_WIDTH, MOBA_BLOCK), blocked),
            pl.BlockSpec((None, ATTN_WIDTH, tm), featmajor),
            pl.BlockSpec((None, ATTN_WIDTH, tm), featmajor),
            pl.BlockSpec((tm, POOL_WIDTH), row),
            pl.BlockSpec((tm, ATTN_WIDTH), row),
            pl.BlockSpec((nblk, ATTN_WIDTH, MOBA_BLOCK), blocked),
            pl.BlockSpec((None, nblk, ATTN_WIDTH), blocked),
        ),
        compiler_params=pltpu.CompilerParams(
            dimension_semantics=("arbitrary",), vmem_limit_bytes=VMEM_LIMIT),
        name="qkvu",
    )(x2d, w_qkvu, cos, sin)


BIAS_ROWS = 128
SUM_ROWS = 16


ATTN_PAIRS = 4


def _attn_kernel(qt_ref, k_ref, vt_ref, km_ref, o_ref, qa_ref, acc_ref):
    i = pl.program_id(2)
    n_past = km_ref.shape[0]
    n_heads = 2 * ATTN_PAIRS
    feat = lax.broadcasted_iota(jnp.int32, (LANES, MOBA_BLOCK), 0)
    head0 = feat < HEAD_DIM
    blk_id = lax.broadcasted_iota(jnp.int32, (n_past, MOBA_BLOCK), 0)
    blk_f = blk_id.astype(F32)
    past = blk_id < i
    key_id = lax.broadcasted_iota(jnp.int32, (MOBA_BLOCK, MOBA_BLOCK), 0)
    qry_id = lax.broadcasted_iota(jnp.int32, (MOBA_BLOCK, MOBA_BLOCK), 1)
    causal = key_id <= qry_id
    lane = lax.broadcasted_iota(jnp.int32, (MOBA_BLOCK, LANES), 1)
    ones_rows = jnp.ones((SUM_ROWS, MOBA_BLOCK), BF16)
    bias_pad = jnp.zeros((BIAS_ROWS - n_past, MOBA_BLOCK), BF16)
    own = pl.multiple_of(i * MOBA_BLOCK, MOBA_BLOCK)

    def slab(pair):
        return slice(pair * LANES, (pair + 1) * LANES)

    def vt_aug(blk, pair):
        return jnp.concatenate([vt_ref[blk, slab(pair), :], ones_rows], axis=0)

    def softmax_pv(ss, ms, blk):
        out = []
        for h in range(n_heads):
            m_new = jnp.maximum(ms[h], jnp.max(ss[h], axis=0, keepdims=True))
            alpha = jnp.exp2(ms[h] - m_new)
            p = jnp.exp2(ss[h] - m_new)
            acc_ref[h] = alpha * acc_ref[h] + _dot(vt_aug(blk, h // 2), p.astype(BF16))
            out.append(m_new)
        return tuple(out)

    ss = []
    for h in range(n_heads):
        pair = h // 2
        qt = qt_ref[slab(pair), :]
        qh = jnp.where(head0 if h % 2 == 0 else jnp.logical_not(head0), qt, jnp.zeros_like(qt))
        gate = _dot(km_ref[:, slab(pair)], qh.astype(F32), HIGHEST)
        g = jnp.where(past, gate, -jnp.inf)
        sel = jnp.zeros(g.shape, F32)
        for _ in range(MOBA_TOPK):
            mx = jnp.max(g, axis=0, keepdims=True)
            idx = jnp.min(jnp.where(g == mx, blk_f, float(n_past)), axis=0, keepdims=True)
            pick = blk_f == idx
            sel = jnp.where(pick, 1.0, sel)
            g = jnp.where(pick, -jnp.inf, g)
        bias = jnp.where(jnp.logical_and(sel > 0.5, past), 0.0, MASK_BIAS).astype(BF16)
        qa_ref[h] = jnp.concatenate([qh, bias, bias_pad], axis=0)
        kd = k_ref[pl.ds(own, MOBA_BLOCK), slab(pair)]
        ss.append(jnp.where(causal, _dot(kd, qh), -jnp.inf))
        acc_ref[h] = jnp.zeros(acc_ref.shape[1:], F32)
    m0 = jnp.full((1, MOBA_BLOCK), MASK_BIAS, F32)
    ms = softmax_pv(ss, (m0,) * n_heads, i)

    def body(j, ms):
        off = pl.multiple_of(j * MOBA_BLOCK, MOBA_BLOCK)
        onehot = jnp.where(lane == j, 1.0, 0.0).astype(BF16)
        ss = []
        for pair in range(ATTN_PAIRS):
            ka = jnp.concatenate([k_ref[pl.ds(off, MOBA_BLOCK), slab(pair)], onehot], axis=1)
            ss += [_dot(ka, qa_ref[2 * pair]), _dot(ka, qa_ref[2 * pair + 1])]
        return softmax_pv(ss, ms, j)

    lax.fori_loop(0, i, body, ms)
    for pair in range(ATTN_PAIRS):
        a0 = acc_ref[2 * pair]
        a1 = acc_ref[2 * pair + 1]
        ot = jnp.where(head0, a0[:LANES] / a0[LANES:LANES + 1], a1[:LANES] / a1[LANES:LANES + 1])
        o_ref[:, slab(pair)] = ot.T.astype(BF16)


def _attn(qt, kb, vt, km):
    b, s, _ = kb.shape
    nb = s // MOBA_BLOCK
    width = ATTN_PAIRS * LANES
    return pl.pallas_call(
        _attn_kernel,
        out_shape=jax.ShapeDtypeStruct((b, s, ATTN_WIDTH), BF16),
        grid=(b, ATTN_WIDTH // width, nb),
        in_specs=[
            pl.BlockSpec((None, width, MOBA_BLOCK), lambda bi, p, i: (bi * nb + i, p, 0)),
            pl.BlockSpec((None, s, width), lambda bi, p, i: (bi, 0, p)),
            pl.BlockSpec((None, nb, width, MOBA_BLOCK), lambda bi, p, i: (bi, 0, p, 0)),
            pl.BlockSpec((None, nb, width), lambda bi, p, i: (bi, 0, p)),
        ],
        out_specs=pl.BlockSpec((None, MOBA_BLOCK, width), lambda bi, p, i: (bi, i, p)),
        scratch_shapes=[
            pltpu.VMEM((2 * ATTN_PAIRS, LANES + BIAS_ROWS, MOBA_BLOCK), BF16),
            pltpu.VMEM((2 * ATTN_PAIRS, LANES + SUM_ROWS, MOBA_BLOCK), F32),
        ],
        compiler_params=pltpu.CompilerParams(
            dimension_semantics=("arbitrary", "arbitrary", "arbitrary"),
            vmem_limit_bytes=VMEM_LIMIT),
        name="moba_attn",
    )(qt, kb, vt, km)


def _pool_group(window_sum, u_g, count, w_pool_g, scale_g, precision=None, cast=None):
    d = window_sum / count - u_g
    if cast is not None:
        d = d.astype(cast)
    return _dot(d, w_pool_g, precision) * scale_g


def _mix_kernel(x_ref, attn_ref, u_ref, halo_ref, wg_ref, bg_ref, wpool_ref, pscale_ref, wa_ref,
                wb_ref, wout_ref, g1_ref, b1_ref, wr_ref, br_ref, x1_ref, grp_ref, z_ref,
                *, tm, alpha):
    i = pl.program_id(1)
    x = x_ref[...]
    xb = x.astype(BF16)
    u = u_ref[...]
    z_ref[0:HALO_ROWS, :] = jnp.where(i > 0, halo_ref[...], 0.0)
    z_ref[HALO_ROWS:, :] = u
    pos1 = (i * tm + 1 + lax.broadcasted_iota(jnp.int32, (tm, 1), 0)).astype(F32)
    parts = []
    for g, w in enumerate(POOL_WINDOWS):
        sl = slice(g * POOL_GROUP_WIDTH, (g + 1) * POOL_GROUP_WIDTH)
        acc = u[:, sl]
        for back in range(1, w):
            acc = acc + z_ref[HALO_ROWS - back:HALO_ROWS - back + tm, sl]
        count = jnp.minimum(float(w), pos1)
        parts.append(_pool_group(acc, u[:, sl], count, wpool_ref[g], pscale_ref[:, sl], cast=BF16))
    pooled = jnp.concatenate(parts, axis=1).astype(BF16)
    d = x.shape[1]
    ga = _dot(xb, wg_ref[:, :d]) + bg_ref[0:1, :]
    gb = _dot(xb, wg_ref[:, d:]) + bg_ref[1:2, :]
    a = _dot(attn_ref[...], wa_ref[...])
    bb = _dot(pooled, wb_ref[...])
    merged = _sigmoid(ga) * a + _sigmoid(gb) * bb
    y = alpha * x + _dot(merged.astype(BF16), wout_ref[...])
    x1 = _layer_norm(y, g1_ref[...], b1_ref[...])
    logits = _dot(x1, wr_ref[...], HIGHEST) + br_ref[...]
    comb_t, group = _route_rows(logits.T)
    x1_ref[:, :d] = x1
    x1_ref[:, d:] = comb_t.T
    grp_ref[...] = group.astype(jnp.int32)


def _mix(x, attn, u, w_gates, b_gate, w_pool, pool_scale, w_a, w_b, w_out, ln_g, ln_b, w_r, b_r,
         tm, alpha):
    b, s, d = x.shape
    halo_per_tile = tm // HALO_ROWS
    tiles = s // tm
    tile = lambda bi, i: (bi, i, 0)
    const2 = lambda bi, i: (0, 0)
    return pl.pallas_call(
        functools.partial(_mix_kernel, tm=tm, alpha=alpha),
        out_shape=(jax.ShapeDtypeStruct((b, s, d + LANES), F32),
                   jax.ShapeDtypeStruct((b * tiles, 1, tm), jnp.int32)),
        grid=(b, tiles),
        in_specs=[
            pl.BlockSpec((None, tm, d), tile),
            pl.BlockSpec((None, tm, ATTN_WIDTH), tile),
            pl.BlockSpec((None, tm, POOL_WIDTH), tile),
            pl.BlockSpec((None, HALO_ROWS, POOL_WIDTH),
                         lambda bi, i: (bi, jnp.maximum(i * halo_per_tile - 1, 0), 0)),
            pl.BlockSpec(w_gates.shape, const2),
            pl.BlockSpec(b_gate.shape, const2),
            pl.BlockSpec(w_pool.shape, lambda bi, i: (0, 0, 0)),
            pl.BlockSpec(pool_scale.shape, const2),
            pl.BlockSpec(w_a.shape, const2),
            pl.BlockSpec(w_b.shape, const2),
            pl.BlockSpec(w_out.shape, const2),
            pl.BlockSpec(ln_g.shape, const2),
            pl.BlockSpec(ln_b.shape, const2),
            pl.BlockSpec(w_r.shape, const2),
            pl.BlockSpec(b_r.shape, const2),
        ],
        out_specs=(pl.BlockSpec((None, tm, d + LANES), tile),
                   pl.BlockSpec((None, 1, tm), lambda bi, i: (bi * tiles + i, 0, 0))),
        scratch_shapes=[pltpu.VMEM((HALO_ROWS + tm, POOL_WIDTH), F32)],
        compiler_params=pltpu.CompilerParams(
            dimension_semantics=("arbitrary", "arbitrary"), vmem_limit_bytes=VMEM_LIMIT),
        name="mix",
    )(x, attn, u, u, w_gates, b_gate, w_pool, pool_scale, w_a, w_b, w_out, ln_g, ln_b, w_r, b_r)


MOE_TILE = 256
MOVE_ROWS = 256
MOVE_UNROLL = 8


def _moe_plan_kernel(grp_ref, pos_ref, tile_ref):
    rows = grp_ref.shape[0]
    grp = grp_ref[...]
    li = lax.broadcasted_iota(jnp.int32, (LANES, LANES), 0)
    lj = lax.broadcasted_iota(jnp.int32, (LANES, LANES), 1)
    upper = jnp.where(li <= lj, 1.0, 0.0).astype(BF16)
    ri = lax.broadcasted_iota(jnp.int32, (rows, rows), 0)
    rj = lax.broadcasted_iota(jnp.int32, (rows, rows), 1)
    before = jnp.where(rj < ri, 1.0, 0.0).astype(BF16)
    tile_start = (lax.broadcasted_iota(jnp.int32, (1, LANES), 1) * MOE_TILE).astype(F32)
    start = jnp.zeros((1, LANES), F32)
    pos = jnp.zeros((rows, LANES), F32)
    tile_group = jnp.zeros((1, LANES), F32)
    for g in range(N_EXPERT_GROUPS):
        hit = jnp.where(grp == g, 1.0, 0.0)
        incl = _dot(hit.astype(BF16), upper)
        row_tot = jnp.broadcast_to(incl[:, LANES - 1:LANES], (rows, LANES))
        above = _dot(before, row_tot.astype(BF16))
        count = above[rows - 1:rows, :] + row_tot[rows - 1:rows, :]
        pos = pos + hit * (start + above + incl - 1.0)
        start = start + jnp.floor((count + (MOE_TILE - 1)) * (1.0 / MOE_TILE)) * MOE_TILE
        tile_group = tile_group + jnp.where(start <= tile_start, 1.0, 0.0)
    pos_ref[...] = pos.astype(jnp.int32)
    row8 = lax.broadcasted_iota(jnp.int32, (8, LANES), 0)
    info = jnp.where(row8 == 0, jnp.minimum(tile_group, N_EXPERT_GROUPS - 1.0),
                     jnp.where(row8 == 1, start * (1.0 / MOE_TILE), 0.0))
    tile_ref[...] = info.astype(jnp.int32)


def _moe_plan(grp):
    rows = grp.shape[0]
    return pl.pallas_call(
        _moe_plan_kernel,
        out_shape=(jax.ShapeDtypeStruct((rows, LANES), jnp.int32),
                   jax.ShapeDtypeStruct((8, LANES), jnp.int32)),
        name="moe_plan",
    )(grp)


def _move_rows_kernel(pos_ref, src_ref, *refs, scatter):
    dst_ref, sem = refs[-2:]
    t = pl.program_id(0)
    base = t * MOVE_ROWS

    def issue(r, carry):
        for u in range(MOVE_UNROLL):
            n = base + r * MOVE_UNROLL + u
            p = pos_ref[n]
            src, dst = (n, p) if scatter else (p, n)
            pltpu.make_async_copy(src_ref.at[pl.ds(src, 1)], dst_ref.at[pl.ds(dst, 1)], sem).start()
        return carry

    lax.fori_loop(0, MOVE_ROWS // MOVE_UNROLL, issue, 0)

    def wait_one_step():
        pltpu.make_async_copy(src_ref.at[pl.ds(0, MOVE_ROWS)], dst_ref.at[pl.ds(0, MOVE_ROWS)],
                              sem).wait()

    @pl.when(t > 0)
    def _():
        wait_one_step()

    @pl.when(t == pl.num_programs(0) - 1)
    def _():
        wait_one_step()


def _move_rows(pos, src, out_rows, scatter, name):
    n = pos.shape[0]
    out_shape = jax.ShapeDtypeStruct((out_rows, src.shape[1]), src.dtype)
    extra = [jnp.zeros(out_shape.shape, out_shape.dtype)] if scatter else []
    any_space = pl.BlockSpec(memory_space=pl.ANY)
    return pl.pallas_call(
        functools.partial(_move_rows_kernel, scatter=scatter),
        out_shape=out_shape,
        grid_spec=pltpu.PrefetchScalarGridSpec(
            num_scalar_prefetch=1,
            grid=(n // MOVE_ROWS,),
            in_specs=[any_space] * (1 + len(extra)),
            out_specs=any_space,
            scratch_shapes=[pltpu.SemaphoreType.DMA(())],
        ),
        input_output_aliases={2: 0} if scatter else {},
        compiler_params=pltpu.CompilerParams(dimension_semantics=("arbitrary",)),
        name=name,
    )(pos, src, *extra)


def _moe_experts_kernel(tile_ref, xs_ref, wg_ref, wu_ref, wd_ref, g2_ref, b2_ref, y_ref, *, alpha):
    t = pl.program_id(0)
    d = y_ref.shape[1]

    @pl.when(t < tile_ref[1, 0])
    def _():
        x = xs_ref[:, :d]
        xb = x.astype(BF16)
        comb = xs_ref[:, d:]
        lane = lax.broadcasted_iota(jnp.int32, comb.shape, 1)
        first = tile_ref[0, t] * EXPERTS_PER_GROUP
        acc = jnp.zeros(x.shape, F32)
        for k in range(EXPERTS_PER_GROUP):
            hg = _dot(xb, wg_ref[k])
            hu = _dot(xb, wu_ref[k])
            c_k = jnp.sum(jnp.where(lane == first + k, comb, 0.0), axis=1, keepdims=True)
            h = hg * _sigmoid(hg) * hu * c_k
            acc = acc + _dot(h.astype(BF16), wd_ref[k])
        y_ref[...] = _layer_norm(alpha * x + acc, g2_ref[...], b2_ref[...])

    @pl.when(t >= tile_ref[1, 0])
    def _():
        y_ref[...] = jnp.zeros_like(y_ref)


def _moe_experts(tile_info, xs, w_g, w_u, w_d, ln_g, ln_b, alpha):
    ns, wide = xs.shape
    d = wide - LANES
    f = w_g.shape[2]
    per_group = (N_EXPERT_GROUPS, EXPERTS_PER_GROUP)
    group_w = lambda t, ti: (ti[0, t], 0, 0, 0)
    const2 = lambda t, ti: (0, 0)
    return pl.pallas_call(
        functools.partial(_moe_experts_kernel, alpha=alpha),
        out_shape=jax.ShapeDtypeStruct((ns, d), F32),
        grid_spec=pltpu.PrefetchScalarGridSpec(
            num_scalar_prefetch=1,
            grid=(ns // MOE_TILE,),
            in_specs=[
                pl.BlockSpec((MOE_TILE, wide), lambda t, ti: (t, 0)),
                pl.BlockSpec((None, EXPERTS_PER_GROUP, d, f), group_w),
                pl.BlockSpec((None, EXPERTS_PER_GROUP, d, f), group_w),
                pl.BlockSpec((None, EXPERTS_PER_GROUP, f, d), group_w),
                pl.BlockSpec(ln_g.shape, const2),
                pl.BlockSpec(ln_b.shape, const2),
            ],
            out_specs=pl.BlockSpec((MOE_TILE, d), lambda t, ti: (t, 0)),
        ),
        compiler_params=pltpu.CompilerParams(
            dimension_semantics=("arbitrary",), vmem_limit_bytes=VMEM_LIMIT),
        name="moe_experts",
    )(tile_info, xs, w_g.reshape(per_group + (d, f)), w_u.reshape(per_group + (d, f)),
      w_d.reshape(per_group + (f, d)), ln_g, ln_b)


def _s_proj_kernel(x_ref, w_ref, cos_ref, sin_ref, h_ref):
    c = pl.program_id(0)
    h = _dot(x_ref[...], w_ref[...], HIGHEST)
    rows = h.shape[0]
    lane = lax.broadcasted_iota(jnp.int32, (rows, LANES), 1)
    first_half = (lane & 32) == 0
    rotary = c < 2
    for j in range(h.shape[1] // LANES):
        sl = slice(j * LANES, (j + 1) * LANES)
        hc = h[:, sl]
        h_ref[:, sl] = jnp.where(rotary, _rope_chunk(hc, cos_ref[...], sin_ref[...], first_half), hc)


def _s_proj(x, w_in, cos, sin):
    rows, d = x.shape
    width = w_in.shape[1]
    chunk = ATTN_WIDTH
    return pl.pallas_call(
        _s_proj_kernel,
        out_shape=jax.ShapeDtypeStruct((rows, width), F32),
        grid=(width // chunk,),
        in_specs=[
            pl.BlockSpec((rows, d), lambda c: (0, 0)),
            pl.BlockSpec((d, chunk), lambda c: (0, c)),
            pl.BlockSpec((1, LANES), lambda c: (0, 0)),
            pl.BlockSpec((1, LANES), lambda c: (0, 0)),
        ],
        out_specs=pl.BlockSpec((rows, chunk), lambda c: (0, c)),
        compiler_params=pltpu.CompilerParams(
            dimension_semantics=("arbitrary",), vmem_limit_bytes=VMEM_LIMIT),
        name="s_proj",
    )(x, w_in, cos, sin)


PAGES_PER_STEP = 32


def _s_scores_kernel(pt_ref, q_ref, *refs, page_size, n_blocks):
    page_refs = refs[:PAGES_PER_STEP]
    sc_ref, sel_ref, gs_ref = refs[PAGES_PER_STEP:]
    c = pl.program_id(1)
    pages_per_block = MOBA_BLOCK // page_size
    blocks_per_step = PAGES_PER_STEP // pages_per_block
    q = q_ref[...] * (HEAD_DIM ** -0.5)
    for r in range(blocks_per_step):
        tot = jnp.zeros((N_HEADS, page_size), F32)
        for pp in range(pages_per_block):
            p = r * pages_per_block + pp
            s = jnp.sum(page_refs[p][...] * q, axis=1)
            sc_ref[p] = s
            tot = tot + s
        gs_ref[c * blocks_per_step + r] = jnp.sum(tot, axis=1, keepdims=True)

    @pl.when(c == pl.num_programs(1) - 1)
    def _():
        g = gs_ref[...]
        blk_f = lax.broadcasted_iota(jnp.int32, g.shape, 0).astype(F32)
        for t in range(MOBA_TOPK):
            mx = jnp.max(g, axis=0, keepdims=True)
            idx = jnp.min(jnp.where(g == mx, blk_f, float(n_blocks)), axis=0, keepdims=True)
            sel_ref[t:t + 1] = idx.astype(jnp.int32)
            g = jnp.where(blk_f == idx, -jnp.inf, g)


def _s_scores(page_table, q4, cache_kt, n_blocks):
    db, n_pages = page_table.shape
    page_size = cache_kt.shape[3]
    page_block = (None, N_HEADS, HEAD_DIM, page_size)

    def page_spec(r):
        return pl.BlockSpec(page_block, lambda b, c, pt: (pt[b, c * PAGES_PER_STEP + r], 0, 0, 0))

    return pl.pallas_call(
        functools.partial(_s_scores_kernel, page_size=page_size, n_blocks=n_blocks),
        out_shape=(jax.ShapeDtypeStruct((db, n_pages, N_HEADS, page_size), F32),
                   jax.ShapeDtypeStruct((db, MOBA_TOPK, N_HEADS, 1), jnp.int32)),
        grid_spec=pltpu.PrefetchScalarGridSpec(
            num_scalar_prefetch=1,
            grid=(db, n_pages // PAGES_PER_STEP),
            in_specs=[pl.BlockSpec((None, N_HEADS, HEAD_DIM, 1), lambda b, c, pt: (b, 0, 0, 0))]
            + [page_spec(r) for r in range(PAGES_PER_STEP)],
            out_specs=(pl.BlockSpec((None, PAGES_PER_STEP, N_HEADS, page_size),
                                    lambda b, c, pt: (b, c, 0, 0)),
                       pl.BlockSpec((None, MOBA_TOPK, N_HEADS, 1), lambda b, c, pt: (b, 0, 0, 0))),
            scratch_shapes=[pltpu.VMEM((n_blocks, N_HEADS, 1), F32)],
        ),
        compiler_params=pltpu.CompilerParams(
            dimension_semantics=("arbitrary", "arbitrary"), vmem_limit_bytes=VMEM_LIMIT),
        name="s_scores",
    )(page_table, q4, *([cache_kt] * PAGES_PER_STEP))


def _s_attn_kernel(pt_ref, sel_ref, sc_ref, q_ref, kn_ref, vn_ref, cv_ref, o_ref, vbuf, sem,
                   *, pages_per_block):
    b = pl.program_id(0)
    slices_per_seq = N_HEADS * MOBA_TOPK * pages_per_block

    def slot_index(slot, h, t, r):
        return slot * slices_per_seq + (h * MOBA_TOPK + t) * pages_per_block + r

    def copies(seq, slot):
        out = []
        for h in range(N_HEADS):
            for t in range(MOBA_TOPK):
                blk = sel_ref[seq, t * N_HEADS + h]
                for r in range(pages_per_block):
                    page = pt_ref[seq, blk * pages_per_block + r]
                    out.append(pltpu.make_async_copy(
                        cv_ref.at[page, h], vbuf.at[slot_index(slot, h, t, r)], sem.at[slot]))
        return out

    @pl.when(b == 0)
    def _():
        for cp in copies(0, 0):
            cp.start()

    @pl.when(b + 1 < pl.num_programs(0))
    def _():
        for cp in copies(b + 1, (b + 1) % 2):
            cp.start()

    slot = b % 2
    for cp in copies(b, slot):
        cp.wait()

    scale = HEAD_DIM ** -0.5
    for h in range(N_HEADS):
        s_self = jnp.sum(q_ref[h] * kn_ref[h], axis=0, keepdims=True) * scale
        scores = []
        mx = s_self
        for t in range(MOBA_TOPK):
            blk = sel_ref[b, t * N_HEADS + h]
            for r in range(pages_per_block):
                sc = sc_ref[blk * pages_per_block + r, h:h + 1, :]
                scores.append(sc)
                mx = jnp.maximum(mx, jnp.max(sc, axis=1, keepdims=True))
        p_self = jnp.exp(s_self - mx)
        den = p_self
        acc = p_self * vn_ref[h]
        n = 0
        for t in range(MOBA_TOPK):
            for r in range(pages_per_block):
                p = jnp.exp(scores[n] - mx)
                n += 1
                den = den + jnp.sum(p, axis=1, keepdims=True)
                acc = acc + jnp.sum(vbuf[slot_index(slot, h, t, r)] * p, axis=1, keepdims=True)
        o_ref[h] = acc / den


def _s_attn(page_table, sel, scores, q4, kn4, vn4, cache_vt):
    db, n_pages = page_table.shape
    page_size = cache_vt.shape[3]
    pages_per_block = MOBA_BLOCK // page_size
    n_slices = 2 * N_HEADS * MOBA_TOPK * pages_per_block
    per_b = pl.BlockSpec((None, N_HEADS, HEAD_DIM, 1), lambda b, pt, sl: (b, 0, 0, 0))
    return pl.pallas_call(
        functools.partial(_s_attn_kernel, pages_per_block=pages_per_block),
        out_shape=jax.ShapeDtypeStruct((db, N_HEADS, HEAD_DIM, 1), F32),
        grid_spec=pltpu.PrefetchScalarGridSpec(
            num_scalar_prefetch=2,
            grid=(db,),
            in_specs=[pl.BlockSpec((None, n_pages, N_HEADS, page_size),
                                   lambda b, pt, sl: (b, 0, 0, 0)),
                      per_b, per_b, per_b, pl.BlockSpec(memory_space=pl.ANY)],
            out_specs=per_b,
            scratch_shapes=[pltpu.VMEM((n_slices, HEAD_DIM, page_size), F32),
                            pltpu.SemaphoreType.DMA((2,))],
        ),
        compiler_params=pltpu.CompilerParams(
            dimension_semantics=("arbitrary",), vmem_limit_bytes=VMEM_LIMIT),
        name="s_attn",
    )(page_table, sel, scores, q4, kn4, vn4, cache_vt)


def _s_tail_kernel(x_ref, attn_ref, u_ref, ga_ref, gb_ref, sp_ref, bg_ref, wpool_ref, pscale_ref,
                   wa_ref, wb_ref, wout_ref, g1_ref, b1_ref, wr_ref, br_ref, wg_ref, wu_ref,
                   wd_ref, g2_ref, b2_ref, y_ref, x1_ref, comb_ref, acc_ref, *, alpha):
    e = pl.program_id(0)
    rows = x_ref.shape[0]
    lane = lax.broadcasted_iota(jnp.int32, (rows, LANES), 1)

    @pl.when(e == 0)
    def _():
        u = u_ref[...]
        parts = []
        for g, w in enumerate(POOL_WINDOWS):
            sl = slice(g * POOL_GROUP_WIDTH, (g + 1) * POOL_GROUP_WIDTH)
            acc = u[:, sl]
            for back in range(1, w):
                acc = acc + sp_ref[POOL_STATE - back][:, sl]
            parts.append(_pool_group(acc, u[:, sl], float(w), wpool_ref[g], pscale_ref[:, sl],
                                     precision=HIGHEST))
        pooled = jnp.concatenate(parts, axis=1)
        a = _dot(attn_ref[...], wa_ref[...], HIGHEST)
        bb = _dot(pooled, wb_ref[...], HIGHEST)
        merged = _sigmoid(ga_ref[...] + bg_ref[0:1, :]) * a + _sigmoid(gb_ref[...] + bg_ref[1:2, :]) * bb
        y = alpha * x_ref[...] + _dot(merged, wout_ref[...], HIGHEST)
        x1 = _layer_norm(y, g1_ref[...], b1_ref[...])
        x1_ref[...] = x1
        comb_ref[...] = _route(_dot(x1, wr_ref[...], HIGHEST) + br_ref[...])
        acc_ref[...] = jnp.zeros_like(acc_ref)

    x1 = x1_ref[...]
    hg = _dot(x1, wg_ref[...], HIGHEST)
    hu = _dot(x1, wu_ref[...], HIGHEST)
    c_e = jnp.sum(jnp.where(lane == e, comb_ref[...], 0.0), axis=1, keepdims=True)
    h = hg * _sigmoid(hg) * hu * c_e
    acc_ref[...] += _dot(h, wd_ref[...], HIGHEST)

    @pl.when(e == N_EXPERTS - 1)
    def _():
        y_ref[...] = _layer_norm(alpha * x1_ref[...] + acc_ref[...], g2_ref[...], b2_ref[...])


def _s_tail(x, attn, u, ga, gb, sp_t, b_gate, w_pool, pool_scale, w_a, w_b, w_out, ln1_g, ln1_b,
            w_r, b_r, w_g, w_u, w_d, ln2_g, ln2_b, alpha):
    rows, d = x.shape
    f = w_g.shape[2]
    const2 = lambda e: (0, 0)
    const3 = lambda e: (0, 0, 0)
    full = lambda a: pl.BlockSpec(a.shape, const2 if a.ndim == 2 else const3)
    per_e = lambda blk: pl.BlockSpec((None,) + blk, lambda e: (e, 0, 0))
    small = [x, attn, u, ga, gb, sp_t, b_gate, w_pool, pool_scale, w_a, w_b, w_out, ln1_g, ln1_b,
             w_r, b_r]
    return pl.pallas_call(
        functools.partial(_s_tail_kernel, alpha=alpha),
        out_shape=jax.ShapeDtypeStruct((rows, d), F32),
        grid=(N_EXPERTS,),
        in_specs=[full(a) for a in small]
        + [per_e((d, f)), per_e((d, f)), per_e((f, d)), full(ln2_g), full(ln2_b)],
        out_specs=pl.BlockSpec((rows, d), const2),
        scratch_shapes=[pltpu.VMEM((rows, d), F32), pltpu.VMEM((rows, LANES), F32),
                        pltpu.VMEM((rows, d), F32)],
        compiler_params=pltpu.CompilerParams(
            dimension_semantics=("arbitrary",), vmem_limit_bytes=VMEM_LIMIT),
        name="s_tail",
    )(*small, w_g, w_u, w_d, ln2_g, ln2_b)


def _rope_tables(pos):
    inv_freq = 1.0 / (ROPE_THETA ** (jnp.arange(0, HEAD_DIM, 2, dtype=F32) / HEAD_DIM))
    ang = pos.astype(F32)[:, None] * inv_freq[None, :]
    cos = jnp.cos(ang)
    sin = jnp.sin(ang)
    cos_t = jnp.tile(cos, (1, LANES // (HEAD_DIM // 2)))
    sin_t = jnp.tile(jnp.concatenate([-sin, sin], axis=1), (1, LANES // HEAD_DIM))
    return cos_t, sin_t


def kernel(x_prompt, x_sample, cache_k, cache_v, state_pool, page_table, w_in, w_pool, pool_scale,
           w_branch_a, w_branch_b, b_gate, w_out, ln1_g, ln1_b, w_group_router, b_group_router,
           w_expert_router, b_expert_router, w_e_gate, w_e_up, w_e_down, ln2_g, ln2_b):
    depth = w_in.shape[0]
    assert depth == 1 and x_sample.shape[1] == 1
    alpha = (2 * depth) ** 0.25
    b, s, d = x_prompt.shape
    db = x_sample.shape[0]
    n_pages = page_table.shape[1]
    page_size = cache_k.shape[2]
    past_len = n_pages * page_size
    n_blocks = past_len // MOBA_BLOCK
    assert past_len % MOBA_BLOCK == 0 and s % MOBA_BLOCK == 0

    w_in0 = w_in[0]
    qkvu_cols = 3 * ATTN_WIDTH + POOL_WIDTH
    row2 = lambda a: a.reshape(1, -1)
    w_r = jnp.concatenate(
        [w_group_router[0], jnp.transpose(w_expert_router[0], (1, 0, 2)).reshape(d, N_EXPERTS),
         jnp.zeros((d, LANES - N_EXPERT_GROUPS - N_EXPERTS), F32)], axis=1)
    b_r = jnp.concatenate(
        [b_group_router[0], b_expert_router[0].reshape(-1),
         jnp.zeros((LANES - N_EXPERT_GROUPS - N_EXPERTS,), F32)]).reshape(1, LANES)

    cos_p, sin_p = _rope_tables(jnp.arange(s, dtype=jnp.int32))
    tm_a = 512
    qt, k, v, u, kb, vt, km = _qkvu(x_prompt.reshape(b * s, d), w_in0[:, :qkvu_cols].astype(BF16),
                                    cos_p, sin_p, s, tm_a)
    nb = s // MOBA_BLOCK
    attn = _attn(qt, kb.reshape(b, s, ATTN_WIDTH), vt.reshape(b, nb, ATTN_WIDTH, MOBA_BLOCK),
                 km.reshape(b, nb, ATTN_WIDTH))
    u3 = u.reshape(b, s, POOL_WIDTH)
    x1c, grp = _mix(x_prompt, attn, u3, w_in0[:, qkvu_cols:].astype(BF16), b_gate[0],
                    w_pool[0].astype(BF16), row2(pool_scale[0]), w_branch_a[0].astype(BF16),
                    w_branch_b[0].astype(BF16), w_out[0].astype(BF16), row2(ln1_g[0]),
                    row2(ln1_b[0]), w_r, b_r, 256, alpha)
    n_tok = b * s
    pos2d, tile_info = _moe_plan(grp.reshape(n_tok // LANES, LANES))
    pos = pos2d.reshape(n_tok)
    xs = _move_rows(pos, x1c.reshape(n_tok, d + LANES), n_tok + N_EXPERT_GROUPS * MOE_TILE, True,
                    "moe_scatter")
    ys = _moe_experts(tile_info, xs, w_e_gate[0].astype(BF16), w_e_up[0].astype(BF16),
                      w_e_down[0].astype(BF16), row2(ln2_g[0]), row2(ln2_b[0]), alpha)
    y_p = _move_rows(pos, ys, n_tok, False, "moe_unsort")

    cos_s, sin_s = _rope_tables(jnp.full((1,), past_len, jnp.int32))
    x_s = x_sample.reshape(db, d)
    h_s = _s_proj(x_s, w_in0, cos_s, sin_s)
    q_s = h_s[:, :ATTN_WIDTH]
    k_s = h_s[:, ATTN_WIDTH:2 * ATTN_WIDTH]
    v_s = h_s[:, 2 * ATTN_WIDTH:3 * ATTN_WIDTH]
    u_s = h_s[:, 3 * ATTN_WIDTH:qkvu_cols]
    ga_s = h_s[:, qkvu_cols:qkvu_cols + d]
    gb_s = h_s[:, qkvu_cols + d:]
    cache_kt = jnp.transpose(cache_k[0], (0, 2, 3, 1))
    cache_vt = jnp.transpose(cache_v[0], (0, 2, 3, 1))
    col4 = lambda a: a.reshape(db, N_HEADS, HEAD_DIM, 1)
    scores, sel = _s_scores(page_table, col4(q_s), cache_kt, n_blocks)
    sel2 = sel.reshape(db, MOBA_TOPK * N_HEADS)
    attn_s = _s_attn(page_table, sel2, scores, col4(q_s), col4(k_s), col4(v_s),
                     cache_vt).reshape(db, ATTN_WIDTH)
    sp = state_pool[0]
    y_s = _s_tail(x_s, attn_s, u_s, ga_s, gb_s, jnp.transpose(sp, (1, 0, 2)), b_gate[0], w_pool[0],
                  row2(pool_scale[0]), w_branch_a[0], w_branch_b[0], w_out[0], row2(ln1_g[0]),
                  row2(ln1_b[0]), w_r, b_r, w_e_gate[0], w_e_up[0], w_e_down[0], row2(ln2_g[0]),
                  row2(ln2_b[0]), alpha)

    heads = (N_HEADS, HEAD_DIM)
    return (
        y_p.reshape(b, s, d),
        y_s.reshape(db, 1, d),
        jnp.transpose(k.reshape((b,) + heads + (s,)), (0, 3, 1, 2))[None],
        jnp.transpose(v.reshape((b,) + heads + (s,)), (0, 3, 1, 2))[None],
        u3[:, s - POOL_STATE:, :][None],
        k_s.reshape((1, db, 1) + heads),
        v_s.reshape((1, db, 1) + heads),
        jnp.concatenate([sp[:, 1:, :], u_s[:, None, :]], axis=1)[None],
    )
```

```python
import functools

import jax
import jax.numpy as jnp
from jax import lax
from jax.experimental import pallas as pl
from jax.experimental.pallas import tpu as pltpu

F32 = jnp.float32
BF16 = jnp.bfloat16
HIGHEST = lax.Precision.HIGHEST

N_HEADS = 8
HEAD_DIM = 64
ATTN_WIDTH = N_HEADS * HEAD_DIM
MOBA_BLOCK = 256
MOBA_TOPK = 3
ROPE_THETA = 10000.0
POOL_WINDOWS = (2, 4, 8, 16)
POOL_GROUP_WIDTH = 128
POOL_WIDTH = 512
POOL_STATE = 15
N_EXPERT_GROUPS = 4
EXPERTS_PER_GROUP = 4
N_EXPERTS = 16
LN_EPS = 1e-5
LANES = 128
HALO_ROWS = 16
MASK_BIAS = -1e30
VMEM_LIMIT = 56 * 1024 * 1024

NT_DIMS = (((1,), (1,)), ((), ()))


def _dot(a, b, precision=None):
    return jnp.dot(a, b, preferred_element_type=F32, precision=precision)


def _dot_nt(a, b, precision=None):
    return lax.dot_general(a, b, NT_DIMS, preferred_element_type=F32, precision=precision)


def _layer_norm(y, g, b):
    mu = jnp.mean(y, axis=-1, keepdims=True)
    var = jnp.mean(jnp.square(y - mu), axis=-1, keepdims=True)
    return (y - mu) * lax.rsqrt(var + LN_EPS) * g + b


def _sigmoid(x):
    return 1.0 / (1.0 + jnp.exp(-x))


def _rope_chunk(x, cos, sin_signed, first_half):
    partner = jnp.where(first_half, pltpu.roll(x, 96, 1), pltpu.roll(x, 32, 1))
    return x * cos + partner * sin_signed


def _route_weights(logit):
    g = [logit(k) for k in range(N_EXPERT_GROUPS)]
    gmax = jnp.maximum(jnp.maximum(g[0], g[1]), jnp.maximum(g[2], g[3]))
    den = sum(jnp.exp(gk - gmax) for gk in g)
    g_w = 1.0 / den
    is_g = []
    taken = jnp.zeros_like(gmax)
    for k in range(N_EXPERT_GROUPS):
        hit = jnp.where(g[k] == gmax, 1.0, 0.0) * (1.0 - taken)
        is_g.append(hit)
        taken = taken + hit
    e = []
    for k in range(EXPERTS_PER_GROUP):
        col = jnp.zeros_like(gmax)
        for gi in range(N_EXPERT_GROUPS):
            col = jnp.where(is_g[gi] > 0.5, logit(N_EXPERT_GROUPS + gi * EXPERTS_PER_GROUP + k), col)
        e.append(col)
    v1 = jnp.maximum(jnp.maximum(e[0], e[1]), jnp.maximum(e[2], e[3]))
    first = []
    taken = jnp.zeros_like(v1)
    for k in range(EXPERTS_PER_GROUP):
        hit = jnp.where(e[k] == v1, 1.0, 0.0) * (1.0 - taken)
        first.append(hit)
        taken = taken + hit
    e2 = [jnp.where(first[k] > 0.5, -jnp.inf, e[k]) for k in range(EXPERTS_PER_GROUP)]
    v2 = jnp.maximum(jnp.maximum(e2[0], e2[1]), jnp.maximum(e2[2], e2[3]))
    second = []
    taken = jnp.zeros_like(v2)
    for k in range(EXPERTS_PER_GROUP):
        hit = jnp.where(e2[k] == v2, 1.0, 0.0) * (1.0 - taken)
        second.append(hit)
        taken = taken + hit
    t = jnp.exp(v2 - v1)
    w1 = 1.0 / (1.0 + t)
    w2 = t * w1
    return [is_g[gi] * (first[k] * w1 + second[k] * w2) * g_w
            for gi in range(N_EXPERT_GROUPS) for k in range(EXPERTS_PER_GROUP)]


def _route(logits):
    lane = lax.broadcasted_iota(jnp.int32, logits.shape, 1)
    comb = jnp.zeros(logits.shape, F32)
    for e, col in enumerate(_route_weights(lambda k: logits[:, k:k + 1])):
        comb = jnp.where(lane == e, col, comb)
    return comb


def _route_rows(logits_t):
    tokens = logits_t.shape[1]
    row = lax.broadcasted_iota(jnp.int32, (N_EXPERTS, tokens), 0)
    comb = jnp.zeros((N_EXPERTS, tokens), F32)
    for e, r in enumerate(_route_weights(lambda k: logits_t[k:k + 1, :])):
        comb = jnp.where(row == e, r, comb)
    return jnp.concatenate([comb, jnp.zeros((logits_t.shape[0] - N_EXPERTS, tokens), F32)], axis=0)


Q_SCALE = 1.4426950408889634 * HEAD_DIM ** -0.5


def _qkvu_kernel(x_ref, w_ref, cos_ref, sin_ref, qt_ref, k_ref, v_ref, u_ref, kb_ref, vt_ref,
                 km_ref, *, tm):
    xb = x_ref[...].astype(BF16)
    cos = cos_ref[...]
    sin = sin_ref[...]
    lane = lax.broadcasted_iota(jnp.int32, (tm, LANES), 1)
    first_half = (lane & 32) == 0
    hq = _dot(xb, w_ref[:, 0:ATTN_WIDTH])
    hk = _dot(xb, w_ref[:, ATTN_WIDTH:2 * ATTN_WIDTH])
    hv = _dot(xb, w_ref[:, 2 * ATTN_WIDTH:3 * ATTN_WIDTH])
    for c in range(ATTN_WIDTH // LANES):
        sl = slice(c * LANES, (c + 1) * LANES)
        qc = _rope_chunk(hq[:, sl], cos, sin, first_half) * Q_SCALE
        kc = _rope_chunk(hk[:, sl], cos, sin, first_half)
        kb_ref[:, sl] = kc.astype(BF16)
        k_ref[sl, :] = kc.T
        vct = hv[:, sl].T
        v_ref[sl, :] = vct
        qct = qc.T.astype(BF16)
        for r in range(tm // MOBA_BLOCK):
            rows = slice(r * MOBA_BLOCK, (r + 1) * MOBA_BLOCK)
            km_ref[r:r + 1, sl] = jnp.sum(kc[rows], axis=0, keepdims=True) * (1.0 / MOBA_BLOCK)
            qt_ref[r, sl, :] = qct[:, rows]
            vt_ref[r, sl, :] = vct[:, rows].astype(BF16)
    u_ref[...] = _dot(xb, w_ref[:, 3 * ATTN_WIDTH:])


def _qkvu(x2d, w_qkvu, cos, sin, seq, tm):
    n, d = x2d.shape
    s_tiles = seq // tm
    nblk = tm // MOBA_BLOCK
    row = lambda i: (i, 0)
    blocked = lambda i: (i, 0, 0)
    featmajor = lambda i: (i // s_tiles, 0, i % s_tiles)
    out_shape = (
        jax.ShapeDtypeStruct((n // MOBA_BLOCK, ATTN_WIDTH, MOBA_BLOCK), BF16),
        jax.ShapeDtypeStruct((n // seq, ATTN_WIDTH, seq), F32),
        jax.ShapeDtypeStruct((n // seq, ATTN_WIDTH, seq), F32),
        jax.ShapeDtypeStruct((n, POOL_WIDTH), F32),
        jax.ShapeDtypeStruct((n, ATTN_WIDTH), BF16),
        jax.ShapeDtypeStruct((n // MOBA_BLOCK, ATTN_WIDTH, MOBA_BLOCK), BF16),
        jax.ShapeDtypeStruct((n // tm, nblk, ATTN_WIDTH), F32),
    )
    return pl.pallas_call(
        functools.partial(_qkvu_kernel, tm=tm),
        out_shape=out_shape,
        grid=(n // tm,),
        in_specs=[
            pl.BlockSpec((tm, d), row),
            pl.BlockSpec((d, 4 * ATTN_WIDTH), lambda i: (0, 0)),
            pl.BlockSpec((tm, LANES), lambda i: (i % s_tiles, 0)),
            pl.BlockSpec((tm, LANES), lambda i: (i % s_tiles, 0)),
        ],
        out_specs=(
            pl.BlockSpec((nblk, ATTN_WIDTH, MOBA_BLOCK), blocked),
            pl.BlockSpec((None, ATTN_WIDTH, tm), featmajor),
            pl.BlockSpec((None, ATTN_WIDTH, tm), featmajor),
            pl.BlockSpec((tm, POOL_WIDTH), row),
            pl.BlockSpec((tm, ATTN_WIDTH), row),
            pl.BlockSpec((nblk, ATTN_WIDTH, MOBA_BLOCK), blocked),
            pl.BlockSpec((None, nblk, ATTN_WIDTH), blocked),
        ),
        compiler_params=pltpu.CompilerParams(
            dimension_semantics=("arbitrary",), vmem_limit_bytes=VMEM_LIMIT),
        name="qkvu",
    )(x2d, w_qkvu, cos, sin)


BIAS_ROWS = 128
SUM_ROWS = 16
ATTN_PAIRS = 4


def _attn_kernel(qt_ref, k_ref, vt_ref, km_ref, o_ref, qa_ref, acc_ref, sa_ref, sb_ref):
    i = pl.program_id(2)
    n_past = km_ref.shape[0]
    n_heads = 2 * ATTN_PAIRS
    feat = lax.broadcasted_iota(jnp.int32, (LANES, MOBA_BLOCK), 0)
    head0 = feat < HEAD_DIM
    blk_id = lax.broadcasted_iota(jnp.int32, (n_past, MOBA_BLOCK), 0)
    blk_f = blk_id.astype(F32)
    past = blk_id < i
    key_id = lax.broadcasted_iota(jnp.int32, (MOBA_BLOCK, MOBA_BLOCK), 0)
    qry_id = lax.broadcasted_iota(jnp.int32, (MOBA_BLOCK, MOBA_BLOCK), 1)
    causal = key_id <= qry_id
    lane = lax.broadcasted_iota(jnp.int32, (MOBA_BLOCK, LANES), 1)
    ones_rows = jnp.ones((SUM_ROWS, MOBA_BLOCK), BF16)
    bias_pad = jnp.zeros((BIAS_ROWS - n_past, MOBA_BLOCK), BF16)
    own = pl.multiple_of(i * MOBA_BLOCK, MOBA_BLOCK)

    def slab(pair):
        return slice(pair * LANES, (pair + 1) * LANES)

    def vt_aug(blk, pair):
        return jnp.concatenate([vt_ref[blk, slab(pair), :], ones_rows], axis=0)

    def consume_refill(s_ref, ms, blk, nxt):
        off = pl.multiple_of(nxt * MOBA_BLOCK, MOBA_BLOCK)
        onehot = jnp.where(lane == nxt, 1.0, 0.0).astype(BF16)
        out = []
        for h in range(n_heads):
            s = s_ref[h]
            m_new = jnp.maximum(ms[h], jnp.max(s, axis=0, keepdims=True))
            alpha = jnp.exp2(ms[h] - m_new)
            p = jnp.exp2(s - m_new)
            acc_ref[h] = alpha * acc_ref[h] + _dot(vt_aug(blk, h // 2), p.astype(BF16))
            out.append(m_new)
            ka = jnp.concatenate([k_ref[pl.ds(off, MOBA_BLOCK), slab(h // 2)], onehot], axis=1)
            s_ref[h] = _dot(ka, qa_ref[h])
        return tuple(out)

    for h in range(n_heads):
        pair = h // 2
        qt = qt_ref[slab(pair), :]
        qh = jnp.where(head0 if h % 2 == 0 else jnp.logical_not(head0), qt, jnp.zeros_like(qt))
        gate = _dot(km_ref[:, slab(pair)], qh.astype(F32), HIGHEST)
        g = jnp.where(past, gate, -jnp.inf)
        sel = jnp.zeros(g.shape, F32)
        for _ in range(MOBA_TOPK):
            mx = jnp.max(g, axis=0, keepdims=True)
            idx = jnp.min(jnp.where(g == mx, blk_f, float(n_past)), axis=0, keepdims=True)
            pick = blk_f == idx
            sel = jnp.where(pick, 1.0, sel)
            g = jnp.where(pick, -jnp.inf, g)
        bias = jnp.where(jnp.logical_and(sel > 0.5, past), 0.0, MASK_BIAS).astype(BF16)
        qa_ref[h] = jnp.concatenate([qh, bias, bias_pad], axis=0)
        kd = k_ref[pl.ds(own, MOBA_BLOCK), slab(pair)]
        sa_ref[h] = jnp.where(causal, _dot(kd, qh), -jnp.inf)
        acc_ref[h] = jnp.zeros(acc_ref.shape[1:], F32)
    m0 = jnp.full((1, MOBA_BLOCK), MASK_BIAS, F32)

    last_past = jnp.maximum(i - 1, 0)
    off0 = pl.multiple_of(0 * MOBA_BLOCK, MOBA_BLOCK)
    onehot0 = jnp.where(lane == 0, 1.0, 0.0).astype(BF16)
    for h in range(n_heads):
        ka = jnp.concatenate([k_ref[pl.ds(off0, MOBA_BLOCK), slab(h // 2)], onehot0], axis=1)
        sb_ref[h] = _dot(ka, qa_ref[h])

    def pair_body(u, ms):
        ms = consume_refill(sa_ref, ms, jnp.where(u == 0, i, 2 * u - 1),
                            jnp.minimum(2 * u + 1, last_past))
        return consume_refill(sb_ref, ms, 2 * u, jnp.minimum(2 * u + 2, last_past))

    ms = lax.fori_loop(0, (i + 1) // 2, pair_body, (m0,) * n_heads)

    @pl.when((i + 1) % 2 == 1)
    def _():
        blk = jnp.where(i == 0, i, i - 1)
        for h in range(n_heads):
            s = sa_ref[h]
            m_new = jnp.maximum(ms[h], jnp.max(s, axis=0, keepdims=True))
            alpha = jnp.exp2(ms[h] - m_new)
            p = jnp.exp2(s - m_new)
            acc_ref[h] = alpha * acc_ref[h] + _dot(vt_aug(blk, h // 2), p.astype(BF16))

    for pair in range(ATTN_PAIRS):
        a0 = acc_ref[2 * pair]
        a1 = acc_ref[2 * pair + 1]
        ot = jnp.where(head0, a0[:LANES] / a0[LANES:LANES + 1], a1[:LANES] / a1[LANES:LANES + 1])
        o_ref[:, slab(pair)] = ot.T.astype(BF16)


def _attn(qt, kb, vt, km):
    b, s, _ = kb.shape
    nb = s // MOBA_BLOCK
    width = ATTN_PAIRS * LANES
    return pl.pallas_call(
        _attn_kernel,
        out_shape=jax.ShapeDtypeStruct((b, s, ATTN_WIDTH), BF16),
        grid=(b, ATTN_WIDTH // width, nb),
        in_specs=[
            pl.BlockSpec((None, width, MOBA_BLOCK), lambda bi, p, i: (bi * nb + i, p, 0)),
            pl.BlockSpec((None, s, width), lambda bi, p, i: (bi, 0, p)),
            pl.BlockSpec((None, nb, width, MOBA_BLOCK), lambda bi, p, i: (bi, 0, p, 0)),
            pl.BlockSpec((None, nb, width), lambda bi, p, i: (bi, 0, p)),
        ],
        out_specs=pl.BlockSpec((None, MOBA_BLOCK, width), lambda bi, p, i: (bi, i, p)),
        scratch_shapes=[
            pltpu.VMEM((2 * ATTN_PAIRS, LANES + BIAS_ROWS, MOBA_BLOCK), BF16),
            pltpu.VMEM((2 * ATTN_PAIRS, LANES + SUM_ROWS, MOBA_BLOCK), F32),
            pltpu.VMEM((2 * ATTN_PAIRS, MOBA_BLOCK, MOBA_BLOCK), F32),
            pltpu.VMEM((2 * ATTN_PAIRS, MOBA_BLOCK, MOBA_BLOCK), F32),
        ],
        compiler_params=pltpu.CompilerParams(
            dimension_semantics=("arbitrary", "arbitrary", "arbitrary"),
            vmem_limit_bytes=VMEM_LIMIT),
        name="moba_attn",
    )(qt, kb, vt, km)


def _pool_group(window_sum, u_g, count, w_pool_g, scale_g, precision=None, cast=None):
    d = window_sum / count - u_g
    if cast is not None:
        d = d.astype(cast)
    return _dot(d, w_pool_g, precision) * scale_g


def _mix_kernel(x_ref, attn_ref, u_ref, halo_ref, wg_ref, bg_ref, wpool_ref, pscale_ref, wa_ref,
                wb_ref, wout_ref, g1_ref, b1_ref, x1_ref, z_ref, *, tm, alpha):
    i = pl.program_id(1)
    x = x_ref[...]
    xb = x.astype(BF16)
    u = u_ref[...]
    z_ref[0:HALO_ROWS, :] = jnp.where(i > 0, halo_ref[...], 0.0)
    z_ref[HALO_ROWS:, :] = u
    pos1 = (i * tm + 1 + lax.broadcasted_iota(jnp.int32, (tm, 1), 0)).astype(F32)
    parts = []
    for g, w in enumerate(POOL_WINDOWS):
        sl = slice(g * POOL_GROUP_WIDTH, (g + 1) * POOL_GROUP_WIDTH)
        acc = u[:, sl]
        for back in range(1, w):
            acc = acc + z_ref[HALO_ROWS - back:HALO_ROWS - back + tm, sl]
        count = jnp.minimum(float(w), pos1)
        parts.append(_pool_group(acc, u[:, sl], count, wpool_ref[g], pscale_ref[:, sl], cast=BF16))
    pooled = jnp.concatenate(parts, axis=1).astype(BF16)
    d = x.shape[1]
    ga = _dot(xb, wg_ref[:, :d]) + bg_ref[0:1, :]
    gb = _dot(xb, wg_ref[:, d:]) + bg_ref[1:2, :]
    a = _dot(attn_ref[...], wa_ref[...])
    bb = _dot(pooled, wb_ref[...])
    merged = _sigmoid(ga) * a + _sigmoid(gb) * bb
    y = alpha * x + _dot(merged.astype(BF16), wout_ref[...])
    x1_ref[...] = _layer_norm(y, g1_ref[...], b1_ref[...])


def _mix(x, attn, u, w_gates, b_gate, w_pool, pool_scale, w_a, w_b, w_out, ln_g, ln_b, tm, alpha):
    b, s, d = x.shape
    halo_per_tile = tm // HALO_ROWS
    tile = lambda bi, i: (bi, i, 0)
    const2 = lambda bi, i: (0, 0)
    return pl.pallas_call(
        functools.partial(_mix_kernel, tm=tm, alpha=alpha),
        out_shape=jax.ShapeDtypeStruct((b, s, d), F32),
        grid=(b, s // tm),
        in_specs=[
            pl.BlockSpec((None, tm, d), tile),
            pl.BlockSpec((None, tm, ATTN_WIDTH), tile),
            pl.BlockSpec((None, tm, POOL_WIDTH), tile),
            pl.BlockSpec((None, HALO_ROWS, POOL_WIDTH),
                         lambda bi, i: (bi, jnp.maximum(i * halo_per_tile - 1, 0), 0)),
            pl.BlockSpec(w_gates.shape, const2),
            pl.BlockSpec(b_gate.shape, const2),
            pl.BlockSpec(w_pool.shape, lambda bi, i: (0, 0, 0)),
            pl.BlockSpec(pool_scale.shape, const2),
            pl.BlockSpec(w_a.shape, const2),
            pl.BlockSpec(w_b.shape, const2),
            pl.BlockSpec(w_out.shape, const2),
            pl.BlockSpec(ln_g.shape, const2),
            pl.BlockSpec(ln_b.shape, const2),
        ],
        out_specs=pl.BlockSpec((None, tm, d), tile),
        scratch_shapes=[pltpu.VMEM((HALO_ROWS + tm, POOL_WIDTH), F32)],
        compiler_params=pltpu.CompilerParams(
            dimension_semantics=("arbitrary", "arbitrary"), vmem_limit_bytes=VMEM_LIMIT),
        name="mix",
    )(x, attn, u, u, w_gates, b_gate, w_pool, pool_scale, w_a, w_b, w_out, ln_g, ln_b)


def _moe_kernel(x1_ref, wrh_ref, wrl_ref, br_ref, wg_ref, wu_ref, wd_ref, g2_ref, b2_ref, y_ref,
                xb_ref, comb_ref, acc_ref, *, alpha):
    e = pl.program_id(1)
    tm = x1_ref.shape[0]
    lane = lax.broadcasted_iota(jnp.int32, (tm, LANES), 1)

    @pl.when(e == 0)
    def _():
        x1 = x1_ref[...]
        xh = x1.astype(BF16)
        xb_ref[...] = xh
        xl = (x1 - xh.astype(F32)).astype(BF16)
        logits = (_dot(xh, wrh_ref[...]) + _dot(xl, wrh_ref[...]) + _dot(xh, wrl_ref[...])
                  + br_ref[...])
        comb_ref[...] = _route_rows(logits.T).T
        acc_ref[...] = jnp.zeros_like(acc_ref)

    xb = xb_ref[...]
    hg = _dot(xb, wg_ref[...])
    hu = _dot(xb, wu_ref[...])
    c_e = jnp.sum(jnp.where(lane == e, comb_ref[...], 0.0), axis=1, keepdims=True)
    h = hg * _sigmoid(hg) * hu * c_e
    acc_ref[...] += _dot(h.astype(BF16), wd_ref[...])

    @pl.when(e == N_EXPERTS - 1)
    def _():
        y = alpha * x1_ref[...] + acc_ref[...]
        y_ref[...] = _layer_norm(y, g2_ref[...], b2_ref[...])


def _moe(x1, w_r, b_r, w_g, w_u, w_d, ln_g, ln_b, tm, alpha):
    w_r_hi = w_r.astype(BF16)
    w_r_lo = (w_r - w_r_hi.astype(F32)).astype(BF16)
    n, d = x1.shape
    f = w_g.shape[2]
    tile = lambda i, e: (i, 0)
    const2 = lambda i, e: (0, 0)
    return pl.pallas_call(
        functools.partial(_moe_kernel, alpha=alpha),
        out_shape=jax.ShapeDtypeStruct((n, d), F32),
        grid=(n // tm, N_EXPERTS),
        in_specs=[
            pl.BlockSpec((tm, d), tile),
            pl.BlockSpec(w_r.shape, const2),
            pl.BlockSpec(w_r.shape, const2),
            pl.BlockSpec(b_r.shape, const2),
            pl.BlockSpec((None, d, f), lambda i, e: (e, 0, 0)),
            pl.BlockSpec((None, d, f), lambda i, e: (e, 0, 0)),
            pl.BlockSpec((None, f, d), lambda i, e: (e, 0, 0)),
            pl.BlockSpec(ln_g.shape, const2),
            pl.BlockSpec(ln_b.shape, const2),
        ],
        out_specs=pl.BlockSpec((tm, d), tile),
        scratch_shapes=[
            pltpu.VMEM((tm, d), BF16),
            pltpu.VMEM((tm, LANES), F32),
            pltpu.VMEM((tm, d), F32),
        ],
        compiler_params=pltpu.CompilerParams(
            dimension_semantics=("arbitrary", "arbitrary"), vmem_limit_bytes=VMEM_LIMIT),
        name="moe",
    )(x1, w_r_hi, w_r_lo, b_r, w_g, w_u, w_d, ln_g, ln_b)


def _s_proj_kernel(x_ref, w_ref, cos_ref, sin_ref, h_ref):
    c = pl.program_id(0)
    h = _dot(x_ref[...], w_ref[...], HIGHEST)
    rows = h.shape[0]
    lane = lax.broadcasted_iota(jnp.int32, (rows, LANES), 1)
    first_half = (lane & 32) == 0
    rotary = c < 2
    for j in range(h.shape[1] // LANES):
        sl = slice(j * LANES, (j + 1) * LANES)
        hc = h[:, sl]
        h_ref[:, sl] = jnp.where(rotary, _rope_chunk(hc, cos_ref[...], sin_ref[...], first_half), hc)


def _s_proj(x, w_in, cos, sin):
    rows, d = x.shape
    width = w_in.shape[1]
    chunk = ATTN_WIDTH
    return pl.pallas_call(
        _s_proj_kernel,
        out_shape=jax.ShapeDtypeStruct((rows, width), F32),
        grid=(width // chunk,),
        in_specs=[
            pl.BlockSpec((rows, d), lambda c: (0, 0)),
            pl.BlockSpec((d, chunk), lambda c: (0, c)),
            pl.BlockSpec((1, LANES), lambda c: (0, 0)),
            pl.BlockSpec((1, LANES), lambda c: (0, 0)),
        ],
        out_specs=pl.BlockSpec((rows, chunk), lambda c: (0, c)),
        compiler_params=pltpu.CompilerParams(
            dimension_semantics=("arbitrary",), vmem_limit_bytes=VMEM_LIMIT),
        name="s_proj",
    )(x, w_in, cos, sin)


PAGES_PER_STEP = 32


def _column(ref, b):
    x = ref[...]
    lane = lax.broadcasted_iota(jnp.int32, x.shape, 1)
    col = jnp.sum(jnp.where(lane == b, x, 0.0), axis=1, keepdims=True)
    return col.reshape(N_HEADS, HEAD_DIM, 1)


def _s_scores_kernel(pt_ref, q_ref, *refs, page_size, n_blocks):
    page_refs = refs[:PAGES_PER_STEP]
    sc_ref, sel_ref, gs_ref = refs[PAGES_PER_STEP:]
    c = pl.program_id(1)
    pages_per_block = MOBA_BLOCK // page_size
    blocks_per_step = PAGES_PER_STEP // pages_per_block
    q = _column(q_ref, pl.program_id(0)) * (HEAD_DIM ** -0.5)
    for r in range(blocks_per_step):
        tot = jnp.zeros((N_HEADS, page_size), F32)
        for pp in range(pages_per_block):
            p = r * pages_per_block + pp
            s = jnp.sum(page_refs[p][...] * q, axis=1)
            sc_ref[p] = s
            tot = tot + s
        gs_ref[c * blocks_per_step + r] = jnp.sum(tot, axis=1, keepdims=True)

    @pl.when(c == pl.num_programs(1) - 1)
    def _():
        g = gs_ref[...]
        blk_f = lax.broadcasted_iota(jnp.int32, g.shape, 0).astype(F32)
        for t in range(MOBA_TOPK):
            mx = jnp.max(g, axis=0, keepdims=True)
            idx = jnp.min(jnp.where(g == mx, blk_f, float(n_blocks)), axis=0, keepdims=True)
            sel_ref[t:t + 1] = idx.astype(jnp.int32)
            g = jnp.where(blk_f == idx, -jnp.inf, g)


def _s_scores(page_table, qt, cache_kt, n_blocks):
    db, n_pages = page_table.shape
    page_size = cache_kt.shape[3]
    page_block = (None, N_HEADS, HEAD_DIM, page_size)

    def page_spec(r):
        return pl.BlockSpec(page_block, lambda b, c, pt: (pt[b, c * PAGES_PER_STEP + r], 0, 0, 0))

    return pl.pallas_call(
        functools.partial(_s_scores_kernel, page_size=page_size, n_blocks=n_blocks),
        out_shape=(jax.ShapeDtypeStruct((db, n_pages, N_HEADS, page_size), F32),
                   jax.ShapeDtypeStruct((db, MOBA_TOPK, N_HEADS, 1), jnp.int32)),
        grid_spec=pltpu.PrefetchScalarGridSpec(
            num_scalar_prefetch=1,
            grid=(db, n_pages // PAGES_PER_STEP),
            in_specs=[pl.BlockSpec(qt.shape, lambda b, c, pt: (0, 0))]
            + [page_spec(r) for r in range(PAGES_PER_STEP)],
            out_specs=(pl.BlockSpec((None, PAGES_PER_STEP, N_HEADS, page_size),
                                    lambda b, c, pt: (b, c, 0, 0)),
                       pl.BlockSpec((None, MOBA_TOPK, N_HEADS, 1), lambda b, c, pt: (b, 0, 0, 0))),
            scratch_shapes=[pltpu.VMEM((n_blocks, N_HEADS, 1), F32)],
        ),
        compiler_params=pltpu.CompilerParams(
            dimension_semantics=("arbitrary", "arbitrary"), vmem_limit_bytes=VMEM_LIMIT),
        name="s_scores",
    )(page_table, qt, *([cache_kt] * PAGES_PER_STEP))


def _s_attn_kernel(pt_ref, sel_ref, sc_ref, q_ref, kn_ref, vn_ref, cv_ref, o_ref, vbuf, sem,
                   *, pages_per_block):
    b = pl.program_id(0)
    slices_per_seq = N_HEADS * MOBA_TOPK * pages_per_block

    def slot_index(slot, h, t, r):
        return slot * slices_per_seq + (h * MOBA_TOPK + t) * pages_per_block + r

    def copies(seq, slot):
        out = []
        for h in range(N_HEADS):
            for t in range(MOBA_TOPK):
                blk = sel_ref[seq, t * N_HEADS + h]
                for r in range(pages_per_block):
                    page = pt_ref[seq, blk * pages_per_block + r]
                    out.append(pltpu.make_async_copy(
                        cv_ref.at[page, h], vbuf.at[slot_index(slot, h, t, r)], sem.at[slot]))
        return out

    @pl.when(b == 0)
    def _():
        for cp in copies(0, 0):
            cp.start()

    @pl.when(b + 1 < pl.num_programs(0))
    def _():
        for cp in copies(b + 1, (b + 1) % 2):
            cp.start()

    slot = b % 2
    for cp in copies(b, slot):
        cp.wait()

    scale = HEAD_DIM ** -0.5
    q = _column(q_ref, b)
    kn = _column(kn_ref, b)
    vn = _column(vn_ref, b)
    for h in range(N_HEADS):
        s_self = jnp.sum(q[h] * kn[h], axis=0, keepdims=True) * scale
        scores = []
        mx = s_self
        for t in range(MOBA_TOPK):
            blk = sel_ref[b, t * N_HEADS + h]
            for r in range(pages_per_block):
                sc = sc_ref[blk * pages_per_block + r, h:h + 1, :]
                scores.append(sc)
                mx = jnp.maximum(mx, jnp.max(sc, axis=1, keepdims=True))
        p_self = jnp.exp(s_self - mx)
        den = p_self
        acc = p_self * vn[h]
        n = 0
        for t in range(MOBA_TOPK):
            for r in range(pages_per_block):
                p = jnp.exp(scores[n] - mx)
                n += 1
                den = den + jnp.sum(p, axis=1, keepdims=True)
                acc = acc + jnp.sum(vbuf[slot_index(slot, h, t, r)] * p, axis=1, keepdims=True)
        o_ref[h] = acc / den


def _s_attn(page_table, sel, scores, qt, knt, vnt, cache_vt):
    db, n_pages = page_table.shape
    page_size = cache_vt.shape[3]
    pages_per_block = MOBA_BLOCK // page_size
    n_slices = 2 * N_HEADS * MOBA_TOPK * pages_per_block
    per_b = pl.BlockSpec((None, N_HEADS, HEAD_DIM, 1), lambda b, pt, sl: (b, 0, 0, 0))
    whole = pl.BlockSpec(qt.shape, lambda b, pt, sl: (0, 0))
    return pl.pallas_call(
        functools.partial(_s_attn_kernel, pages_per_block=pages_per_block),
        out_shape=jax.ShapeDtypeStruct((db, N_HEADS, HEAD_DIM, 1), F32),
        grid_spec=pltpu.PrefetchScalarGridSpec(
            num_scalar_prefetch=2,
            grid=(db,),
            in_specs=[pl.BlockSpec((None, n_pages, N_HEADS, page_size),
                                   lambda b, pt, sl: (b, 0, 0, 0)),
                      whole, whole, whole, pl.BlockSpec(memory_space=pl.ANY)],
            out_specs=per_b,
            scratch_shapes=[pltpu.VMEM((n_slices, HEAD_DIM, page_size), F32),
                            pltpu.SemaphoreType.DMA((2,))],
        ),
        compiler_params=pltpu.CompilerParams(
            dimension_semantics=("arbitrary",), vmem_limit_bytes=VMEM_LIMIT),
        name="s_attn",
    )(page_table, sel, scores, qt, knt, vnt, cache_vt)


def _s_tail_kernel(x_ref, attn_ref, u_ref, ga_ref, gb_ref, sp_ref, bg_ref, wpool_ref, pscale_ref,
                   wa_ref, wb_ref, wout_ref, g1_ref, b1_ref, wr_ref, br_ref, wg_ref, wu_ref,
                   wd_ref, g2_ref, b2_ref, y_ref, x1_ref, comb_ref, acc_ref, *, alpha):
    e = pl.program_id(0)
    rows = x_ref.shape[0]
    lane = lax.broadcasted_iota(jnp.int32, (rows, LANES), 1)

    @pl.when(e == 0)
    def _():
        u = u_ref[...]
        parts = []
        for g, w in enumerate(POOL_WINDOWS):
            sl = slice(g * POOL_GROUP_WIDTH, (g + 1) * POOL_GROUP_WIDTH)
            acc = u[:, sl]
            for back in range(1, w):
                acc = acc + sp_ref[POOL_STATE - back][:, sl]
            parts.append(_pool_group(acc, u[:, sl], float(w), wpool_ref[g], pscale_ref[:, sl],
                                     precision=HIGHEST))
        pooled = jnp.concatenate(parts, axis=1)
        a = _dot(attn_ref[...], wa_ref[...], HIGHEST)
        bb = _dot(pooled, wb_ref[...], HIGHEST)
        merged = _sigmoid(ga_ref[...] + bg_ref[0:1, :]) * a + _sigmoid(gb_ref[...] + bg_ref[1:2, :]) * bb
        y = alpha * x_ref[...] + _dot(merged, wout_ref[...], HIGHEST)
        x1 = _layer_norm(y, g1_ref[...], b1_ref[...])
        x1_ref[...] = x1
        comb_ref[...] = _route(_dot(x1, wr_ref[...], HIGHEST) + br_ref[...])
        acc_ref[...] = jnp.zeros_like(acc_ref)

    x1 = x1_ref[...]
    hg = _dot(x1, wg_ref[...], HIGHEST)
    hu = _dot(x1, wu_ref[...], HIGHEST)
    c_e = jnp.sum(jnp.where(lane == e, comb_ref[...], 0.0), axis=1, keepdims=True)
    h = hg * _sigmoid(hg) * hu * c_e
    acc_ref[...] += _dot(h, wd_ref[...], HIGHEST)

    @pl.when(e == N_EXPERTS - 1)
    def _():
        y_ref[...] = _layer_norm(alpha * x1_ref[...] + acc_ref[...], g2_ref[...], b2_ref[...])


def _s_tail(x, attn, u, ga, gb, sp_t, b_gate, w_pool, pool_scale, w_a, w_b, w_out, ln1_g, ln1_b,
            w_r, b_r, w_g, w_u, w_d, ln2_g, ln2_b, alpha):
    rows, d = x.shape
    f = w_g.shape[2]
    const2 = lambda e: (0, 0)
    const3 = lambda e: (0, 0, 0)
    full = lambda a: pl.BlockSpec(a.shape, const2 if a.ndim == 2 else const3)
    per_e = lambda blk: pl.BlockSpec((None,) + blk, lambda e: (e, 0, 0))
    small = [x, attn, u, ga, gb, sp_t, b_gate, w_pool, pool_scale, w_a, w_b, w_out, ln1_g, ln1_b,
             w_r, b_r]
    return pl.pallas_call(
        functools.partial(_s_tail_kernel, alpha=alpha),
        out_shape=jax.ShapeDtypeStruct((rows, d), F32),
        grid=(N_EXPERTS,),
        in_specs=[full(a) for a in small]
        + [per_e((d, f)), per_e((d, f)), per_e((f, d)), full(ln2_g), full(ln2_b)],
        out_specs=pl.BlockSpec((rows, d), const2),
        scratch_shapes=[pltpu.VMEM((rows, d), F32), pltpu.VMEM((rows, LANES), F32),
                        pltpu.VMEM((rows, d), F32)],
        compiler_params=pltpu.CompilerParams(
            dimension_semantics=("arbitrary",), vmem_limit_bytes=VMEM_LIMIT),
        name="s_tail",
    )(*small, w_g, w_u, w_d, ln2_g, ln2_b)


def _rope_tables(pos):
    inv_freq = 1.0 / (ROPE_THETA ** (jnp.arange(0, HEAD_DIM, 2, dtype=F32) / HEAD_DIM))
    ang = pos.astype(F32)[:, None] * inv_freq[None, :]
    cos = jnp.cos(ang)
    sin = jnp.sin(ang)
    cos_t = jnp.tile(cos, (1, LANES // (HEAD_DIM // 2)))
    sin_t = jnp.tile(jnp.concatenate([-sin, sin], axis=1), (1, LANES // HEAD_DIM))
    return cos_t, sin_t


def kernel(x_prompt, x_sample, cache_k, cache_v, state_pool, page_table, w_in, w_pool, pool_scale,
           w_branch_a, w_branch_b, b_gate, w_out, ln1_g, ln1_b, w_group_router, b_group_router,
           w_expert_router, b_expert_router, w_e_gate, w_e_up, w_e_down, ln2_g, ln2_b):
    depth = w_in.shape[0]
    assert depth == 1 and x_sample.shape[1] == 1
    alpha = (2 * depth) ** 0.25
    b, s, d = x_prompt.shape
    db = x_sample.shape[0]
    n_pages = page_table.shape[1]
    page_size = cache_k.shape[2]
    past_len = n_pages * page_size
    n_blocks = past_len // MOBA_BLOCK
    assert past_len % MOBA_BLOCK == 0 and s % MOBA_BLOCK == 0

    w_in0 = w_in[0]
    qkvu_cols = 3 * ATTN_WIDTH + POOL_WIDTH
    row2 = lambda a: a.reshape(1, -1)
    w_r = jnp.concatenate(
        [w_group_router[0], jnp.transpose(w_expert_router[0], (1, 0, 2)).reshape(d, N_EXPERTS),
         jnp.zeros((d, LANES - N_EXPERT_GROUPS - N_EXPERTS), F32)], axis=1)
    b_r = jnp.concatenate(
        [b_group_router[0], b_expert_router[0].reshape(-1),
         jnp.zeros((LANES - N_EXPERT_GROUPS - N_EXPERTS,), F32)]).reshape(1, LANES)

    cos_p, sin_p = _rope_tables(jnp.arange(s, dtype=jnp.int32))
    tm_a = 512
    qt, k, v, u, kb, vt, km = _qkvu(x_prompt.reshape(b * s, d), w_in0[:, :qkvu_cols].astype(BF16),
                                    cos_p, sin_p, s, tm_a)
    nb = s // MOBA_BLOCK
    attn = _attn(qt, kb.reshape(b, s, ATTN_WIDTH), vt.reshape(b, nb, ATTN_WIDTH, MOBA_BLOCK),
                 km.reshape(b, nb, ATTN_WIDTH))
    u3 = u.reshape(b, s, POOL_WIDTH)
    x1 = _mix(x_prompt, attn, u3, w_in0[:, qkvu_cols:].astype(BF16), b_gate[0],
              w_pool[0].astype(BF16), row2(pool_scale[0]), w_branch_a[0].astype(BF16),
              w_branch_b[0].astype(BF16), w_out[0].astype(BF16), row2(ln1_g[0]), row2(ln1_b[0]),
              256, alpha)
    y_p = _moe(x1.reshape(b * s, d), w_r, b_r, w_e_gate[0].astype(BF16), w_e_up[0].astype(BF16),
               w_e_down[0].astype(BF16), row2(ln2_g[0]), row2(ln2_b[0]), 1024, alpha)

    cos_s, sin_s = _rope_tables(jnp.full((1,), past_len, jnp.int32))
    x_s = x_sample.reshape(db, d)
    h_s = _s_proj(x_s, w_in0, cos_s, sin_s)
    q_s = h_s[:, :ATTN_WIDTH]
    k_s = h_s[:, ATTN_WIDTH:2 * ATTN_WIDTH]
    v_s = h_s[:, 2 * ATTN_WIDTH:3 * ATTN_WIDTH]
    u_s = h_s[:, 3 * ATTN_WIDTH:qkvu_cols]
    ga_s = h_s[:, qkvu_cols:qkvu_cols + d]
    gb_s = h_s[:, qkvu_cols + d:]
    cache_kt = jnp.transpose(cache_k[0], (0, 2, 3, 1))
    cache_vt = jnp.transpose(cache_v[0], (0, 2, 3, 1))
    scores, sel = _s_scores(page_table, q_s.T, cache_kt, n_blocks)
    sel2 = sel.reshape(db, MOBA_TOPK * N_HEADS)
    attn_s = _s_attn(page_table, sel2, scores, q_s.T, k_s.T, v_s.T, cache_vt).reshape(db, ATTN_WIDTH)
    sp = state_pool[0]
    y_s = _s_tail(x_s, attn_s, u_s, ga_s, gb_s, jnp.transpose(sp, (1, 0, 2)), b_gate[0], w_pool[0],
                  row2(pool_scale[0]), w_branch_a[0], w_branch_b[0], w_out[0], row2(ln1_g[0]),
                  row2(ln1_b[0]), w_r, b_r, w_e_gate[0], w_e_up[0], w_e_down[0], row2(ln2_g[0]),
                  row2(ln2_b[0]), alpha)

    heads = (N_HEADS, HEAD_DIM)
    return (
        y_p.reshape(b, s, d),
        y_s.reshape(db, 1, d),
        jnp.transpose(k.reshape((b,) + heads + (s,)), (0, 3, 1, 2))[None],
        jnp.transpose(v.reshape((b,) + heads + (s,)), (0, 3, 1, 2))[None],
        u3[:, s - POOL_STATE:, :][None],
        k_s.reshape((1, db, 1) + heads),
        v_s.reshape((1, db, 1) + heads),
        jnp.concatenate([sp[:, 1:, :], u_s[:, None, :]], axis=1)[None],
    )
```

```python
import functools

import jax
import jax.numpy as jnp
from jax import lax
from jax.experimental import pallas as pl
from jax.experimental.pallas import tpu as pltpu

F32 = jnp.float32
BF16 = jnp.bfloat16
HIGHEST = lax.Precision.HIGHEST

N_HEADS = 8
HEAD_DIM = 64
ATTN_WIDTH = N_HEADS * HEAD_DIM
MOBA_BLOCK = 256
MOBA_TOPK = 3
ROPE_THETA = 10000.0
POOL_WINDOWS = (2, 4, 8, 16)
POOL_GROUP_WIDTH = 128
POOL_WIDTH = 512
POOL_STATE = 15
N_EXPERT_GROUPS = 4
EXPERTS_PER_GROUP = 4
N_EXPERTS = 16
LN_EPS = 1e-5
LANES = 128
HALO_ROWS = 16
MASK_BIAS = -1e30
VMEM_LIMIT = 56 * 1024 * 1024

NT_DIMS = (((1,), (1,)), ((), ()))


def _dot(a, b, precision=None):
    return jnp.dot(a, b, preferred_element_type=F32, precision=precision)


def _dot_nt(a, b, precision=None):
    return lax.dot_general(a, b, NT_DIMS, preferred_element_type=F32, precision=precision)


def _layer_norm(y, g, b):
    mu = jnp.mean(y, axis=-1, keepdims=True)
    var = jnp.mean(jnp.square(y - mu), axis=-1, keepdims=True)
    return (y - mu) * lax.rsqrt(var + LN_EPS) * g + b


def _sigmoid(x):
    return 1.0 / (1.0 + jnp.exp(-x))


def _rope_chunk(x, cos, sin_signed, first_half):
    partner = jnp.where(first_half, pltpu.roll(x, 96, 1), pltpu.roll(x, 32, 1))
    return x * cos + partner * sin_signed


def _route_weights(logit):
    g = [logit(k) for k in range(N_EXPERT_GROUPS)]
    gmax = jnp.maximum(jnp.maximum(g[0], g[1]), jnp.maximum(g[2], g[3]))
    den = sum(jnp.exp(gk - gmax) for gk in g)
    g_w = 1.0 / den
    is_g = []
    taken = jnp.zeros_like(gmax)
    for k in range(N_EXPERT_GROUPS):
        hit = jnp.where(g[k] == gmax, 1.0, 0.0) * (1.0 - taken)
        is_g.append(hit)
        taken = taken + hit
    e = []
    for k in range(EXPERTS_PER_GROUP):
        col = jnp.zeros_like(gmax)
        for gi in range(N_EXPERT_GROUPS):
            col = jnp.where(is_g[gi] > 0.5, logit(N_EXPERT_GROUPS + gi * EXPERTS_PER_GROUP + k), col)
        e.append(col)
    v1 = jnp.maximum(jnp.maximum(e[0], e[1]), jnp.maximum(e[2], e[3]))
    first = []
    taken = jnp.zeros_like(v1)
    for k in range(EXPERTS_PER_GROUP):
        hit = jnp.where(e[k] == v1, 1.0, 0.0) * (1.0 - taken)
        first.append(hit)
        taken = taken + hit
    e2 = [jnp.where(first[k] > 0.5, -jnp.inf, e[k]) for k in range(EXPERTS_PER_GROUP)]
    v2 = jnp.maximum(jnp.maximum(e2[0], e2[1]), jnp.maximum(e2[2], e2[3]))
    second = []
    taken = jnp.zeros_like(v2)
    for k in range(EXPERTS_PER_GROUP):
        hit = jnp.where(e2[k] == v2, 1.0, 0.0) * (1.0 - taken)
        second.append(hit)
        taken = taken + hit
    t = jnp.exp(v2 - v1)
    w1 = 1.0 / (1.0 + t)
    w2 = t * w1
    return [is_g[gi] * (first[k] * w1 + second[k] * w2) * g_w
            for gi in range(N_EXPERT_GROUPS) for k in range(EXPERTS_PER_GROUP)]


def _route(logits):
    lane = lax.broadcasted_iota(jnp.int32, logits.shape, 1)
    comb = jnp.zeros(logits.shape, F32)
    for e, col in enumerate(_route_weights(lambda k: logits[:, k:k + 1])):
        comb = jnp.where(lane == e, col, comb)
    return comb


def _route_rows(logits_t):
    tokens = logits_t.shape[1]
    row = lax.broadcasted_iota(jnp.int32, (N_EXPERTS, tokens), 0)
    comb = jnp.zeros((N_EXPERTS, tokens), F32)
    for e, r in enumerate(_route_weights(lambda k: logits_t[k:k + 1, :])):
        comb = jnp.where(row == e, r, comb)
    return jnp.concatenate([comb, jnp.zeros((logits_t.shape[0] - N_EXPERTS, tokens), F32)], axis=0)


Q_SCALE = 1.4426950408889634 * HEAD_DIM ** -0.5


def _qkvu_kernel(x_ref, w_ref, cos_ref, sin_ref, qt_ref, k_ref, v_ref, u_ref, kb_ref, vt_ref,
                 km_ref, *, tm):
    xb = x_ref[...].astype(BF16)
    cos = cos_ref[...]
    sin = sin_ref[...]
    lane = lax.broadcasted_iota(jnp.int32, (tm, LANES), 1)
    first_half = (lane & 32) == 0
    hq = _dot(xb, w_ref[:, 0:ATTN_WIDTH])
    hk = _dot(xb, w_ref[:, ATTN_WIDTH:2 * ATTN_WIDTH])
    hv = _dot(xb, w_ref[:, 2 * ATTN_WIDTH:3 * ATTN_WIDTH])
    for c in range(ATTN_WIDTH // LANES):
        sl = slice(c * LANES, (c + 1) * LANES)
        qc = _rope_chunk(hq[:, sl], cos, sin, first_half) * Q_SCALE
        kc = _rope_chunk(hk[:, sl], cos, sin, first_half)
        kb_ref[:, sl] = kc.astype(BF16)
        k_ref[sl, :] = kc.T
        vct = hv[:, sl].T
        v_ref[sl, :] = vct
        qct = qc.T.astype(BF16)
        for r in range(tm // MOBA_BLOCK):
            rows = slice(r * MOBA_BLOCK, (r + 1) * MOBA_BLOCK)
            km_ref[r:r + 1, sl] = jnp.sum(kc[rows], axis=0, keepdims=True) * (1.0 / MOBA_BLOCK)
            qt_ref[r, sl, :] = qct[:, rows]
            vt_ref[r, sl, :] = vct[:, rows].astype(BF16)
    u_ref[...] = _dot(xb, w_ref[:, 3 * ATTN_WIDTH:])


def _qkvu(x2d, w_qkvu, cos, sin, seq, tm):
    n, d = x2d.shape
    s_tiles = seq // tm
    nblk = tm // MOBA_BLOCK
    row = lambda i: (i, 0)
    blocked = lambda i: (i, 0, 0)
    featmajor = lambda i: (i // s_tiles, 0, i % s_tiles)
    out_shape = (
        jax.ShapeDtypeStruct((n // MOBA_BLOCK, ATTN_WIDTH, MOBA_BLOCK), BF16),
        jax.ShapeDtypeStruct((n // seq, ATTN_WIDTH, seq), F32),
        jax.ShapeDtypeStruct((n // seq, ATTN_WIDTH, seq), F32),
        jax.ShapeDtypeStruct((n, POOL_WIDTH), F32),
        jax.ShapeDtypeStruct((n, ATTN_WIDTH), BF16),
        jax.ShapeDtypeStruct((n // MOBA_BLOCK, ATTN_WIDTH, MOBA_BLOCK), BF16),
        jax.ShapeDtypeStruct((n // tm, nblk, ATTN_WIDTH), F32),
    )
    return pl.pallas_call(
        functools.partial(_qkvu_kernel, tm=tm),
        out_shape=out_shape,
        grid=(n // tm,),
        in_specs=[
            pl.BlockSpec((tm, d), row),
            pl.BlockSpec((d, 4 * ATTN_WIDTH), lambda i: (0, 0)),
            pl.BlockSpec((tm, LANES), lambda i: (i % s_tiles, 0)),
            pl.BlockSpec((tm, LANES), lambda i: (i % s_tiles, 0)),
        ],
        out_specs=(
            pl.BlockSpec((nblk, ATTN_WIDTH, MOBA_BLOCK), blocked),
            pl.BlockSpec((None, ATTN_WIDTH, tm), featmajor),
            pl.BlockSpec((None, ATTN_WIDTH, tm), featmajor),
            pl.BlockSpec((tm, POOL_WIDTH), row),
            pl.BlockSpec((tm, ATTN_WIDTH), row),
            pl.BlockSpec((nblk, ATTN_WIDTH, MOBA_BLOCK), blocked),
            pl.BlockSpec((None, nblk, ATTN_WIDTH), blocked),
        ),
        compiler_params=pltpu.CompilerParams(
            dimension_semantics=("arbitrary",), vmem_limit_bytes=VMEM_LIMIT),
        name="qkvu",
    )(x2d, w_qkvu, cos, sin)


BIAS_ROWS = 128
SUM_ROWS = 16
ATTN_PAIRS = 4


def _attn_kernel(qt_ref, k_ref, vt_ref, km_ref, o_ref, qa_ref, acc_ref, sa_ref, sb_ref):
    i = pl.program_id(2)
    n_past = km_ref.shape[0]
    n_heads = 2 * ATTN_PAIRS
    feat = lax.broadcasted_iota(jnp.int32, (LANES, MOBA_BLOCK), 0)
    head0 = feat < HEAD_DIM
    blk_id = lax.broadcasted_iota(jnp.int32, (n_past, MOBA_BLOCK), 0)
    blk_f = blk_id.astype(F32)
    past = blk_id < i
    key_id = lax.broadcasted_iota(jnp.int32, (MOBA_BLOCK, MOBA_BLOCK), 0)
    qry_id = lax.broadcasted_iota(jnp.int32, (MOBA_BLOCK, MOBA_BLOCK), 1)
    causal = key_id <= qry_id
    lane = lax.broadcasted_iota(jnp.int32, (MOBA_BLOCK, LANES), 1)
    ones_rows = jnp.ones((SUM_ROWS, MOBA_BLOCK), BF16)
    bias_pad = jnp.zeros((BIAS_ROWS - n_past, MOBA_BLOCK), BF16)
    own = pl.multiple_of(i * MOBA_BLOCK, MOBA_BLOCK)

    def slab(pair):
        return slice(pair * LANES, (pair + 1) * LANES)

    def vt_aug(blk, pair):
        return jnp.concatenate([vt_ref[blk, slab(pair), :], ones_rows], axis=0)

    def consume_refill(s_ref, ms, blk, nxt):
        off = pl.multiple_of(nxt * MOBA_BLOCK, MOBA_BLOCK)
        onehot = jnp.where(lane == nxt, 1.0, 0.0).astype(BF16)
        out = []
        for h in range(n_heads):
            s = s_ref[h]
            m_new = jnp.maximum(ms[h], jnp.max(s, axis=0, keepdims=True))
            alpha = jnp.exp2(ms[h] - m_new)
            p = jnp.exp2(s - m_new)
            acc_ref[h] = alpha * acc_ref[h] + _dot(vt_aug(blk, h // 2), p.astype(BF16))
            out.append(m_new)
            ka = jnp.concatenate([k_ref[pl.ds(off, MOBA_BLOCK), slab(h // 2)], onehot], axis=1)
            s_ref[h] = _dot(ka, qa_ref[h])
        return tuple(out)

    for h in range(n_heads):
        pair = h // 2
        qt = qt_ref[slab(pair), :]
        qh = jnp.where(head0 if h % 2 == 0 else jnp.logical_not(head0), qt, jnp.zeros_like(qt))
        gate = _dot(km_ref[:, slab(pair)], qh.astype(F32), HIGHEST)
        g = jnp.where(past, gate, -jnp.inf)
        sel = jnp.zeros(g.shape, F32)
        for _ in range(MOBA_TOPK):
            mx = jnp.max(g, axis=0, keepdims=True)
            idx = jnp.min(jnp.where(g == mx, blk_f, float(n_past)), axis=0, keepdims=True)
            pick = blk_f == idx
            sel = jnp.where(pick, 1.0, sel)
            g = jnp.where(pick, -jnp.inf, g)
        bias = jnp.where(jnp.logical_and(sel > 0.5, past), 0.0, MASK_BIAS).astype(BF16)
        qa_ref[h] = jnp.concatenate([qh, bias, bias_pad], axis=0)
        kd = k_ref[pl.ds(own, MOBA_BLOCK), slab(pair)]
        sa_ref[h] = jnp.where(causal, _dot(kd, qh), -jnp.inf)
        acc_ref[h] = jnp.zeros(acc_ref.shape[1:], F32)
    m0 = jnp.full((1, MOBA_BLOCK), MASK_BIAS, F32)

    last_past = jnp.maximum(i - 1, 0)
    off0 = pl.multiple_of(0 * MOBA_BLOCK, MOBA_BLOCK)
    onehot0 = jnp.where(lane == 0, 1.0, 0.0).astype(BF16)
    for h in range(n_heads):
        ka = jnp.concatenate([k_ref[pl.ds(off0, MOBA_BLOCK), slab(h // 2)], onehot0], axis=1)
        sb_ref[h] = _dot(ka, qa_ref[h])

    def pair_body(u, ms):
        ms = consume_refill(sa_ref, ms, jnp.where(u == 0, i, 2 * u - 1),
                            jnp.minimum(2 * u + 1, last_past))
        return consume_refill(sb_ref, ms, 2 * u, jnp.minimum(2 * u + 2, last_past))

    ms = lax.fori_loop(0, (i + 1) // 2, pair_body, (m0,) * n_heads)

    @pl.when((i + 1) % 2 == 1)
    def _():
        blk = jnp.where(i == 0, i, i - 1)
        for h in range(n_heads):
            s = sa_ref[h]
            m_new = jnp.maximum(ms[h], jnp.max(s, axis=0, keepdims=True))
            alpha = jnp.exp2(ms[h] - m_new)
            p = jnp.exp2(s - m_new)
            acc_ref[h] = alpha * acc_ref[h] + _dot(vt_aug(blk, h // 2), p.astype(BF16))

    for pair in range(ATTN_PAIRS):
        a0 = acc_ref[2 * pair]
        a1 = acc_ref[2 * pair + 1]
        ot = jnp.where(head0, a0[:LANES] / a0[LANES:LANES + 1], a1[:LANES] / a1[LANES:LANES + 1])
        o_ref[:, slab(pair)] = ot.T.astype(BF16)


def _attn(qt, kb, vt, km):
    b, s, _ = kb.shape
    nb = s // MOBA_BLOCK
    width = ATTN_PAIRS * LANES
    return pl.pallas_call(
        _attn_kernel,
        out_shape=jax.ShapeDtypeStruct((b, s, ATTN_WIDTH), BF16),
        grid=(b, ATTN_WIDTH // width, nb),
        in_specs=[
            pl.BlockSpec((None, width, MOBA_BLOCK), lambda bi, p, i: (bi * nb + i, p, 0)),
            pl.BlockSpec((None, s, width), lambda bi, p, i: (bi, 0, p)),
            pl.BlockSpec((None, nb, width, MOBA_BLOCK), lambda bi, p, i: (bi, 0, p, 0)),
            pl.BlockSpec((None, nb, width), lambda bi, p, i: (bi, 0, p)),
        ],
        out_specs=pl.BlockSpec((None, MOBA_BLOCK, width), lambda bi, p, i: (bi, i, p)),
        scratch_shapes=[
            pltpu.VMEM((2 * ATTN_PAIRS, LANES + BIAS_ROWS, MOBA_BLOCK), BF16),
            pltpu.VMEM((2 * ATTN_PAIRS, LANES + SUM_ROWS, MOBA_BLOCK), F32),
            pltpu.VMEM((2 * ATTN_PAIRS, MOBA_BLOCK, MOBA_BLOCK), F32),
            pltpu.VMEM((2 * ATTN_PAIRS, MOBA_BLOCK, MOBA_BLOCK), F32),
        ],
        compiler_params=pltpu.CompilerParams(
            dimension_semantics=("arbitrary", "arbitrary", "arbitrary"),
            vmem_limit_bytes=VMEM_LIMIT),
        name="moba_attn",
    )(qt, kb, vt, km)


def _pool_group(window_sum, u_g, count, w_pool_g, scale_g, precision=None, cast=None):
    d = window_sum / count - u_g
    if cast is not None:
        d = d.astype(cast)
    return _dot(d, w_pool_g, precision) * scale_g


def _mix_kernel(x_ref, attn_ref, u_ref, halo_ref, wg_ref, bg_ref, wpool_ref, pscale_ref, wa_ref,
                wb_ref, wout_ref, g1_ref, b1_ref, x1_ref, z_ref, *, tm, alpha):
    i = pl.program_id(1)
    x = x_ref[...]
    xb = x.astype(BF16)
    u = u_ref[...]
    z_ref[0:HALO_ROWS, :] = jnp.where(i > 0, halo_ref[...], 0.0)
    z_ref[HALO_ROWS:, :] = u
    pos1 = (i * tm + 1 + lax.broadcasted_iota(jnp.int32, (tm, 1), 0)).astype(F32)
    parts = []
    for g, w in enumerate(POOL_WINDOWS):
        sl = slice(g * POOL_GROUP_WIDTH, (g + 1) * POOL_GROUP_WIDTH)
        acc = u[:, sl]
        for back in range(1, w):
            acc = acc + z_ref[HALO_ROWS - back:HALO_ROWS - back + tm, sl]
        count = jnp.minimum(float(w), pos1)
        parts.append(_pool_group(acc, u[:, sl], count, wpool_ref[g], pscale_ref[:, sl], cast=BF16))
    pooled = jnp.concatenate(parts, axis=1).astype(BF16)
    d = x.shape[1]
    ga = _dot(xb, wg_ref[:, :d]) + bg_ref[0:1, :]
    gb = _dot(xb, wg_ref[:, d:]) + bg_ref[1:2, :]
    a = _dot(attn_ref[...], wa_ref[...])
    bb = _dot(pooled, wb_ref[...])
    merged = _sigmoid(ga) * a + _sigmoid(gb) * bb
    y = alpha * x + _dot(merged.astype(BF16), wout_ref[...])
    x1_ref[...] = _layer_norm(y, g1_ref[...], b1_ref[...])


def _mix(x, attn, u, w_gates, b_gate, w_pool, pool_scale, w_a, w_b, w_out, ln_g, ln_b, tm, alpha):
    b, s, d = x.shape
    halo_per_tile = tm // HALO_ROWS
    tile = lambda bi, i: (bi, i, 0)
    const2 = lambda bi, i: (0, 0)
    return pl.pallas_call(
        functools.partial(_mix_kernel, tm=tm, alpha=alpha),
        out_shape=jax.ShapeDtypeStruct((b, s, d), F32),
        grid=(b, s // tm),
        in_specs=[
            pl.BlockSpec((None, tm, d), tile),
            pl.BlockSpec((None, tm, ATTN_WIDTH), tile),
            pl.BlockSpec((None, tm, POOL_WIDTH), tile),
            pl.BlockSpec((None, HALO_ROWS, POOL_WIDTH),
                         lambda bi, i: (bi, jnp.maximum(i * halo_per_tile - 1, 0), 0)),
            pl.BlockSpec(w_gates.shape, const2),
            pl.BlockSpec(b_gate.shape, const2),
            pl.BlockSpec(w_pool.shape, lambda bi, i: (0, 0, 0)),
            pl.BlockSpec(pool_scale.shape, const2),
            pl.BlockSpec(w_a.shape, const2),
            pl.BlockSpec(w_b.shape, const2),
            pl.BlockSpec(w_out.shape, const2),
            pl.BlockSpec(ln_g.shape, const2),
            pl.BlockSpec(ln_b.shape, const2),
        ],
        out_specs=pl.BlockSpec((None, tm, d), tile),
        scratch_shapes=[pltpu.VMEM((HALO_ROWS + tm, POOL_WIDTH), F32)],
        compiler_params=pltpu.CompilerParams(
            dimension_semantics=("arbitrary", "arbitrary"), vmem_limit_bytes=VMEM_LIMIT),
        name="mix",
    )(x, attn, u, u, w_gates, b_gate, w_pool, pool_scale, w_a, w_b, w_out, ln_g, ln_b)


def _moe_kernel(x1_ref, wrh_ref, wrl_ref, br_ref, wg_ref, wu_ref, wd_ref, g2_ref, b2_ref, y_ref,
                xb_ref, comb_ref, acc_ref, *, alpha):
    g = pl.program_id(1)
    tm = x1_ref.shape[0]
    lane = lax.broadcasted_iota(jnp.int32, (tm, LANES), 1)

    @pl.when(g == 0)
    def _():
        x1 = x1_ref[...]
        xh = x1.astype(BF16)
        xb_ref[...] = xh
        xl = (x1 - xh.astype(F32)).astype(BF16)
        logits = (_dot(xh, wrh_ref[...]) + _dot(xl, wrh_ref[...]) + _dot(xh, wrl_ref[...])
                  + br_ref[...])
        comb_ref[...] = _route_rows(logits.T).T

    xb = xb_ref[...]
    comb = comb_ref[...]
    hidden = []
    for k in range(EXPERTS_PER_GROUP):
        hg = _dot(xb, wg_ref[k])
        hu = _dot(xb, wu_ref[k])
        c_k = jnp.sum(jnp.where(lane == g * EXPERTS_PER_GROUP + k, comb, 0.0), axis=1, keepdims=True)
        hidden.append((hg * _sigmoid(hg) * hu * c_k).astype(BF16))
    part = _dot(jnp.concatenate(hidden, axis=1), wd_ref[...])

    @pl.when(g == 0)
    def _():
        acc_ref[...] = part

    @pl.when(g > 0)
    def _():
        acc_ref[...] += part

    @pl.when(g == N_EXPERT_GROUPS - 1)
    def _():
        y = alpha * x1_ref[...] + acc_ref[...]
        y_ref[...] = _layer_norm(y, g2_ref[...], b2_ref[...])


def _moe(x1, w_r, b_r, w_g, w_u, w_d, ln_g, ln_b, tm, alpha):
    w_r_hi = w_r.astype(BF16)
    w_r_lo = (w_r - w_r_hi.astype(F32)).astype(BF16)
    n, d = x1.shape
    f = w_g.shape[2]
    per_group = (N_EXPERT_GROUPS, EXPERTS_PER_GROUP)
    tile = lambda i, g: (i, 0)
    const2 = lambda i, g: (0, 0)
    group4 = lambda i, g: (g, 0, 0, 0)
    return pl.pallas_call(
        functools.partial(_moe_kernel, alpha=alpha),
        out_shape=jax.ShapeDtypeStruct((n, d), F32),
        grid=(n // tm, N_EXPERT_GROUPS),
        in_specs=[
            pl.BlockSpec((tm, d), tile),
            pl.BlockSpec(w_r.shape, const2),
            pl.BlockSpec(w_r.shape, const2),
            pl.BlockSpec(b_r.shape, const2),
            pl.BlockSpec((None, EXPERTS_PER_GROUP, d, f), group4),
            pl.BlockSpec((None, EXPERTS_PER_GROUP, d, f), group4),
            pl.BlockSpec((None, EXPERTS_PER_GROUP * f, d), lambda i, g: (g, 0, 0)),
            pl.BlockSpec(ln_g.shape, const2),
            pl.BlockSpec(ln_b.shape, const2),
        ],
        out_specs=pl.BlockSpec((tm, d), tile),
        scratch_shapes=[
            pltpu.VMEM((tm, d), BF16),
            pltpu.VMEM((tm, LANES), F32),
            pltpu.VMEM((tm, d), F32),
        ],
        compiler_params=pltpu.CompilerParams(
            dimension_semantics=("arbitrary", "arbitrary"), vmem_limit_bytes=VMEM_LIMIT),
        name="moe",
    )(x1, w_r_hi, w_r_lo, b_r, w_g.reshape(per_group + (d, f)), w_u.reshape(per_group + (d, f)),
      w_d.reshape(N_EXPERT_GROUPS, EXPERTS_PER_GROUP * f, d), ln_g, ln_b)


def _s_proj_kernel(x_ref, w_ref, cos_ref, sin_ref, h_ref):
    c = pl.program_id(0)
    h = _dot(x_ref[...], w_ref[...], HIGHEST)
    rows = h.shape[0]
    lane = lax.broadcasted_iota(jnp.int32, (rows, LANES), 1)
    first_half = (lane & 32) == 0
    rotary = c < 2
    for j in range(h.shape[1] // LANES):
        sl = slice(j * LANES, (j + 1) * LANES)
        hc = h[:, sl]
        h_ref[:, sl] = jnp.where(rotary, _rope_chunk(hc, cos_ref[...], sin_ref[...], first_half), hc)


def _s_proj(x, w_in, cos, sin):
    rows, d = x.shape
    width = w_in.shape[1]
    chunk = ATTN_WIDTH
    return pl.pallas_call(
        _s_proj_kernel,
        out_shape=jax.ShapeDtypeStruct((rows, width), F32),
        grid=(width // chunk,),
        in_specs=[
            pl.BlockSpec((rows, d), lambda c: (0, 0)),
            pl.BlockSpec((d, chunk), lambda c: (0, c)),
            pl.BlockSpec((1, LANES), lambda c: (0, 0)),
            pl.BlockSpec((1, LANES), lambda c: (0, 0)),
        ],
        out_specs=pl.BlockSpec((rows, chunk), lambda c: (0, c)),
        compiler_params=pltpu.CompilerParams(
            dimension_semantics=("arbitrary",), vmem_limit_bytes=VMEM_LIMIT),
        name="s_proj",
    )(x, w_in, cos, sin)


PAGES_PER_STEP = 32


def _column(ref, b):
    x = ref[...]
    lane = lax.broadcasted_iota(jnp.int32, x.shape, 1)
    col = jnp.sum(jnp.where(lane == b, x, 0.0), axis=1, keepdims=True)
    return col.reshape(N_HEADS, HEAD_DIM, 1)


def _s_scores_kernel(pt_ref, q_ref, *refs, page_size, n_blocks):
    page_refs = refs[:PAGES_PER_STEP]
    sc_ref, sel_ref, gs_ref = refs[PAGES_PER_STEP:]
    c = pl.program_id(1)
    pages_per_block = MOBA_BLOCK // page_size
    blocks_per_step = PAGES_PER_STEP // pages_per_block
    q = _column(q_ref, pl.program_id(0)) * (HEAD_DIM ** -0.5)
    for r in range(blocks_per_step):
        tot = jnp.zeros((N_HEADS, page_size), F32)
        for pp in range(pages_per_block):
            p = r * pages_per_block + pp
            s = jnp.sum(page_refs[p][...] * q, axis=1)
            sc_ref[p] = s
            tot = tot + s
        gs_ref[c * blocks_per_step + r] = jnp.sum(tot, axis=1, keepdims=True)

    @pl.when(c == pl.num_programs(1) - 1)
    def _():
        g = gs_ref[...]
        blk_f = lax.broadcasted_iota(jnp.int32, g.shape, 0).astype(F32)
        for t in range(MOBA_TOPK):
            mx = jnp.max(g, axis=0, keepdims=True)
            idx = jnp.min(jnp.where(g == mx, blk_f, float(n_blocks)), axis=0, keepdims=True)
            sel_ref[t:t + 1] = idx.astype(jnp.int32)
            g = jnp.where(blk_f == idx, -jnp.inf, g)


def _s_scores(page_table, qt, cache_kt, n_blocks):
    db, n_pages = page_table.shape
    page_size = cache_kt.shape[3]
    page_block = (None, N_HEADS, HEAD_DIM, page_size)

    def page_spec(r):
        return pl.BlockSpec(page_block, lambda b, c, pt: (pt[b, c * PAGES_PER_STEP + r], 0, 0, 0))

    return pl.pallas_call(
        functools.partial(_s_scores_kernel, page_size=page_size, n_blocks=n_blocks),
        out_shape=(jax.ShapeDtypeStruct((db, n_pages, N_HEADS, page_size), F32),
                   jax.ShapeDtypeStruct((db, MOBA_TOPK, N_HEADS, 1), jnp.int32)),
        grid_spec=pltpu.PrefetchScalarGridSpec(
            num_scalar_prefetch=1,
            grid=(db, n_pages // PAGES_PER_STEP),
            in_specs=[pl.BlockSpec(qt.shape, lambda b, c, pt: (0, 0))]
            + [page_spec(r) for r in range(PAGES_PER_STEP)],
            out_specs=(pl.BlockSpec((None, PAGES_PER_STEP, N_HEADS, page_size),
                                    lambda b, c, pt: (b, c, 0, 0)),
                       pl.BlockSpec((None, MOBA_TOPK, N_HEADS, 1), lambda b, c, pt: (b, 0, 0, 0))),
            scratch_shapes=[pltpu.VMEM((n_blocks, N_HEADS, 1), F32)],
        ),
        compiler_params=pltpu.CompilerParams(
            dimension_semantics=("arbitrary", "arbitrary"), vmem_limit_bytes=VMEM_LIMIT),
        name="s_scores",
    )(page_table, qt, *([cache_kt] * PAGES_PER_STEP))


def _s_attn_kernel(pt_ref, sel_ref, sc_ref, q_ref, kn_ref, vn_ref, cv_ref, o_ref, vbuf, sem,
                   *, pages_per_block):
    b = pl.program_id(0)
    slices_per_seq = N_HEADS * MOBA_TOPK * pages_per_block

    def slot_index(slot, h, t, r):
        return slot * slices_per_seq + (h * MOBA_TOPK + t) * pages_per_block + r

    def copies(seq, slot):
        out = []
        for h in range(N_HEADS):
            for t in range(MOBA_TOPK):
                blk = sel_ref[seq, t * N_HEADS + h]
                for r in range(pages_per_block):
                    page = pt_ref[seq, blk * pages_per_block + r]
                    out.append(pltpu.make_async_copy(
                        cv_ref.at[page, h], vbuf.at[slot_index(slot, h, t, r)], sem.at[slot]))
        return out

    @pl.when(b == 0)
    def _():
        for cp in copies(0, 0):
            cp.start()

    @pl.when(b + 1 < pl.num_programs(0))
    def _():
        for cp in copies(b + 1, (b + 1) % 2):
            cp.start()

    slot = b % 2
    for cp in copies(b, slot):
        cp.wait()

    scale = HEAD_DIM ** -0.5
    q = _column(q_ref, b)
    kn = _column(kn_ref, b)
    vn = _column(vn_ref, b)
    for h in range(N_HEADS):
        s_self = jnp.sum(q[h] * kn[h], axis=0, keepdims=True) * scale
        scores = []
        mx = s_self
        for t in range(MOBA_TOPK):
            blk = sel_ref[b, t * N_HEADS + h]
            for r in range(pages_per_block):
                sc = sc_ref[blk * pages_per_block + r, h:h + 1, :]
                scores.append(sc)
                mx = jnp.maximum(mx, jnp.max(sc, axis=1, keepdims=True))
        p_self = jnp.exp(s_self - mx)
        den = p_self
        acc = p_self * vn[h]
        n = 0
        for t in range(MOBA_TOPK):
            for r in range(pages_per_block):
                p = jnp.exp(scores[n] - mx)
                n += 1
                den = den + jnp.sum(p, axis=1, keepdims=True)
                acc = acc + jnp.sum(vbuf[slot_index(slot, h, t, r)] * p, axis=1, keepdims=True)
        o_ref[h] = acc / den


def _s_attn(page_table, sel, scores, qt, knt, vnt, cache_vt):
    db, n_pages = page_table.shape
    page_size = cache_vt.shape[3]
    pages_per_block = MOBA_BLOCK // page_size
    n_slices = 2 * N_HEADS * MOBA_TOPK * pages_per_block
    per_b = pl.BlockSpec((None, N_HEADS, HEAD_DIM, 1), lambda b, pt, sl: (b, 0, 0, 0))
    whole = pl.BlockSpec(qt.shape, lambda b, pt, sl: (0, 0))
    return pl.pallas_call(
        functools.partial(_s_attn_kernel, pages_per_block=pages_per_block),
        out_shape=jax.ShapeDtypeStruct((db, N_HEADS, HEAD_DIM, 1), F32),
        grid_spec=pltpu.PrefetchScalarGridSpec(
            num_scalar_prefetch=2,
            grid=(db,),
            in_specs=[pl.BlockSpec((None, n_pages, N_HEADS, page_size),
                                   lambda b, pt, sl: (b, 0, 0, 0)),
                      whole, whole, whole, pl.BlockSpec(memory_space=pl.ANY)],
            out_specs=per_b,
            scratch_shapes=[pltpu.VMEM((n_slices, HEAD_DIM, page_size), F32),
                            pltpu.SemaphoreType.DMA((2,))],
        ),
        compiler_params=pltpu.CompilerParams(
            dimension_semantics=("arbitrary",), vmem_limit_bytes=VMEM_LIMIT),
        name="s_attn",
    )(page_table, sel, scores, qt, knt, vnt, cache_vt)


def _s_tail_kernel(x_ref, attn_ref, u_ref, ga_ref, gb_ref, sp_ref, bg_ref, wpool_ref, pscale_ref,
                   wa_ref, wb_ref, wout_ref, g1_ref, b1_ref, wr_ref, br_ref, wg_ref, wu_ref,
                   wd_ref, g2_ref, b2_ref, y_ref, x1_ref, comb_ref, acc_ref, *, alpha):
    e = pl.program_id(0)
    rows = x_ref.shape[0]
    lane = lax.broadcasted_iota(jnp.int32, (rows, LANES), 1)

    @pl.when(e == 0)
    def _():
        u = u_ref[...]
        parts = []
        for g, w in enumerate(POOL_WINDOWS):
            sl = slice(g * POOL_GROUP_WIDTH, (g + 1) * POOL_GROUP_WIDTH)
            acc = u[:, sl]
            for back in range(1, w):
                acc = acc + sp_ref[POOL_STATE - back][:, sl]
            parts.append(_pool_group(acc, u[:, sl], float(w), wpool_ref[g], pscale_ref[:, sl],
                                     precision=HIGHEST))
        pooled = jnp.concatenate(parts, axis=1)
        a = _dot(attn_ref[...], wa_ref[...], HIGHEST)
        bb = _dot(pooled, wb_ref[...], HIGHEST)
        merged = _sigmoid(ga_ref[...] + bg_ref[0:1, :]) * a + _sigmoid(gb_ref[...] + bg_ref[1:2, :]) * bb
        y = alpha * x_ref[...] + _dot(merged, wout_ref[...], HIGHEST)
        x1 = _layer_norm(y, g1_ref[...], b1_ref[...])
        x1_ref[...] = x1
        comb_ref[...] = _route(_dot(x1, wr_ref[...], HIGHEST) + br_ref[...])
        acc_ref[...] = jnp.zeros_like(acc_ref)

    x1 = x1_ref[...]
    hg = _dot(x1, wg_ref[...], HIGHEST)
    hu = _dot(x1, wu_ref[...], HIGHEST)
    c_e = jnp.sum(jnp.where(lane == e, comb_ref[...], 0.0), axis=1, keepdims=True)
    h = hg * _sigmoid(hg) * hu * c_e
    acc_ref[...] += _dot(h, wd_ref[...], HIGHEST)

    @pl.when(e == N_EXPERTS - 1)
    def _():
        y_ref[...] = _layer_norm(alpha * x1_ref[...] + acc_ref[...], g2_ref[...], b2_ref[...])


def _s_tail(x, attn, u, ga, gb, sp_t, b_gate, w_pool, pool_scale, w_a, w_b, w_out, ln1_g, ln1_b,
            w_r, b_r, w_g, w_u, w_d, ln2_g, ln2_b, alpha):
    rows, d = x.shape
    f = w_g.shape[2]
    const2 = lambda e: (0, 0)
    const3 = lambda e: (0, 0, 0)
    full = lambda a: pl.BlockSpec(a.shape, const2 if a.ndim == 2 else const3)
    per_e = lambda blk: pl.BlockSpec((None,) + blk, lambda e: (e, 0, 0))
    small = [x, attn, u, ga, gb, sp_t, b_gate, w_pool, pool_scale, w_a, w_b, w_out, ln1_g, ln1_b,
             w_r, b_r]
    return pl.pallas_call(
        functools.partial(_s_tail_kernel, alpha=alpha),
        out_shape=jax.ShapeDtypeStruct((rows, d), F32),
        grid=(N_EXPERTS,),
        in_specs=[full(a) for a in small]
        + [per_e((d, f)), per_e((d, f)), per_e((f, d)), full(ln2_g), full(ln2_b)],
        out_specs=pl.BlockSpec((rows, d), const2),
        scratch_shapes=[pltpu.VMEM((rows, d), F32), pltpu.VMEM((rows, LANES), F32),
                        pltpu.VMEM((rows, d), F32)],
        compiler_params=pltpu.CompilerParams(
            dimension_semantics=("arbitrary",), vmem_limit_bytes=VMEM_LIMIT),
        name="s_tail",
    )(*small, w_g, w_u, w_d, ln2_g, ln2_b)


def _rope_tables(pos):
    inv_freq = 1.0 / (ROPE_THETA ** (jnp.arange(0, HEAD_DIM, 2, dtype=F32) / HEAD_DIM))
    ang = pos.astype(F32)[:, None] * inv_freq[None, :]
    cos = jnp.cos(ang)
    sin = jnp.sin(ang)
    cos_t = jnp.tile(cos, (1, LANES // (HEAD_DIM // 2)))
    sin_t = jnp.tile(jnp.concatenate([-sin, sin], axis=1), (1, LANES // HEAD_DIM))
    return cos_t, sin_t


def kernel(x_prompt, x_sample, cache_k, cache_v, state_pool, page_table, w_in, w_pool, pool_scale,
           w_branch_a, w_branch_b, b_gate, w_out, ln1_g, ln1_b, w_group_router, b_group_router,
           w_expert_router, b_expert_router, w_e_gate, w_e_up, w_e_down, ln2_g, ln2_b):
    depth = w_in.shape[0]
    assert depth == 1 and x_sample.shape[1] == 1
    alpha = (2 * depth) ** 0.25
    b, s, d = x_prompt.shape
    db = x_sample.shape[0]
    n_pages = page_table.shape[1]
    page_size = cache_k.shape[2]
    past_len = n_pages * page_size
    n_blocks = past_len // MOBA_BLOCK
    assert past_len % MOBA_BLOCK == 0 and s % MOBA_BLOCK == 0

    w_in0 = w_in[0]
    qkvu_cols = 3 * ATTN_WIDTH + POOL_WIDTH
    row2 = lambda a: a.reshape(1, -1)
    w_r = jnp.concatenate(
        [w_group_router[0], jnp.transpose(w_expert_router[0], (1, 0, 2)).reshape(d, N_EXPERTS),
         jnp.zeros((d, LANES - N_EXPERT_GROUPS - N_EXPERTS), F32)], axis=1)
    b_r = jnp.concatenate(
        [b_group_router[0], b_expert_router[0].reshape(-1),
         jnp.zeros((LANES - N_EXPERT_GROUPS - N_EXPERTS,), F32)]).reshape(1, LANES)

    cos_p, sin_p = _rope_tables(jnp.arange(s, dtype=jnp.int32))
    tm_a = 512
    qt, k, v, u, kb, vt, km = _qkvu(x_prompt.reshape(b * s, d), w_in0[:, :qkvu_cols].astype(BF16),
                                    cos_p, sin_p, s, tm_a)
    nb = s // MOBA_BLOCK
    attn = _attn(qt, kb.reshape(b, s, ATTN_WIDTH), vt.reshape(b, nb, ATTN_WIDTH, MOBA_BLOCK),
                 km.reshape(b, nb, ATTN_WIDTH))
    u3 = u.reshape(b, s, POOL_WIDTH)
    x1 = _mix(x_prompt, attn, u3, w_in0[:, qkvu_cols:].astype(BF16), b_gate[0],
              w_pool[0].astype(BF16), row2(pool_scale[0]), w_branch_a[0].astype(BF16),
              w_branch_b[0].astype(BF16), w_out[0].astype(BF16), row2(ln1_g[0]), row2(ln1_b[0]),
              256, alpha)
    y_p = _moe(x1.reshape(b * s, d), w_r, b_r, w_e_gate[0].astype(BF16), w_e_up[0].astype(BF16),
               w_e_down[0].astype(BF16), row2(ln2_g[0]), row2(ln2_b[0]), 1024, alpha)

    cos_s, sin_s = _rope_tables(jnp.full((1,), past_len, jnp.int32))
    x_s = x_sample.reshape(db, d)
    h_s = _s_proj(x_s, w_in0, cos_s, sin_s)
    q_s = h_s[:, :ATTN_WIDTH]
    k_s = h_s[:, ATTN_WIDTH:2 * ATTN_WIDTH]
    v_s = h_s[:, 2 * ATTN_WIDTH:3 * ATTN_WIDTH]
    u_s = h_s[:, 3 * ATTN_WIDTH:qkvu_cols]
    ga_s = h_s[:, qkvu_cols:qkvu_cols + d]
    gb_s = h_s[:, qkvu_cols + d:]
    cache_kt = jnp.transpose(cache_k[0], (0, 2, 3, 1))
    cache_vt = jnp.transpose(cache_v[0], (0, 2, 3, 1))
    scores, sel = _s_scores(page_table, q_s.T, cache_kt, n_blocks)
    sel2 = sel.reshape(db, MOBA_TOPK * N_HEADS)
    attn_s = _s_attn(page_table, sel2, scores, q_s.T, k_s.T, v_s.T, cache_vt).reshape(db, ATTN_WIDTH)
    sp = state_pool[0]
    y_s = _s_tail(x_s, attn_s, u_s, ga_s, gb_s, jnp.transpose(sp, (1, 0, 2)), b_gate[0], w_pool[0],
                  row2(pool_scale[0]), w_branch_a[0], w_branch_b[0], w_out[0], row2(ln1_g[0]),
                  row2(ln1_b[0]), w_r, b_r, w_e_gate[0], w_e_up[0], w_e_down[0], row2(ln2_g[0]),
                  row2(ln2_b[0]), alpha)

    heads = (N_HEADS, HEAD_DIM)
    return (
        y_p.reshape(b, s, d),
        y_s.reshape(db, 1, d),
        jnp.transpose(k.reshape((b,) + heads + (s,)), (0, 3, 1, 2))[None],
        jnp.transpose(v.reshape((b,) + heads + (s,)), (0, 3, 1, 2))[None],
        u3[:, s - POOL_STATE:, :][None],
        k_s.reshape((1, db, 1) + heads),
        v_s.reshape((1, db, 1) + heads),
        jnp.concatenate([sp[:, 1:, :], u_s[:, None, :]], axis=1)[None],
    )
```

```python
import functools

import jax
import jax.numpy as jnp
from jax import lax
from jax.experimental import pallas as pl
from jax.experimental.pallas import tpu as pltpu

F32 = jnp.float32
BF16 = jnp.bfloat16
HIGHEST = lax.Precision.HIGHEST

N_HEADS = 8
HEAD_DIM = 64
ATTN_WIDTH = N_HEADS * HEAD_DIM
MOBA_BLOCK = 256
MOBA_TOPK = 3
ROPE_THETA = 10000.0
POOL_WINDOWS = (2, 4, 8, 16)
POOL_GROUP_WIDTH = 128
POOL_WIDTH = 512
POOL_STATE = 15
N_EXPERT_GROUPS = 4
EXPERTS_PER_GROUP = 4
N_EXPERTS = 16
LN_EPS = 1e-5
LANES = 128
HALO_ROWS = 16
MASK_BIAS = -1e30
VMEM_LIMIT = 56 * 1024 * 1024

NT_DIMS = (((1,), (1,)), ((), ()))


def _dot(a, b, precision=None):
    return jnp.dot(a, b, preferred_element_type=F32, precision=precision)


def _dot_nt(a, b, precision=None):
    return lax.dot_general(a, b, NT_DIMS, preferred_element_type=F32, precision=precision)


def _layer_norm(y, g, b):
    mu = jnp.mean(y, axis=-1, keepdims=True)
    var = jnp.mean(jnp.square(y - mu), axis=-1, keepdims=True)
    return (y - mu) * lax.rsqrt(var + LN_EPS) * g + b


def _sigmoid(x):
    return 1.0 / (1.0 + jnp.exp(-x))


def _rope_chunk(x, cos, sin_signed, first_half):
    partner = jnp.where(first_half, pltpu.roll(x, 96, 1), pltpu.roll(x, 32, 1))
    return x * cos + partner * sin_signed


def _route_weights(logit):
    g = [logit(k) for k in range(N_EXPERT_GROUPS)]
    gmax = jnp.maximum(jnp.maximum(g[0], g[1]), jnp.maximum(g[2], g[3]))
    den = sum(jnp.exp(gk - gmax) for gk in g)
    g_w = 1.0 / den
    is_g = []
    taken = jnp.zeros_like(gmax)
    for k in range(N_EXPERT_GROUPS):
        hit = jnp.where(g[k] == gmax, 1.0, 0.0) * (1.0 - taken)
        is_g.append(hit)
        taken = taken + hit
    e = []
    for k in range(EXPERTS_PER_GROUP):
        col = jnp.zeros_like(gmax)
        for gi in range(N_EXPERT_GROUPS):
            col = jnp.where(is_g[gi] > 0.5, logit(N_EXPERT_GROUPS + gi * EXPERTS_PER_GROUP + k), col)
        e.append(col)
    v1 = jnp.maximum(jnp.maximum(e[0], e[1]), jnp.maximum(e[2], e[3]))
    first = []
    taken = jnp.zeros_like(v1)
    for k in range(EXPERTS_PER_GROUP):
        hit = jnp.where(e[k] == v1, 1.0, 0.0) * (1.0 - taken)
        first.append(hit)
        taken = taken + hit
    e2 = [jnp.where(first[k] > 0.5, -jnp.inf, e[k]) for k in range(EXPERTS_PER_GROUP)]
    v2 = jnp.maximum(jnp.maximum(e2[0], e2[1]), jnp.maximum(e2[2], e2[3]))
    second = []
    taken = jnp.zeros_like(v2)
    for k in range(EXPERTS_PER_GROUP):
        hit = jnp.where(e2[k] == v2, 1.0, 0.0) * (1.0 - taken)
        second.append(hit)
        taken = taken + hit
    t = jnp.exp(v2 - v1)
    w1 = 1.0 / (1.0 + t)
    w2 = t * w1
    return [is_g[gi] * (first[k] * w1 + second[k] * w2) * g_w
            for gi in range(N_EXPERT_GROUPS) for k in range(EXPERTS_PER_GROUP)]


def _route(logits):
    lane = lax.broadcasted_iota(jnp.int32, logits.shape, 1)
    comb = jnp.zeros(logits.shape, F32)
    for e, col in enumerate(_route_weights(lambda k: logits[:, k:k + 1])):
        comb = jnp.where(lane == e, col, comb)
    return comb


def _route_rows(logits_t):
    tokens = logits_t.shape[1]
    row = lax.broadcasted_iota(jnp.int32, (N_EXPERTS, tokens), 0)
    comb = jnp.zeros((N_EXPERTS, tokens), F32)
    for e, r in enumerate(_route_weights(lambda k: logits_t[k:k + 1, :])):
        comb = jnp.where(row == e, r, comb)
    return jnp.concatenate([comb, jnp.zeros((logits_t.shape[0] - N_EXPERTS, tokens), F32)], axis=0)


Q_SCALE = 1.4426950408889634 * HEAD_DIM ** -0.5


def _qkvu_kernel(x_ref, w_ref, cos_ref, sin_ref, qt_ref, k_ref, v_ref, u_ref, kb_ref, vt_ref,
                 km_ref, *, tm):
    xb = x_ref[...].astype(BF16)
    cos = cos_ref[...]
    sin = sin_ref[...]
    lane = lax.broadcasted_iota(jnp.int32, (tm, LANES), 1)
    first_half = (lane & 32) == 0
    hq = _dot(xb, w_ref[:, 0:ATTN_WIDTH].astype(BF16))
    hk = _dot(xb, w_ref[:, ATTN_WIDTH:2 * ATTN_WIDTH].astype(BF16))
    hv = _dot(xb, w_ref[:, 2 * ATTN_WIDTH:3 * ATTN_WIDTH].astype(BF16))
    for c in range(ATTN_WIDTH // LANES):
        sl = slice(c * LANES, (c + 1) * LANES)
        qc = _rope_chunk(hq[:, sl], cos, sin, first_half) * Q_SCALE
        kc = _rope_chunk(hk[:, sl], cos, sin, first_half)
        kb_ref[:, sl] = kc.astype(BF16)
        k_ref[sl, :] = kc.T
        vct = hv[:, sl].T
        v_ref[sl, :] = vct
        qct = qc.T.astype(BF16)
        for r in range(tm // MOBA_BLOCK):
            rows = slice(r * MOBA_BLOCK, (r + 1) * MOBA_BLOCK)
            km_ref[r:r + 1, sl] = jnp.sum(kc[rows], axis=0, keepdims=True) * (1.0 / MOBA_BLOCK)
            qt_ref[r, sl, :] = qct[:, rows]
            vt_ref[r, sl, :] = vct[:, rows].astype(BF16)
    u_ref[...] = _dot(xb, w_ref[:, 3 * ATTN_WIDTH:].astype(BF16))


def _qkvu(x2d, w_qkvu, cos, sin, seq, tm):
    n, d = x2d.shape
    s_tiles = seq // tm
    nblk = tm // MOBA_BLOCK
    row = lambda i: (i, 0)
    blocked = lambda i: (i, 0, 0)
    featmajor = lambda i: (i // s_tiles, 0, i % s_tiles)
    out_shape = (
        jax.ShapeDtypeStruct((n // MOBA_BLOCK, ATTN_WIDTH, MOBA_BLOCK), BF16),
        jax.ShapeDtypeStruct((n // seq, ATTN_WIDTH, seq), F32),
        jax.ShapeDtypeStruct((n // seq, ATTN_WIDTH, seq), F32),
        jax.ShapeDtypeStruct((n, POOL_WIDTH), F32),
        jax.ShapeDtypeStruct((n, ATTN_WIDTH), BF16),
        jax.ShapeDtypeStruct((n // MOBA_BLOCK, ATTN_WIDTH, MOBA_BLOCK), BF16),
        jax.ShapeDtypeStruct((n // tm, nblk, ATTN_WIDTH), F32),
    )
    return pl.pallas_call(
        functools.partial(_qkvu_kernel, tm=tm),
        out_shape=out_shape,
        grid=(n // tm,),
        in_specs=[
            pl.BlockSpec((tm, d), row),
            pl.BlockSpec((d, 4 * ATTN_WIDTH), lambda i: (0, 0)),
            pl.BlockSpec((tm, LANES), lambda i: (i % s_tiles, 0)),
            pl.BlockSpec((tm, LANES), lambda i: (i % s_tiles, 0)),
        ],
        out_specs=(
            pl.BlockSpec((nblk, ATTN_WIDTH, MOBA_BLOCK), blocked),
            pl.BlockSpec((None, ATTN_WIDTH, tm), featmajor),
            pl.BlockSpec((None, ATTN_WIDTH, tm), featmajor),
            pl.BlockSpec((tm, POOL_WIDTH), row),
            pl.BlockSpec((tm, ATTN_WIDTH), row),
            pl.BlockSpec((nblk, ATTN_WIDTH, MOBA_BLOCK), blocked),
            pl.BlockSpec((None, nblk, ATTN_WIDTH), blocked),
        ),
        compiler_params=pltpu.CompilerParams(
            dimension_semantics=("arbitrary",), vmem_limit_bytes=VMEM_LIMIT),
        name="qkvu",
    )(x2d, w_qkvu, cos, sin)


BIAS_ROWS = 128
SUM_ROWS = 16
ATTN_PAIRS = 4


def _attn_kernel(qt_ref, k_ref, vt_ref, km_ref, o_ref, qa_ref, acc_ref, sa_ref, sb_ref):
    i = pl.program_id(2)
    n_past = km_ref.shape[0]
    n_heads = 2 * ATTN_PAIRS
    feat = lax.broadcasted_iota(jnp.int32, (LANES, MOBA_BLOCK), 0)
    head0 = feat < HEAD_DIM
    blk_id = lax.broadcasted_iota(jnp.int32, (n_past, MOBA_BLOCK), 0)
    blk_f = blk_id.astype(F32)
    past = blk_id < i
    key_id = lax.broadcasted_iota(jnp.int32, (MOBA_BLOCK, MOBA_BLOCK), 0)
    qry_id = lax.broadcasted_iota(jnp.int32, (MOBA_BLOCK, MOBA_BLOCK), 1)
    causal = key_id <= qry_id
    lane = lax.broadcasted_iota(jnp.int32, (MOBA_BLOCK, LANES), 1)
    ones_rows = jnp.ones((SUM_ROWS, MOBA_BLOCK), BF16)
    bias_pad = jnp.zeros((BIAS_ROWS - n_past, MOBA_BLOCK), BF16)
    own = pl.multiple_of(i * MOBA_BLOCK, MOBA_BLOCK)

    def slab(pair):
        return slice(pair * LANES, (pair + 1) * LANES)

    def vt_aug(blk, pair):
        return jnp.concatenate([vt_ref[blk, slab(pair), :], ones_rows], axis=0)

    def consume_refill(s_ref, ms, blk, nxt):
        off = pl.multiple_of(nxt * MOBA_BLOCK, MOBA_BLOCK)
        onehot = jnp.where(lane == nxt, 1.0, 0.0).astype(BF16)
        out = []
        for h in range(n_heads):
            s = s_ref[h]
            m_new = jnp.maximum(ms[h], jnp.max(s, axis=0, keepdims=True))
            alpha = jnp.exp2(ms[h] - m_new)
            p = jnp.exp2(s - m_new)
            acc_ref[h] = alpha * acc_ref[h] + _dot(vt_aug(blk, h // 2), p.astype(BF16))
            out.append(m_new)
            ka = jnp.concatenate([k_ref[pl.ds(off, MOBA_BLOCK), slab(h // 2)], onehot], axis=1)
            s_ref[h] = _dot(ka, qa_ref[h])
        return tuple(out)

    for h in range(n_heads):
        pair = h // 2
        qt = qt_ref[slab(pair), :]
        qh = jnp.where(head0 if h % 2 == 0 else jnp.logical_not(head0), qt, jnp.zeros_like(qt))
        gate = _dot(km_ref[:, slab(pair)], qh.astype(F32), HIGHEST)
        g = jnp.where(past, gate, -jnp.inf)
        sel = jnp.zeros(g.shape, F32)
        for _ in range(MOBA_TOPK):
            mx = jnp.max(g, axis=0, keepdims=True)
            idx = jnp.min(jnp.where(g == mx, blk_f, float(n_past)), axis=0, keepdims=True)
            pick = blk_f == idx
            sel = jnp.where(pick, 1.0, sel)
            g = jnp.where(pick, -jnp.inf, g)
        bias = jnp.where(jnp.logical_and(sel > 0.5, past), 0.0, MASK_BIAS).astype(BF16)
        qa_ref[h] = jnp.concatenate([qh, bias, bias_pad], axis=0)
        kd = k_ref[pl.ds(own, MOBA_BLOCK), slab(pair)]
        sa_ref[h] = jnp.where(causal, _dot(kd, qh), -jnp.inf)
        acc_ref[h] = jnp.zeros(acc_ref.shape[1:], F32)
    m0 = jnp.full((1, MOBA_BLOCK), MASK_BIAS, F32)

    last_past = jnp.maximum(i - 1, 0)
    off0 = pl.multiple_of(0 * MOBA_BLOCK, MOBA_BLOCK)
    onehot0 = jnp.where(lane == 0, 1.0, 0.0).astype(BF16)
    for h in range(n_heads):
        ka = jnp.concatenate([k_ref[pl.ds(off0, MOBA_BLOCK), slab(h // 2)], onehot0], axis=1)
        sb_ref[h] = _dot(ka, qa_ref[h])

    def pair_body(u, ms):
        ms = consume_refill(sa_ref, ms, jnp.where(u == 0, i, 2 * u - 1),
                            jnp.minimum(2 * u + 1, last_past))
        return consume_refill(sb_ref, ms, 2 * u, jnp.minimum(2 * u + 2, last_past))

    ms = lax.fori_loop(0, (i + 1) // 2, pair_body, (m0,) * n_heads)

    @pl.when((i + 1) % 2 == 1)
    def _():
        blk = jnp.where(i == 0, i, i - 1)
        for h in range(n_heads):
            s = sa_ref[h]
            m_new = jnp.maximum(ms[h], jnp.max(s, axis=0, keepdims=True))
            alpha = jnp.exp2(ms[h] - m_new)
            p = jnp.exp2(s - m_new)
            acc_ref[h] = alpha * acc_ref[h] + _dot(vt_aug(blk, h // 2), p.astype(BF16))

    for pair in range(ATTN_PAIRS):
        a0 = acc_ref[2 * pair]
        a1 = acc_ref[2 * pair + 1]
        ot = jnp.where(head0, a0[:LANES] / a0[LANES:LANES + 1], a1[:LANES] / a1[LANES:LANES + 1])
        o_ref[:, slab(pair)] = ot.T.astype(BF16)


def _attn(qt, kb, vt, km):
    b, s, _ = kb.shape
    nb = s // MOBA_BLOCK
    width = ATTN_PAIRS * LANES
    return pl.pallas_call(
        _attn_kernel,
        out_shape=jax.ShapeDtypeStruct((b, s, ATTN_WIDTH), BF16),
        grid=(b, ATTN_WIDTH // width, nb),
        in_specs=[
            pl.BlockSpec((None, width, MOBA_BLOCK), lambda bi, p, i: (bi * nb + i, p, 0)),
            pl.BlockSpec((None, s, width), lambda bi, p, i: (bi, 0, p)),
            pl.BlockSpec((None, nb, width, MOBA_BLOCK), lambda bi, p, i: (bi, 0, p, 0)),
            pl.BlockSpec((None, nb, width), lambda bi, p, i: (bi, 0, p)),
        ],
        out_specs=pl.BlockSpec((None, MOBA_BLOCK, width), lambda bi, p, i: (bi, i, p)),
        scratch_shapes=[
            pltpu.VMEM((2 * ATTN_PAIRS, LANES + BIAS_ROWS, MOBA_BLOCK), BF16),
            pltpu.VMEM((2 * ATTN_PAIRS, LANES + SUM_ROWS, MOBA_BLOCK), F32),
            pltpu.VMEM((2 * ATTN_PAIRS, MOBA_BLOCK, MOBA_BLOCK), F32),
            pltpu.VMEM((2 * ATTN_PAIRS, MOBA_BLOCK, MOBA_BLOCK), F32),
        ],
        compiler_params=pltpu.CompilerParams(
            dimension_semantics=("arbitrary", "arbitrary", "arbitrary"),
            vmem_limit_bytes=VMEM_LIMIT),
        name="moba_attn",
    )(qt, kb, vt, km)


def _pool_group(window_sum, u_g, count, w_pool_g, scale_g, precision=None, cast=None):
    d = window_sum / count - u_g
    if cast is not None:
        d = d.astype(cast)
    return _dot(d, w_pool_g, precision) * scale_g


def _mix_kernel(x_ref, attn_ref, u_ref, halo_ref, wg_ref, bg_ref, wpool_ref, pscale_ref, wa_ref,
                wb_ref, wout_ref, g1_ref, b1_ref, x1_ref, z_ref, *, tm, alpha):
    i = pl.program_id(1)
    x = x_ref[...]
    xb = x.astype(BF16)
    u = u_ref[...]
    z_ref[0:HALO_ROWS, :] = jnp.where(i > 0, halo_ref[...], 0.0)
    z_ref[HALO_ROWS:, :] = u
    pos1 = (i * tm + 1 + lax.broadcasted_iota(jnp.int32, (tm, 1), 0)).astype(F32)
    parts = []
    for g, w in enumerate(POOL_WINDOWS):
        sl = slice(g * POOL_GROUP_WIDTH, (g + 1) * POOL_GROUP_WIDTH)
        acc = u[:, sl]
        for back in range(1, w):
            acc = acc + z_ref[HALO_ROWS - back:HALO_ROWS - back + tm, sl]
        count = jnp.minimum(float(w), pos1)
        parts.append(_pool_group(acc, u[:, sl], count, wpool_ref[g].astype(BF16), pscale_ref[:, sl],
                                 cast=BF16))
    pooled = jnp.concatenate(parts, axis=1).astype(BF16)
    d = x.shape[1]
    ga = _dot(xb, wg_ref[:, :d].astype(BF16)) + bg_ref[0:1, :]
    gb = _dot(xb, wg_ref[:, d:].astype(BF16)) + bg_ref[1:2, :]
    a = _dot(attn_ref[...], wa_ref[...].astype(BF16))
    bb = _dot(pooled, wb_ref[...].astype(BF16))
    merged = _sigmoid(ga) * a + _sigmoid(gb) * bb
    y = alpha * x + _dot(merged.astype(BF16), wout_ref[...].astype(BF16))
    x1_ref[...] = _layer_norm(y, g1_ref[...], b1_ref[...])


def _mix(x, attn, u, w_in, b_gate, w_pool, pool_scale, w_a, w_b, w_out, ln_g, ln_b, tm, alpha):
    b, s, d = x.shape
    gate_block = w_in.shape[1] // (2 * d) - 1
    assert w_in.shape[1] == (gate_block + 1) * 2 * d
    w_gates = w_in
    halo_per_tile = tm // HALO_ROWS
    tile = lambda bi, i: (bi, i, 0)
    const2 = lambda bi, i: (0, 0)
    return pl.pallas_call(
        functools.partial(_mix_kernel, tm=tm, alpha=alpha),
        out_shape=jax.ShapeDtypeStruct((b, s, d), F32),
        grid=(b, s // tm),
        in_specs=[
            pl.BlockSpec((None, tm, d), tile),
            pl.BlockSpec((None, tm, ATTN_WIDTH), tile),
            pl.BlockSpec((None, tm, POOL_WIDTH), tile),
            pl.BlockSpec((None, HALO_ROWS, POOL_WIDTH),
                         lambda bi, i: (bi, jnp.maximum(i * halo_per_tile - 1, 0), 0)),
            pl.BlockSpec((d, 2 * d), lambda bi, i: (0, gate_block)),
            pl.BlockSpec(b_gate.shape, const2),
            pl.BlockSpec(w_pool.shape, lambda bi, i: (0, 0, 0)),
            pl.BlockSpec(pool_scale.shape, const2),
            pl.BlockSpec(w_a.shape, const2),
            pl.BlockSpec(w_b.shape, const2),
            pl.BlockSpec(w_out.shape, const2),
            pl.BlockSpec(ln_g.shape, const2),
            pl.BlockSpec(ln_b.shape, const2),
        ],
        out_specs=pl.BlockSpec((None, tm, d), tile),
        scratch_shapes=[pltpu.VMEM((HALO_ROWS + tm, POOL_WIDTH), F32)],
        compiler_params=pltpu.CompilerParams(
            dimension_semantics=("arbitrary", "arbitrary"), vmem_limit_bytes=VMEM_LIMIT),
        name="mix",
    )(x, attn, u, u, w_gates, b_gate, w_pool, pool_scale, w_a, w_b, w_out, ln_g, ln_b)


def _moe_kernel(x1_ref, wrh_ref, wrl_ref, br_ref, wg_ref, wu_ref, wd_ref, g2_ref, b2_ref, y_ref,
                xb_ref, comb_ref, acc_ref, *, alpha):
    g = pl.program_id(1)
    tm = x1_ref.shape[0]
    lane = lax.broadcasted_iota(jnp.int32, (tm, LANES), 1)

    @pl.when(g == 0)
    def _():
        x1 = x1_ref[...]
        xh = x1.astype(BF16)
        xb_ref[...] = xh
        xl = (x1 - xh.astype(F32)).astype(BF16)
        logits = (_dot(xh, wrh_ref[...]) + _dot(xl, wrh_ref[...]) + _dot(xh, wrl_ref[...])
                  + br_ref[...])
        comb_ref[...] = _route_rows(logits.T).T

    xb = xb_ref[...]
    comb = comb_ref[...]
    hidden = []
    for k in range(EXPERTS_PER_GROUP):
        hg = _dot(xb, wg_ref[k].astype(BF16))
        hu = _dot(xb, wu_ref[k].astype(BF16))
        c_k = jnp.sum(jnp.where(lane == g * EXPERTS_PER_GROUP + k, comb, 0.0), axis=1, keepdims=True)
        hidden.append((hg * _sigmoid(hg) * hu * c_k).astype(BF16))
    part = _dot(jnp.concatenate(hidden, axis=1), wd_ref[...].astype(BF16))

    @pl.when(g == 0)
    def _():
        acc_ref[...] = part

    @pl.when(g > 0)
    def _():
        acc_ref[...] += part

    @pl.when(g == N_EXPERT_GROUPS - 1)
    def _():
        y = alpha * x1_ref[...] + acc_ref[...]
        y_ref[...] = _layer_norm(y, g2_ref[...], b2_ref[...])


def _moe(x1, w_r, b_r, w_g, w_u, w_d, ln_g, ln_b, tm, alpha):
    w_r_hi = w_r.astype(BF16)
    w_r_lo = (w_r - w_r_hi.astype(F32)).astype(BF16)
    n, d = x1.shape
    f = w_g.shape[2]
    per_group = (N_EXPERT_GROUPS, EXPERTS_PER_GROUP)
    tile = lambda i, g: (i, 0)
    const2 = lambda i, g: (0, 0)
    group4 = lambda i, g: (g, 0, 0, 0)
    return pl.pallas_call(
        functools.partial(_moe_kernel, alpha=alpha),
        out_shape=jax.ShapeDtypeStruct((n, d), F32),
        grid=(n // tm, N_EXPERT_GROUPS),
        in_specs=[
            pl.BlockSpec((tm, d), tile),
            pl.BlockSpec(w_r.shape, const2),
            pl.BlockSpec(w_r.shape, const2),
            pl.BlockSpec(b_r.shape, const2),
            pl.BlockSpec((None, EXPERTS_PER_GROUP, d, f), group4),
            pl.BlockSpec((None, EXPERTS_PER_GROUP, d, f), group4),
            pl.BlockSpec((None, EXPERTS_PER_GROUP * f, d), lambda i, g: (g, 0, 0)),
            pl.BlockSpec(ln_g.shape, const2),
            pl.BlockSpec(ln_b.shape, const2),
        ],
        out_specs=pl.BlockSpec((tm, d), tile),
        scratch_shapes=[
            pltpu.VMEM((tm, d), BF16),
            pltpu.VMEM((tm, LANES), F32),
            pltpu.VMEM((tm, d), F32),
        ],
        compiler_params=pltpu.CompilerParams(
            dimension_semantics=("arbitrary", "arbitrary"), vmem_limit_bytes=VMEM_LIMIT),
        name="moe",
    )(x1, w_r_hi, w_r_lo, b_r, w_g.reshape(per_group + (d, f)), w_u.reshape(per_group + (d, f)),
      w_d.reshape(N_EXPERT_GROUPS, EXPERTS_PER_GROUP * f, d), ln_g, ln_b)


def _s_proj_kernel(x_ref, w_ref, cos_ref, sin_ref, h_ref):
    c = pl.program_id(0)
    h = _dot(x_ref[...], w_ref[...], HIGHEST)
    rows = h.shape[0]
    lane = lax.broadcasted_iota(jnp.int32, (rows, LANES), 1)
    first_half = (lane & 32) == 0
    rotary = c < 2
    for j in range(h.shape[1] // LANES):
        sl = slice(j * LANES, (j + 1) * LANES)
        hc = h[:, sl]
        h_ref[:, sl] = jnp.where(rotary, _rope_chunk(hc, cos_ref[...], sin_ref[...], first_half), hc)


def _s_proj(x, w_in, cos, sin):
    rows, d = x.shape
    width = w_in.shape[1]
    chunk = ATTN_WIDTH
    return pl.pallas_call(
        _s_proj_kernel,
        out_shape=jax.ShapeDtypeStruct((rows, width), F32),
        grid=(width // chunk,),
        in_specs=[
            pl.BlockSpec((rows, d), lambda c: (0, 0)),
            pl.BlockSpec((d, chunk), lambda c: (0, c)),
            pl.BlockSpec((1, LANES), lambda c: (0, 0)),
            pl.BlockSpec((1, LANES), lambda c: (0, 0)),
        ],
        out_specs=pl.BlockSpec((rows, chunk), lambda c: (0, c)),
        compiler_params=pltpu.CompilerParams(
            dimension_semantics=("arbitrary",), vmem_limit_bytes=VMEM_LIMIT),
        name="s_proj",
    )(x, w_in, cos, sin)


PAGES_PER_STEP = 32


def _column(ref, b):
    x = ref[...]
    lane = lax.broadcasted_iota(jnp.int32, x.shape, 1)
    col = jnp.sum(jnp.where(lane == b, x, 0.0), axis=1, keepdims=True)
    return col.reshape(N_HEADS, HEAD_DIM, 1)


def _s_scores_kernel(pt_ref, q_ref, *refs, page_size, n_blocks):
    page_refs = refs[:PAGES_PER_STEP]
    sc_ref, sel_ref, gs_ref = refs[PAGES_PER_STEP:]
    c = pl.program_id(1)
    pages_per_block = MOBA_BLOCK // page_size
    blocks_per_step = PAGES_PER_STEP // pages_per_block
    q = _column(q_ref, pl.program_id(0)) * (HEAD_DIM ** -0.5)
    for r in range(blocks_per_step):
        tot = jnp.zeros((N_HEADS, page_size), F32)
        for pp in range(pages_per_block):
            p = r * pages_per_block + pp
            s = jnp.sum(page_refs[p][...] * q, axis=1)
            sc_ref[p] = s
            tot = tot + s
        gs_ref[c * blocks_per_step + r] = jnp.sum(tot, axis=1, keepdims=True)

    @pl.when(c == pl.num_programs(1) - 1)
    def _():
        g = gs_ref[...]
        blk_f = lax.broadcasted_iota(jnp.int32, g.shape, 0).astype(F32)
        for t in range(MOBA_TOPK):
            mx = jnp.max(g, axis=0, keepdims=True)
            idx = jnp.min(jnp.where(g == mx, blk_f, float(n_blocks)), axis=0, keepdims=True)
            sel_ref[t:t + 1] = idx.astype(jnp.int32)
            g = jnp.where(blk_f == idx, -jnp.inf, g)


def _s_scores(page_table, qt, cache_kt, n_blocks):
    db, n_pages = page_table.shape
    page_size = cache_kt.shape[3]
    page_block = (None, N_HEADS, HEAD_DIM, page_size)

    def page_spec(r):
        return pl.BlockSpec(page_block, lambda b, c, pt: (pt[b, c * PAGES_PER_STEP + r], 0, 0, 0))

    return pl.pallas_call(
        functools.partial(_s_scores_kernel, page_size=page_size, n_blocks=n_blocks),
        out_shape=(jax.ShapeDtypeStruct((db, n_pages, N_HEADS, page_size), F32),
                   jax.ShapeDtypeStruct((db, MOBA_TOPK, N_HEADS, 1), jnp.int32)),
        grid_spec=pltpu.PrefetchScalarGridSpec(
            num_scalar_prefetch=1,
            grid=(db, n_pages // PAGES_PER_STEP),
            in_specs=[pl.BlockSpec(qt.shape, lambda b, c, pt: (0, 0))]
            + [page_spec(r) for r in range(PAGES_PER_STEP)],
            out_specs=(pl.BlockSpec((None, PAGES_PER_STEP, N_HEADS, page_size),
                                    lambda b, c, pt: (b, c, 0, 0)),
                       pl.BlockSpec((None, MOBA_TOPK, N_HEADS, 1), lambda b, c, pt: (b, 0, 0, 0))),
            scratch_shapes=[pltpu.VMEM((n_blocks, N_HEADS, 1), F32)],
        ),
        compiler_params=pltpu.CompilerParams(
            dimension_semantics=("arbitrary", "arbitrary"), vmem_limit_bytes=VMEM_LIMIT),
        name="s_scores",
    )(page_table, qt, *([cache_kt] * PAGES_PER_STEP))


def _s_attn_kernel(pt_ref, sel_ref, sc_ref, q_ref, kn_ref, vn_ref, cv_ref, o_ref, vbuf, sem,
                   *, pages_per_block):
    b = pl.program_id(0)
    slices_per_seq = N_HEADS * MOBA_TOPK * pages_per_block

    def slot_index(slot, h, t, r):
        return slot * slices_per_seq + (h * MOBA_TOPK + t) * pages_per_block + r

    def copies(seq, slot):
        out = []
        for h in range(N_HEADS):
            for t in range(MOBA_TOPK):
                blk = sel_ref[seq, t * N_HEADS + h]
                for r in range(pages_per_block):
                    page = pt_ref[seq, blk * pages_per_block + r]
                    out.append(pltpu.make_async_copy(
                        cv_ref.at[page, h], vbuf.at[slot_index(slot, h, t, r)], sem.at[slot]))
        return out

    @pl.when(b == 0)
    def _():
        for cp in copies(0, 0):
            cp.start()

    @pl.when(b + 1 < pl.num_programs(0))
    def _():
        for cp in copies(b + 1, (b + 1) % 2):
            cp.start()

    slot = b % 2
    for cp in copies(b, slot):
        cp.wait()

    scale = HEAD_DIM ** -0.5
    q = _column(q_ref, b)
    kn = _column(kn_ref, b)
    vn = _column(vn_ref, b)
    for h in range(N_HEADS):
        s_self = jnp.sum(q[h] * kn[h], axis=0, keepdims=True) * scale
        scores = []
        mx = s_self
        for t in range(MOBA_TOPK):
            blk = sel_ref[b, t * N_HEADS + h]
            for r in range(pages_per_block):
                sc = sc_ref[blk * pages_per_block + r, h:h + 1, :]
                scores.append(sc)
                mx = jnp.maximum(mx, jnp.max(sc, axis=1, keepdims=True))
        p_self = jnp.exp(s_self - mx)
        den = p_self
        acc = p_self * vn[h]
        n = 0
        for t in range(MOBA_TOPK):
            for r in range(pages_per_block):
                p = jnp.exp(scores[n] - mx)
                n += 1
                den = den + jnp.sum(p, axis=1, keepdims=True)
                acc = acc + jnp.sum(vbuf[slot_index(slot, h, t, r)] * p, axis=1, keepdims=True)
        o_ref[h] = acc / den


def _s_attn(page_table, sel, scores, qt, knt, vnt, cache_vt):
    db, n_pages = page_table.shape
    page_size = cache_vt.shape[3]
    pages_per_block = MOBA_BLOCK // page_size
    n_slices = 2 * N_HEADS * MOBA_TOPK * pages_per_block
    per_b = pl.BlockSpec((None, N_HEADS, HEAD_DIM, 1), lambda b, pt, sl: (b, 0, 0, 0))
    whole = pl.BlockSpec(qt.shape, lambda b, pt, sl: (0, 0))
    return pl.pallas_call(
        functools.partial(_s_attn_kernel, pages_per_block=pages_per_block),
        out_shape=jax.ShapeDtypeStruct((db, N_HEADS, HEAD_DIM, 1), F32),
        grid_spec=pltpu.PrefetchScalarGridSpec(
            num_scalar_prefetch=2,
            grid=(db,),
            in_specs=[pl.BlockSpec((None, n_pages, N_HEADS, page_size),
                                   lambda b, pt, sl: (b, 0, 0, 0)),
                      whole, whole, whole, pl.BlockSpec(memory_space=pl.ANY)],
            out_specs=per_b,
            scratch_shapes=[pltpu.VMEM((n_slices, HEAD_DIM, page_size), F32),
                            pltpu.SemaphoreType.DMA((2,))],
        ),
        compiler_params=pltpu.CompilerParams(
            dimension_semantics=("arbitrary",), vmem_limit_bytes=VMEM_LIMIT),
        name="s_attn",
    )(page_table, sel, scores, qt, knt, vnt, cache_vt)


def _s_tail_kernel(x_ref, attn_ref, u_ref, ga_ref, gb_ref, sp_ref, bg_ref, wpool_ref, pscale_ref,
                   wa_ref, wb_ref, wout_ref, g1_ref, b1_ref, wr_ref, br_ref, wg_ref, wu_ref,
                   wd_ref, g2_ref, b2_ref, y_ref, x1_ref, comb_ref, acc_ref, *, alpha):
    e = pl.program_id(0)
    rows = x_ref.shape[0]
    lane = lax.broadcasted_iota(jnp.int32, (rows, LANES), 1)

    @pl.when(e == 0)
    def _():
        u = u_ref[...]
        parts = []
        for g, w in enumerate(POOL_WINDOWS):
            sl = slice(g * POOL_GROUP_WIDTH, (g + 1) * POOL_GROUP_WIDTH)
            acc = u[:, sl]
            for back in range(1, w):
                acc = acc + sp_ref[POOL_STATE - back][:, sl]
            parts.append(_pool_group(acc, u[:, sl], float(w), wpool_ref[g], pscale_ref[:, sl],
                                     precision=HIGHEST))
        pooled = jnp.concatenate(parts, axis=1)
        a = _dot(attn_ref[...], wa_ref[...], HIGHEST)
        bb = _dot(pooled, wb_ref[...], HIGHEST)
        merged = _sigmoid(ga_ref[...] + bg_ref[0:1, :]) * a + _sigmoid(gb_ref[...] + bg_ref[1:2, :]) * bb
        y = alpha * x_ref[...] + _dot(merged, wout_ref[...], HIGHEST)
        x1 = _layer_norm(y, g1_ref[...], b1_ref[...])
        x1_ref[...] = x1
        comb_ref[...] = _route(_dot(x1, wr_ref[...], HIGHEST) + br_ref[...])
        acc_ref[...] = jnp.zeros_like(acc_ref)

    x1 = x1_ref[...]
    hg = _dot(x1, wg_ref[...], HIGHEST)
    hu = _dot(x1, wu_ref[...], HIGHEST)
    c_e = jnp.sum(jnp.where(lane == e, comb_ref[...], 0.0), axis=1, keepdims=True)
    h = hg * _sigmoid(hg) * hu * c_e
    acc_ref[...] += _dot(h, wd_ref[...], HIGHEST)

    @pl.when(e == N_EXPERTS - 1)
    def _():
        y_ref[...] = _layer_norm(alpha * x1_ref[...] + acc_ref[...], g2_ref[...], b2_ref[...])


def _s_tail(x, attn, u, ga, gb, sp_t, b_gate, w_pool, pool_scale, w_a, w_b, w_out, ln1_g, ln1_b,
            w_r, b_r, w_g, w_u, w_d, ln2_g, ln2_b, alpha):
    rows, d = x.shape
    f = w_g.shape[2]
    const2 = lambda e: (0, 0)
    const3 = lambda e: (0, 0, 0)
    full = lambda a: pl.BlockSpec(a.shape, const2 if a.ndim == 2 else const3)
    per_e = lambda blk: pl.BlockSpec((None,) + blk, lambda e: (e, 0, 0))
    small = [x, attn, u, ga, gb, sp_t, b_gate, w_pool, pool_scale, w_a, w_b, w_out, ln1_g, ln1_b,
             w_r, b_r]
    return pl.pallas_call(
        functools.partial(_s_tail_kernel, alpha=alpha),
        out_shape=jax.ShapeDtypeStruct((rows, d), F32),
        grid=(N_EXPERTS,),
        in_specs=[full(a) for a in small]
        + [per_e((d, f)), per_e((d, f)), per_e((f, d)), full(ln2_g), full(ln2_b)],
        out_specs=pl.BlockSpec((rows, d), const2),
        scratch_shapes=[pltpu.VMEM((rows, d), F32), pltpu.VMEM((rows, LANES), F32),
                        pltpu.VMEM((rows, d), F32)],
        compiler_params=pltpu.CompilerParams(
            dimension_semantics=("arbitrary",), vmem_limit_bytes=VMEM_LIMIT),
        name="s_tail",
    )(*small, w_g, w_u, w_d, ln2_g, ln2_b)


def _rope_tables(pos):
    inv_freq = 1.0 / (ROPE_THETA ** (jnp.arange(0, HEAD_DIM, 2, dtype=F32) / HEAD_DIM))
    ang = pos.astype(F32)[:, None] * inv_freq[None, :]
    cos = jnp.cos(ang)
    sin = jnp.sin(ang)
    cos_t = jnp.tile(cos, (1, LANES // (HEAD_DIM // 2)))
    sin_t = jnp.tile(jnp.concatenate([-sin, sin], axis=1), (1, LANES // HEAD_DIM))
    return cos_t, sin_t


def kernel(x_prompt, x_sample, cache_k, cache_v, state_pool, page_table, w_in, w_pool, pool_scale,
           w_branch_a, w_branch_b, b_gate, w_out, ln1_g, ln1_b, w_group_router, b_group_router,
           w_expert_router, b_expert_router, w_e_gate, w_e_up, w_e_down, ln2_g, ln2_b):
    depth = w_in.shape[0]
    assert depth == 1 and x_sample.shape[1] == 1
    alpha = (2 * depth) ** 0.25
    b, s, d = x_prompt.shape
    db = x_sample.shape[0]
    n_pages = page_table.shape[1]
    page_size = cache_k.shape[2]
    past_len = n_pages * page_size
    n_blocks = past_len // MOBA_BLOCK
    assert past_len % MOBA_BLOCK == 0 and s % MOBA_BLOCK == 0

    w_in0 = w_in[0]
    qkvu_cols = 3 * ATTN_WIDTH + POOL_WIDTH
    row2 = lambda a: a.reshape(1, -1)
    w_r = jnp.concatenate(
        [w_group_router[0], jnp.transpose(w_expert_router[0], (1, 0, 2)).reshape(d, N_EXPERTS),
         jnp.zeros((d, LANES - N_EXPERT_GROUPS - N_EXPERTS), F32)], axis=1)
    b_r = jnp.concatenate(
        [b_group_router[0], b_expert_router[0].reshape(-1),
         jnp.zeros((LANES - N_EXPERT_GROUPS - N_EXPERTS,), F32)]).reshape(1, LANES)

    cos_p, sin_p = _rope_tables(jnp.arange(s, dtype=jnp.int32))
    tm_a = 512
    assert qkvu_cols == 4 * ATTN_WIDTH
    qt, k, v, u, kb, vt, km = _qkvu(x_prompt.reshape(b * s, d), w_in0, cos_p, sin_p, s, tm_a)
    nb = s // MOBA_BLOCK
    attn = _attn(qt, kb.reshape(b, s, ATTN_WIDTH), vt.reshape(b, nb, ATTN_WIDTH, MOBA_BLOCK),
                 km.reshape(b, nb, ATTN_WIDTH))
    u3 = u.reshape(b, s, POOL_WIDTH)
    x1 = _mix(x_prompt, attn, u3, w_in0, b_gate[0], w_pool[0], row2(pool_scale[0]), w_branch_a[0],
              w_branch_b[0], w_out[0], row2(ln1_g[0]), row2(ln1_b[0]), 256, alpha)
    y_p = _moe(x1.reshape(b * s, d), w_r, b_r, w_e_gate[0], w_e_up[0], w_e_down[0],
               row2(ln2_g[0]), row2(ln2_b[0]), 1024, alpha)

    cos_s, sin_s = _rope_tables(jnp.full((1,), past_len, jnp.int32))
    x_s = x_sample.reshape(db, d)
    h_s = _s_proj(x_s, w_in0, cos_s, sin_s)
    q_s = h_s[:, :ATTN_WIDTH]
    k_s = h_s[:, ATTN_WIDTH:2 * ATTN_WIDTH]
    v_s = h_s[:, 2 * ATTN_WIDTH:3 * ATTN_WIDTH]
    u_s = h_s[:, 3 * ATTN_WIDTH:qkvu_cols]
    ga_s = h_s[:, qkvu_cols:qkvu_cols + d]
    gb_s = h_s[:, qkvu_cols + d:]
    cache_kt = jnp.transpose(cache_k[0], (0, 2, 3, 1))
    cache_vt = jnp.transpose(cache_v[0], (0, 2, 3, 1))
    scores, sel = _s_scores(page_table, q_s.T, cache_kt, n_blocks)
    sel2 = sel.reshape(db, MOBA_TOPK * N_HEADS)
    attn_s = _s_attn(page_table, sel2, scores, q_s.T, k_s.T, v_s.T, cache_vt).reshape(db, ATTN_WIDTH)
    sp = state_pool[0]
    y_s = _s_tail(x_s, attn_s, u_s, ga_s, gb_s, jnp.transpose(sp, (1, 0, 2)), b_gate[0], w_pool[0],
                  row2(pool_scale[0]), w_branch_a[0], w_branch_b[0], w_out[0], row2(ln1_g[0]),
                  row2(ln1_b[0]), w_r, b_r, w_e_gate[0], w_e_up[0], w_e_down[0], row2(ln2_g[0]),
                  row2(ln2_b[0]), alpha)

    heads = (N_HEADS, HEAD_DIM)
    return (
        y_p.reshape(b, s, d),
        y_s.reshape(db, 1, d),
        jnp.transpose(k.reshape((b,) + heads + (s,)), (0, 3, 1, 2))[None],
        jnp.transpose(v.reshape((b,) + heads + (s,)), (0, 3, 1, 2))[None],
        u3[:, s - POOL_STATE:, :][None],
        k_s.reshape((1, db, 1) + heads),
        v_s.reshape((1, db, 1) + heads),
        jnp.concatenate([sp[:, 1:, :], u_s[:, None, :]], axis=1)[None],
    )
```

```python
import functools

import jax
import jax.numpy as jnp
from jax import lax
from jax.experimental import pallas as pl
from jax.experimental.pallas import tpu as pltpu

F32 = jnp.float32
BF16 = jnp.bfloat16
HIGHEST = lax.Precision.HIGHEST

N_HEADS = 8
HEAD_DIM = 64
ATTN_WIDTH = N_HEADS * HEAD_DIM
MOBA_BLOCK = 256
MOBA_TOPK = 3
ROPE_THETA = 10000.0
POOL_WINDOWS = (2, 4, 8, 16)
POOL_GROUP_WIDTH = 128
POOL_WIDTH = 512
POOL_STATE = 15
N_EXPERT_GROUPS = 4
EXPERTS_PER_GROUP = 4
N_EXPERTS = 16
LN_EPS = 1e-5
LANES = 128
HALO_ROWS = 16
MASK_BIAS = -1e30
VMEM_LIMIT = 56 * 1024 * 1024

NT_DIMS = (((1,), (1,)), ((), ()))


def _dot(a, b, precision=None):
    return jnp.dot(a, b, preferred_element_type=F32, precision=precision)


def _dot_nt(a, b, precision=None):
    return lax.dot_general(a, b, NT_DIMS, preferred_element_type=F32, precision=precision)


def _layer_norm(y, g, b):
    mu = jnp.mean(y, axis=-1, keepdims=True)
    var = jnp.mean(jnp.square(y - mu), axis=-1, keepdims=True)
    return (y - mu) * lax.rsqrt(var + LN_EPS) * g + b


def _sigmoid(x):
    return 1.0 / (1.0 + jnp.exp(-x))


def _rope_chunk(x, cos, sin_signed, first_half):
    partner = jnp.where(first_half, pltpu.roll(x, 96, 1), pltpu.roll(x, 32, 1))
    return x * cos + partner * sin_signed


def _route_weights(logit):
    g = [logit(k) for k in range(N_EXPERT_GROUPS)]
    gmax = jnp.maximum(jnp.maximum(g[0], g[1]), jnp.maximum(g[2], g[3]))
    den = sum(jnp.exp(gk - gmax) for gk in g)
    g_w = 1.0 / den
    is_g = []
    taken = jnp.zeros_like(gmax)
    for k in range(N_EXPERT_GROUPS):
        hit = jnp.where(g[k] == gmax, 1.0, 0.0) * (1.0 - taken)
        is_g.append(hit)
        taken = taken + hit
    e = []
    for k in range(EXPERTS_PER_GROUP):
        col = jnp.zeros_like(gmax)
        for gi in range(N_EXPERT_GROUPS):
            col = jnp.where(is_g[gi] > 0.5, logit(N_EXPERT_GROUPS + gi * EXPERTS_PER_GROUP + k), col)
        e.append(col)
    v1 = jnp.maximum(jnp.maximum(e[0], e[1]), jnp.maximum(e[2], e[3]))
    first = []
    taken = jnp.zeros_like(v1)
    for k in range(EXPERTS_PER_GROUP):
        hit = jnp.where(e[k] == v1, 1.0, 0.0) * (1.0 - taken)
        first.append(hit)
        taken = taken + hit
    e2 = [jnp.where(first[k] > 0.5, -jnp.inf, e[k]) for k in range(EXPERTS_PER_GROUP)]
    v2 = jnp.maximum(jnp.maximum(e2[0], e2[1]), jnp.maximum(e2[2], e2[3]))
    second = []
    taken = jnp.zeros_like(v2)
    for k in range(EXPERTS_PER_GROUP):
        hit = jnp.where(e2[k] == v2, 1.0, 0.0) * (1.0 - taken)
        second.append(hit)
        taken = taken + hit
    t = jnp.exp(v2 - v1)
    w1 = 1.0 / (1.0 + t)
    w2 = t * w1
    return [is_g[gi] * (first[k] * w1 + second[k] * w2) * g_w
            for gi in range(N_EXPERT_GROUPS) for k in range(EXPERTS_PER_GROUP)]


def _route(logits):
    lane = lax.broadcasted_iota(jnp.int32, logits.shape, 1)
    comb = jnp.zeros(logits.shape, F32)
    for e, col in enumerate(_route_weights(lambda k: logits[:, k:k + 1])):
        comb = jnp.where(lane == e, col, comb)
    return comb


def _route_rows(logits_t):
    tokens = logits_t.shape[1]
    row = lax.broadcasted_iota(jnp.int32, (N_EXPERTS, tokens), 0)
    comb = jnp.zeros((N_EXPERTS, tokens), F32)
    for e, r in enumerate(_route_weights(lambda k: logits_t[k:k + 1, :])):
        comb = jnp.where(row == e, r, comb)
    return jnp.concatenate([comb, jnp.zeros((logits_t.shape[0] - N_EXPERTS, tokens), F32)], axis=0)


Q_SCALE = 1.4426950408889634 * HEAD_DIM ** -0.5


def _qkvu_kernel(x_ref, w_ref, cos_ref, sin_ref, qt_ref, k_ref, v_ref, u_ref, kb_ref, vt_ref,
                 km_ref, *, tm):
    xb = x_ref[...].astype(BF16)
    cos = cos_ref[...]
    sin = sin_ref[...]
    lane = lax.broadcasted_iota(jnp.int32, (tm, LANES), 1)
    first_half = (lane & 32) == 0
    hq = _dot(xb, w_ref[:, 0:ATTN_WIDTH].astype(BF16))
    hk = _dot(xb, w_ref[:, ATTN_WIDTH:2 * ATTN_WIDTH].astype(BF16))
    hv = _dot(xb, w_ref[:, 2 * ATTN_WIDTH:3 * ATTN_WIDTH].astype(BF16))
    for c in range(ATTN_WIDTH // LANES):
        sl = slice(c * LANES, (c + 1) * LANES)
        qc = _rope_chunk(hq[:, sl], cos, sin, first_half) * Q_SCALE
        kc = _rope_chunk(hk[:, sl], cos, sin, first_half)
        kb_ref[:, sl] = kc.astype(BF16)
        k_ref[sl, :] = kc.T
        vct = hv[:, sl].T
        v_ref[sl, :] = vct
        qct = qc.T.astype(BF16)
        for r in range(tm // MOBA_BLOCK):
            rows = slice(r * MOBA_BLOCK, (r + 1) * MOBA_BLOCK)
            km_ref[r:r + 1, sl] = jnp.sum(kc[rows], axis=0, keepdims=True) * (1.0 / MOBA_BLOCK)
            qt_ref[r, sl, :] = qct[:, rows]
            vt_ref[r, sl, :] = vct[:, rows].astype(BF16)
    u_ref[...] = _dot(xb, w_ref[:, 3 * ATTN_WIDTH:].astype(BF16))


def _qkvu(x2d, w_qkvu, cos, sin, seq, tm):
    n, d = x2d.shape
    s_tiles = seq // tm
    nblk = tm // MOBA_BLOCK
    row = lambda i: (i, 0)
    blocked = lambda i: (i, 0, 0)
    featmajor = lambda i: (i // s_tiles, 0, i % s_tiles)
    out_shape = (
        jax.ShapeDtypeStruct((n // MOBA_BLOCK, ATTN_WIDTH, MOBA_BLOCK), BF16),
        jax.ShapeDtypeStruct((n // seq, ATTN_WIDTH, seq), F32),
        jax.ShapeDtypeStruct((n // seq, ATTN_WIDTH, seq), F32),
        jax.ShapeDtypeStruct((n, POOL_WIDTH), F32),
        jax.ShapeDtypeStruct((n, ATTN_WIDTH), BF16),
        jax.ShapeDtypeStruct((n // MOBA_BLOCK, ATTN_WIDTH, MOBA_BLOCK), BF16),
        jax.ShapeDtypeStruct((n // tm, nblk, ATTN_WIDTH), F32),
    )
    return pl.pallas_call(
        functools.partial(_qkvu_kernel, tm=tm),
        out_shape=out_shape,
        grid=(n // tm,),
        in_specs=[
            pl.BlockSpec((tm, d), row),
            pl.BlockSpec((d, 4 * ATTN_WIDTH), lambda i: (0, 0)),
            pl.BlockSpec((tm, LANES), lambda i: (i % s_tiles, 0)),
            pl.BlockSpec((tm, LANES), lambda i: (i % s_tiles, 0)),
        ],
        out_specs=(
            pl.BlockSpec((nblk, ATTN_WIDTH, MOBA_BLOCK), blocked),
            pl.BlockSpec((None, ATTN_WIDTH, tm), featmajor),
            pl.BlockSpec((None, ATTN_WIDTH, tm), featmajor),
            pl.BlockSpec((tm, POOL_WIDTH), row),
            pl.BlockSpec((tm, ATTN_WIDTH), row),
            pl.BlockSpec((nblk, ATTN_WIDTH, MOBA_BLOCK), blocked),
            pl.BlockSpec((None, nblk, ATTN_WIDTH), blocked),
        ),
        compiler_params=pltpu.CompilerParams(
            dimension_semantics=("arbitrary",), vmem_limit_bytes=VMEM_LIMIT),
        name="qkvu",
    )(x2d, w_qkvu, cos, sin)


BIAS_ROWS = 128
SUM_ROWS = 16
ATTN_PAIRS = 4


def _attn_kernel(qt_ref, k_ref, vt_ref, km_ref, o_ref, qa_ref, acc_ref, sa_ref, sb_ref):
    i = pl.program_id(2)
    n_past = km_ref.shape[0]
    n_heads = 2 * ATTN_PAIRS
    feat = lax.broadcasted_iota(jnp.int32, (LANES, MOBA_BLOCK), 0)
    head0 = feat < HEAD_DIM
    blk_id = lax.broadcasted_iota(jnp.int32, (n_past, MOBA_BLOCK), 0)
    blk_f = blk_id.astype(F32)
    past = blk_id < i
    key_id = lax.broadcasted_iota(jnp.int32, (MOBA_BLOCK, MOBA_BLOCK), 0)
    qry_id = lax.broadcasted_iota(jnp.int32, (MOBA_BLOCK, MOBA_BLOCK), 1)
    causal = key_id <= qry_id
    lane = lax.broadcasted_iota(jnp.int32, (MOBA_BLOCK, LANES), 1)
    ones_rows = jnp.ones((SUM_ROWS, MOBA_BLOCK), BF16)
    bias_pad = jnp.zeros((BIAS_ROWS - n_past, MOBA_BLOCK), BF16)
    own = pl.multiple_of(i * MOBA_BLOCK, MOBA_BLOCK)

    def slab(pair):
        return slice(pair * LANES, (pair + 1) * LANES)

    def vt_aug(blk, pair):
        return jnp.concatenate([vt_ref[blk, slab(pair), :], ones_rows], axis=0)

    def consume_refill(s_ref, ms, blk, nxt):
        off = pl.multiple_of(nxt * MOBA_BLOCK, MOBA_BLOCK)
        onehot = jnp.where(lane == nxt, 1.0, 0.0).astype(BF16)
        out = []
        for h in range(n_heads):
            s = s_ref[h]
            m_new = jnp.maximum(ms[h], jnp.max(s, axis=0, keepdims=True))
            alpha = jnp.exp2(ms[h] - m_new)
            p = jnp.exp2(s - m_new)
            acc_ref[h] = alpha * acc_ref[h] + _dot(vt_aug(blk, h // 2), p.astype(BF16))
            out.append(m_new)
            ka = jnp.concatenate([k_ref[pl.ds(off, MOBA_BLOCK), slab(h // 2)], onehot], axis=1)
            s_ref[h] = _dot(ka, qa_ref[h])
        return tuple(out)

    for h in range(n_heads):
        pair = h // 2
        qt = qt_ref[slab(pair), :]
        qh = jnp.where(head0 if h % 2 == 0 else jnp.logical_not(head0), qt, jnp.zeros_like(qt))
        gate = _dot(km_ref[:, slab(pair)], qh.astype(F32), HIGHEST)
        g = jnp.where(past, gate, -jnp.inf)
        sel = jnp.zeros(g.shape, F32)
        for _ in range(MOBA_TOPK):
            mx = jnp.max(g, axis=0, keepdims=True)
            idx = jnp.min(jnp.where(g == mx, blk_f, float(n_past)), axis=0, keepdims=True)
            pick = blk_f == idx
            sel = jnp.where(pick, 1.0, sel)
            g = jnp.where(pick, -jnp.inf, g)
        bias = jnp.where(jnp.logical_and(sel > 0.5, past), 0.0, MASK_BIAS).astype(BF16)
        qa_ref[h] = jnp.concatenate([qh, bias, bias_pad], axis=0)
        kd = k_ref[pl.ds(own, MOBA_BLOCK), slab(pair)]
        sa_ref[h] = jnp.where(causal, _dot(kd, qh), -jnp.inf)
        acc_ref[h] = jnp.zeros(acc_ref.shape[1:], F32)
    m0 = jnp.full((1, MOBA_BLOCK), MASK_BIAS, F32)

    last_past = jnp.maximum(i - 1, 0)
    off0 = pl.multiple_of(0 * MOBA_BLOCK, MOBA_BLOCK)
    onehot0 = jnp.where(lane == 0, 1.0, 0.0).astype(BF16)
    for h in range(n_heads):
        ka = jnp.concatenate([k_ref[pl.ds(off0, MOBA_BLOCK), slab(h // 2)], onehot0], axis=1)
        sb_ref[h] = _dot(ka, qa_ref[h])

    def pair_body(u, ms):
        ms = consume_refill(sa_ref, ms, jnp.where(u == 0, i, 2 * u - 1),
                            jnp.minimum(2 * u + 1, last_past))
        return consume_refill(sb_ref, ms, 2 * u, jnp.minimum(2 * u + 2, last_past))

    ms = lax.fori_loop(0, (i + 1) // 2, pair_body, (m0,) * n_heads)

    @pl.when((i + 1) % 2 == 1)
    def _():
        blk = jnp.where(i == 0, i, i - 1)
        for h in range(n_heads):
            s = sa_ref[h]
            m_new = jnp.maximum(ms[h], jnp.max(s, axis=0, keepdims=True))
            alpha = jnp.exp2(ms[h] - m_new)
            p = jnp.exp2(s - m_new)
            acc_ref[h] = alpha * acc_ref[h] + _dot(vt_aug(blk, h // 2), p.astype(BF16))

    for pair in range(ATTN_PAIRS):
        a0 = acc_ref[2 * pair]
        a1 = acc_ref[2 * pair + 1]
        ot = jnp.where(head0, a0[:LANES] / a0[LANES:LANES + 1], a1[:LANES] / a1[LANES:LANES + 1])
        o_ref[:, slab(pair)] = ot.T.astype(BF16)


def _attn(qt, kb, vt, km):
    b, s, _ = kb.shape
    nb = s // MOBA_BLOCK
    width = ATTN_PAIRS * LANES
    return pl.pallas_call(
        _attn_kernel,
        out_shape=jax.ShapeDtypeStruct((b, s, ATTN_WIDTH), BF16),
        grid=(b, ATTN_WIDTH // width, nb),
        in_specs=[
            pl.BlockSpec((None, width, MOBA_BLOCK), lambda bi, p, i: (bi * nb + i, p, 0)),
            pl.BlockSpec((None, s, width), lambda bi, p, i: (bi, 0, p)),
            pl.BlockSpec((None, nb, width, MOBA_BLOCK), lambda bi, p, i: (bi, 0, p, 0)),
            pl.BlockSpec((None, nb, width), lambda bi, p, i: (bi, 0, p)),
        ],
        out_specs=pl.BlockSpec((None, MOBA_BLOCK, width), lambda bi, p, i: (bi, i, p)),
        scratch_shapes=[
            pltpu.VMEM((2 * ATTN_PAIRS, LANES + BIAS_ROWS, MOBA_BLOCK), BF16),
            pltpu.VMEM((2 * ATTN_PAIRS, LANES + SUM_ROWS, MOBA_BLOCK), F32),
            pltpu.VMEM((2 * ATTN_PAIRS, MOBA_BLOCK, MOBA_BLOCK), F32),
            pltpu.VMEM((2 * ATTN_PAIRS, MOBA_BLOCK, MOBA_BLOCK), F32),
        ],
        compiler_params=pltpu.CompilerParams(
            dimension_semantics=("arbitrary", "arbitrary", "arbitrary"),
            vmem_limit_bytes=VMEM_LIMIT),
        name="moba_attn",
    )(qt, kb, vt, km)


def _pool_group(window_sum, u_g, count, w_pool_g, scale_g, precision=None, cast=None):
    d = window_sum / count - u_g
    if cast is not None:
        d = d.astype(cast)
    return _dot(d, w_pool_g, precision) * scale_g


def _mix_kernel(x_ref, attn_ref, u_ref, halo_ref, wg_ref, bg_ref, wpool_ref, pscale_ref, wa_ref,
                wb_ref, wout_ref, g1_ref, b1_ref, x1_ref, z_ref, *, tm, alpha):
    i = pl.program_id(1)
    x = x_ref[...]
    xb = x.astype(BF16)
    u = u_ref[...]
    z_ref[0:HALO_ROWS, :] = jnp.where(i > 0, halo_ref[...], 0.0)
    z_ref[HALO_ROWS:, :] = u
    pos1 = (i * tm + 1 + lax.broadcasted_iota(jnp.int32, (tm, 1), 0)).astype(F32)
    parts = []
    for g, w in enumerate(POOL_WINDOWS):
        sl = slice(g * POOL_GROUP_WIDTH, (g + 1) * POOL_GROUP_WIDTH)
        acc = u[:, sl]
        for back in range(1, w):
            acc = acc + z_ref[HALO_ROWS - back:HALO_ROWS - back + tm, sl]
        count = jnp.minimum(float(w), pos1)
        parts.append(_pool_group(acc, u[:, sl], count, wpool_ref[g].astype(BF16), pscale_ref[:, sl],
                                 cast=BF16))
    pooled = jnp.concatenate(parts, axis=1).astype(BF16)
    d = x.shape[1]
    ga = _dot(xb, wg_ref[:, :d].astype(BF16)) + bg_ref[0:1, :]
    gb = _dot(xb, wg_ref[:, d:].astype(BF16)) + bg_ref[1:2, :]
    a = _dot(attn_ref[...], wa_ref[...].astype(BF16))
    bb = _dot(pooled, wb_ref[...].astype(BF16))
    merged = _sigmoid(ga) * a + _sigmoid(gb) * bb
    y = alpha * x + _dot(merged.astype(BF16), wout_ref[...].astype(BF16))
    x1_ref[...] = _layer_norm(y, g1_ref[...], b1_ref[...])


def _mix(x, attn, u, w_in, b_gate, w_pool, pool_scale, w_a, w_b, w_out, ln_g, ln_b, tm, alpha):
    b, s, d = x.shape
    gate_block = w_in.shape[1] // (2 * d) - 1
    assert w_in.shape[1] == (gate_block + 1) * 2 * d
    w_gates = w_in
    halo_per_tile = tm // HALO_ROWS
    tile = lambda bi, i: (bi, i, 0)
    const2 = lambda bi, i: (0, 0)
    return pl.pallas_call(
        functools.partial(_mix_kernel, tm=tm, alpha=alpha),
        out_shape=jax.ShapeDtypeStruct((b, s, d), F32),
        grid=(b, s // tm),
        in_specs=[
            pl.BlockSpec((None, tm, d), tile),
            pl.BlockSpec((None, tm, ATTN_WIDTH), tile),
            pl.BlockSpec((None, tm, POOL_WIDTH), tile),
            pl.BlockSpec((None, HALO_ROWS, POOL_WIDTH),
                         lambda bi, i: (bi, jnp.maximum(i * halo_per_tile - 1, 0), 0)),
            pl.BlockSpec((d, 2 * d), lambda bi, i: (0, gate_block)),
            pl.BlockSpec(b_gate.shape, const2),
            pl.BlockSpec(w_pool.shape, lambda bi, i: (0, 0, 0)),
            pl.BlockSpec(pool_scale.shape, const2),
            pl.BlockSpec(w_a.shape, const2),
            pl.BlockSpec(w_b.shape, const2),
            pl.BlockSpec(w_out.shape, const2),
            pl.BlockSpec(ln_g.shape, const2),
            pl.BlockSpec(ln_b.shape, const2),
        ],
        out_specs=pl.BlockSpec((None, tm, d), tile),
        scratch_shapes=[pltpu.VMEM((HALO_ROWS + tm, POOL_WIDTH), F32)],
        compiler_params=pltpu.CompilerParams(
            dimension_semantics=("arbitrary", "arbitrary"), vmem_limit_bytes=VMEM_LIMIT),
        name="mix",
    )(x, attn, u, u, w_gates, b_gate, w_pool, pool_scale, w_a, w_b, w_out, ln_g, ln_b)


def _moe_kernel(x1_ref, wrh_ref, wrl_ref, br_ref, wg_ref, wu_ref, wd_ref, g2_ref, b2_ref, y_ref,
                xb_ref, comb_ref, acc_ref, *, alpha):
    g = pl.program_id(1)
    tm = x1_ref.shape[0]
    lane = lax.broadcasted_iota(jnp.int32, (tm, LANES), 1)

    @pl.when(g == 0)
    def _():
        x1 = x1_ref[...]
        xh = x1.astype(BF16)
        xb_ref[...] = xh
        xl = (x1 - xh.astype(F32)).astype(BF16)
        logits = (_dot(xh, wrh_ref[...]) + _dot(xl, wrh_ref[...]) + _dot(xh, wrl_ref[...])
                  + br_ref[...])
        comb_ref[...] = _route_rows(logits.T).T

    xb = xb_ref[...]
    comb = comb_ref[...]
    hidden = []
    for k in range(EXPERTS_PER_GROUP):
        hg = _dot(xb, wg_ref[k].astype(BF16))
        hu = _dot(xb, wu_ref[k].astype(BF16))
        c_k = jnp.sum(jnp.where(lane == g * EXPERTS_PER_GROUP + k, comb, 0.0), axis=1, keepdims=True)
        hidden.append((hg * _sigmoid(hg) * hu * c_k).astype(BF16))
    part = _dot(jnp.concatenate(hidden, axis=1), wd_ref[...].astype(BF16))

    @pl.when(g == 0)
    def _():
        acc_ref[...] = part

    @pl.when(g > 0)
    def _():
        acc_ref[...] += part

    @pl.when(g == N_EXPERT_GROUPS - 1)
    def _():
        y = alpha * x1_ref[...] + acc_ref[...]
        y_ref[...] = _layer_norm(y, g2_ref[...], b2_ref[...])


def _moe(x1, w_r, b_r, w_g, w_u, w_d, ln_g, ln_b, tm, alpha):
    w_r_hi = w_r.astype(BF16)
    w_r_lo = (w_r - w_r_hi.astype(F32)).astype(BF16)
    n, d = x1.shape
    f = w_g.shape[2]
    per_group = (N_EXPERT_GROUPS, EXPERTS_PER_GROUP)
    tile = lambda i, g: (i, 0)
    const2 = lambda i, g: (0, 0)
    group4 = lambda i, g: (g, 0, 0, 0)
    return pl.pallas_call(
        functools.partial(_moe_kernel, alpha=alpha),
        out_shape=jax.ShapeDtypeStruct((n, d), F32),
        grid=(n // tm, N_EXPERT_GROUPS),
        in_specs=[
            pl.BlockSpec((tm, d), tile),
            pl.BlockSpec(w_r.shape, const2),
            pl.BlockSpec(w_r.shape, const2),
            pl.BlockSpec(b_r.shape, const2),
            pl.BlockSpec((None, EXPERTS_PER_GROUP, d, f), group4),
            pl.BlockSpec((None, EXPERTS_PER_GROUP, d, f), group4),
            pl.BlockSpec((None, EXPERTS_PER_GROUP * f, d), lambda i, g: (g, 0, 0)),
            pl.BlockSpec(ln_g.shape, const2),
            pl.BlockSpec(ln_b.shape, const2),
        ],
        out_specs=pl.BlockSpec((tm, d), tile),
        scratch_shapes=[
            pltpu.VMEM((tm, d), BF16),
            pltpu.VMEM((tm, LANES), F32),
            pltpu.VMEM((tm, d), F32),
        ],
        compiler_params=pltpu.CompilerParams(
            dimension_semantics=("arbitrary", "arbitrary"), vmem_limit_bytes=VMEM_LIMIT),
        name="moe",
    )(x1, w_r_hi, w_r_lo, b_r, w_g.reshape(per_group + (d, f)), w_u.reshape(per_group + (d, f)),
      w_d.reshape(N_EXPERT_GROUPS, EXPERTS_PER_GROUP * f, d), ln_g, ln_b)


def _s_proj_kernel(x_ref, w_ref, cos_ref, sin_ref, h_ref):
    c = pl.program_id(0)
    h = _dot(x_ref[...], w_ref[...], HIGHEST)
    rows = h.shape[0]
    lane = lax.broadcasted_iota(jnp.int32, (rows, LANES), 1)
    first_half = (lane & 32) == 0
    rotary = c < 2
    for j in range(h.shape[1] // LANES):
        sl = slice(j * LANES, (j + 1) * LANES)
        hc = h[:, sl]
        h_ref[:, sl] = jnp.where(rotary, _rope_chunk(hc, cos_ref[...], sin_ref[...], first_half), hc)


def _s_proj(x, w_in, cos, sin):
    rows, d = x.shape
    width = w_in.shape[1]
    chunk = ATTN_WIDTH
    return pl.pallas_call(
        _s_proj_kernel,
        out_shape=jax.ShapeDtypeStruct((rows, width), F32),
        grid=(width // chunk,),
        in_specs=[
            pl.BlockSpec((rows, d), lambda c: (0, 0)),
            pl.BlockSpec((d, chunk), lambda c: (0, c)),
            pl.BlockSpec((1, LANES), lambda c: (0, 0)),
            pl.BlockSpec((1, LANES), lambda c: (0, 0)),
        ],
        out_specs=pl.BlockSpec((rows, chunk), lambda c: (0, c)),
        compiler_params=pltpu.CompilerParams(
            dimension_semantics=("arbitrary",), vmem_limit_bytes=VMEM_LIMIT),
        name="s_proj",
    )(x, w_in, cos, sin)


PAGES_PER_STEP = 64


def _column(ref, b):
    x = ref[...]
    lane = lax.broadcasted_iota(jnp.int32, x.shape, 1)
    col = jnp.sum(jnp.where(lane == b, x, 0.0), axis=1, keepdims=True)
    return col.reshape(N_HEADS, HEAD_DIM, 1)


def _s_scores_kernel(pt_ref, q_ref, *refs, page_size, n_blocks):
    page_refs = refs[:PAGES_PER_STEP]
    sc_ref, sel_ref, gs_ref = refs[PAGES_PER_STEP:]
    c = pl.program_id(1)
    pages_per_block = MOBA_BLOCK // page_size
    blocks_per_step = PAGES_PER_STEP // pages_per_block
    q = _column(q_ref, pl.program_id(0)) * (HEAD_DIM ** -0.5)
    for r in range(blocks_per_step):
        tot = jnp.zeros((N_HEADS, page_size), F32)
        for pp in range(pages_per_block):
            p = r * pages_per_block + pp
            s = jnp.sum(page_refs[p][...] * q, axis=1)
            sc_ref[p] = s
            tot = tot + s
        gs_ref[c * blocks_per_step + r] = jnp.sum(tot, axis=1, keepdims=True)

    @pl.when(c == pl.num_programs(1) - 1)
    def _():
        g = gs_ref[...]
        blk_f = lax.broadcasted_iota(jnp.int32, g.shape, 0).astype(F32)
        for t in range(MOBA_TOPK):
            mx = jnp.max(g, axis=0, keepdims=True)
            idx = jnp.min(jnp.where(g == mx, blk_f, float(n_blocks)), axis=0, keepdims=True)
            sel_ref[t:t + 1] = idx.astype(jnp.int32)
            g = jnp.where(blk_f == idx, -jnp.inf, g)


def _s_scores(page_table, qt, cache_kt, n_blocks):
    db, n_pages = page_table.shape
    page_size = cache_kt.shape[3]
    page_block = (None, N_HEADS, HEAD_DIM, page_size)

    def page_spec(r):
        return pl.BlockSpec(page_block, lambda b, c, pt: (pt[b, c * PAGES_PER_STEP + r], 0, 0, 0))

    return pl.pallas_call(
        functools.partial(_s_scores_kernel, page_size=page_size, n_blocks=n_blocks),
        out_shape=(jax.ShapeDtypeStruct((db, n_pages, N_HEADS, page_size), F32),
                   jax.ShapeDtypeStruct((db, MOBA_TOPK, N_HEADS, 1), jnp.int32)),
        grid_spec=pltpu.PrefetchScalarGridSpec(
            num_scalar_prefetch=1,
            grid=(db, n_pages // PAGES_PER_STEP),
            in_specs=[pl.BlockSpec(qt.shape, lambda b, c, pt: (0, 0))]
            + [page_spec(r) for r in range(PAGES_PER_STEP)],
            out_specs=(pl.BlockSpec((None, PAGES_PER_STEP, N_HEADS, page_size),
                                    lambda b, c, pt: (b, c, 0, 0)),
                       pl.BlockSpec((None, MOBA_TOPK, N_HEADS, 1), lambda b, c, pt: (b, 0, 0, 0))),
            scratch_shapes=[pltpu.VMEM((n_blocks, N_HEADS, 1), F32)],
        ),
        compiler_params=pltpu.CompilerParams(
            dimension_semantics=("arbitrary", "arbitrary"), vmem_limit_bytes=VMEM_LIMIT),
        name="s_scores",
    )(page_table, qt, *([cache_kt] * PAGES_PER_STEP))


def _s_attn_kernel(pt_ref, sel_ref, sc_ref, q_ref, kn_ref, vn_ref, cv_ref, o_ref, vbuf, sem,
                   *, pages_per_block):
    b = pl.program_id(0)
    slices_per_seq = N_HEADS * MOBA_TOPK * pages_per_block

    def slot_index(slot, h, t, r):
        return slot * slices_per_seq + (h * MOBA_TOPK + t) * pages_per_block + r

    def copies(seq, slot):
        out = []
        for h in range(N_HEADS):
            for t in range(MOBA_TOPK):
                blk = sel_ref[seq, t * N_HEADS + h]
                for r in range(pages_per_block):
                    page = pt_ref[seq, blk * pages_per_block + r]
                    out.append(pltpu.make_async_copy(
                        cv_ref.at[page, h], vbuf.at[slot_index(slot, h, t, r)], sem.at[slot]))
        return out

    @pl.when(b == 0)
    def _():
        for cp in copies(0, 0):
            cp.start()

    @pl.when(b + 1 < pl.num_programs(0))
    def _():
        for cp in copies(b + 1, (b + 1) % 2):
            cp.start()

    slot = b % 2
    for cp in copies(b, slot):
        cp.wait()

    scale = HEAD_DIM ** -0.5
    q = _column(q_ref, b)
    kn = _column(kn_ref, b)
    vn = _column(vn_ref, b)
    for h in range(N_HEADS):
        s_self = jnp.sum(q[h] * kn[h], axis=0, keepdims=True) * scale
        scores = []
        mx = s_self
        for t in range(MOBA_TOPK):
            blk = sel_ref[b, t * N_HEADS + h]
            for r in range(pages_per_block):
                sc = sc_ref[blk * pages_per_block + r, h:h + 1, :]
                scores.append(sc)
                mx = jnp.maximum(mx, jnp.max(sc, axis=1, keepdims=True))
        p_self = jnp.exp(s_self - mx)
        p_sum = jnp.zeros(scores[0].shape, F32)
        pv_sum = jnp.zeros(vbuf.shape[1:], F32)
        n = 0
        for t in range(MOBA_TOPK):
            for r in range(pages_per_block):
                p = jnp.exp(scores[n] - mx)
                n += 1
                p_sum = p_sum + p
                pv_sum = pv_sum + vbuf[slot_index(slot, h, t, r)] * p
        den = p_self + jnp.sum(p_sum, axis=1, keepdims=True)
        acc = p_self * vn[h] + jnp.sum(pv_sum, axis=1, keepdims=True)
        o_ref[h] = acc / den


def _s_attn(page_table, sel, scores, qt, knt, vnt, cache_vt):
    db, n_pages = page_table.shape
    page_size = cache_vt.shape[3]
    pages_per_block = MOBA_BLOCK // page_size
    n_slices = 2 * N_HEADS * MOBA_TOPK * pages_per_block
    per_b = pl.BlockSpec((None, N_HEADS, HEAD_DIM, 1), lambda b, pt, sl: (b, 0, 0, 0))
    whole = pl.BlockSpec(qt.shape, lambda b, pt, sl: (0, 0))
    return pl.pallas_call(
        functools.partial(_s_attn_kernel, pages_per_block=pages_per_block),
        out_shape=jax.ShapeDtypeStruct((db, N_HEADS, HEAD_DIM, 1), F32),
        grid_spec=pltpu.PrefetchScalarGridSpec(
            num_scalar_prefetch=2,
            grid=(db,),
            in_specs=[pl.BlockSpec((None, n_pages, N_HEADS, page_size),
                                   lambda b, pt, sl: (b, 0, 0, 0)),
                      whole, whole, whole, pl.BlockSpec(memory_space=pl.ANY)],
            out_specs=per_b,
            scratch_shapes=[pltpu.VMEM((n_slices, HEAD_DIM, page_size), F32),
                            pltpu.SemaphoreType.DMA((2,))],
        ),
        compiler_params=pltpu.CompilerParams(
            dimension_semantics=("arbitrary",), vmem_limit_bytes=VMEM_LIMIT),
        name="s_attn",
    )(page_table, sel, scores, qt, knt, vnt, cache_vt)


def _s_tail_kernel(x_ref, attn_ref, u_ref, ga_ref, gb_ref, sp_ref, bg_ref, wpool_ref, pscale_ref,
                   wa_ref, wb_ref, wout_ref, g1_ref, b1_ref, wr_ref, br_ref, wg_ref, wu_ref,
                   wd_ref, g2_ref, b2_ref, y_ref, x1_ref, comb_ref, acc_ref, *, alpha):
    e = pl.program_id(0)
    rows = x_ref.shape[0]
    lane = lax.broadcasted_iota(jnp.int32, (rows, LANES), 1)

    @pl.when(e == 0)
    def _():
        u = u_ref[...]
        parts = []
        for g, w in enumerate(POOL_WINDOWS):
            sl = slice(g * POOL_GROUP_WIDTH, (g + 1) * POOL_GROUP_WIDTH)
            acc = u[:, sl]
            for back in range(1, w):
                acc = acc + sp_ref[POOL_STATE - back][:, sl]
            parts.append(_pool_group(acc, u[:, sl], float(w), wpool_ref[g], pscale_ref[:, sl],
                                     precision=HIGHEST))
        pooled = jnp.concatenate(parts, axis=1)
        a = _dot(attn_ref[...], wa_ref[...], HIGHEST)
        bb = _dot(pooled, wb_ref[...], HIGHEST)
        merged = _sigmoid(ga_ref[...] + bg_ref[0:1, :]) * a + _sigmoid(gb_ref[...] + bg_ref[1:2, :]) * bb
        y = alpha * x_ref[...] + _dot(merged, wout_ref[...], HIGHEST)
        x1 = _layer_norm(y, g1_ref[...], b1_ref[...])
        x1_ref[...] = x1
        comb_ref[...] = _route(_dot(x1, wr_ref[...], HIGHEST) + br_ref[...])
        acc_ref[...] = jnp.zeros_like(acc_ref)

    x1 = x1_ref[...]
    hg = _dot(x1, wg_ref[...], HIGHEST)
    hu = _dot(x1, wu_ref[...], HIGHEST)
    c_e = jnp.sum(jnp.where(lane == e, comb_ref[...], 0.0), axis=1, keepdims=True)
    h = hg * _sigmoid(hg) * hu * c_e
    acc_ref[...] += _dot(h, wd_ref[...], HIGHEST)

    @pl.when(e == N_EXPERTS - 1)
    def _():
        y_ref[...] = _layer_norm(alpha * x1_ref[...] + acc_ref[...], g2_ref[...], b2_ref[...])


def _s_tail(x, attn, u, ga, gb, sp_t, b_gate, w_pool, pool_scale, w_a, w_b, w_out, ln1_g, ln1_b,
            w_r, b_r, w_g, w_u, w_d, ln2_g, ln2_b, alpha):
    rows, d = x.shape
    f = w_g.shape[2]
    const2 = lambda e: (0, 0)
    const3 = lambda e: (0, 0, 0)
    full = lambda a: pl.BlockSpec(a.shape, const2 if a.ndim == 2 else const3)
    per_e = lambda blk: pl.BlockSpec((None,) + blk, lambda e: (e, 0, 0))
    small = [x, attn, u, ga, gb, sp_t, b_gate, w_pool, pool_scale, w_a, w_b, w_out, ln1_g, ln1_b,
             w_r, b_r]
    return pl.pallas_call(
        functools.partial(_s_tail_kernel, alpha=alpha),
        out_shape=jax.ShapeDtypeStruct((rows, d), F32),
        grid=(N_EXPERTS,),
        in_specs=[full(a) for a in small]
        + [per_e((d, f)), per_e((d, f)), per_e((f, d)), full(ln2_g), full(ln2_b)],
        out_specs=pl.BlockSpec((rows, d), const2),
        scratch_shapes=[pltpu.VMEM((rows, d), F32), pltpu.VMEM((rows, LANES), F32),
                        pltpu.VMEM((rows, d), F32)],
        compiler_params=pltpu.CompilerParams(
            dimension_semantics=("arbitrary",), vmem_limit_bytes=VMEM_LIMIT),
        name="s_tail",
    )(*small, w_g, w_u, w_d, ln2_g, ln2_b)


def _rope_tables(pos):
    inv_freq = 1.0 / (ROPE_THETA ** (jnp.arange(0, HEAD_DIM, 2, dtype=F32) / HEAD_DIM))
    ang = pos.astype(F32)[:, None] * inv_freq[None, :]
    cos = jnp.cos(ang)
    sin = jnp.sin(ang)
    cos_t = jnp.tile(cos, (1, LANES // (HEAD_DIM // 2)))
    sin_t = jnp.tile(jnp.concatenate([-sin, sin], axis=1), (1, LANES // HEAD_DIM))
    return cos_t, sin_t


def kernel(x_prompt, x_sample, cache_k, cache_v, state_pool, page_table, w_in, w_pool, pool_scale,
           w_branch_a, w_branch_b, b_gate, w_out, ln1_g, ln1_b, w_group_router, b_group_router,
           w_expert_router, b_expert_router, w_e_gate, w_e_up, w_e_down, ln2_g, ln2_b):
    depth = w_in.shape[0]
    assert depth == 1 and x_sample.shape[1] == 1
    alpha = (2 * depth) ** 0.25
    b, s, d = x_prompt.shape
    db = x_sample.shape[0]
    n_pages = page_table.shape[1]
    page_size = cache_k.shape[2]
    past_len = n_pages * page_size
    n_blocks = past_len // MOBA_BLOCK
    assert past_len % MOBA_BLOCK == 0 and s % MOBA_BLOCK == 0

    w_in0 = w_in[0]
    qkvu_cols = 3 * ATTN_WIDTH + POOL_WIDTH
    row2 = lambda a: a.reshape(1, -1)
    w_r = jnp.concatenate(
        [w_group_router[0], jnp.transpose(w_expert_router[0], (1, 0, 2)).reshape(d, N_EXPERTS),
         jnp.zeros((d, LANES - N_EXPERT_GROUPS - N_EXPERTS), F32)], axis=1)
    b_r = jnp.concatenate(
        [b_group_router[0], b_expert_router[0].reshape(-1),
         jnp.zeros((LANES - N_EXPERT_GROUPS - N_EXPERTS,), F32)]).reshape(1, LANES)

    cos_p, sin_p = _rope_tables(jnp.arange(s, dtype=jnp.int32))
    tm_a = 512
    assert qkvu_cols == 4 * ATTN_WIDTH
    qt, k, v, u, kb, vt, km = _qkvu(x_prompt.reshape(b * s, d), w_in0, cos_p, sin_p, s, tm_a)
    nb = s // MOBA_BLOCK
    attn = _attn(qt, kb.reshape(b, s, ATTN_WIDTH), vt.reshape(b, nb, ATTN_WIDTH, MOBA_BLOCK),
                 km.reshape(b, nb, ATTN_WIDTH))
    u3 = u.reshape(b, s, POOL_WIDTH)
    x1 = _mix(x_prompt, attn, u3, w_in0, b_gate[0], w_pool[0], row2(pool_scale[0]), w_branch_a[0],
              w_branch_b[0], w_out[0], row2(ln1_g[0]), row2(ln1_b[0]), 256, alpha)
    y_p = _moe(x1.reshape(b * s, d), w_r, b_r, w_e_gate[0], w_e_up[0], w_e_down[0],
               row2(ln2_g[0]), row2(ln2_b[0]), 1024, alpha)

    cos_s, sin_s = _rope_tables(jnp.full((1,), past_len, jnp.int32))
    x_s = x_sample.reshape(db, d)
    h_s = _s_proj(x_s, w_in0, cos_s, sin_s)
    q_s = h_s[:, :ATTN_WIDTH]
    k_s = h_s[:, ATTN_WIDTH:2 * ATTN_WIDTH]
    v_s = h_s[:, 2 * ATTN_WIDTH:3 * ATTN_WIDTH]
    u_s = h_s[:, 3 * ATTN_WIDTH:qkvu_cols]
    ga_s = h_s[:, qkvu_cols:qkvu_cols + d]
    gb_s = h_s[:, qkvu_cols + d:]
    cache_kt = jnp.transpose(cache_k[0], (0, 2, 3, 1))
    cache_vt = jnp.transpose(cache_v[0], (0, 2, 3, 1))
    scores, sel = _s_scores(page_table, q_s.T, cache_kt, n_blocks)
    sel2 = sel.reshape(db, MOBA_TOPK * N_HEADS)
    attn_s = _s_attn(page_table, sel2, scores, q_s.T, k_s.T, v_s.T, cache_vt).reshape(db, ATTN_WIDTH)
    sp = state_pool[0]
    y_s = _s_tail(x_s, attn_s, u_s, ga_s, gb_s, jnp.transpose(sp, (1, 0, 2)), b_gate[0], w_pool[0],
                  row2(pool_scale[0]), w_branch_a[0], w_branch_b[0], w_out[0], row2(ln1_g[0]),
                  row2(ln1_b[0]), w_r, b_r, w_e_gate[0], w_e_up[0], w_e_down[0], row2(ln2_g[0]),
                  row2(ln2_b[0]), alpha)

    heads = (N_HEADS, HEAD_DIM)
    return (
        y_p.reshape(b, s, d),
        y_s.reshape(db, 1, d),
        jnp.transpose(k.reshape((b,) + heads + (s,)), (0, 3, 1, 2))[None],
        jnp.transpose(v.reshape((b,) + heads + (s,)), (0, 3, 1, 2))[None],
        u3[:, s - POOL_STATE:, :][None],
        k_s.reshape((1, db, 1) + heads),
        v_s.reshape((1, db, 1) + heads),
        jnp.concatenate([sp[:, 1:, :], u_s[:, None, :]], axis=1)[None],
    )
```

```python
import functools

import jax
import jax.numpy as jnp
from jax import lax
from jax.experimental import pallas as pl
from jax.experimental.pallas import tpu as pltpu

F32 = jnp.float32
BF16 = jnp.bfloat16
HIGHEST = lax.Precision.HIGHEST

N_HEADS = 8
HEAD_DIM = 64
ATTN_WIDTH = N_HEADS * HEAD_DIM
MOBA_BLOCK = 256
MOBA_TOPK = 3
ROPE_THETA = 10000.0
POOL_WINDOWS = (2, 4, 8, 16)
POOL_GROUP_WIDTH = 128
POOL_WIDTH = 512
POOL_STATE = 15
N_EXPERT_GROUPS = 4
EXPERTS_PER_GROUP = 4
N_EXPERTS = 16
LN_EPS = 1e-5
LANES = 128
HALO_ROWS = 16
MASK_BIAS = -1e30
VMEM_LIMIT = 56 * 1024 * 1024
QKVU_TILE = 512
MIX_TILE = 256
MOE_TILE = 1024


def _dot(a, b, precision=None):
    return jnp.dot(a, b, preferred_element_type=F32, precision=precision)


def _layer_norm(y, g, b):
    mu = jnp.mean(y, axis=-1, keepdims=True)
    var = jnp.mean(jnp.square(y - mu), axis=-1, keepdims=True)
    return (y - mu) * lax.rsqrt(var + LN_EPS) * g + b


def _sigmoid(x):
    return 1.0 / (1.0 + jnp.exp(-x))


def _rope_chunk(x, cos, sin_signed, first_half):
    partner = jnp.where(first_half, pltpu.roll(x, 96, 1), pltpu.roll(x, 32, 1))
    return x * cos + partner * sin_signed


def _route_weights(logit):
    g = [logit(k) for k in range(N_EXPERT_GROUPS)]
    gmax = jnp.maximum(jnp.maximum(g[0], g[1]), jnp.maximum(g[2], g[3]))
    den = sum(jnp.exp(gk - gmax) for gk in g)
    g_w = 1.0 / den
    is_g = []
    taken = jnp.zeros_like(gmax)
    for k in range(N_EXPERT_GROUPS):
        hit = jnp.where(g[k] == gmax, 1.0, 0.0) * (1.0 - taken)
        is_g.append(hit)
        taken = taken + hit
    e = []
    for k in range(EXPERTS_PER_GROUP):
        col = jnp.zeros_like(gmax)
        for gi in range(N_EXPERT_GROUPS):
            col = jnp.where(is_g[gi] > 0.5, logit(N_EXPERT_GROUPS + gi * EXPERTS_PER_GROUP + k), col)
        e.append(col)
    v1 = jnp.maximum(jnp.maximum(e[0], e[1]), jnp.maximum(e[2], e[3]))
    first = []
    taken = jnp.zeros_like(v1)
    for k in range(EXPERTS_PER_GROUP):
        hit = jnp.where(e[k] == v1, 1.0, 0.0) * (1.0 - taken)
        first.append(hit)
        taken = taken + hit
    e2 = [jnp.where(first[k] > 0.5, -jnp.inf, e[k]) for k in range(EXPERTS_PER_GROUP)]
    v2 = jnp.maximum(jnp.maximum(e2[0], e2[1]), jnp.maximum(e2[2], e2[3]))
    second = []
    taken = jnp.zeros_like(v2)
    for k in range(EXPERTS_PER_GROUP):
        hit = jnp.where(e2[k] == v2, 1.0, 0.0) * (1.0 - taken)
        second.append(hit)
        taken = taken + hit
    t = jnp.exp(v2 - v1)
    w1 = 1.0 / (1.0 + t)
    w2 = t * w1
    return [is_g[gi] * (first[k] * w1 + second[k] * w2) * g_w
            for gi in range(N_EXPERT_GROUPS) for k in range(EXPERTS_PER_GROUP)]


def _route(logits):
    lane = lax.broadcasted_iota(jnp.int32, logits.shape, 1)
    comb = jnp.zeros(logits.shape, F32)
    for e, col in enumerate(_route_weights(lambda k: logits[:, k:k + 1])):
        comb = jnp.where(lane == e, col, comb)
    return comb


def _route_rows(logits_t):
    tokens = logits_t.shape[1]
    row = lax.broadcasted_iota(jnp.int32, (N_EXPERTS, tokens), 0)
    comb = jnp.zeros((N_EXPERTS, tokens), F32)
    for e, r in enumerate(_route_weights(lambda k: logits_t[k:k + 1, :])):
        comb = jnp.where(row == e, r, comb)
    return jnp.concatenate([comb, jnp.zeros((logits_t.shape[0] - N_EXPERTS, tokens), F32)], axis=0)


Q_SCALE = 1.4426950408889634 * HEAD_DIM ** -0.5


def _qkvu_kernel(x_ref, w_ref, cos_ref, sin_ref, qt_ref, k_ref, v_ref, u_ref, kb_ref, vt_ref,
                 km_ref, *, tm):
    xb = x_ref[...].astype(BF16)
    cos = cos_ref[...]
    sin = sin_ref[...]
    lane = lax.broadcasted_iota(jnp.int32, (tm, LANES), 1)
    first_half = (lane & 32) == 0
    hq = _dot(xb, w_ref[:, 0:ATTN_WIDTH].astype(BF16))
    hk = _dot(xb, w_ref[:, ATTN_WIDTH:2 * ATTN_WIDTH].astype(BF16))
    hv = _dot(xb, w_ref[:, 2 * ATTN_WIDTH:3 * ATTN_WIDTH].astype(BF16))
    for c in range(ATTN_WIDTH // LANES):
        sl = slice(c * LANES, (c + 1) * LANES)
        qc = _rope_chunk(hq[:, sl], cos, sin, first_half) * Q_SCALE
        kc = _rope_chunk(hk[:, sl], cos, sin, first_half)
        kb_ref[:, sl] = kc.astype(BF16)
        k_ref[sl, :] = kc.T
        vct = hv[:, sl].T
        v_ref[sl, :] = vct
        qct = qc.T.astype(BF16)
        for r in range(tm // MOBA_BLOCK):
            rows = slice(r * MOBA_BLOCK, (r + 1) * MOBA_BLOCK)
            km_ref[r:r + 1, sl] = jnp.sum(kc[rows], axis=0, keepdims=True) * (1.0 / MOBA_BLOCK)
            qt_ref[r, sl, :] = qct[:, rows]
            vt_ref[r, sl, :] = vct[:, rows].astype(BF16)
    u_ref[...] = _dot(xb, w_ref[:, 3 * ATTN_WIDTH:].astype(BF16))


def _qkvu(x2d, w_qkvu, cos, sin, seq, tm):
    n, d = x2d.shape
    s_tiles = seq // tm
    nblk = tm // MOBA_BLOCK
    row = lambda i: (i, 0)
    blocked = lambda i: (i, 0, 0)
    featmajor = lambda i: (i // s_tiles, 0, i % s_tiles)
    out_shape = (
        jax.ShapeDtypeStruct((n // MOBA_BLOCK, ATTN_WIDTH, MOBA_BLOCK), BF16),
        jax.ShapeDtypeStruct((n // seq, ATTN_WIDTH, seq), F32),
        jax.ShapeDtypeStruct((n // seq, ATTN_WIDTH, seq), F32),
        jax.ShapeDtypeStruct((n, POOL_WIDTH), F32),
        jax.ShapeDtypeStruct((n, ATTN_WIDTH), BF16),
        jax.ShapeDtypeStruct((n // MOBA_BLOCK, ATTN_WIDTH, MOBA_BLOCK), BF16),
        jax.ShapeDtypeStruct((n // tm, nblk, ATTN_WIDTH), F32),
    )
    return pl.pallas_call(
        functools.partial(_qkvu_kernel, tm=tm),
        out_shape=out_shape,
        grid=(n // tm,),
        in_specs=[
            pl.BlockSpec((tm, d), row),
            pl.BlockSpec((d, 4 * ATTN_WIDTH), lambda i: (0, 0)),
            pl.BlockSpec((tm, LANES), lambda i: (i % s_tiles, 0)),
            pl.BlockSpec((tm, LANES), lambda i: (i % s_tiles, 0)),
        ],
        out_specs=(
            pl.BlockSpec((nblk, ATTN_WIDTH, MOBA_BLOCK), blocked),
            pl.BlockSpec((None, ATTN_WIDTH, tm), featmajor),
            pl.BlockSpec((None, ATTN_WIDTH, tm), featmajor),
            pl.BlockSpec((tm, POOL_WIDTH), row),
            pl.BlockSpec((tm, ATTN_WIDTH), row),
            pl.BlockSpec((nblk, ATTN_WIDTH, MOBA_BLOCK), blocked),
            pl.BlockSpec((None, nblk, ATTN_WIDTH), blocked),
        ),
        compiler_params=pltpu.CompilerParams(
            dimension_semantics=("arbitrary",), vmem_limit_bytes=VMEM_LIMIT),
        name="qkvu",
    )(x2d, w_qkvu, cos, sin)


BIAS_ROWS = 128
SUM_ROWS = 16
ATTN_PAIRS = 4


def _attn_kernel(qt_ref, k_ref, vt_ref, km_ref, o_ref, qa_ref, acc_ref, sa_ref, sb_ref):
    i = pl.program_id(2)
    n_past = km_ref.shape[0]
    n_heads = 2 * ATTN_PAIRS
    feat = lax.broadcasted_iota(jnp.int32, (LANES, MOBA_BLOCK), 0)
    head0 = feat < HEAD_DIM
    blk_id = lax.broadcasted_iota(jnp.int32, (n_past, MOBA_BLOCK), 0)
    blk_f = blk_id.astype(F32)
    past = blk_id < i
    key_id = lax.broadcasted_iota(jnp.int32, (MOBA_BLOCK, MOBA_BLOCK), 0)
    qry_id = lax.broadcasted_iota(jnp.int32, (MOBA_BLOCK, MOBA_BLOCK), 1)
    causal = key_id <= qry_id
    lane = lax.broadcasted_iota(jnp.int32, (MOBA_BLOCK, LANES), 1)
    ones_rows = jnp.ones((SUM_ROWS, MOBA_BLOCK), BF16)
    bias_pad = jnp.zeros((BIAS_ROWS - n_past, MOBA_BLOCK), BF16)
    own = pl.multiple_of(i * MOBA_BLOCK, MOBA_BLOCK)

    def slab(pair):
        return slice(pair * LANES, (pair + 1) * LANES)

    def vt_aug(blk, pair):
        return jnp.concatenate([vt_ref[blk, slab(pair), :], ones_rows], axis=0)

    def consume_refill(s_ref, ms, blk, nxt):
        off = pl.multiple_of(nxt * MOBA_BLOCK, MOBA_BLOCK)
        onehot = jnp.where(lane == nxt, 1.0, 0.0).astype(BF16)
        out = []
        for h in range(n_heads):
            s = s_ref[h]
            m_new = jnp.maximum(ms[h], jnp.max(s, axis=0, keepdims=True))
            alpha = jnp.exp2(ms[h] - m_new)
            p = jnp.exp2(s - m_new)
            acc_ref[h] = alpha * acc_ref[h] + _dot(vt_aug(blk, h // 2), p.astype(BF16))
            out.append(m_new)
            ka = jnp.concatenate([k_ref[pl.ds(off, MOBA_BLOCK), slab(h // 2)], onehot], axis=1)
            s_ref[h] = _dot(ka, qa_ref[h])
        return tuple(out)

    for h in range(n_heads):
        pair = h // 2
        qt = qt_ref[slab(pair), :]
        qh = jnp.where(head0 if h % 2 == 0 else jnp.logical_not(head0), qt, jnp.zeros_like(qt))
        gate = _dot(km_ref[:, slab(pair)], qh.astype(F32), HIGHEST)
        g = jnp.where(past, gate, -jnp.inf)
        sel = jnp.zeros(g.shape, F32)
        for _ in range(MOBA_TOPK):
            mx = jnp.max(g, axis=0, keepdims=True)
            idx = jnp.min(jnp.where(g == mx, blk_f, float(n_past)), axis=0, keepdims=True)
            pick = blk_f == idx
            sel = jnp.where(pick, 1.0, sel)
            g = jnp.where(pick, -jnp.inf, g)
        bias = jnp.where(jnp.logical_and(sel > 0.5, past), 0.0, MASK_BIAS).astype(BF16)
        qa_ref[h] = jnp.concatenate([qh, bias, bias_pad], axis=0)
        kd = k_ref[pl.ds(own, MOBA_BLOCK), slab(pair)]
        sa_ref[h] = jnp.where(causal, _dot(kd, qh), -jnp.inf)
        acc_ref[h] = jnp.zeros(acc_ref.shape[1:], F32)
    m0 = jnp.full((1, MOBA_BLOCK), MASK_BIAS, F32)

    last_past = jnp.maximum(i - 1, 0)
    off0 = pl.multiple_of(0 * MOBA_BLOCK, MOBA_BLOCK)
    onehot0 = jnp.where(lane == 0, 1.0, 0.0).astype(BF16)
    for h in range(n_heads):
        ka = jnp.concatenate([k_ref[pl.ds(off0, MOBA_BLOCK), slab(h // 2)], onehot0], axis=1)
        sb_ref[h] = _dot(ka, qa_ref[h])

    def pair_body(u, ms):
        ms = consume_refill(sa_ref, ms, jnp.where(u == 0, i, 2 * u - 1),
                            jnp.minimum(2 * u + 1, last_past))
        return consume_refill(sb_ref, ms, 2 * u, jnp.minimum(2 * u + 2, last_past))

    ms = lax.fori_loop(0, (i + 1) // 2, pair_body, (m0,) * n_heads)

    @pl.when((i + 1) % 2 == 1)
    def _():
        blk = jnp.where(i == 0, i, i - 1)
        for h in range(n_heads):
            s = sa_ref[h]
            m_new = jnp.maximum(ms[h], jnp.max(s, axis=0, keepdims=True))
            alpha = jnp.exp2(ms[h] - m_new)
            p = jnp.exp2(s - m_new)
            acc_ref[h] = alpha * acc_ref[h] + _dot(vt_aug(blk, h // 2), p.astype(BF16))

    for pair in range(ATTN_PAIRS):
        a0 = acc_ref[2 * pair]
        a1 = acc_ref[2 * pair + 1]
        ot = jnp.where(head0, a0[:LANES] / a0[LANES:LANES + 1], a1[:LANES] / a1[LANES:LANES + 1])
        o_ref[:, slab(pair)] = ot.T.astype(BF16)


def _attn(qt, kb, vt, km):
    b, s, _ = kb.shape
    nb = s // MOBA_BLOCK
    width = ATTN_PAIRS * LANES
    return pl.pallas_call(
        _attn_kernel,
        out_shape=jax.ShapeDtypeStruct((b, s, ATTN_WIDTH), BF16),
        grid=(b, ATTN_WIDTH // width, nb),
        in_specs=[
            pl.BlockSpec((None, width, MOBA_BLOCK), lambda bi, p, i: (bi * nb + i, p, 0)),
            pl.BlockSpec((None, s, width), lambda bi, p, i: (bi, 0, p)),
            pl.BlockSpec((None, nb, width, MOBA_BLOCK), lambda bi, p, i: (bi, 0, p, 0)),
            pl.BlockSpec((None, nb, width), lambda bi, p, i: (bi, 0, p)),
        ],
        out_specs=pl.BlockSpec((None, MOBA_BLOCK, width), lambda bi, p, i: (bi, i, p)),
        scratch_shapes=[
            pltpu.VMEM((2 * ATTN_PAIRS, LANES + BIAS_ROWS, MOBA_BLOCK), BF16),
            pltpu.VMEM((2 * ATTN_PAIRS, LANES + SUM_ROWS, MOBA_BLOCK), F32),
            pltpu.VMEM((2 * ATTN_PAIRS, MOBA_BLOCK, MOBA_BLOCK), F32),
            pltpu.VMEM((2 * ATTN_PAIRS, MOBA_BLOCK, MOBA_BLOCK), F32),
        ],
        compiler_params=pltpu.CompilerParams(
            dimension_semantics=("arbitrary", "arbitrary", "arbitrary"),
            vmem_limit_bytes=VMEM_LIMIT),
        name="moba_attn",
    )(qt, kb, vt, km)


def _pool_group(window_sum, u_g, count, w_pool_g, scale_g, precision=None, cast=None):
    d = window_sum / count - u_g
    if cast is not None:
        d = d.astype(cast)
    return _dot(d, w_pool_g, precision) * scale_g


def _mix_kernel(x_ref, attn_ref, u_ref, halo_ref, wg_ref, bg_ref, wpool_ref, pscale_ref, wa_ref,
                wb_ref, wout_ref, g1_ref, b1_ref, x1_ref, z_ref, *, tm, alpha):
    i = pl.program_id(1)
    x = x_ref[...]
    xb = x.astype(BF16)
    u = u_ref[...]
    z_ref[0:HALO_ROWS, :] = jnp.where(i > 0, halo_ref[...], 0.0)
    z_ref[HALO_ROWS:, :] = u
    pos1 = (i * tm + 1 + lax.broadcasted_iota(jnp.int32, (tm, 1), 0)).astype(F32)
    parts = []
    for g, w in enumerate(POOL_WINDOWS):
        sl = slice(g * POOL_GROUP_WIDTH, (g + 1) * POOL_GROUP_WIDTH)
        acc = u[:, sl]
        for back in range(1, w):
            acc = acc + z_ref[HALO_ROWS - back:HALO_ROWS - back + tm, sl]
        count = jnp.minimum(float(w), pos1)
        parts.append(_pool_group(acc, u[:, sl], count, wpool_ref[g].astype(BF16), pscale_ref[:, sl],
                                 cast=BF16))
    pooled = jnp.concatenate(parts, axis=1).astype(BF16)
    d = x.shape[1]
    ga = _dot(xb, wg_ref[:, :d].astype(BF16)) + bg_ref[0:1, :]
    gb = _dot(xb, wg_ref[:, d:].astype(BF16)) + bg_ref[1:2, :]
    a = _dot(attn_ref[...], wa_ref[...].astype(BF16))
    bb = _dot(pooled, wb_ref[...].astype(BF16))
    merged = _sigmoid(ga) * a + _sigmoid(gb) * bb
    y = alpha * x + _dot(merged.astype(BF16), wout_ref[...].astype(BF16))
    x1_ref[...] = _layer_norm(y, g1_ref[...], b1_ref[...])


def _mix(x, attn, u, w_in, b_gate, w_pool, pool_scale, w_a, w_b, w_out, ln_g, ln_b, tm, alpha):
    b, s, d = x.shape
    gate_block = w_in.shape[1] // (2 * d) - 1
    assert w_in.shape[1] == (gate_block + 1) * 2 * d
    w_gates = w_in
    halo_per_tile = tm // HALO_ROWS
    tile = lambda bi, i: (bi, i, 0)
    const2 = lambda bi, i: (0, 0)
    return pl.pallas_call(
        functools.partial(_mix_kernel, tm=tm, alpha=alpha),
        out_shape=jax.ShapeDtypeStruct((b, s, d), F32),
        grid=(b, s // tm),
        in_specs=[
            pl.BlockSpec((None, tm, d), tile),
            pl.BlockSpec((None, tm, ATTN_WIDTH), tile),
            pl.BlockSpec((None, tm, POOL_WIDTH), tile),
            pl.BlockSpec((None, HALO_ROWS, POOL_WIDTH),
                         lambda bi, i: (bi, jnp.maximum(i * halo_per_tile - 1, 0), 0)),
            pl.BlockSpec((d, 2 * d), lambda bi, i: (0, gate_block)),
            pl.BlockSpec(b_gate.shape, const2),
            pl.BlockSpec(w_pool.shape, lambda bi, i: (0, 0, 0)),
            pl.BlockSpec(pool_scale.shape, const2),
            pl.BlockSpec(w_a.shape, const2),
            pl.BlockSpec(w_b.shape, const2),
            pl.BlockSpec(w_out.shape, const2),
            pl.BlockSpec(ln_g.shape, const2),
            pl.BlockSpec(ln_b.shape, const2),
        ],
        out_specs=pl.BlockSpec((None, tm, d), tile),
        scratch_shapes=[pltpu.VMEM((HALO_ROWS + tm, POOL_WIDTH), F32)],
        compiler_params=pltpu.CompilerParams(
            dimension_semantics=("arbitrary", "arbitrary"), vmem_limit_bytes=VMEM_LIMIT),
        name="mix",
    )(x, attn, u, u, w_gates, b_gate, w_pool, pool_scale, w_a, w_b, w_out, ln_g, ln_b)


def _moe_kernel(x1_ref, wrh_ref, wrl_ref, br_ref, wg_ref, wu_ref, wd_ref, g2_ref, b2_ref, y_ref,
                xb_ref, comb_ref, acc_ref, *, alpha):
    g = pl.program_id(1)
    tm = x1_ref.shape[0]
    lane = lax.broadcasted_iota(jnp.int32, (tm, LANES), 1)

    @pl.when(g == 0)
    def _():
        x1 = x1_ref[...]
        xh = x1.astype(BF16)
        xb_ref[...] = xh
        xl = (x1 - xh.astype(F32)).astype(BF16)
        logits = (_dot(xh, wrh_ref[...]) + _dot(xl, wrh_ref[...]) + _dot(xh, wrl_ref[...])
                  + br_ref[...])
        comb_ref[...] = _route_rows(logits.T).T

    xb = xb_ref[...]
    comb = comb_ref[...]
    hidden = []
    for k in range(EXPERTS_PER_GROUP):
        hg = _dot(xb, wg_ref[k].astype(BF16))
        hu = _dot(xb, wu_ref[k].astype(BF16))
        c_k = jnp.sum(jnp.where(lane == g * EXPERTS_PER_GROUP + k, comb, 0.0), axis=1, keepdims=True)
        hidden.append((hg * _sigmoid(hg) * hu * c_k).astype(BF16))
    part = _dot(jnp.concatenate(hidden, axis=1), wd_ref[...].astype(BF16))

    @pl.when(g == 0)
    def _():
        acc_ref[...] = part

    @pl.when(g > 0)
    def _():
        acc_ref[...] += part

    @pl.when(g == N_EXPERT_GROUPS - 1)
    def _():
        y = alpha * x1_ref[...] + acc_ref[...]
        y_ref[...] = _layer_norm(y, g2_ref[...], b2_ref[...])


def _moe(x1, w_r, b_r, w_g, w_u, w_d, ln_g, ln_b, tm, alpha):
    w_r_hi = w_r.astype(BF16)
    w_r_lo = (w_r - w_r_hi.astype(F32)).astype(BF16)
    n, d = x1.shape
    f = w_g.shape[2]
    per_group = (N_EXPERT_GROUPS, EXPERTS_PER_GROUP)
    tile = lambda i, g: (i, 0)
    const2 = lambda i, g: (0, 0)
    group4 = lambda i, g: (g, 0, 0, 0)
    return pl.pallas_call(
        functools.partial(_moe_kernel, alpha=alpha),
        out_shape=jax.ShapeDtypeStruct((n, d), F32),
        grid=(n // tm, N_EXPERT_GROUPS),
        in_specs=[
            pl.BlockSpec((tm, d), tile),
            pl.BlockSpec(w_r.shape, const2),
            pl.BlockSpec(w_r.shape, const2),
            pl.BlockSpec(b_r.shape, const2),
            pl.BlockSpec((None, EXPERTS_PER_GROUP, d, f), group4),
            pl.BlockSpec((None, EXPERTS_PER_GROUP, d, f), group4),
            pl.BlockSpec((None, EXPERTS_PER_GROUP * f, d), lambda i, g: (g, 0, 0)),
            pl.BlockSpec(ln_g.shape, const2),
            pl.BlockSpec(ln_b.shape, const2),
        ],
        out_specs=pl.BlockSpec((tm, d), tile),
        scratch_shapes=[
            pltpu.VMEM((tm, d), BF16),
            pltpu.VMEM((tm, LANES), F32),
            pltpu.VMEM((tm, d), F32),
        ],
        compiler_params=pltpu.CompilerParams(
            dimension_semantics=("arbitrary", "arbitrary"), vmem_limit_bytes=VMEM_LIMIT),
        name="moe",
    )(x1, w_r_hi, w_r_lo, b_r, w_g.reshape(per_group + (d, f)), w_u.reshape(per_group + (d, f)),
      w_d.reshape(N_EXPERT_GROUPS, EXPERTS_PER_GROUP * f, d), ln_g, ln_b)


def _s_proj_kernel(x_ref, w_ref, cos_ref, sin_ref, h_ref):
    c = pl.program_id(0)
    h = _dot(x_ref[...], w_ref[...], HIGHEST)
    rows = h.shape[0]
    lane = lax.broadcasted_iota(jnp.int32, (rows, LANES), 1)
    first_half = (lane & 32) == 0
    rotary = c < 2
    for j in range(h.shape[1] // LANES):
        sl = slice(j * LANES, (j + 1) * LANES)
        hc = h[:, sl]
        h_ref[:, sl] = jnp.where(rotary, _rope_chunk(hc, cos_ref[...], sin_ref[...], first_half), hc)


def _s_proj(x, w_in, cos, sin):
    rows, d = x.shape
    width = w_in.shape[1]
    chunk = ATTN_WIDTH
    return pl.pallas_call(
        _s_proj_kernel,
        out_shape=jax.ShapeDtypeStruct((rows, width), F32),
        grid=(width // chunk,),
        in_specs=[
            pl.BlockSpec((rows, d), lambda c: (0, 0)),
            pl.BlockSpec((d, chunk), lambda c: (0, c)),
            pl.BlockSpec((1, LANES), lambda c: (0, 0)),
            pl.BlockSpec((1, LANES), lambda c: (0, 0)),
        ],
        out_specs=pl.BlockSpec((rows, chunk), lambda c: (0, c)),
        compiler_params=pltpu.CompilerParams(
            dimension_semantics=("arbitrary",), vmem_limit_bytes=VMEM_LIMIT),
        name="s_proj",
    )(x, w_in, cos, sin)


PAGES_PER_STEP = 64


def _column(ref, b):
    x = ref[...]
    lane = lax.broadcasted_iota(jnp.int32, x.shape, 1)
    col = jnp.sum(jnp.where(lane == b, x, 0.0), axis=1, keepdims=True)
    return col.reshape(N_HEADS, HEAD_DIM, 1)


def _s_scores_kernel(pt_ref, q_ref, *refs, page_size, n_blocks):
    page_refs = refs[:PAGES_PER_STEP]
    sc_ref, sel_ref, gs_ref = refs[PAGES_PER_STEP:]
    c = pl.program_id(1)
    pages_per_block = MOBA_BLOCK // page_size
    blocks_per_step = PAGES_PER_STEP // pages_per_block
    q = _column(q_ref, pl.program_id(0)) * (HEAD_DIM ** -0.5)
    for r in range(blocks_per_step):
        tot = jnp.zeros((N_HEADS, page_size), F32)
        for pp in range(pages_per_block):
            p = r * pages_per_block + pp
            s = jnp.sum(page_refs[p][...] * q, axis=1)
            sc_ref[p] = s
            tot = tot + s
        gs_ref[c * blocks_per_step + r] = jnp.sum(tot, axis=1, keepdims=True)

    @pl.when(c == pl.num_programs(1) - 1)
    def _():
        g = gs_ref[...]
        blk_f = lax.broadcasted_iota(jnp.int32, g.shape, 0).astype(F32)
        for t in range(MOBA_TOPK):
            mx = jnp.max(g, axis=0, keepdims=True)
            idx = jnp.min(jnp.where(g == mx, blk_f, float(n_blocks)), axis=0, keepdims=True)
            sel_ref[t:t + 1] = idx.astype(jnp.int32)
            g = jnp.where(blk_f == idx, -jnp.inf, g)


def _s_scores(page_table, qt, cache_kt, n_blocks):
    db, n_pages = page_table.shape
    page_size = cache_kt.shape[3]
    page_block = (None, N_HEADS, HEAD_DIM, page_size)

    def page_spec(r):
        return pl.BlockSpec(page_block, lambda b, c, pt: (pt[b, c * PAGES_PER_STEP + r], 0, 0, 0))

    return pl.pallas_call(
        functools.partial(_s_scores_kernel, page_size=page_size, n_blocks=n_blocks),
        out_shape=(jax.ShapeDtypeStruct((db, n_pages, N_HEADS, page_size), F32),
                   jax.ShapeDtypeStruct((db, MOBA_TOPK, N_HEADS, 1), jnp.int32)),
        grid_spec=pltpu.PrefetchScalarGridSpec(
            num_scalar_prefetch=1,
            grid=(db, n_pages // PAGES_PER_STEP),
            in_specs=[pl.BlockSpec(qt.shape, lambda b, c, pt: (0, 0))]
            + [page_spec(r) for r in range(PAGES_PER_STEP)],
            out_specs=(pl.BlockSpec((None, PAGES_PER_STEP, N_HEADS, page_size),
                                    lambda b, c, pt: (b, c, 0, 0)),
                       pl.BlockSpec((None, MOBA_TOPK, N_HEADS, 1), lambda b, c, pt: (b, 0, 0, 0))),
            scratch_shapes=[pltpu.VMEM((n_blocks, N_HEADS, 1), F32)],
        ),
        compiler_params=pltpu.CompilerParams(
            dimension_semantics=("arbitrary", "arbitrary"), vmem_limit_bytes=VMEM_LIMIT),
        name="s_scores",
    )(page_table, qt, *([cache_kt] * PAGES_PER_STEP))


def _s_attn_kernel(pt_ref, sel_ref, sc_ref, q_ref, kn_ref, vn_ref, cv_ref, o_ref, vbuf, sem,
                   *, pages_per_block):
    b = pl.program_id(0)
    slices_per_seq = N_HEADS * MOBA_TOPK * pages_per_block

    def slot_index(slot, h, t, r):
        return slot * slices_per_seq + (h * MOBA_TOPK + t) * pages_per_block + r

    def copies(seq, slot):
        out = []
        for h in range(N_HEADS):
            for t in range(MOBA_TOPK):
                blk = sel_ref[seq, t * N_HEADS + h]
                for r in range(pages_per_block):
                    page = pt_ref[seq, blk * pages_per_block + r]
                    out.append(pltpu.make_async_copy(
                        cv_ref.at[page, h], vbuf.at[slot_index(slot, h, t, r)], sem.at[slot]))
        return out

    @pl.when(b == 0)
    def _():
        for cp in copies(0, 0):
            cp.start()

    @pl.when(b + 1 < pl.num_programs(0))
    def _():
        for cp in copies(b + 1, (b + 1) % 2):
            cp.start()

    slot = b % 2
    for cp in copies(b, slot):
        cp.wait()

    scale = HEAD_DIM ** -0.5
    q = _column(q_ref, b)
    kn = _column(kn_ref, b)
    vn = _column(vn_ref, b)
    outs = []
    for h in range(N_HEADS):
        s_self = jnp.sum(q[h] * kn[h], axis=0, keepdims=True) * scale
        scores = []
        mx = s_self
        for t in range(MOBA_TOPK):
            blk = sel_ref[b, t * N_HEADS + h]
            for r in range(pages_per_block):
                sc = sc_ref[blk * pages_per_block + r, h:h + 1, :]
                scores.append(sc)
                mx = jnp.maximum(mx, jnp.max(sc, axis=1, keepdims=True))
        p_self = jnp.exp(s_self - mx)
        p_sum = jnp.zeros(scores[0].shape, F32)
        pv_sum = jnp.zeros(vbuf.shape[1:], F32)
        n = 0
        for t in range(MOBA_TOPK):
            for r in range(pages_per_block):
                p = jnp.exp(scores[n] - mx)
                n += 1
                p_sum = p_sum + p
                pv_sum = pv_sum + vbuf[slot_index(slot, h, t, r)] * p
        den = p_self + jnp.sum(p_sum, axis=1, keepdims=True)
        acc = p_self * vn[h] + jnp.sum(pv_sum, axis=1, keepdims=True)
        outs.append(acc / den)
    col = jnp.concatenate(outs, axis=0)
    width = col.shape[0]
    on_diag = (lax.broadcasted_iota(jnp.int32, (width, width), 0)
               == lax.broadcasted_iota(jnp.int32, (width, width), 1))
    o_ref[...] = jnp.sum(jnp.where(on_diag, col, 0.0), axis=0, keepdims=True)


def _s_attn(page_table, sel, scores, qt, knt, vnt, cache_vt):
    db, n_pages = page_table.shape
    page_size = cache_vt.shape[3]
    pages_per_block = MOBA_BLOCK // page_size
    n_slices = 2 * N_HEADS * MOBA_TOPK * pages_per_block
    per_b = pl.BlockSpec((None, 1, ATTN_WIDTH), lambda b, pt, sl: (b, 0, 0))
    whole = pl.BlockSpec(qt.shape, lambda b, pt, sl: (0, 0))
    return pl.pallas_call(
        functools.partial(_s_attn_kernel, pages_per_block=pages_per_block),
        out_shape=jax.ShapeDtypeStruct((db, 1, ATTN_WIDTH), F32),
        grid_spec=pltpu.PrefetchScalarGridSpec(
            num_scalar_prefetch=2,
            grid=(db,),
            in_specs=[pl.BlockSpec((None, n_pages, N_HEADS, page_size),
                                   lambda b, pt, sl: (b, 0, 0, 0)),
                      whole, whole, whole, pl.BlockSpec(memory_space=pl.ANY)],
            out_specs=per_b,
            scratch_shapes=[pltpu.VMEM((n_slices, HEAD_DIM, page_size), F32),
                            pltpu.SemaphoreType.DMA((2,))],
        ),
        compiler_params=pltpu.CompilerParams(
            dimension_semantics=("arbitrary",), vmem_limit_bytes=VMEM_LIMIT),
        name="s_attn",
    )(page_table, sel, scores, qt, knt, vnt, cache_vt)


def _s_tail_kernel(x_ref, attn_ref, u_ref, ga_ref, gb_ref, sp_ref, bg_ref, wpool_ref, pscale_ref,
                   wa_ref, wb_ref, wout_ref, g1_ref, b1_ref, wr_ref, br_ref, wg_ref, wu_ref,
                   wd_ref, g2_ref, b2_ref, y_ref, x1_ref, comb_ref, acc_ref, *, alpha):
    e = pl.program_id(0)
    rows = x_ref.shape[0]
    lane = lax.broadcasted_iota(jnp.int32, (rows, LANES), 1)

    @pl.when(e == 0)
    def _():
        u = u_ref[...]
        parts = []
        for g, w in enumerate(POOL_WINDOWS):
            sl = slice(g * POOL_GROUP_WIDTH, (g + 1) * POOL_GROUP_WIDTH)
            acc = u[:, sl]
            for back in range(1, w):
                acc = acc + sp_ref[POOL_STATE - back][:, sl]
            parts.append(_pool_group(acc, u[:, sl], float(w), wpool_ref[g], pscale_ref[:, sl],
                                     precision=HIGHEST))
        pooled = jnp.concatenate(parts, axis=1)
        a = _dot(attn_ref[...], wa_ref[...], HIGHEST)
        bb = _dot(pooled, wb_ref[...], HIGHEST)
        merged = _sigmoid(ga_ref[...] + bg_ref[0:1, :]) * a + _sigmoid(gb_ref[...] + bg_ref[1:2, :]) * bb
        y = alpha * x_ref[...] + _dot(merged, wout_ref[...], HIGHEST)
        x1 = _layer_norm(y, g1_ref[...], b1_ref[...])
        x1_ref[...] = x1
        comb_ref[...] = _route(_dot(x1, wr_ref[...], HIGHEST) + br_ref[...])
        acc_ref[...] = jnp.zeros_like(acc_ref)

    x1 = x1_ref[...]
    hg = _dot(x1, wg_ref[...], HIGHEST)
    hu = _dot(x1, wu_ref[...], HIGHEST)
    c_e = jnp.sum(jnp.where(lane == e, comb_ref[...], 0.0), axis=1, keepdims=True)
    h = hg * _sigmoid(hg) * hu * c_e
    acc_ref[...] += _dot(h, wd_ref[...], HIGHEST)

    @pl.when(e == N_EXPERTS - 1)
    def _():
        y_ref[...] = _layer_norm(alpha * x1_ref[...] + acc_ref[...], g2_ref[...], b2_ref[...])


def _s_tail(x, attn, u, ga, gb, sp_t, b_gate, w_pool, pool_scale, w_a, w_b, w_out, ln1_g, ln1_b,
            w_r, b_r, w_g, w_u, w_d, ln2_g, ln2_b, alpha):
    rows, d = x.shape
    f = w_g.shape[2]
    const2 = lambda e: (0, 0)
    const3 = lambda e: (0, 0, 0)
    full = lambda a: pl.BlockSpec(a.shape, const2 if a.ndim == 2 else const3)
    per_e = lambda blk: pl.BlockSpec((None,) + blk, lambda e: (e, 0, 0))
    small = [x, attn, u, ga, gb, sp_t, b_gate, w_pool, pool_scale, w_a, w_b, w_out, ln1_g, ln1_b,
             w_r, b_r]
    return pl.pallas_call(
        functools.partial(_s_tail_kernel, alpha=alpha),
        out_shape=jax.ShapeDtypeStruct((rows, d), F32),
        grid=(N_EXPERTS,),
        in_specs=[full(a) for a in small]
        + [per_e((d, f)), per_e((d, f)), per_e((f, d)), full(ln2_g), full(ln2_b)],
        out_specs=pl.BlockSpec((rows, d), const2),
        scratch_shapes=[pltpu.VMEM((rows, d), F32), pltpu.VMEM((rows, LANES), F32),
                        pltpu.VMEM((rows, d), F32)],
        compiler_params=pltpu.CompilerParams(
            dimension_semantics=("arbitrary",), vmem_limit_bytes=VMEM_LIMIT),
        name="s_tail",
    )(*small, w_g, w_u, w_d, ln2_g, ln2_b)


def _rope_tables(pos):
    inv_freq = 1.0 / (ROPE_THETA ** (jnp.arange(0, HEAD_DIM, 2, dtype=F32) / HEAD_DIM))
    ang = pos.astype(F32)[:, None] * inv_freq[None, :]
    cos = jnp.cos(ang)
    sin = jnp.sin(ang)
    cos_t = jnp.tile(cos, (1, LANES // (HEAD_DIM // 2)))
    sin_t = jnp.tile(jnp.concatenate([-sin, sin], axis=1), (1, LANES // HEAD_DIM))
    return cos_t, sin_t


def kernel(x_prompt, x_sample, cache_k, cache_v, state_pool, page_table, w_in, w_pool, pool_scale,
           w_branch_a, w_branch_b, b_gate, w_out, ln1_g, ln1_b, w_group_router, b_group_router,
           w_expert_router, b_expert_router, w_e_gate, w_e_up, w_e_down, ln2_g, ln2_b):
    depth = w_in.shape[0]
    assert depth == 1 and x_sample.shape[1] == 1
    alpha = (2 * depth) ** 0.25
    b, s, d = x_prompt.shape
    db = x_sample.shape[0]
    n_pages = page_table.shape[1]
    page_size = cache_k.shape[2]
    past_len = n_pages * page_size
    n_blocks = past_len // MOBA_BLOCK
    assert past_len % MOBA_BLOCK == 0 and s % MOBA_BLOCK == 0

    w_in0 = w_in[0]
    qkvu_cols = 3 * ATTN_WIDTH + POOL_WIDTH
    row2 = lambda a: a.reshape(1, -1)
    w_r = jnp.concatenate(
        [w_group_router[0], jnp.transpose(w_expert_router[0], (1, 0, 2)).reshape(d, N_EXPERTS),
         jnp.zeros((d, LANES - N_EXPERT_GROUPS - N_EXPERTS), F32)], axis=1)
    b_r = jnp.concatenate(
        [b_group_router[0], b_expert_router[0].reshape(-1),
         jnp.zeros((LANES - N_EXPERT_GROUPS - N_EXPERTS,), F32)]).reshape(1, LANES)

    cos_p, sin_p = _rope_tables(jnp.arange(s, dtype=jnp.int32))
    assert qkvu_cols == 4 * ATTN_WIDTH
    qt, k, v, u, kb, vt, km = _qkvu(x_prompt.reshape(b * s, d), w_in0, cos_p, sin_p, s, QKVU_TILE)
    nb = s // MOBA_BLOCK
    attn = _attn(qt, kb.reshape(b, s, ATTN_WIDTH), vt.reshape(b, nb, ATTN_WIDTH, MOBA_BLOCK),
                 km.reshape(b, nb, ATTN_WIDTH))
    u3 = u.reshape(b, s, POOL_WIDTH)
    x1 = _mix(x_prompt, attn, u3, w_in0, b_gate[0], w_pool[0], row2(pool_scale[0]), w_branch_a[0],
              w_branch_b[0], w_out[0], row2(ln1_g[0]), row2(ln1_b[0]), MIX_TILE, alpha)
    y_p = _moe(x1.reshape(b * s, d), w_r, b_r, w_e_gate[0], w_e_up[0], w_e_down[0],
               row2(ln2_g[0]), row2(ln2_b[0]), MOE_TILE, alpha)

    cos_s, sin_s = _rope_tables(jnp.full((1,), past_len, jnp.int32))
    x_s = x_sample.reshape(db, d)
    h_s = _s_proj(x_s, w_in0, cos_s, sin_s)
    q_s = h_s[:, :ATTN_WIDTH]
    k_s = h_s[:, ATTN_WIDTH:2 * ATTN_WIDTH]
    v_s = h_s[:, 2 * ATTN_WIDTH:3 * ATTN_WIDTH]
    u_s = h_s[:, 3 * ATTN_WIDTH:qkvu_cols]
    ga_s = h_s[:, qkvu_cols:qkvu_cols + d]
    gb_s = h_s[:, qkvu_cols + d:]
    cache_kt = jnp.transpose(cache_k[0], (0, 2, 3, 1))
    cache_vt = jnp.transpose(cache_v[0], (0, 2, 3, 1))
    scores, sel = _s_scores(page_table, q_s.T, cache_kt, n_blocks)
    sel2 = sel.reshape(db, MOBA_TOPK * N_HEADS)
    attn_s = _s_attn(page_table, sel2, scores, q_s.T, k_s.T, v_s.T, cache_vt).reshape(db, ATTN_WIDTH)
    sp = state_pool[0]
    y_s = _s_tail(x_s, attn_s, u_s, ga_s, gb_s, jnp.transpose(sp, (1, 0, 2)), b_gate[0], w_pool[0],
                  row2(pool_scale[0]), w_branch_a[0], w_branch_b[0], w_out[0], row2(ln1_g[0]),
                  row2(ln1_b[0]), w_r, b_r, w_e_gate[0], w_e_up[0], w_e_down[0], row2(ln2_g[0]),
                  row2(ln2_b[0]), alpha)

    heads = (N_HEADS, HEAD_DIM)
    return (
        y_p.reshape(b, s, d),
        y_s.reshape(db, 1, d),
        jnp.transpose(k.reshape((b,) + heads + (s,)), (0, 3, 1, 2))[None],
        jnp.transpose(v.reshape((b,) + heads + (s,)), (0, 3, 1, 2))[None],
        u3[:, s - POOL_STATE:, :][None],
        k_s.reshape((1, db, 1) + heads),
        v_s.reshape((1, db, 1) + heads),
        jnp.concatenate([sp[:, 1:, :], u_s[:, None, :]], axis=1)[None],
    )
```

```python
import functools

import jax
import jax.numpy as jnp
from jax import lax
from jax.experimental import pallas as pl
from jax.experimental.pallas import tpu as pltpu

F32 = jnp.float32
BF16 = jnp.bfloat16
HIGHEST = lax.Precision.HIGHEST

N_HEADS = 8
HEAD_DIM = 64
ATTN_WIDTH = N_HEADS * HEAD_DIM
MOBA_BLOCK = 256
MOBA_TOPK = 3
ROPE_THETA = 10000.0
POOL_WINDOWS = (2, 4, 8, 16)
POOL_GROUP_WIDTH = 128
POOL_WIDTH = 512
POOL_STATE = 15
N_EXPERT_GROUPS = 4
EXPERTS_PER_GROUP = 4
N_EXPERTS = 16
LN_EPS = 1e-5
LANES = 128
HALO_ROWS = 16
MASK_BIAS = -1e30
VMEM_LIMIT = 56 * 1024 * 1024
QKVU_TILE = 512
MIX_TILE = 256
MOE_TILE = 1024


def _dot(a, b, precision=None):
    return jnp.dot(a, b, preferred_element_type=F32, precision=precision)


def _layer_norm(y, g, b):
    mu = jnp.mean(y, axis=-1, keepdims=True)
    var = jnp.mean(jnp.square(y - mu), axis=-1, keepdims=True)
    return (y - mu) * lax.rsqrt(var + LN_EPS) * g + b


def _sigmoid(x):
    return 1.0 / (1.0 + jnp.exp(-x))


def _rope_chunk(x, cos, sin_signed, first_half):
    partner = jnp.where(first_half, pltpu.roll(x, 96, 1), pltpu.roll(x, 32, 1))
    return x * cos + partner * sin_signed


def _route_weights(logit):
    g = [logit(k) for k in range(N_EXPERT_GROUPS)]
    gmax = jnp.maximum(jnp.maximum(g[0], g[1]), jnp.maximum(g[2], g[3]))
    den = sum(jnp.exp(gk - gmax) for gk in g)
    g_w = 1.0 / den
    is_g = []
    taken = jnp.zeros_like(gmax)
    for k in range(N_EXPERT_GROUPS):
        hit = jnp.where(g[k] == gmax, 1.0, 0.0) * (1.0 - taken)
        is_g.append(hit)
        taken = taken + hit
    e = []
    for k in range(EXPERTS_PER_GROUP):
        col = jnp.zeros_like(gmax)
        for gi in range(N_EXPERT_GROUPS):
            col = jnp.where(is_g[gi] > 0.5, logit(N_EXPERT_GROUPS + gi * EXPERTS_PER_GROUP + k), col)
        e.append(col)
    v1 = jnp.maximum(jnp.maximum(e[0], e[1]), jnp.maximum(e[2], e[3]))
    first = []
    taken = jnp.zeros_like(v1)
    for k in range(EXPERTS_PER_GROUP):
        hit = jnp.where(e[k] == v1, 1.0, 0.0) * (1.0 - taken)
        first.append(hit)
        taken = taken + hit
    e2 = [jnp.where(first[k] > 0.5, -jnp.inf, e[k]) for k in range(EXPERTS_PER_GROUP)]
    v2 = jnp.maximum(jnp.maximum(e2[0], e2[1]), jnp.maximum(e2[2], e2[3]))
    second = []
    taken = jnp.zeros_like(v2)
    for k in range(EXPERTS_PER_GROUP):
        hit = jnp.where(e2[k] == v2, 1.0, 0.0) * (1.0 - taken)
        second.append(hit)
        taken = taken + hit
    t = jnp.exp(v2 - v1)
    w1 = 1.0 / (1.0 + t)
    w2 = t * w1
    return [is_g[gi] * (first[k] * w1 + second[k] * w2) * g_w
            for gi in range(N_EXPERT_GROUPS) for k in range(EXPERTS_PER_GROUP)]


def _route(logits):
    lane = lax.broadcasted_iota(jnp.int32, logits.shape, 1)
    comb = jnp.zeros(logits.shape, F32)
    for e, col in enumerate(_route_weights(lambda k: logits[:, k:k + 1])):
        comb = jnp.where(lane == e, col, comb)
    return comb


def _route_rows(logits_t):
    tokens = logits_t.shape[1]
    row = lax.broadcasted_iota(jnp.int32, (N_EXPERTS, tokens), 0)
    comb = jnp.zeros((N_EXPERTS, tokens), F32)
    for e, r in enumerate(_route_weights(lambda k: logits_t[k:k + 1, :])):
        comb = jnp.where(row == e, r, comb)
    return jnp.concatenate([comb, jnp.zeros((logits_t.shape[0] - N_EXPERTS, tokens), F32)], axis=0)


Q_SCALE = 1.4426950408889634 * HEAD_DIM ** -0.5


def _qkvu_kernel(x_ref, w_ref, cos_ref, sin_ref, qt_ref, k_ref, v_ref, u_ref, kb_ref, vt_ref,
                 km_ref, *, tm):
    xb = x_ref[...].astype(BF16)
    cos = cos_ref[...]
    sin = sin_ref[...]
    lane = lax.broadcasted_iota(jnp.int32, (tm, LANES), 1)
    first_half = (lane & 32) == 0
    hq = _dot(xb, w_ref[:, 0:ATTN_WIDTH].astype(BF16))
    hk = _dot(xb, w_ref[:, ATTN_WIDTH:2 * ATTN_WIDTH].astype(BF16))
    hv = _dot(xb, w_ref[:, 2 * ATTN_WIDTH:3 * ATTN_WIDTH].astype(BF16))
    for c in range(ATTN_WIDTH // LANES):
        sl = slice(c * LANES, (c + 1) * LANES)
        qc = _rope_chunk(hq[:, sl], cos, sin, first_half) * Q_SCALE
        kc = _rope_chunk(hk[:, sl], cos, sin, first_half)
        kb_ref[:, sl] = kc.astype(BF16)
        k_ref[sl, :] = kc.T
        vct = hv[:, sl].T
        v_ref[sl, :] = vct
        qct = qc.T.astype(BF16)
        for r in range(tm // MOBA_BLOCK):
            rows = slice(r * MOBA_BLOCK, (r + 1) * MOBA_BLOCK)
            km_ref[r:r + 1, sl] = jnp.sum(kc[rows], axis=0, keepdims=True) * (1.0 / MOBA_BLOCK)
            qt_ref[r, sl, :] = qct[:, rows]
            vt_ref[r, sl, :] = vct[:, rows].astype(BF16)
    u_ref[...] = _dot(xb, w_ref[:, 3 * ATTN_WIDTH:].astype(BF16))


def _qkvu(x2d, w_qkvu, cos, sin, seq, tm):
    n, d = x2d.shape
    s_tiles = seq // tm
    nblk = tm // MOBA_BLOCK
    row = lambda i: (i, 0)
    blocked = lambda i: (i, 0, 0)
    featmajor = lambda i: (i // s_tiles, 0, i % s_tiles)
    out_shape = (
        jax.ShapeDtypeStruct((n // MOBA_BLOCK, ATTN_WIDTH, MOBA_BLOCK), BF16),
        jax.ShapeDtypeStruct((n // seq, ATTN_WIDTH, seq), F32),
        jax.ShapeDtypeStruct((n // seq, ATTN_WIDTH, seq), F32),
        jax.ShapeDtypeStruct((n, POOL_WIDTH), F32),
        jax.ShapeDtypeStruct((n, ATTN_WIDTH), BF16),
        jax.ShapeDtypeStruct((n // MOBA_BLOCK, ATTN_WIDTH, MOBA_BLOCK), BF16),
        jax.ShapeDtypeStruct((n // tm, nblk, ATTN_WIDTH), F32),
    )
    return pl.pallas_call(
        functools.partial(_qkvu_kernel, tm=tm),
        out_shape=out_shape,
        grid=(n // tm,),
        in_specs=[
            pl.BlockSpec((tm, d), row),
            pl.BlockSpec((d, 4 * ATTN_WIDTH), lambda i: (0, 0)),
            pl.BlockSpec((tm, LANES), lambda i: (i % s_tiles, 0)),
            pl.BlockSpec((tm, LANES), lambda i: (i % s_tiles, 0)),
        ],
        out_specs=(
            pl.BlockSpec((nblk, ATTN_WIDTH, MOBA_BLOCK), blocked),
            pl.BlockSpec((None, ATTN_WIDTH, tm), featmajor),
            pl.BlockSpec((None, ATTN_WIDTH, tm), featmajor),
            pl.BlockSpec((tm, POOL_WIDTH), row),
            pl.BlockSpec((tm, ATTN_WIDTH), row),
            pl.BlockSpec((nblk, ATTN_WIDTH, MOBA_BLOCK), blocked),
            pl.BlockSpec((None, nblk, ATTN_WIDTH), blocked),
        ),
        compiler_params=pltpu.CompilerParams(
            dimension_semantics=("arbitrary",), vmem_limit_bytes=VMEM_LIMIT),
        name="qkvu",
    )(x2d, w_qkvu, cos, sin)


BIAS_ROWS = 128
SUM_ROWS = 16
ATTN_PAIRS = 4


def _attn_kernel(qt_ref, k_ref, vt_ref, km_ref, o_ref, qa_ref, acc_ref, sa_ref, sb_ref):
    i = pl.program_id(2)
    n_past = km_ref.shape[0]
    n_heads = 2 * ATTN_PAIRS
    feat = lax.broadcasted_iota(jnp.int32, (LANES, MOBA_BLOCK), 0)
    head0 = feat < HEAD_DIM
    blk_id = lax.broadcasted_iota(jnp.int32, (n_past, MOBA_BLOCK), 0)
    blk_f = blk_id.astype(F32)
    past = blk_id < i
    key_id = lax.broadcasted_iota(jnp.int32, (MOBA_BLOCK, MOBA_BLOCK), 0)
    qry_id = lax.broadcasted_iota(jnp.int32, (MOBA_BLOCK, MOBA_BLOCK), 1)
    causal = key_id <= qry_id
    lane = lax.broadcasted_iota(jnp.int32, (MOBA_BLOCK, LANES), 1)
    ones_rows = jnp.ones((SUM_ROWS, MOBA_BLOCK), BF16)
    bias_pad = jnp.zeros((BIAS_ROWS - n_past, MOBA_BLOCK), BF16)
    own = pl.multiple_of(i * MOBA_BLOCK, MOBA_BLOCK)

    def slab(pair):
        return slice(pair * LANES, (pair + 1) * LANES)

    def vt_aug(blk, pair):
        return jnp.concatenate([vt_ref[blk, slab(pair), :], ones_rows], axis=0)

    def consume_refill(s_ref, ms, blk, nxt):
        off = pl.multiple_of(nxt * MOBA_BLOCK, MOBA_BLOCK)
        onehot = jnp.where(lane == nxt, 1.0, 0.0).astype(BF16)
        out = []
        for h in range(n_heads):
            s = s_ref[h]
            m_new = jnp.maximum(ms[h], jnp.max(s, axis=0, keepdims=True))
            alpha = jnp.exp2(ms[h] - m_new)
            p = jnp.exp2(s - m_new)
            acc_ref[h] = alpha * acc_ref[h] + _dot(vt_aug(blk, h // 2), p.astype(BF16))
            out.append(m_new)
            ka = jnp.concatenate([k_ref[pl.ds(off, MOBA_BLOCK), slab(h // 2)], onehot], axis=1)
            s_ref[h] = _dot(ka, qa_ref[h])
        return tuple(out)

    for h in range(n_heads):
        pair = h // 2
        qt = qt_ref[slab(pair), :]
        qh = jnp.where(head0 if h % 2 == 0 else jnp.logical_not(head0), qt, jnp.zeros_like(qt))
        km = km_ref[:, slab(pair)]
        km_hi = km.astype(BF16)
        km_mid = (km - km_hi.astype(F32)).astype(BF16)
        km_lo = (km - km_hi.astype(F32) - km_mid.astype(F32)).astype(BF16)
        gate = _dot(km_hi, qh) + _dot(km_mid, qh) + _dot(km_lo, qh)
        g = jnp.where(past, gate, -jnp.inf)
        sel = jnp.zeros(g.shape, F32)
        for _ in range(MOBA_TOPK):
            mx = jnp.max(g, axis=0, keepdims=True)
            idx = jnp.min(jnp.where(g == mx, blk_f, float(n_past)), axis=0, keepdims=True)
            pick = blk_f == idx
            sel = jnp.where(pick, 1.0, sel)
            g = jnp.where(pick, -jnp.inf, g)
        bias = jnp.where(jnp.logical_and(sel > 0.5, past), 0.0, MASK_BIAS).astype(BF16)
        qa_ref[h] = jnp.concatenate([qh, bias, bias_pad], axis=0)
        kd = k_ref[pl.ds(own, MOBA_BLOCK), slab(pair)]
        sa_ref[h] = jnp.where(causal, _dot(kd, qh), -jnp.inf)
        acc_ref[h] = jnp.zeros(acc_ref.shape[1:], F32)
    m0 = jnp.full((1, MOBA_BLOCK), MASK_BIAS, F32)

    last_past = jnp.maximum(i - 1, 0)
    off0 = pl.multiple_of(0 * MOBA_BLOCK, MOBA_BLOCK)
    onehot0 = jnp.where(lane == 0, 1.0, 0.0).astype(BF16)
    for h in range(n_heads):
        ka = jnp.concatenate([k_ref[pl.ds(off0, MOBA_BLOCK), slab(h // 2)], onehot0], axis=1)
        sb_ref[h] = _dot(ka, qa_ref[h])

    def pair_body(u, ms):
        ms = consume_refill(sa_ref, ms, jnp.where(u == 0, i, 2 * u - 1),
                            jnp.minimum(2 * u + 1, last_past))
        return consume_refill(sb_ref, ms, 2 * u, jnp.minimum(2 * u + 2, last_past))

    ms = lax.fori_loop(0, (i + 1) // 2, pair_body, (m0,) * n_heads)

    @pl.when((i + 1) % 2 == 1)
    def _():
        blk = jnp.where(i == 0, i, i - 1)
        for h in range(n_heads):
            s = sa_ref[h]
            m_new = jnp.maximum(ms[h], jnp.max(s, axis=0, keepdims=True))
            alpha = jnp.exp2(ms[h] - m_new)
            p = jnp.exp2(s - m_new)
            acc_ref[h] = alpha * acc_ref[h] + _dot(vt_aug(blk, h // 2), p.astype(BF16))

    for pair in range(ATTN_PAIRS):
        a0 = acc_ref[2 * pair]
        a1 = acc_ref[2 * pair + 1]
        ot = jnp.where(head0, a0[:LANES] / a0[LANES:LANES + 1], a1[:LANES] / a1[LANES:LANES + 1])
        o_ref[:, slab(pair)] = ot.T.astype(BF16)


def _attn(qt, kb, vt, km):
    b, s, _ = kb.shape
    nb = s // MOBA_BLOCK
    width = ATTN_PAIRS * LANES
    return pl.pallas_call(
        _attn_kernel,
        out_shape=jax.ShapeDtypeStruct((b, s, ATTN_WIDTH), BF16),
        grid=(b, ATTN_WIDTH // width, nb),
        in_specs=[
            pl.BlockSpec((None, width, MOBA_BLOCK), lambda bi, p, i: (bi * nb + i, p, 0)),
            pl.BlockSpec((None, s, width), lambda bi, p, i: (bi, 0, p)),
            pl.BlockSpec((None, nb, width, MOBA_BLOCK), lambda bi, p, i: (bi, 0, p, 0)),
            pl.BlockSpec((None, nb, width), lambda bi, p, i: (bi, 0, p)),
        ],
        out_specs=pl.BlockSpec((None, MOBA_BLOCK, width), lambda bi, p, i: (bi, i, p)),
        scratch_shapes=[
            pltpu.VMEM((2 * ATTN_PAIRS, LANES + BIAS_ROWS, MOBA_BLOCK), BF16),
            pltpu.VMEM((2 * ATTN_PAIRS, LANES + SUM_ROWS, MOBA_BLOCK), F32),
            pltpu.VMEM((2 * ATTN_PAIRS, MOBA_BLOCK, MOBA_BLOCK), F32),
            pltpu.VMEM((2 * ATTN_PAIRS, MOBA_BLOCK, MOBA_BLOCK), F32),
        ],
        compiler_params=pltpu.CompilerParams(
            dimension_semantics=("arbitrary", "arbitrary", "arbitrary"),
            vmem_limit_bytes=VMEM_LIMIT),
        name="moba_attn",
    )(qt, kb, vt, km)


def _pool_group(window_sum, u_g, count, w_pool_g, scale_g, precision=None, cast=None):
    d = window_sum / count - u_g
    if cast is not None:
        d = d.astype(cast)
    return _dot(d, w_pool_g, precision) * scale_g


def _mix_kernel(x_ref, attn_ref, u_ref, halo_ref, wg_ref, bg_ref, wpool_ref, pscale_ref, wa_ref,
                wb_ref, wout_ref, g1_ref, b1_ref, x1_ref, z_ref, *, tm, alpha):
    i = pl.program_id(1)
    x = x_ref[...]
    xb = x.astype(BF16)
    u = u_ref[...]
    z_ref[0:HALO_ROWS, :] = jnp.where(i > 0, halo_ref[...], 0.0)
    z_ref[HALO_ROWS:, :] = u
    pos1 = (i * tm + 1 + lax.broadcasted_iota(jnp.int32, (tm, 1), 0)).astype(F32)
    parts = []
    for g, w in enumerate(POOL_WINDOWS):
        sl = slice(g * POOL_GROUP_WIDTH, (g + 1) * POOL_GROUP_WIDTH)
        acc = u[:, sl]
        for back in range(1, w):
            acc = acc + z_ref[HALO_ROWS - back:HALO_ROWS - back + tm, sl]
        count = jnp.minimum(float(w), pos1)
        parts.append(_pool_group(acc, u[:, sl], count, wpool_ref[g].astype(BF16), pscale_ref[:, sl],
                                 cast=BF16))
    pooled = jnp.concatenate(parts, axis=1).astype(BF16)
    d = x.shape[1]
    ga = _dot(xb, wg_ref[:, :d].astype(BF16)) + bg_ref[0:1, :]
    gb = _dot(xb, wg_ref[:, d:].astype(BF16)) + bg_ref[1:2, :]
    a = _dot(attn_ref[...], wa_ref[...].astype(BF16))
    bb = _dot(pooled, wb_ref[...].astype(BF16))
    merged = _sigmoid(ga) * a + _sigmoid(gb) * bb
    y = alpha * x + _dot(merged.astype(BF16), wout_ref[...].astype(BF16))
    x1_ref[...] = _layer_norm(y, g1_ref[...], b1_ref[...])


def _mix(x, attn, u, w_in, b_gate, w_pool, pool_scale, w_a, w_b, w_out, ln_g, ln_b, tm, alpha):
    b, s, d = x.shape
    gate_block = w_in.shape[1] // (2 * d) - 1
    assert w_in.shape[1] == (gate_block + 1) * 2 * d
    w_gates = w_in
    halo_per_tile = tm // HALO_ROWS
    tile = lambda bi, i: (bi, i, 0)
    const2 = lambda bi, i: (0, 0)
    return pl.pallas_call(
        functools.partial(_mix_kernel, tm=tm, alpha=alpha),
        out_shape=jax.ShapeDtypeStruct((b, s, d), F32),
        grid=(b, s // tm),
        in_specs=[
            pl.BlockSpec((None, tm, d), tile),
            pl.BlockSpec((None, tm, ATTN_WIDTH), tile),
            pl.BlockSpec((None, tm, POOL_WIDTH), tile),
            pl.BlockSpec((None, HALO_ROWS, POOL_WIDTH),
                         lambda bi, i: (bi, jnp.maximum(i * halo_per_tile - 1, 0), 0)),
            pl.BlockSpec((d, 2 * d), lambda bi, i: (0, gate_block)),
            pl.BlockSpec(b_gate.shape, const2),
            pl.BlockSpec(w_pool.shape, lambda bi, i: (0, 0, 0)),
            pl.BlockSpec(pool_scale.shape, const2),
            pl.BlockSpec(w_a.shape, const2),
            pl.BlockSpec(w_b.shape, const2),
            pl.BlockSpec(w_out.shape, const2),
            pl.BlockSpec(ln_g.shape, const2),
            pl.BlockSpec(ln_b.shape, const2),
        ],
        out_specs=pl.BlockSpec((None, tm, d), tile),
        scratch_shapes=[pltpu.VMEM((HALO_ROWS + tm, POOL_WIDTH), F32)],
        compiler_params=pltpu.CompilerParams(
            dimension_semantics=("arbitrary", "arbitrary"), vmem_limit_bytes=VMEM_LIMIT),
        name="mix",
    )(x, attn, u, u, w_gates, b_gate, w_pool, pool_scale, w_a, w_b, w_out, ln_g, ln_b)


def _moe_kernel(x1_ref, wrh_ref, wrl_ref, br_ref, wg_ref, wu_ref, wd_ref, g2_ref, b2_ref, y_ref,
                xb_ref, comb_ref, acc_ref, *, alpha):
    g = pl.program_id(1)
    tm = x1_ref.shape[0]
    lane = lax.broadcasted_iota(jnp.int32, (tm, LANES), 1)

    @pl.when(g == 0)
    def _():
        x1 = x1_ref[...]
        xh = x1.astype(BF16)
        xb_ref[...] = xh
        xl = (x1 - xh.astype(F32)).astype(BF16)
        logits = (_dot(xh, wrh_ref[...]) + _dot(xl, wrh_ref[...]) + _dot(xh, wrl_ref[...])
                  + br_ref[...])
        comb_ref[...] = _route_rows(logits.T).T

    xb = xb_ref[...]
    comb = comb_ref[...]
    hidden = []
    for k in range(EXPERTS_PER_GROUP):
        hg = _dot(xb, wg_ref[k].astype(BF16))
        hu = _dot(xb, wu_ref[k].astype(BF16))
        c_k = jnp.sum(jnp.where(lane == g * EXPERTS_PER_GROUP + k, comb, 0.0), axis=1, keepdims=True)
        hidden.append((hg * _sigmoid(hg) * hu * c_k).astype(BF16))
    part = _dot(jnp.concatenate(hidden, axis=1), wd_ref[...].astype(BF16))

    @pl.when(g == 0)
    def _():
        acc_ref[...] = part

    @pl.when(g > 0)
    def _():
        acc_ref[...] += part

    @pl.when(g == N_EXPERT_GROUPS - 1)
    def _():
        y = alpha * x1_ref[...] + acc_ref[...]
        y_ref[...] = _layer_norm(y, g2_ref[...], b2_ref[...])


def _moe(x1, w_r, b_r, w_g, w_u, w_d, ln_g, ln_b, tm, alpha):
    w_r_hi = w_r.astype(BF16)
    w_r_lo = (w_r - w_r_hi.astype(F32)).astype(BF16)
    n, d = x1.shape
    f = w_g.shape[2]
    per_group = (N_EXPERT_GROUPS, EXPERTS_PER_GROUP)
    tile = lambda i, g: (i, 0)
    const2 = lambda i, g: (0, 0)
    group4 = lambda i, g: (g, 0, 0, 0)
    return pl.pallas_call(
        functools.partial(_moe_kernel, alpha=alpha),
        out_shape=jax.ShapeDtypeStruct((n, d), F32),
        grid=(n // tm, N_EXPERT_GROUPS),
        in_specs=[
            pl.BlockSpec((tm, d), tile),
            pl.BlockSpec(w_r.shape, const2),
            pl.BlockSpec(w_r.shape, const2),
            pl.BlockSpec(b_r.shape, const2),
            pl.BlockSpec((None, EXPERTS_PER_GROUP, d, f), group4),
            pl.BlockSpec((None, EXPERTS_PER_GROUP, d, f), group4),
            pl.BlockSpec((None, EXPERTS_PER_GROUP * f, d), lambda i, g: (g, 0, 0)),
            pl.BlockSpec(ln_g.shape, const2),
            pl.BlockSpec(ln_b.shape, const2),
        ],
        out_specs=pl.BlockSpec((tm, d), tile),
        scratch_shapes=[
            pltpu.VMEM((tm, d), BF16),
            pltpu.VMEM((tm, LANES), F32),
            pltpu.VMEM((tm, d), F32),
        ],
        compiler_params=pltpu.CompilerParams(
            dimension_semantics=("arbitrary", "arbitrary"), vmem_limit_bytes=VMEM_LIMIT),
        name="moe",
    )(x1, w_r_hi, w_r_lo, b_r, w_g.reshape(per_group + (d, f)), w_u.reshape(per_group + (d, f)),
      w_d.reshape(N_EXPERT_GROUPS, EXPERTS_PER_GROUP * f, d), ln_g, ln_b)


def _s_proj_kernel(x_ref, w_ref, cos_ref, sin_ref, h_ref):
    c = pl.program_id(0)
    h = _dot(x_ref[...], w_ref[...], HIGHEST)
    rows = h.shape[0]
    lane = lax.broadcasted_iota(jnp.int32, (rows, LANES), 1)
    first_half = (lane & 32) == 0
    rotary = c < 2
    for j in range(h.shape[1] // LANES):
        sl = slice(j * LANES, (j + 1) * LANES)
        hc = h[:, sl]
        h_ref[:, sl] = jnp.where(rotary, _rope_chunk(hc, cos_ref[...], sin_ref[...], first_half), hc)


def _s_proj(x, w_in, cos, sin):
    rows, d = x.shape
    width = w_in.shape[1]
    chunk = ATTN_WIDTH
    return pl.pallas_call(
        _s_proj_kernel,
        out_shape=jax.ShapeDtypeStruct((rows, width), F32),
        grid=(width // chunk,),
        in_specs=[
            pl.BlockSpec((rows, d), lambda c: (0, 0)),
            pl.BlockSpec((d, chunk), lambda c: (0, c)),
            pl.BlockSpec((1, LANES), lambda c: (0, 0)),
            pl.BlockSpec((1, LANES), lambda c: (0, 0)),
        ],
        out_specs=pl.BlockSpec((rows, chunk), lambda c: (0, c)),
        compiler_params=pltpu.CompilerParams(
            dimension_semantics=("arbitrary",), vmem_limit_bytes=VMEM_LIMIT),
        name="s_proj",
    )(x, w_in, cos, sin)


PAGES_PER_STEP = 64


def _column(ref, b):
    x = ref[...]
    lane = lax.broadcasted_iota(jnp.int32, x.shape, 1)
    col = jnp.sum(jnp.where(lane == b, x, 0.0), axis=1, keepdims=True)
    return col.reshape(N_HEADS, HEAD_DIM, 1)


def _s_scores_kernel(pt_ref, q_ref, *refs, page_size, n_blocks):
    page_refs = refs[:PAGES_PER_STEP]
    sc_ref, sel_ref, gs_ref = refs[PAGES_PER_STEP:]
    c = pl.program_id(1)
    pages_per_block = MOBA_BLOCK // page_size
    blocks_per_step = PAGES_PER_STEP // pages_per_block
    q = _column(q_ref, pl.program_id(0)) * (HEAD_DIM ** -0.5)
    for r in range(blocks_per_step):
        tot = jnp.zeros((N_HEADS, page_size), F32)
        for pp in range(pages_per_block):
            p = r * pages_per_block + pp
            s = jnp.sum(page_refs[p][...] * q, axis=1)
            sc_ref[p] = s
            tot = tot + s
        gs_ref[c * blocks_per_step + r] = jnp.sum(tot, axis=1, keepdims=True)

    @pl.when(c == pl.num_programs(1) - 1)
    def _():
        g = gs_ref[...]
        blk_f = lax.broadcasted_iota(jnp.int32, g.shape, 0).astype(F32)
        for t in range(MOBA_TOPK):
            mx = jnp.max(g, axis=0, keepdims=True)
            idx = jnp.min(jnp.where(g == mx, blk_f, float(n_blocks)), axis=0, keepdims=True)
            sel_ref[t:t + 1] = idx.astype(jnp.int32)
            g = jnp.where(blk_f == idx, -jnp.inf, g)


def _s_scores(page_table, qt, cache_kt, n_blocks):
    db, n_pages = page_table.shape
    page_size = cache_kt.shape[3]
    page_block = (None, N_HEADS, HEAD_DIM, page_size)

    def page_spec(r):
        return pl.BlockSpec(page_block, lambda b, c, pt: (pt[b, c * PAGES_PER_STEP + r], 0, 0, 0))

    return pl.pallas_call(
        functools.partial(_s_scores_kernel, page_size=page_size, n_blocks=n_blocks),
        out_shape=(jax.ShapeDtypeStruct((db, n_pages, N_HEADS, page_size), F32),
                   jax.ShapeDtypeStruct((db, MOBA_TOPK, N_HEADS, 1), jnp.int32)),
        grid_spec=pltpu.PrefetchScalarGridSpec(
            num_scalar_prefetch=1,
            grid=(db, n_pages // PAGES_PER_STEP),
            in_specs=[pl.BlockSpec(qt.shape, lambda b, c, pt: (0, 0))]
            + [page_spec(r) for r in range(PAGES_PER_STEP)],
            out_specs=(pl.BlockSpec((None, PAGES_PER_STEP, N_HEADS, page_size),
                                    lambda b, c, pt: (b, c, 0, 0)),
                       pl.BlockSpec((None, MOBA_TOPK, N_HEADS, 1), lambda b, c, pt: (b, 0, 0, 0))),
            scratch_shapes=[pltpu.VMEM((n_blocks, N_HEADS, 1), F32)],
        ),
        compiler_params=pltpu.CompilerParams(
            dimension_semantics=("arbitrary", "arbitrary"), vmem_limit_bytes=VMEM_LIMIT),
        name="s_scores",
    )(page_table, qt, *([cache_kt] * PAGES_PER_STEP))


def _s_attn_kernel(pt_ref, sel_ref, sc_ref, q_ref, kn_ref, vn_ref, cv_ref, o_ref, vbuf, sem,
                   *, pages_per_block):
    b = pl.program_id(0)
    slices_per_seq = N_HEADS * MOBA_TOPK * pages_per_block

    def slot_index(slot, h, t, r):
        return slot * slices_per_seq + (h * MOBA_TOPK + t) * pages_per_block + r

    def copies(seq, slot):
        out = []
        for h in range(N_HEADS):
            for t in range(MOBA_TOPK):
                blk = sel_ref[seq, t * N_HEADS + h]
                for r in range(pages_per_block):
                    page = pt_ref[seq, blk * pages_per_block + r]
                    out.append(pltpu.make_async_copy(
                        cv_ref.at[page, h], vbuf.at[slot_index(slot, h, t, r)], sem.at[slot]))
        return out

    @pl.when(b == 0)
    def _():
        for cp in copies(0, 0):
            cp.start()

    @pl.when(b + 1 < pl.num_programs(0))
    def _():
        for cp in copies(b + 1, (b + 1) % 2):
            cp.start()

    slot = b % 2
    for cp in copies(b, slot):
        cp.wait()

    scale = HEAD_DIM ** -0.5
    q = _column(q_ref, b)
    kn = _column(kn_ref, b)
    vn = _column(vn_ref, b)
    for h in range(N_HEADS):
        s_self = jnp.sum(q[h] * kn[h], axis=0, keepdims=True) * scale
        scores = []
        mx = s_self
        for t in range(MOBA_TOPK):
            blk = sel_ref[b, t * N_HEADS + h]
            for r in range(pages_per_block):
                sc = sc_ref[blk * pages_per_block + r, h:h + 1, :]
                scores.append(sc)
                mx = jnp.maximum(mx, jnp.max(sc, axis=1, keepdims=True))
        p_self = jnp.exp(s_self - mx)
        p_sum = jnp.zeros(scores[0].shape, F32)
        pv_sum = jnp.zeros(vbuf.shape[1:], F32)
        n = 0
        for t in range(MOBA_TOPK):
            for r in range(pages_per_block):
                p = jnp.exp(scores[n] - mx)
                n += 1
                p_sum = p_sum + p
                pv_sum = pv_sum + vbuf[slot_index(slot, h, t, r)] * p
        den = p_self + jnp.sum(p_sum, axis=1, keepdims=True)
        acc = p_self * vn[h] + jnp.sum(pv_sum, axis=1, keepdims=True)
        o_ref[h] = acc / den


def _s_attn(page_table, sel, scores, qt, knt, vnt, cache_vt):
    db, n_pages = page_table.shape
    page_size = cache_vt.shape[3]
    pages_per_block = MOBA_BLOCK // page_size
    n_slices = 2 * N_HEADS * MOBA_TOPK * pages_per_block
    per_b = pl.BlockSpec((None, N_HEADS, HEAD_DIM, 1), lambda b, pt, sl: (b, 0, 0, 0))
    whole = pl.BlockSpec(qt.shape, lambda b, pt, sl: (0, 0))
    return pl.pallas_call(
        functools.partial(_s_attn_kernel, pages_per_block=pages_per_block),
        out_shape=jax.ShapeDtypeStruct((db, N_HEADS, HEAD_DIM, 1), F32),
        grid_spec=pltpu.PrefetchScalarGridSpec(
            num_scalar_prefetch=2,
            grid=(db,),
            in_specs=[pl.BlockSpec((None, n_pages, N_HEADS, page_size),
                                   lambda b, pt, sl: (b, 0, 0, 0)),
                      whole, whole, whole, pl.BlockSpec(memory_space=pl.ANY)],
            out_specs=per_b,
            scratch_shapes=[pltpu.VMEM((n_slices, HEAD_DIM, page_size), F32),
                            pltpu.SemaphoreType.DMA((2,))],
        ),
        compiler_params=pltpu.CompilerParams(
            dimension_semantics=("arbitrary",), vmem_limit_bytes=VMEM_LIMIT),
        name="s_attn",
    )(page_table, sel, scores, qt, knt, vnt, cache_vt)


def _s_tail_kernel(x_ref, attn_ref, u_ref, ga_ref, gb_ref, sp_ref, bg_ref, wpool_ref, pscale_ref,
                   wa_ref, wb_ref, wout_ref, g1_ref, b1_ref, wr_ref, br_ref, wg_ref, wu_ref,
                   wd_ref, g2_ref, b2_ref, y_ref, x1_ref, comb_ref, acc_ref, *, alpha):
    e = pl.program_id(0)
    rows = x_ref.shape[0]
    lane = lax.broadcasted_iota(jnp.int32, (rows, LANES), 1)

    @pl.when(e == 0)
    def _():
        u = u_ref[...]
        parts = []
        for g, w in enumerate(POOL_WINDOWS):
            sl = slice(g * POOL_GROUP_WIDTH, (g + 1) * POOL_GROUP_WIDTH)
            acc = u[:, sl]
            for back in range(1, w):
                acc = acc + sp_ref[POOL_STATE - back][:, sl]
            parts.append(_pool_group(acc, u[:, sl], float(w), wpool_ref[g], pscale_ref[:, sl],
                                     precision=HIGHEST))
        pooled = jnp.concatenate(parts, axis=1)
        a = _dot(attn_ref[...], wa_ref[...], HIGHEST)
        bb = _dot(pooled, wb_ref[...], HIGHEST)
        merged = _sigmoid(ga_ref[...] + bg_ref[0:1, :]) * a + _sigmoid(gb_ref[...] + bg_ref[1:2, :]) * bb
        y = alpha * x_ref[...] + _dot(merged, wout_ref[...], HIGHEST)
        x1 = _layer_norm(y, g1_ref[...], b1_ref[...])
        x1_ref[...] = x1
        comb_ref[...] = _route(_dot(x1, wr_ref[...], HIGHEST) + br_ref[...])
        acc_ref[...] = jnp.zeros_like(acc_ref)

    x1 = x1_ref[...]
    hg = _dot(x1, wg_ref[...], HIGHEST)
    hu = _dot(x1, wu_ref[...], HIGHEST)
    c_e = jnp.sum(jnp.where(lane == e, comb_ref[...], 0.0), axis=1, keepdims=True)
    h = hg * _sigmoid(hg) * hu * c_e
    acc_ref[...] += _dot(h, wd_ref[...], HIGHEST)

    @pl.when(e == N_EXPERTS - 1)
    def _():
        y_ref[...] = _layer_norm(alpha * x1_ref[...] + acc_ref[...], g2_ref[...], b2_ref[...])


def _s_tail(x, attn, u, ga, gb, sp_t, b_gate, w_pool, pool_scale, w_a, w_b, w_out, ln1_g, ln1_b,
            w_r, b_r, w_g, w_u, w_d, ln2_g, ln2_b, alpha):
    rows, d = x.shape
    f = w_g.shape[2]
    const2 = lambda e: (0, 0)
    const3 = lambda e: (0, 0, 0)
    full = lambda a: pl.BlockSpec(a.shape, const2 if a.ndim == 2 else const3)
    per_e = lambda blk: pl.BlockSpec((None,) + blk, lambda e: (e, 0, 0))
    small = [x, attn, u, ga, gb, sp_t, b_gate, w_pool, pool_scale, w_a, w_b, w_out, ln1_g, ln1_b,
             w_r, b_r]
    return pl.pallas_call(
        functools.partial(_s_tail_kernel, alpha=alpha),
        out_shape=jax.ShapeDtypeStruct((rows, d), F32),
        grid=(N_EXPERTS,),
        in_specs=[full(a) for a in small]
        + [per_e((d, f)), per_e((d, f)), per_e((f, d)), full(ln2_g), full(ln2_b)],
        out_specs=pl.BlockSpec((rows, d), const2),
        scratch_shapes=[pltpu.VMEM((rows, d), F32), pltpu.VMEM((rows, LANES), F32),
                        pltpu.VMEM((rows, d), F32)],
        compiler_params=pltpu.CompilerParams(
            dimension_semantics=("arbitrary",), vmem_limit_bytes=VMEM_LIMIT),
        name="s_tail",
    )(*small, w_g, w_u, w_d, ln2_g, ln2_b)


def _rope_tables(pos):
    inv_freq = 1.0 / (ROPE_THETA ** (jnp.arange(0, HEAD_DIM, 2, dtype=F32) / HEAD_DIM))
    ang = pos.astype(F32)[:, None] * inv_freq[None, :]
    cos = jnp.cos(ang)
    sin = jnp.sin(ang)
    cos_t = jnp.tile(cos, (1, LANES // (HEAD_DIM // 2)))
    sin_t = jnp.tile(jnp.concatenate([-sin, sin], axis=1), (1, LANES // HEAD_DIM))
    return cos_t, sin_t


def kernel(x_prompt, x_sample, cache_k, cache_v, state_pool, page_table, w_in, w_pool, pool_scale,
           w_branch_a, w_branch_b, b_gate, w_out, ln1_g, ln1_b, w_group_router, b_group_router,
           w_expert_router, b_expert_router, w_e_gate, w_e_up, w_e_down, ln2_g, ln2_b):
    depth = w_in.shape[0]
    assert depth == 1 and x_sample.shape[1] == 1
    alpha = (2 * depth) ** 0.25
    b, s, d = x_prompt.shape
    db = x_sample.shape[0]
    n_pages = page_table.shape[1]
    page_size = cache_k.shape[2]
    past_len = n_pages * page_size
    n_blocks = past_len // MOBA_BLOCK
    assert past_len % MOBA_BLOCK == 0 and s % MOBA_BLOCK == 0

    w_in0 = w_in[0]
    qkvu_cols = 3 * ATTN_WIDTH + POOL_WIDTH
    row2 = lambda a: a.reshape(1, -1)
    w_r = jnp.concatenate(
        [w_group_router[0], jnp.transpose(w_expert_router[0], (1, 0, 2)).reshape(d, N_EXPERTS),
         jnp.zeros((d, LANES - N_EXPERT_GROUPS - N_EXPERTS), F32)], axis=1)
    b_r = jnp.concatenate(
        [b_group_router[0], b_expert_router[0].reshape(-1),
         jnp.zeros((LANES - N_EXPERT_GROUPS - N_EXPERTS,), F32)]).reshape(1, LANES)

    cos_p, sin_p = _rope_tables(jnp.arange(s, dtype=jnp.int32))
    assert qkvu_cols == 4 * ATTN_WIDTH
    qt, k, v, u, kb, vt, km = _qkvu(x_prompt.reshape(b * s, d), w_in0, cos_p, sin_p, s, QKVU_TILE)
    nb = s // MOBA_BLOCK
    attn = _attn(qt, kb.reshape(b, s, ATTN_WIDTH), vt.reshape(b, nb, ATTN_WIDTH, MOBA_BLOCK),
                 km.reshape(b, nb, ATTN_WIDTH))
    u3 = u.reshape(b, s, POOL_WIDTH)
    x1 = _mix(x_prompt, attn, u3, w_in0, b_gate[0], w_pool[0], row2(pool_scale[0]), w_branch_a[0],
              w_branch_b[0], w_out[0], row2(ln1_g[0]), row2(ln1_b[0]), MIX_TILE, alpha)
    y_p = _moe(x1.reshape(b * s, d), w_r, b_r, w_e_gate[0], w_e_up[0], w_e_down[0],
               row2(ln2_g[0]), row2(ln2_b[0]), MOE_TILE, alpha)

    cos_s, sin_s = _rope_tables(jnp.full((1,), past_len, jnp.int32))
    x_s = x_sample.reshape(db, d)
    h_s = _s_proj(x_s, w_in0, cos_s, sin_s)
    q_s = h_s[:, :ATTN_WIDTH]
    k_s = h_s[:, ATTN_WIDTH:2 * ATTN_WIDTH]
    v_s = h_s[:, 2 * ATTN_WIDTH:3 * ATTN_WIDTH]
    u_s = h_s[:, 3 * ATTN_WIDTH:qkvu_cols]
    ga_s = h_s[:, qkvu_cols:qkvu_cols + d]
    gb_s = h_s[:, qkvu_cols + d:]
    cache_kt = jnp.transpose(cache_k[0], (0, 2, 3, 1))
    cache_vt = jnp.transpose(cache_v[0], (0, 2, 3, 1))
    scores, sel = _s_scores(page_table, q_s.T, cache_kt, n_blocks)
    sel2 = sel.reshape(db, MOBA_TOPK * N_HEADS)
    attn_s = _s_attn(page_table, sel2, scores, q_s.T, k_s.T, v_s.T, cache_vt).reshape(db, ATTN_WIDTH)
    sp = state_pool[0]
    y_s = _s_tail(x_s, attn_s, u_s, ga_s, gb_s, jnp.transpose(sp, (1, 0, 2)), b_gate[0], w_pool[0],
                  row2(pool_scale[0]), w_branch_a[0], w_branch_b[0], w_out[0], row2(ln1_g[0]),
                  row2(ln1_b[0]), w_r, b_r, w_e_gate[0], w_e_up[0], w_e_down[0], row2(ln2_g[0]),
                  row2(ln2_b[0]), alpha)

    heads = (N_HEADS, HEAD_DIM)
    return (
        y_p.reshape(b, s, d),
        y_s.reshape(db, 1, d),
        jnp.transpose(k.reshape((b,) + heads + (s,)), (0, 3, 1, 2))[None],
        jnp.transpose(v.reshape((b,) + heads + (s,)), (0, 3, 1, 2))[None],
        u3[:, s - POOL_STATE:, :][None],
        k_s.reshape((1, db, 1) + heads),
        v_s.reshape((1, db, 1) + heads),
        jnp.concatenate([sp[:, 1:, :], u_s[:, None, :]], axis=1)[None],
    )
```

```python
import functools

import jax
import jax.numpy as jnp
from jax import lax
from jax.experimental import pallas as pl
from jax.experimental.pallas import tpu as pltpu

F32 = jnp.float32
BF16 = jnp.bfloat16

N_HEADS = 8
HEAD_DIM = 64
ATTN_WIDTH = N_HEADS * HEAD_DIM
MOBA_BLOCK = 256
MOBA_TOPK = 3
ROPE_THETA = 10000.0
POOL_WINDOWS = (2, 4, 8, 16)
POOL_GROUP_WIDTH = 128
POOL_WIDTH = 512
POOL_STATE = 15
N_EXPERT_GROUPS = 4
EXPERTS_PER_GROUP = 4
N_EXPERTS = 16
LN_EPS = 1e-5
LANES = 128
HALO_ROWS = 16
MASK_BIAS = -1e30
VMEM_LIMIT = 56 * 1024 * 1024
QKVU_TILE = 1024
MIX_TILE = 512
MOE_TILE = 1024


def _dot(a, b, precision=None):
    return jnp.dot(a, b, preferred_element_type=F32, precision=precision)


def _round_bf16(x):
    return x.astype(BF16).astype(F32)


def _bdot(a, b):
    return _dot(a.astype(BF16), b.astype(BF16))


def _layer_norm(y, g, b):
    mu = jnp.mean(y, axis=-1, keepdims=True)
    var = jnp.mean(jnp.square(y - mu), axis=-1, keepdims=True)
    return (y - mu) * lax.rsqrt(var + LN_EPS) * g + b


def _sigmoid(x):
    return 1.0 / (1.0 + jnp.exp(-x))


def _rope_chunk(x, cos, sin_signed, first_half):
    partner = jnp.where(first_half, pltpu.roll(x, 96, 1), pltpu.roll(x, 32, 1))
    return x * cos + partner * sin_signed


def _route_weights(logit):
    g = [logit(k) for k in range(N_EXPERT_GROUPS)]
    gmax = jnp.maximum(jnp.maximum(g[0], g[1]), jnp.maximum(g[2], g[3]))
    den = sum(jnp.exp(gk - gmax) for gk in g)
    g_w = 1.0 / den
    is_g = []
    taken = jnp.zeros_like(gmax)
    for k in range(N_EXPERT_GROUPS):
        hit = jnp.where(g[k] == gmax, 1.0, 0.0) * (1.0 - taken)
        is_g.append(hit)
        taken = taken + hit
    e = []
    for k in range(EXPERTS_PER_GROUP):
        col = jnp.zeros_like(gmax)
        for gi in range(N_EXPERT_GROUPS):
            col = jnp.where(is_g[gi] > 0.5, logit(N_EXPERT_GROUPS + gi * EXPERTS_PER_GROUP + k), col)
        e.append(col)
    v1 = jnp.maximum(jnp.maximum(e[0], e[1]), jnp.maximum(e[2], e[3]))
    first = []
    taken = jnp.zeros_like(v1)
    for k in range(EXPERTS_PER_GROUP):
        hit = jnp.where(e[k] == v1, 1.0, 0.0) * (1.0 - taken)
        first.append(hit)
        taken = taken + hit
    e2 = [jnp.where(first[k] > 0.5, -jnp.inf, e[k]) for k in range(EXPERTS_PER_GROUP)]
    v2 = jnp.maximum(jnp.maximum(e2[0], e2[1]), jnp.maximum(e2[2], e2[3]))
    second = []
    taken = jnp.zeros_like(v2)
    for k in range(EXPERTS_PER_GROUP):
        hit = jnp.where(e2[k] == v2, 1.0, 0.0) * (1.0 - taken)
        second.append(hit)
        taken = taken + hit
    t = jnp.exp(v2 - v1)
    w1 = 1.0 / (1.0 + t)
    w2 = t * w1
    return [is_g[gi] * (first[k] * w1 + second[k] * w2) * g_w
            for gi in range(N_EXPERT_GROUPS) for k in range(EXPERTS_PER_GROUP)]


def _route(logits):
    lane = lax.broadcasted_iota(jnp.int32, logits.shape, 1)
    comb = jnp.zeros(logits.shape, F32)
    for e, col in enumerate(_route_weights(lambda k: logits[:, k:k + 1])):
        comb = jnp.where(lane == e, col, comb)
    return comb


def _route_rows(logits_t):
    tokens = logits_t.shape[1]
    row = lax.broadcasted_iota(jnp.int32, (N_EXPERTS, tokens), 0)
    comb = jnp.zeros((N_EXPERTS, tokens), F32)
    for e, r in enumerate(_route_weights(lambda k: logits_t[k:k + 1, :])):
        comb = jnp.where(row == e, r, comb)
    return jnp.concatenate([comb, jnp.zeros((logits_t.shape[0] - N_EXPERTS, tokens), F32)], axis=0)


Q_SCALE = 1.4426950408889634 * HEAD_DIM ** -0.5


def _qkvu_kernel(x_ref, w_ref, cos_ref, sin_ref, qt_ref, k_ref, v_ref, u_ref, kb_ref, vt_ref,
                 km_ref, *, tm):
    xb = x_ref[...].astype(BF16)
    cos = cos_ref[...]
    sin = sin_ref[...]
    lane = lax.broadcasted_iota(jnp.int32, (tm, LANES), 1)
    first_half = (lane & 32) == 0
    hq = _dot(xb, w_ref[:, 0:ATTN_WIDTH].astype(BF16))
    hk = _dot(xb, w_ref[:, ATTN_WIDTH:2 * ATTN_WIDTH].astype(BF16))
    hv = _dot(xb, w_ref[:, 2 * ATTN_WIDTH:3 * ATTN_WIDTH].astype(BF16))
    for c in range(ATTN_WIDTH // LANES):
        sl = slice(c * LANES, (c + 1) * LANES)
        qc = _rope_chunk(hq[:, sl], cos, sin, first_half) * Q_SCALE
        kc = _rope_chunk(hk[:, sl], cos, sin, first_half)
        kb_ref[:, sl] = kc.astype(BF16)
        k_ref[sl, :] = kc.T
        vct = hv[:, sl].T
        v_ref[sl, :] = vct
        qct = qc.T.astype(BF16)
        for r in range(tm // MOBA_BLOCK):
            rows = slice(r * MOBA_BLOCK, (r + 1) * MOBA_BLOCK)
            km_ref[r:r + 1, sl] = jnp.sum(kc[rows], axis=0, keepdims=True) * (1.0 / MOBA_BLOCK)
            qt_ref[r, sl, :] = qct[:, rows]
            vt_ref[r, sl, :] = vct[:, rows].astype(BF16)
    u_ref[...] = _dot(xb, w_ref[:, 3 * ATTN_WIDTH:].astype(BF16))


def _qkvu(x2d, w_qkvu, cos, sin, seq, tm):
    n, d = x2d.shape
    s_tiles = seq // tm
    nblk = tm // MOBA_BLOCK
    row = lambda i: (i, 0)
    blocked = lambda i: (i, 0, 0)
    featmajor = lambda i: (i // s_tiles, 0, i % s_tiles)
    out_shape = (
        jax.ShapeDtypeStruct((n // MOBA_BLOCK, ATTN_WIDTH, MOBA_BLOCK), BF16),
        jax.ShapeDtypeStruct((n // seq, ATTN_WIDTH, seq), F32),
        jax.ShapeDtypeStruct((n // seq, ATTN_WIDTH, seq), F32),
        jax.ShapeDtypeStruct((n, POOL_WIDTH), F32),
        jax.ShapeDtypeStruct((n, ATTN_WIDTH), BF16),
        jax.ShapeDtypeStruct((n // MOBA_BLOCK, ATTN_WIDTH, MOBA_BLOCK), BF16),
        jax.ShapeDtypeStruct((n // tm, nblk, ATTN_WIDTH), F32),
    )
    return pl.pallas_call(
        functools.partial(_qkvu_kernel, tm=tm),
        out_shape=out_shape,
        grid=(n // tm,),
        in_specs=[
            pl.BlockSpec((tm, d), row),
            pl.BlockSpec((d, 4 * ATTN_WIDTH), lambda i: (0, 0)),
            pl.BlockSpec((tm, LANES), lambda i: (i % s_tiles, 0)),
            pl.BlockSpec((tm, LANES), lambda i: (i % s_tiles, 0)),
        ],
        out_specs=(
            pl.BlockSpec((nblk, ATTN_WIDTH, MOBA_BLOCK), blocked),
            pl.BlockSpec((None, ATTN_WIDTH, tm), featmajor),
            pl.BlockSpec((None, ATTN_WIDTH, tm), featmajor),
            pl.BlockSpec((tm, POOL_WIDTH), row),
            pl.BlockSpec((tm, ATTN_WIDTH), row),
            pl.BlockSpec((nblk, ATTN_WIDTH, MOBA_BLOCK), blocked),
            pl.BlockSpec((None, nblk, ATTN_WIDTH), blocked),
        ),
        compiler_params=pltpu.CompilerParams(
            dimension_semantics=("arbitrary",), vmem_limit_bytes=VMEM_LIMIT),
        name="qkvu",
    )(x2d, w_qkvu, cos, sin)


BIAS_ROWS = 128
SUM_ROWS = 16
ATTN_PAIRS = 4


def _attn_kernel(qt_ref, k_ref, vt_ref, km_ref, o_ref, qa_ref, acc_ref, sa_ref, sb_ref):
    i = pl.program_id(2)
    n_past = km_ref.shape[0]
    n_heads = 2 * ATTN_PAIRS
    feat = lax.broadcasted_iota(jnp.int32, (LANES, MOBA_BLOCK), 0)
    head0 = feat < HEAD_DIM
    blk_id = lax.broadcasted_iota(jnp.int32, (n_past, MOBA_BLOCK), 0)
    blk_f = blk_id.astype(F32)
    past = blk_id < i
    key_id = lax.broadcasted_iota(jnp.int32, (MOBA_BLOCK, MOBA_BLOCK), 0)
    qry_id = lax.broadcasted_iota(jnp.int32, (MOBA_BLOCK, MOBA_BLOCK), 1)
    causal = key_id <= qry_id
    lane = lax.broadcasted_iota(jnp.int32, (MOBA_BLOCK, LANES), 1)
    ones_rows = jnp.ones((SUM_ROWS, MOBA_BLOCK), BF16)
    bias_pad = jnp.zeros((BIAS_ROWS - n_past, MOBA_BLOCK), BF16)
    own = pl.multiple_of(i * MOBA_BLOCK, MOBA_BLOCK)

    def slab(pair):
        return slice(pair * LANES, (pair + 1) * LANES)

    def vt_aug(blk, pair):
        return jnp.concatenate([vt_ref[blk, slab(pair), :], ones_rows], axis=0)

    def consume_refill(s_ref, ms, blk, nxt):
        off = pl.multiple_of(nxt * MOBA_BLOCK, MOBA_BLOCK)
        onehot = jnp.where(lane == nxt, 1.0, 0.0).astype(BF16)
        out = []
        for h in range(n_heads):
            s = s_ref[h]
            m_new = jnp.maximum(ms[h], jnp.max(s, axis=0, keepdims=True))
            alpha = jnp.exp2(ms[h] - m_new)
            p = jnp.exp2(s - m_new)
            acc_ref[h] = alpha * acc_ref[h] + _dot(vt_aug(blk, h // 2), p.astype(BF16))
            out.append(m_new)
            ka = jnp.concatenate([k_ref[pl.ds(off, MOBA_BLOCK), slab(h // 2)], onehot], axis=1)
            s_ref[h] = _dot(ka, qa_ref[h])
        return tuple(out)

    for h in range(n_heads):
        pair = h // 2
        qt = qt_ref[slab(pair), :]
        qh = jnp.where(head0 if h % 2 == 0 else jnp.logical_not(head0), qt, jnp.zeros_like(qt))
        km = km_ref[:, slab(pair)]
        km_hi = km.astype(BF16)
        km_mid = (km - km_hi.astype(F32)).astype(BF16)
        km_lo = (km - km_hi.astype(F32) - km_mid.astype(F32)).astype(BF16)
        gate = _dot(km_hi, qh) + _dot(km_mid, qh) + _dot(km_lo, qh)
        g = jnp.where(past, gate, -jnp.inf)
        sel = jnp.zeros(g.shape, F32)
        for _ in range(MOBA_TOPK):
            mx = jnp.max(g, axis=0, keepdims=True)
            idx = jnp.min(jnp.where(g == mx, blk_f, float(n_past)), axis=0, keepdims=True)
            pick = blk_f == idx
            sel = jnp.where(pick, 1.0, sel)
            g = jnp.where(pick, -jnp.inf, g)
        bias = jnp.where(jnp.logical_and(sel > 0.5, past), 0.0, MASK_BIAS).astype(BF16)
        qa_ref[h] = jnp.concatenate([qh, bias, bias_pad], axis=0)
        kd = k_ref[pl.ds(own, MOBA_BLOCK), slab(pair)]
        sa_ref[h] = jnp.where(causal, _dot(kd, qh), -jnp.inf)
        acc_ref[h] = jnp.zeros(acc_ref.shape[1:], F32)
    m0 = jnp.full((1, MOBA_BLOCK), MASK_BIAS, F32)

    last_past = jnp.maximum(i - 1, 0)
    off0 = pl.multiple_of(0 * MOBA_BLOCK, MOBA_BLOCK)
    onehot0 = jnp.where(lane == 0, 1.0, 0.0).astype(BF16)
    for h in range(n_heads):
        ka = jnp.concatenate([k_ref[pl.ds(off0, MOBA_BLOCK), slab(h // 2)], onehot0], axis=1)
        sb_ref[h] = _dot(ka, qa_ref[h])

    def pair_body(u, ms):
        ms = consume_refill(sa_ref, ms, jnp.where(u == 0, i, 2 * u - 1),
                            jnp.minimum(2 * u + 1, last_past))
        return consume_refill(sb_ref, ms, 2 * u, jnp.minimum(2 * u + 2, last_past))

    ms = lax.fori_loop(0, (i + 1) // 2, pair_body, (m0,) * n_heads)

    @pl.when((i + 1) % 2 == 1)
    def _():
        blk = jnp.where(i == 0, i, i - 1)
        for h in range(n_heads):
            s = sa_ref[h]
            m_new = jnp.maximum(ms[h], jnp.max(s, axis=0, keepdims=True))
            alpha = jnp.exp2(ms[h] - m_new)
            p = jnp.exp2(s - m_new)
            acc_ref[h] = alpha * acc_ref[h] + _dot(vt_aug(blk, h // 2), p.astype(BF16))

    for pair in range(ATTN_PAIRS):
        a0 = acc_ref[2 * pair]
        a1 = acc_ref[2 * pair + 1]
        ot = jnp.where(head0, a0[:LANES] / a0[LANES:LANES + 1], a1[:LANES] / a1[LANES:LANES + 1])
        o_ref[:, slab(pair)] = ot.T.astype(BF16)


def _attn(qt, kb, vt, km):
    b, s, _ = kb.shape
    nb = s // MOBA_BLOCK
    width = ATTN_PAIRS * LANES
    return pl.pallas_call(
        _attn_kernel,
        out_shape=jax.ShapeDtypeStruct((b, s, ATTN_WIDTH), BF16),
        grid=(b, ATTN_WIDTH // width, nb),
        in_specs=[
            pl.BlockSpec((None, width, MOBA_BLOCK), lambda bi, p, i: (bi * nb + i, p, 0)),
            pl.BlockSpec((None, s, width), lambda bi, p, i: (bi, 0, p)),
            pl.BlockSpec((None, nb, width, MOBA_BLOCK), lambda bi, p, i: (bi, 0, p, 0)),
            pl.BlockSpec((None, nb, width), lambda bi, p, i: (bi, 0, p)),
        ],
        out_specs=pl.BlockSpec((None, MOBA_BLOCK, width), lambda bi, p, i: (bi, i, p)),
        scratch_shapes=[
            pltpu.VMEM((2 * ATTN_PAIRS, LANES + BIAS_ROWS, MOBA_BLOCK), BF16),
            pltpu.VMEM((2 * ATTN_PAIRS, LANES + SUM_ROWS, MOBA_BLOCK), F32),
            pltpu.VMEM((2 * ATTN_PAIRS, MOBA_BLOCK, MOBA_BLOCK), F32),
            pltpu.VMEM((2 * ATTN_PAIRS, MOBA_BLOCK, MOBA_BLOCK), F32),
        ],
        compiler_params=pltpu.CompilerParams(
            dimension_semantics=("arbitrary", "arbitrary", "arbitrary"),
            vmem_limit_bytes=VMEM_LIMIT),
        name="moba_attn",
    )(qt, kb, vt, km)


def _pool_group(window_sum, u_g, count, w_pool_g, scale_g, precision=None, cast=None):
    d = window_sum / count - u_g
    if cast is not None:
        d = d.astype(cast)
    return _dot(d, w_pool_g, precision) * scale_g


def _mix_kernel(x_ref, attn_ref, u_ref, halo_ref, wg_ref, bg_ref, wpool_ref, pscale_ref, wa_ref,
                wb_ref, wout_ref, g1_ref, b1_ref, x1_ref, z_ref, *, tm, alpha):
    i = pl.program_id(1)
    x = x_ref[...]
    xb = x.astype(BF16)
    u = u_ref[...]
    z_ref[0:HALO_ROWS, :] = jnp.where(i > 0, halo_ref[...], 0.0)
    z_ref[HALO_ROWS:, :] = u
    pos1 = (i * tm + 1 + lax.broadcasted_iota(jnp.int32, (tm, 1), 0)).astype(F32)
    parts = []
    for g, w in enumerate(POOL_WINDOWS):
        sl = slice(g * POOL_GROUP_WIDTH, (g + 1) * POOL_GROUP_WIDTH)
        acc = u[:, sl]
        for back in range(1, w):
            acc = acc + z_ref[HALO_ROWS - back:HALO_ROWS - back + tm, sl]
        count = jnp.minimum(float(w), pos1)
        parts.append(_pool_group(acc, u[:, sl], count, wpool_ref[g].astype(BF16), pscale_ref[:, sl],
                                 cast=BF16))
    pooled = jnp.concatenate(parts, axis=1).astype(BF16)
    d = x.shape[1]
    ga = _dot(xb, wg_ref[:, :d].astype(BF16)) + bg_ref[0:1, :]
    gb = _dot(xb, wg_ref[:, d:].astype(BF16)) + bg_ref[1:2, :]
    a = _dot(attn_ref[...], wa_ref[...].astype(BF16))
    bb = _dot(pooled, wb_ref[...].astype(BF16))
    merged = _sigmoid(ga) * a + _sigmoid(gb) * bb
    y = alpha * x + _dot(merged.astype(BF16), wout_ref[...].astype(BF16))
    x1_ref[...] = _layer_norm(y, g1_ref[...], b1_ref[...])


def _mix(x, attn, u, w_in, b_gate, w_pool, pool_scale, w_a, w_b, w_out, ln_g, ln_b, tm, alpha):
    b, s, d = x.shape
    gate_block = w_in.shape[1] // (2 * d) - 1
    assert w_in.shape[1] == (gate_block + 1) * 2 * d
    w_gates = w_in
    halo_per_tile = tm // HALO_ROWS
    tile = lambda bi, i: (bi, i, 0)
    const2 = lambda bi, i: (0, 0)
    return pl.pallas_call(
        functools.partial(_mix_kernel, tm=tm, alpha=alpha),
        out_shape=jax.ShapeDtypeStruct((b, s, d), F32),
        grid=(b, s // tm),
        in_specs=[
            pl.BlockSpec((None, tm, d), tile),
            pl.BlockSpec((None, tm, ATTN_WIDTH), tile),
            pl.BlockSpec((None, tm, POOL_WIDTH), tile),
            pl.BlockSpec((None, HALO_ROWS, POOL_WIDTH),
                         lambda bi, i: (bi, jnp.maximum(i * halo_per_tile - 1, 0), 0)),
            pl.BlockSpec((d, 2 * d), lambda bi, i: (0, gate_block)),
            pl.BlockSpec(b_gate.shape, const2),
            pl.BlockSpec(w_pool.shape, lambda bi, i: (0, 0, 0)),
            pl.BlockSpec(pool_scale.shape, const2),
            pl.BlockSpec(w_a.shape, const2),
            pl.BlockSpec(w_b.shape, const2),
            pl.BlockSpec(w_out.shape, const2),
            pl.BlockSpec(ln_g.shape, const2),
            pl.BlockSpec(ln_b.shape, const2),
        ],
        out_specs=pl.BlockSpec((None, tm, d), tile),
        scratch_shapes=[pltpu.VMEM((HALO_ROWS + tm, POOL_WIDTH), F32)],
        compiler_params=pltpu.CompilerParams(
            dimension_semantics=("arbitrary", "arbitrary"), vmem_limit_bytes=VMEM_LIMIT),
        name="mix",
    )(x, attn, u, u, w_gates, b_gate, w_pool, pool_scale, w_a, w_b, w_out, ln_g, ln_b)


def _moe_kernel(x1_ref, wr_ref, br_ref, wg_ref, wu_ref, wd_ref, g2_ref, b2_ref, y_ref,
                xb_ref, comb_ref, acc_ref, *, alpha):
    g = pl.program_id(1)
    tm = x1_ref.shape[0]
    lane = lax.broadcasted_iota(jnp.int32, (tm, LANES), 1)

    @pl.when(g == 0)
    def _():
        x1 = x1_ref[...]
        xh = x1.astype(BF16)
        xb_ref[...] = xh
        logits = _dot(xh, wr_ref[...].astype(BF16)) + br_ref[...]
        comb_ref[...] = _route_rows(logits.T).T

    xb = xb_ref[...]
    comb = comb_ref[...]
    hidden = []
    for k in range(EXPERTS_PER_GROUP):
        hg = _dot(xb, wg_ref[k].astype(BF16))
        hu = _dot(xb, wu_ref[k].astype(BF16))
        c_k = jnp.sum(jnp.where(lane == g * EXPERTS_PER_GROUP + k, comb, 0.0), axis=1, keepdims=True)
        hidden.append((hg * _sigmoid(hg) * hu * c_k).astype(BF16))
    part = _dot(jnp.concatenate(hidden, axis=1), wd_ref[...].astype(BF16))

    @pl.when(g == 0)
    def _():
        acc_ref[...] = part

    @pl.when(g > 0)
    def _():
        acc_ref[...] += part

    @pl.when(g == N_EXPERT_GROUPS - 1)
    def _():
        y = alpha * x1_ref[...] + acc_ref[...]
        y_ref[...] = _layer_norm(y, g2_ref[...], b2_ref[...])


def _moe(x1, w_r, b_r, w_g, w_u, w_d, ln_g, ln_b, tm, alpha):
    n, d = x1.shape
    f = w_g.shape[2]
    per_group = (N_EXPERT_GROUPS, EXPERTS_PER_GROUP)
    tile = lambda i, g: (i, 0)
    const2 = lambda i, g: (0, 0)
    group4 = lambda i, g: (g, 0, 0, 0)
    return pl.pallas_call(
        functools.partial(_moe_kernel, alpha=alpha),
        out_shape=jax.ShapeDtypeStruct((n, d), F32),
        grid=(n // tm, N_EXPERT_GROUPS),
        in_specs=[
            pl.BlockSpec((tm, d), tile),
            pl.BlockSpec(w_r.shape, const2),
            pl.BlockSpec(b_r.shape, const2),
            pl.BlockSpec((None, EXPERTS_PER_GROUP, d, f), group4),
            pl.BlockSpec((None, EXPERTS_PER_GROUP, d, f), group4),
            pl.BlockSpec((None, EXPERTS_PER_GROUP * f, d), lambda i, g: (g, 0, 0)),
            pl.BlockSpec(ln_g.shape, const2),
            pl.BlockSpec(ln_b.shape, const2),
        ],
        out_specs=pl.BlockSpec((tm, d), tile),
        scratch_shapes=[
            pltpu.VMEM((tm, d), BF16),
            pltpu.VMEM((tm, LANES), F32),
            pltpu.VMEM((tm, d), F32),
        ],
        compiler_params=pltpu.CompilerParams(
            dimension_semantics=("arbitrary", "arbitrary"), vmem_limit_bytes=VMEM_LIMIT),
        name="moe",
    )(x1, w_r, b_r, w_g.reshape(per_group + (d, f)), w_u.reshape(per_group + (d, f)),
      w_d.reshape(N_EXPERT_GROUPS, EXPERTS_PER_GROUP * f, d), ln_g, ln_b)


def _s_proj_kernel(x_ref, w_ref, cos_ref, sin_ref, h_ref):
    c = pl.program_id(0)
    h = _bdot(x_ref[...], w_ref[...])
    rows = h.shape[0]
    lane = lax.broadcasted_iota(jnp.int32, (rows, LANES), 1)
    first_half = (lane & 32) == 0
    rotary = c < 2
    for j in range(h.shape[1] // LANES):
        sl = slice(j * LANES, (j + 1) * LANES)
        hc = h[:, sl]
        h_ref[:, sl] = jnp.where(rotary, _rope_chunk(hc, cos_ref[...], sin_ref[...], first_half), hc)


def _s_proj(x, w_in, cos, sin):
    rows, d = x.shape
    width = w_in.shape[1]
    chunk = ATTN_WIDTH
    return pl.pallas_call(
        _s_proj_kernel,
        out_shape=jax.ShapeDtypeStruct((rows, width), F32),
        grid=(width // chunk,),
        in_specs=[
            pl.BlockSpec((rows, d), lambda c: (0, 0)),
            pl.BlockSpec((d, chunk), lambda c: (0, c)),
            pl.BlockSpec((1, LANES), lambda c: (0, 0)),
            pl.BlockSpec((1, LANES), lambda c: (0, 0)),
        ],
        out_specs=pl.BlockSpec((rows, chunk), lambda c: (0, c)),
        compiler_params=pltpu.CompilerParams(
            dimension_semantics=("arbitrary",), vmem_limit_bytes=VMEM_LIMIT),
        name="s_proj",
    )(x, w_in, cos, sin)


PAGES_PER_STEP = 64


def _column(ref, b):
    x = ref[...]
    lane = lax.broadcasted_iota(jnp.int32, x.shape, 1)
    col = jnp.sum(jnp.where(lane == b, x, 0.0), axis=1, keepdims=True)
    return col.reshape(N_HEADS, HEAD_DIM, 1)


def _s_scores_kernel(pt_ref, q_ref, *refs, page_size, n_blocks):
    page_refs = refs[:PAGES_PER_STEP]
    sc_ref, sel_ref, gs_ref = refs[PAGES_PER_STEP:]
    c = pl.program_id(1)
    pages_per_block = MOBA_BLOCK // page_size
    blocks_per_step = PAGES_PER_STEP // pages_per_block
    qb = _round_bf16(_column(q_ref, pl.program_id(0)))
    q = qb * (HEAD_DIM ** -0.5)
    for r in range(blocks_per_step):
        ksum = jnp.zeros((N_HEADS, HEAD_DIM, page_size), F32)
        for pp in range(pages_per_block):
            p = r * pages_per_block + pp
            page = page_refs[p][...]
            sc_ref[p] = jnp.sum(_round_bf16(page) * q, axis=1)
            ksum = ksum + page
        kmean = jnp.sum(ksum, axis=2, keepdims=True) * (1.0 / MOBA_BLOCK)
        gs_ref[c * blocks_per_step + r] = jnp.sum(_round_bf16(kmean) * qb, axis=1)

    @pl.when(c == pl.num_programs(1) - 1)
    def _():
        g = gs_ref[...]
        blk_f = lax.broadcasted_iota(jnp.int32, g.shape, 0).astype(F32)
        for t in range(MOBA_TOPK):
            mx = jnp.max(g, axis=0, keepdims=True)
            idx = jnp.min(jnp.where(g == mx, blk_f, float(n_blocks)), axis=0, keepdims=True)
            sel_ref[t:t + 1] = idx.astype(jnp.int32)
            g = jnp.where(blk_f == idx, -jnp.inf, g)


def _s_scores(page_table, qt, cache_kt, n_blocks):
    db, n_pages = page_table.shape
    page_size = cache_kt.shape[3]
    page_block = (None, N_HEADS, HEAD_DIM, page_size)

    def page_spec(r):
        return pl.BlockSpec(page_block, lambda b, c, pt: (pt[b, c * PAGES_PER_STEP + r], 0, 0, 0))

    return pl.pallas_call(
        functools.partial(_s_scores_kernel, page_size=page_size, n_blocks=n_blocks),
        out_shape=(jax.ShapeDtypeStruct((db, n_pages, N_HEADS, page_size), F32),
                   jax.ShapeDtypeStruct((db, MOBA_TOPK, N_HEADS, 1), jnp.int32)),
        grid_spec=pltpu.PrefetchScalarGridSpec(
            num_scalar_prefetch=1,
            grid=(db, n_pages // PAGES_PER_STEP),
            in_specs=[pl.BlockSpec(qt.shape, lambda b, c, pt: (0, 0))]
            + [page_spec(r) for r in range(PAGES_PER_STEP)],
            out_specs=(pl.BlockSpec((None, PAGES_PER_STEP, N_HEADS, page_size),
                                    lambda b, c, pt: (b, c, 0, 0)),
                       pl.BlockSpec((None, MOBA_TOPK, N_HEADS, 1), lambda b, c, pt: (b, 0, 0, 0))),
            scratch_shapes=[pltpu.VMEM((n_blocks, N_HEADS, 1), F32)],
        ),
        compiler_params=pltpu.CompilerParams(
            dimension_semantics=("arbitrary", "arbitrary"), vmem_limit_bytes=VMEM_LIMIT),
        name="s_scores",
    )(page_table, qt, *([cache_kt] * PAGES_PER_STEP))


def _s_attn_kernel(pt_ref, sel_ref, sc_ref, q_ref, kn_ref, vn_ref, cv_ref, o_ref, vbuf, sem,
                   *, pages_per_block):
    b = pl.program_id(0)
    slices_per_seq = N_HEADS * MOBA_TOPK * pages_per_block

    def slot_index(slot, h, t, r):
        return slot * slices_per_seq + (h * MOBA_TOPK + t) * pages_per_block + r

    def copies(seq, slot):
        out = []
        for h in range(N_HEADS):
            for t in range(MOBA_TOPK):
                blk = sel_ref[seq, t * N_HEADS + h]
                for r in range(pages_per_block):
                    page = pt_ref[seq, blk * pages_per_block + r]
                    out.append(pltpu.make_async_copy(
                        cv_ref.at[page, h], vbuf.at[slot_index(slot, h, t, r)], sem.at[slot]))
        return out

    @pl.when(b == 0)
    def _():
        for cp in copies(0, 0):
            cp.start()

    @pl.when(b + 1 < pl.num_programs(0))
    def _():
        for cp in copies(b + 1, (b + 1) % 2):
            cp.start()

    slot = b % 2
    for cp in copies(b, slot):
        cp.wait()

    scale = HEAD_DIM ** -0.5
    q = _round_bf16(_column(q_ref, b))
    kn = _round_bf16(_column(kn_ref, b))
    vn = _round_bf16(_column(vn_ref, b))
    for h in range(N_HEADS):
        s_self = jnp.sum(q[h] * kn[h], axis=0, keepdims=True) * scale
        scores = []
        mx = s_self
        for t in range(MOBA_TOPK):
            blk = sel_ref[b, t * N_HEADS + h]
            for r in range(pages_per_block):
                sc = sc_ref[blk * pages_per_block + r, h:h + 1, :]
                scores.append(sc)
                mx = jnp.maximum(mx, jnp.max(sc, axis=1, keepdims=True))
        p_self = jnp.exp(s_self - mx)
        ps = [jnp.exp(sc - mx) for sc in scores]
        den = p_self + jnp.sum(sum(ps), axis=1, keepdims=True)
        pv_sum = jnp.zeros(vbuf.shape[1:], F32)
        n = 0
        for t in range(MOBA_TOPK):
            for r in range(pages_per_block):
                v = _round_bf16(vbuf[slot_index(slot, h, t, r)])
                pv_sum = pv_sum + v * _round_bf16(ps[n] / den)
                n += 1
        o_ref[h] = _round_bf16(p_self / den) * vn[h] + jnp.sum(pv_sum, axis=1, keepdims=True)


def _s_attn(page_table, sel, scores, qt, knt, vnt, cache_vt):
    db, n_pages = page_table.shape
    page_size = cache_vt.shape[3]
    pages_per_block = MOBA_BLOCK // page_size
    n_slices = 2 * N_HEADS * MOBA_TOPK * pages_per_block
    per_b = pl.BlockSpec((None, N_HEADS, HEAD_DIM, 1), lambda b, pt, sl: (b, 0, 0, 0))
    whole = pl.BlockSpec(qt.shape, lambda b, pt, sl: (0, 0))
    return pl.pallas_call(
        functools.partial(_s_attn_kernel, pages_per_block=pages_per_block),
        out_shape=jax.ShapeDtypeStruct((db, N_HEADS, HEAD_DIM, 1), F32),
        grid_spec=pltpu.PrefetchScalarGridSpec(
            num_scalar_prefetch=2,
            grid=(db,),
            in_specs=[pl.BlockSpec((None, n_pages, N_HEADS, page_size),
                                   lambda b, pt, sl: (b, 0, 0, 0)),
                      whole, whole, whole, pl.BlockSpec(memory_space=pl.ANY)],
            out_specs=per_b,
            scratch_shapes=[pltpu.VMEM((n_slices, HEAD_DIM, page_size), F32),
                            pltpu.SemaphoreType.DMA((2,))],
        ),
        compiler_params=pltpu.CompilerParams(
            dimension_semantics=("arbitrary",), vmem_limit_bytes=VMEM_LIMIT),
        name="s_attn",
    )(page_table, sel, scores, qt, knt, vnt, cache_vt)


def _s_tail_kernel(x_ref, attn_ref, u_ref, ga_ref, gb_ref, sp_ref, bg_ref, wpool_ref, pscale_ref,
                   wa_ref, wb_ref, wout_ref, g1_ref, b1_ref, wr_ref, br_ref, wg_ref, wu_ref,
                   wd_ref, g2_ref, b2_ref, y_ref, x1_ref, comb_ref, acc_ref, *, alpha):
    e = pl.program_id(0)
    rows = x_ref.shape[0]
    lane = lax.broadcasted_iota(jnp.int32, (rows, LANES), 1)

    @pl.when(e == 0)
    def _():
        u = u_ref[...]
        parts = []
        for g, w in enumerate(POOL_WINDOWS):
            sl = slice(g * POOL_GROUP_WIDTH, (g + 1) * POOL_GROUP_WIDTH)
            acc = u[:, sl]
            for back in range(1, w):
                acc = acc + sp_ref[POOL_STATE - back][:, sl]
            parts.append(_pool_group(acc, u[:, sl], float(w), wpool_ref[g].astype(BF16),
                                     pscale_ref[:, sl], cast=BF16))
        pooled = jnp.concatenate(parts, axis=1)
        a = _bdot(attn_ref[...], wa_ref[...])
        bb = _bdot(pooled, wb_ref[...])
        merged = _sigmoid(ga_ref[...] + bg_ref[0:1, :]) * a + _sigmoid(gb_ref[...] + bg_ref[1:2, :]) * bb
        y = alpha * x_ref[...] + _bdot(merged, wout_ref[...])
        x1 = _layer_norm(y, g1_ref[...], b1_ref[...])
        x1_ref[...] = x1
        comb_ref[...] = _route(_bdot(x1, wr_ref[...]) + br_ref[...])
        acc_ref[...] = jnp.zeros_like(acc_ref)

    x1 = x1_ref[...]
    hg = _bdot(x1, wg_ref[...])
    hu = _bdot(x1, wu_ref[...])
    c_e = jnp.sum(jnp.where(lane == e, comb_ref[...], 0.0), axis=1, keepdims=True)
    h = hg * _sigmoid(hg) * hu * c_e
    acc_ref[...] += _bdot(h, wd_ref[...])

    @pl.when(e == N_EXPERTS - 1)
    def _():
        y_ref[...] = _layer_norm(alpha * x1_ref[...] + acc_ref[...], g2_ref[...], b2_ref[...])


def _s_tail(x, attn, u, ga, gb, sp_t, b_gate, w_pool, pool_scale, w_a, w_b, w_out, ln1_g, ln1_b,
            w_r, b_r, w_g, w_u, w_d, ln2_g, ln2_b, alpha):
    rows, d = x.shape
    f = w_g.shape[2]
    const2 = lambda e: (0, 0)
    const3 = lambda e: (0, 0, 0)
    full = lambda a: pl.BlockSpec(a.shape, const2 if a.ndim == 2 else const3)
    per_e = lambda blk: pl.BlockSpec((None,) + blk, lambda e: (e, 0, 0))
    small = [x, attn, u, ga, gb, sp_t, b_gate, w_pool, pool_scale, w_a, w_b, w_out, ln1_g, ln1_b,
             w_r, b_r]
    return pl.pallas_call(
        functools.partial(_s_tail_kernel, alpha=alpha),
        out_shape=jax.ShapeDtypeStruct((rows, d), F32),
        grid=(N_EXPERTS,),
        in_specs=[full(a) for a in small]
        + [per_e((d, f)), per_e((d, f)), per_e((f, d)), full(ln2_g), full(ln2_b)],
        out_specs=pl.BlockSpec((rows, d), const2),
        scratch_shapes=[pltpu.VMEM((rows, d), F32), pltpu.VMEM((rows, LANES), F32),
                        pltpu.VMEM((rows, d), F32)],
        compiler_params=pltpu.CompilerParams(
            dimension_semantics=("arbitrary",), vmem_limit_bytes=VMEM_LIMIT),
        name="s_tail",
    )(*small, w_g, w_u, w_d, ln2_g, ln2_b)


def _rope_tables(pos):
    inv_freq = 1.0 / (ROPE_THETA ** (jnp.arange(0, HEAD_DIM, 2, dtype=F32) / HEAD_DIM))
    ang = pos.astype(F32)[:, None] * inv_freq[None, :]
    cos = jnp.cos(ang)
    sin = jnp.sin(ang)
    cos_t = jnp.tile(cos, (1, LANES // (HEAD_DIM // 2)))
    sin_t = jnp.tile(jnp.concatenate([-sin, sin], axis=1), (1, LANES // HEAD_DIM))
    return cos_t, sin_t


def kernel(x_prompt, x_sample, cache_k, cache_v, state_pool, page_table, w_in, w_pool, pool_scale,
           w_branch_a, w_branch_b, b_gate, w_out, ln1_g, ln1_b, w_group_router, b_group_router,
           w_expert_router, b_expert_router, w_e_gate, w_e_up, w_e_down, ln2_g, ln2_b):
    depth = w_in.shape[0]
    assert depth == 1 and x_sample.shape[1] == 1
    alpha = (2 * depth) ** 0.25
    b, s, d = x_prompt.shape
    db = x_sample.shape[0]
    n_pages = page_table.shape[1]
    page_size = cache_k.shape[2]
    past_len = n_pages * page_size
    n_blocks = past_len // MOBA_BLOCK
    assert past_len % MOBA_BLOCK == 0 and s % MOBA_BLOCK == 0

    w_in0 = w_in[0]
    qkvu_cols = 3 * ATTN_WIDTH + POOL_WIDTH
    row2 = lambda a: a.reshape(1, -1)
    w_r = jnp.concatenate(
        [w_group_router[0], jnp.transpose(w_expert_router[0], (1, 0, 2)).reshape(d, N_EXPERTS),
         jnp.zeros((d, LANES - N_EXPERT_GROUPS - N_EXPERTS), F32)], axis=1)
    b_r = jnp.concatenate(
        [b_group_router[0], b_expert_router[0].reshape(-1),
         jnp.zeros((LANES - N_EXPERT_GROUPS - N_EXPERTS,), F32)]).reshape(1, LANES)

    cos_p, sin_p = _rope_tables(jnp.arange(s, dtype=jnp.int32))
    assert qkvu_cols == 4 * ATTN_WIDTH
    qt, k, v, u, kb, vt, km = _qkvu(x_prompt.reshape(b * s, d), w_in0, cos_p, sin_p, s, QKVU_TILE)
    nb = s // MOBA_BLOCK
    attn = _attn(qt, kb.reshape(b, s, ATTN_WIDTH), vt.reshape(b, nb, ATTN_WIDTH, MOBA_BLOCK),
                 km.reshape(b, nb, ATTN_WIDTH))
    u3 = u.reshape(b, s, POOL_WIDTH)
    x1 = _mix(x_prompt, attn, u3, w_in0, b_gate[0], w_pool[0], row2(pool_scale[0]), w_branch_a[0],
              w_branch_b[0], w_out[0], row2(ln1_g[0]), row2(ln1_b[0]), MIX_TILE, alpha)
    y_p = _moe(x1.reshape(b * s, d), w_r, b_r, w_e_gate[0], w_e_up[0], w_e_down[0],
               row2(ln2_g[0]), row2(ln2_b[0]), MOE_TILE, alpha)

    cos_s, sin_s = _rope_tables(jnp.full((1,), past_len, jnp.int32))
    x_s = x_sample.reshape(db, d)
    h_s = _s_proj(x_s, w_in0, cos_s, sin_s)
    q_s = h_s[:, :ATTN_WIDTH]
    k_s = h_s[:, ATTN_WIDTH:2 * ATTN_WIDTH]
    v_s = h_s[:, 2 * ATTN_WIDTH:3 * ATTN_WIDTH]
    u_s = h_s[:, 3 * ATTN_WIDTH:qkvu_cols]
    ga_s = h_s[:, qkvu_cols:qkvu_cols + d]
    gb_s = h_s[:, qkvu_cols + d:]
    cache_kt = jnp.transpose(cache_k[0], (0, 2, 3, 1))
    cache_vt = jnp.transpose(cache_v[0], (0, 2, 3, 1))
    scores, sel = _s_scores(page_table, q_s.T, cache_kt, n_blocks)
    sel2 = sel.reshape(db, MOBA_TOPK * N_HEADS)
    attn_s = _s_attn(page_table, sel2, scores, q_s.T, k_s.T, v_s.T, cache_vt).reshape(db, ATTN_WIDTH)
    sp = state_pool[0]
    y_s = _s_tail(x_s, attn_s, u_s, ga_s, gb_s, jnp.transpose(sp, (1, 0, 2)), b_gate[0], w_pool[0],
                  row2(pool_scale[0]), w_branch_a[0], w_branch_b[0], w_out[0], row2(ln1_g[0]),
                  row2(ln1_b[0]), w_r, b_r, w_e_gate[0], w_e_up[0], w_e_down[0], row2(ln2_g[0]),
                  row2(ln2_b[0]), alpha)

    heads = (N_HEADS, HEAD_DIM)
    return (
        y_p.reshape(b, s, d),
        y_s.reshape(db, 1, d),
        jnp.transpose(k.reshape((b,) + heads + (s,)), (0, 3, 1, 2))[None],
        jnp.transpose(v.reshape((b,) + heads + (s,)), (0, 3, 1, 2))[None],
        u3[:, s - POOL_STATE:, :][None],
        k_s.reshape((1, db, 1) + heads),
        v_s.reshape((1, db, 1) + heads),
        jnp.concatenate([sp[:, 1:, :], u_s[:, None, :]], axis=1)[None],
    )
```

```python
import functools

import jax
import jax.numpy as jnp
from jax import lax
from jax.experimental import pallas as pl
from jax.experimental.pallas import tpu as pltpu

F32 = jnp.float32
BF16 = jnp.bfloat16

N_HEADS = 8
HEAD_DIM = 64
ATTN_WIDTH = N_HEADS * HEAD_DIM
MOBA_BLOCK = 256
MOBA_TOPK = 3
ROPE_THETA = 10000.0
POOL_WINDOWS = (2, 4, 8, 16)
POOL_GROUP_WIDTH = 128
POOL_WIDTH = 512
POOL_STATE = 15
N_EXPERT_GROUPS = 4
EXPERTS_PER_GROUP = 4
N_EXPERTS = 16
LN_EPS = 1e-5
LANES = 128
HALO_ROWS = 16
MASK_BIAS = -1e30
VMEM_LIMIT = 56 * 1024 * 1024
QKVU_TILE = 1024
MIX_TILE = 512
MOE_TILE = 1024


def _dot(a, b, precision=None):
    return jnp.dot(a, b, preferred_element_type=F32, precision=precision)


def _round_bf16(x):
    return x.astype(BF16).astype(F32)


def _bdot(a, b):
    return _dot(a.astype(BF16), b.astype(BF16))


def _layer_norm(y, g, b):
    mu = jnp.mean(y, axis=-1, keepdims=True)
    var = jnp.mean(jnp.square(y - mu), axis=-1, keepdims=True)
    return (y - mu) * lax.rsqrt(var + LN_EPS) * g + b


def _sigmoid(x):
    return 1.0 / (1.0 + jnp.exp(-x))


def _rope_chunk(x, cos, sin_signed, first_half):
    partner = jnp.where(first_half, pltpu.roll(x, 96, 1), pltpu.roll(x, 32, 1))
    return x * cos + partner * sin_signed


def _route_weights(logit):
    g = [logit(k) for k in range(N_EXPERT_GROUPS)]
    gmax = jnp.maximum(jnp.maximum(g[0], g[1]), jnp.maximum(g[2], g[3]))
    den = sum(jnp.exp(gk - gmax) for gk in g)
    g_w = 1.0 / den
    is_g = []
    taken = jnp.zeros_like(gmax)
    for k in range(N_EXPERT_GROUPS):
        hit = jnp.where(g[k] == gmax, 1.0, 0.0) * (1.0 - taken)
        is_g.append(hit)
        taken = taken + hit
    e = []
    for k in range(EXPERTS_PER_GROUP):
        col = jnp.zeros_like(gmax)
        for gi in range(N_EXPERT_GROUPS):
            col = jnp.where(is_g[gi] > 0.5, logit(N_EXPERT_GROUPS + gi * EXPERTS_PER_GROUP + k), col)
        e.append(col)
    v1 = jnp.maximum(jnp.maximum(e[0], e[1]), jnp.maximum(e[2], e[3]))
    first = []
    taken = jnp.zeros_like(v1)
    for k in range(EXPERTS_PER_GROUP):
        hit = jnp.where(e[k] == v1, 1.0, 0.0) * (1.0 - taken)
        first.append(hit)
        taken = taken + hit
    e2 = [jnp.where(first[k] > 0.5, -jnp.inf, e[k]) for k in range(EXPERTS_PER_GROUP)]
    v2 = jnp.maximum(jnp.maximum(e2[0], e2[1]), jnp.maximum(e2[2], e2[3]))
    second = []
    taken = jnp.zeros_like(v2)
    for k in range(EXPERTS_PER_GROUP):
        hit = jnp.where(e2[k] == v2, 1.0, 0.0) * (1.0 - taken)
        second.append(hit)
        taken = taken + hit
    t = jnp.exp(v2 - v1)
    w1 = 1.0 / (1.0 + t)
    w2 = t * w1
    return [is_g[gi] * (first[k] * w1 + second[k] * w2) * g_w
            for gi in range(N_EXPERT_GROUPS) for k in range(EXPERTS_PER_GROUP)]


def _route(logits):
    lane = lax.broadcasted_iota(jnp.int32, logits.shape, 1)
    comb = jnp.zeros(logits.shape, F32)
    for e, col in enumerate(_route_weights(lambda k: logits[:, k:k + 1])):
        comb = jnp.where(lane == e, col, comb)
    return comb


def _route_rows(logits_t):
    tokens = logits_t.shape[1]
    row = lax.broadcasted_iota(jnp.int32, (N_EXPERTS, tokens), 0)
    comb = jnp.zeros((N_EXPERTS, tokens), F32)
    for e, r in enumerate(_route_weights(lambda k: logits_t[k:k + 1, :])):
        comb = jnp.where(row == e, r, comb)
    return jnp.concatenate([comb, jnp.zeros((logits_t.shape[0] - N_EXPERTS, tokens), F32)], axis=0)


Q_SCALE = 1.4426950408889634 * HEAD_DIM ** -0.5


def _qkvu_kernel(x_ref, w_ref, cos_ref, sin_ref, qt_ref, k_ref, v_ref, u_ref, kb_ref, vt_ref,
                 km_ref, *, tm):
    xb = x_ref[...].astype(BF16)
    cos = cos_ref[...]
    sin = sin_ref[...]
    lane = lax.broadcasted_iota(jnp.int32, (tm, LANES), 1)
    first_half = (lane & 32) == 0
    hq = _dot(xb, w_ref[:, 0:ATTN_WIDTH].astype(BF16))
    hk = _dot(xb, w_ref[:, ATTN_WIDTH:2 * ATTN_WIDTH].astype(BF16))
    hv = _dot(xb, w_ref[:, 2 * ATTN_WIDTH:3 * ATTN_WIDTH].astype(BF16))
    for c in range(ATTN_WIDTH // LANES):
        sl = slice(c * LANES, (c + 1) * LANES)
        qc = _rope_chunk(hq[:, sl], cos, sin, first_half) * Q_SCALE
        kc = _rope_chunk(hk[:, sl], cos, sin, first_half)
        kb_ref[:, sl] = kc.astype(BF16)
        k_ref[sl, :] = kc.T
        vct = hv[:, sl].T
        v_ref[sl, :] = vct
        qct = qc.T.astype(BF16)
        for r in range(tm // MOBA_BLOCK):
            rows = slice(r * MOBA_BLOCK, (r + 1) * MOBA_BLOCK)
            km_ref[r:r + 1, sl] = jnp.sum(kc[rows], axis=0, keepdims=True) * (1.0 / MOBA_BLOCK)
            qt_ref[r, sl, :] = qct[:, rows]
            vt_ref[r, sl, :] = vct[:, rows].astype(BF16)
    u_ref[...] = _dot(xb, w_ref[:, 3 * ATTN_WIDTH:].astype(BF16))


def _qkvu(x2d, w_qkvu, cos, sin, seq, tm):
    n, d = x2d.shape
    s_tiles = seq // tm
    nblk = tm // MOBA_BLOCK
    row = lambda i: (i, 0)
    blocked = lambda i: (i, 0, 0)
    featmajor = lambda i: (i // s_tiles, 0, i % s_tiles)
    out_shape = (
        jax.ShapeDtypeStruct((n // MOBA_BLOCK, ATTN_WIDTH, MOBA_BLOCK), BF16),
        jax.ShapeDtypeStruct((n // seq, ATTN_WIDTH, seq), F32),
        jax.ShapeDtypeStruct((n // seq, ATTN_WIDTH, seq), F32),
        jax.ShapeDtypeStruct((n, POOL_WIDTH), F32),
        jax.ShapeDtypeStruct((n, ATTN_WIDTH), BF16),
        jax.ShapeDtypeStruct((n // MOBA_BLOCK, ATTN_WIDTH, MOBA_BLOCK), BF16),
        jax.ShapeDtypeStruct((n // tm, nblk, ATTN_WIDTH), F32),
    )
    return pl.pallas_call(
        functools.partial(_qkvu_kernel, tm=tm),
        out_shape=out_shape,
        grid=(n // tm,),
        in_specs=[
            pl.BlockSpec((tm, d), row),
            pl.BlockSpec((d, 4 * ATTN_WIDTH), lambda i: (0, 0)),
            pl.BlockSpec((tm, LANES), lambda i: (i % s_tiles, 0)),
            pl.BlockSpec((tm, LANES), lambda i: (i % s_tiles, 0)),
        ],
        out_specs=(
            pl.BlockSpec((nblk, ATTN_WIDTH, MOBA_BLOCK), blocked),
            pl.BlockSpec((None, ATTN_WIDTH, tm), featmajor),
            pl.BlockSpec((None, ATTN_WIDTH, tm), featmajor),
            pl.BlockSpec((tm, POOL_WIDTH), row),
            pl.BlockSpec((tm, ATTN_WIDTH), row),
            pl.BlockSpec((nblk, ATTN_WIDTH, MOBA_BLOCK), blocked),
            pl.BlockSpec((None, nblk, ATTN_WIDTH), blocked),
        ),
        compiler_params=pltpu.CompilerParams(
            dimension_semantics=("arbitrary",), vmem_limit_bytes=VMEM_LIMIT),
        name="qkvu",
    )(x2d, w_qkvu, cos, sin)


BIAS_ROWS = 128
SUM_ROWS = 16
ATTN_PAIRS = 4


def _attn_kernel(qt_ref, k_ref, vt_ref, km_ref, o_ref, qa_ref, acc_ref, sa_ref, sb_ref):
    i = pl.program_id(2)
    n_past = km_ref.shape[0]
    n_heads = 2 * ATTN_PAIRS
    feat = lax.broadcasted_iota(jnp.int32, (LANES, MOBA_BLOCK), 0)
    head0 = feat < HEAD_DIM
    blk_id = lax.broadcasted_iota(jnp.int32, (n_past, MOBA_BLOCK), 0)
    blk_f = blk_id.astype(F32)
    past = blk_id < i
    key_id = lax.broadcasted_iota(jnp.int32, (MOBA_BLOCK, MOBA_BLOCK), 0)
    qry_id = lax.broadcasted_iota(jnp.int32, (MOBA_BLOCK, MOBA_BLOCK), 1)
    causal = key_id <= qry_id
    lane = lax.broadcasted_iota(jnp.int32, (MOBA_BLOCK, LANES), 1)
    ones_rows = jnp.ones((SUM_ROWS, MOBA_BLOCK), BF16)
    bias_pad = jnp.zeros((BIAS_ROWS - n_past, MOBA_BLOCK), BF16)
    own = pl.multiple_of(i * MOBA_BLOCK, MOBA_BLOCK)

    def slab(pair):
        return slice(pair * LANES, (pair + 1) * LANES)

    def vt_aug(blk, pair):
        return jnp.concatenate([vt_ref[blk, slab(pair), :], ones_rows], axis=0)

    def consume_refill(s_ref, ms, blk, nxt):
        off = pl.multiple_of(nxt * MOBA_BLOCK, MOBA_BLOCK)
        onehot = jnp.where(lane == nxt, 1.0, 0.0).astype(BF16)
        out = []
        for h in range(n_heads):
            s = s_ref[h]
            m_new = jnp.maximum(ms[h], jnp.max(s, axis=0, keepdims=True))
            alpha = jnp.exp2(ms[h] - m_new)
            p = jnp.exp2(s - m_new)
            acc_ref[h] = alpha * acc_ref[h] + _dot(vt_aug(blk, h // 2), p.astype(BF16))
            out.append(m_new)
            ka = jnp.concatenate([k_ref[pl.ds(off, MOBA_BLOCK), slab(h // 2)], onehot], axis=1)
            s_ref[h] = _dot(ka, qa_ref[h])
        return tuple(out)

    for h in range(n_heads):
        pair = h // 2
        qt = qt_ref[slab(pair), :]
        qh = jnp.where(head0 if h % 2 == 0 else jnp.logical_not(head0), qt, jnp.zeros_like(qt))
        km = km_ref[:, slab(pair)]
        km_hi = km.astype(BF16)
        km_mid = (km - km_hi.astype(F32)).astype(BF16)
        km_lo = (km - km_hi.astype(F32) - km_mid.astype(F32)).astype(BF16)
        gate = _dot(km_hi, qh) + _dot(km_mid, qh) + _dot(km_lo, qh)
        g = jnp.where(past, gate, -jnp.inf)
        sel = jnp.zeros(g.shape, F32)
        for _ in range(MOBA_TOPK):
            mx = jnp.max(g, axis=0, keepdims=True)
            idx = jnp.min(jnp.where(g == mx, blk_f, float(n_past)), axis=0, keepdims=True)
            pick = blk_f == idx
            sel = jnp.where(pick, 1.0, sel)
            g = jnp.where(pick, -jnp.inf, g)
        bias = jnp.where(jnp.logical_and(sel > 0.5, past), 0.0, MASK_BIAS).astype(BF16)
        qa_ref[h] = jnp.concatenate([qh, bias, bias_pad], axis=0)
        kd = k_ref[pl.ds(own, MOBA_BLOCK), slab(pair)]
        sa_ref[h] = jnp.where(causal, _dot(kd, qh), -jnp.inf)
        acc_ref[h] = jnp.zeros(acc_ref.shape[1:], F32)
    m0 = jnp.full((1, MOBA_BLOCK), MASK_BIAS, F32)

    last_past = jnp.maximum(i - 1, 0)
    off0 = pl.multiple_of(0 * MOBA_BLOCK, MOBA_BLOCK)
    onehot0 = jnp.where(lane == 0, 1.0, 0.0).astype(BF16)
    for h in range(n_heads):
        ka = jnp.concatenate([k_ref[pl.ds(off0, MOBA_BLOCK), slab(h // 2)], onehot0], axis=1)
        sb_ref[h] = _dot(ka, qa_ref[h])

    def pair_body(u, ms):
        ms = consume_refill(sa_ref, ms, jnp.where(u == 0, i, 2 * u - 1),
                            jnp.minimum(2 * u + 1, last_past))
        return consume_refill(sb_ref, ms, 2 * u, jnp.minimum(2 * u + 2, last_past))

    ms = lax.fori_loop(0, (i + 1) // 2, pair_body, (m0,) * n_heads)

    @pl.when((i + 1) % 2 == 1)
    def _():
        blk = jnp.where(i == 0, i, i - 1)
        for h in range(n_heads):
            s = sa_ref[h]
            m_new = jnp.maximum(ms[h], jnp.max(s, axis=0, keepdims=True))
            alpha = jnp.exp2(ms[h] - m_new)
            p = jnp.exp2(s - m_new)
            acc_ref[h] = alpha * acc_ref[h] + _dot(vt_aug(blk, h // 2), p.astype(BF16))

    for pair in range(ATTN_PAIRS):
        a0 = acc_ref[2 * pair]
        a1 = acc_ref[2 * pair + 1]
        ot = jnp.where(head0, a0[:LANES] / a0[LANES:LANES + 1], a1[:LANES] / a1[LANES:LANES + 1])
        o_ref[:, slab(pair)] = ot.T.astype(BF16)


def _attn(qt, kb, vt, km):
    b, s, _ = kb.shape
    nb = s // MOBA_BLOCK
    width = ATTN_PAIRS * LANES
    return pl.pallas_call(
        _attn_kernel,
        out_shape=jax.ShapeDtypeStruct((b, s, ATTN_WIDTH), BF16),
        grid=(b, ATTN_WIDTH // width, nb),
        in_specs=[
            pl.BlockSpec((None, width, MOBA_BLOCK), lambda bi, p, i: (bi * nb + i, p, 0)),
            pl.BlockSpec((None, s, width), lambda bi, p, i: (bi, 0, p)),
            pl.BlockSpec((None, nb, width, MOBA_BLOCK), lambda bi, p, i: (bi, 0, p, 0)),
            pl.BlockSpec((None, nb, width), lambda bi, p, i: (bi, 0, p)),
        ],
        out_specs=pl.BlockSpec((None, MOBA_BLOCK, width), lambda bi, p, i: (bi, i, p)),
        scratch_shapes=[
            pltpu.VMEM((2 * ATTN_PAIRS, LANES + BIAS_ROWS, MOBA_BLOCK), BF16),
            pltpu.VMEM((2 * ATTN_PAIRS, LANES + SUM_ROWS, MOBA_BLOCK), F32),
            pltpu.VMEM((2 * ATTN_PAIRS, MOBA_BLOCK, MOBA_BLOCK), F32),
            pltpu.VMEM((2 * ATTN_PAIRS, MOBA_BLOCK, MOBA_BLOCK), F32),
        ],
        compiler_params=pltpu.CompilerParams(
            dimension_semantics=("arbitrary", "arbitrary", "arbitrary"),
            vmem_limit_bytes=VMEM_LIMIT),
        name="moba_attn",
    )(qt, kb, vt, km)


def _pool_group(window_sum, u_g, count, w_pool_g, scale_g, precision=None, cast=None):
    d = window_sum / count - u_g
    if cast is not None:
        d = d.astype(cast)
    return _dot(d, w_pool_g, precision) * scale_g


def _mix_kernel(x_ref, attn_ref, u_ref, halo_ref, wg_ref, bg_ref, wpool_ref, pscale_ref, wa_ref,
                wb_ref, wout_ref, g1_ref, b1_ref, x1_ref, z_ref, *, tm, alpha):
    i = pl.program_id(1)
    x = x_ref[...]
    xb = x.astype(BF16)
    u = u_ref[...]
    z_ref[0:HALO_ROWS, :] = jnp.where(i > 0, halo_ref[...], 0.0)
    z_ref[HALO_ROWS:, :] = u
    pos1 = (i * tm + 1 + lax.broadcasted_iota(jnp.int32, (tm, 1), 0)).astype(F32)
    parts = []
    for g, w in enumerate(POOL_WINDOWS):
        sl = slice(g * POOL_GROUP_WIDTH, (g + 1) * POOL_GROUP_WIDTH)
        acc = u[:, sl]
        for back in range(1, w):
            acc = acc + z_ref[HALO_ROWS - back:HALO_ROWS - back + tm, sl]
        count = jnp.minimum(float(w), pos1)
        parts.append(_pool_group(acc, u[:, sl], count, wpool_ref[g].astype(BF16), pscale_ref[:, sl],
                                 cast=BF16))
    pooled = jnp.concatenate(parts, axis=1).astype(BF16)
    d = x.shape[1]
    ga = _dot(xb, wg_ref[:, :d].astype(BF16)) + bg_ref[0:1, :]
    gb = _dot(xb, wg_ref[:, d:].astype(BF16)) + bg_ref[1:2, :]
    a = _dot(attn_ref[...], wa_ref[...].astype(BF16))
    bb = _dot(pooled, wb_ref[...].astype(BF16))
    merged = _sigmoid(ga) * a + _sigmoid(gb) * bb
    y = alpha * x + _dot(merged.astype(BF16), wout_ref[...].astype(BF16))
    x1_ref[...] = _layer_norm(y, g1_ref[...], b1_ref[...])


def _mix(x, attn, u, w_in, b_gate, w_pool, pool_scale, w_a, w_b, w_out, ln_g, ln_b, tm, alpha):
    b, s, d = x.shape
    gate_block = w_in.shape[1] // (2 * d) - 1
    assert w_in.shape[1] == (gate_block + 1) * 2 * d
    w_gates = w_in
    halo_per_tile = tm // HALO_ROWS
    tile = lambda bi, i: (bi, i, 0)
    const2 = lambda bi, i: (0, 0)
    return pl.pallas_call(
        functools.partial(_mix_kernel, tm=tm, alpha=alpha),
        out_shape=jax.ShapeDtypeStruct((b, s, d), F32),
        grid=(b, s // tm),
        in_specs=[
            pl.BlockSpec((None, tm, d), tile),
            pl.BlockSpec((None, tm, ATTN_WIDTH), tile),
            pl.BlockSpec((None, tm, POOL_WIDTH), tile),
            pl.BlockSpec((None, HALO_ROWS, POOL_WIDTH),
                         lambda bi, i: (bi, jnp.maximum(i * halo_per_tile - 1, 0), 0)),
            pl.BlockSpec((d, 2 * d), lambda bi, i: (0, gate_block)),
            pl.BlockSpec(b_gate.shape, const2),
            pl.BlockSpec(w_pool.shape, lambda bi, i: (0, 0, 0)),
            pl.BlockSpec(pool_scale.shape, const2),
            pl.BlockSpec(w_a.shape, const2),
            pl.BlockSpec(w_b.shape, const2),
            pl.BlockSpec(w_out.shape, const2),
            pl.BlockSpec(ln_g.shape, const2),
            pl.BlockSpec(ln_b.shape, const2),
        ],
        out_specs=pl.BlockSpec((None, tm, d), tile),
        scratch_shapes=[pltpu.VMEM((HALO_ROWS + tm, POOL_WIDTH), F32)],
        compiler_params=pltpu.CompilerParams(
            dimension_semantics=("arbitrary", "arbitrary"), vmem_limit_bytes=VMEM_LIMIT),
        name="mix",
    )(x, attn, u, u, w_gates, b_gate, w_pool, pool_scale, w_a, w_b, w_out, ln_g, ln_b)


def _moe_kernel(x1_ref, wr_ref, br_ref, wg_ref, wu_ref, wd_ref, g2_ref, b2_ref, y_ref,
                xb_ref, comb_ref, acc_ref, *, alpha):
    g = pl.program_id(1)
    tm = x1_ref.shape[0]
    lane = lax.broadcasted_iota(jnp.int32, (tm, LANES), 1)

    @pl.when(g == 0)
    def _():
        x1 = x1_ref[...]
        xh = x1.astype(BF16)
        xb_ref[...] = xh
        logits = _dot(xh, wr_ref[...].astype(BF16)) + br_ref[...]
        comb_ref[...] = _route_rows(logits.T).T

    xb = xb_ref[...]
    comb = comb_ref[...]
    hidden = []
    for k in range(EXPERTS_PER_GROUP):
        hg = _dot(xb, wg_ref[k].astype(BF16))
        hu = _dot(xb, wu_ref[k].astype(BF16))
        c_k = jnp.sum(jnp.where(lane == g * EXPERTS_PER_GROUP + k, comb, 0.0), axis=1, keepdims=True)
        hidden.append((hg * _sigmoid(hg) * hu * c_k).astype(BF16))
    part = _dot(jnp.concatenate(hidden, axis=1), wd_ref[...].astype(BF16))

    @pl.when(g == 0)
    def _():
        acc_ref[...] = part

    @pl.when(g > 0)
    def _():
        acc_ref[...] += part

    @pl.when(g == N_EXPERT_GROUPS - 1)
    def _():
        y = alpha * x1_ref[...] + acc_ref[...]
        y_ref[...] = _layer_norm(y, g2_ref[...], b2_ref[...])


def _moe(x1, w_r, b_r, w_g, w_u, w_d, ln_g, ln_b, tm, alpha):
    n, d = x1.shape
    f = w_g.shape[2]
    per_group = (N_EXPERT_GROUPS, EXPERTS_PER_GROUP)
    tile = lambda i, g: (i, 0)
    const2 = lambda i, g: (0, 0)
    group4 = lambda i, g: (g, 0, 0, 0)
    return pl.pallas_call(
        functools.partial(_moe_kernel, alpha=alpha),
        out_shape=jax.ShapeDtypeStruct((n, d), F32),
        grid=(n // tm, N_EXPERT_GROUPS),
        in_specs=[
            pl.BlockSpec((tm, d), tile),
            pl.BlockSpec(w_r.shape, const2),
            pl.BlockSpec(b_r.shape, const2),
            pl.BlockSpec((None, EXPERTS_PER_GROUP, d, f), group4),
            pl.BlockSpec((None, EXPERTS_PER_GROUP, d, f), group4),
            pl.BlockSpec((None, EXPERTS_PER_GROUP * f, d), lambda i, g: (g, 0, 0)),
            pl.BlockSpec(ln_g.shape, const2),
            pl.BlockSpec(ln_b.shape, const2),
        ],
        out_specs=pl.BlockSpec((tm, d), tile),
        scratch_shapes=[
            pltpu.VMEM((tm, d), BF16),
            pltpu.VMEM((tm, LANES), F32),
            pltpu.VMEM((tm, d), F32),
        ],
        compiler_params=pltpu.CompilerParams(
            dimension_semantics=("arbitrary", "arbitrary"), vmem_limit_bytes=VMEM_LIMIT),
        name="moe",
    )(x1, w_r, b_r, w_g.reshape(per_group + (d, f)), w_u.reshape(per_group + (d, f)),
      w_d.reshape(N_EXPERT_GROUPS, EXPERTS_PER_GROUP * f, d), ln_g, ln_b)


def _s_proj_kernel(x_ref, w_ref, cos_ref, sin_ref, h_ref):
    c = pl.program_id(0)
    h = _bdot(x_ref[...], w_ref[...])
    rows = h.shape[0]
    lane = lax.broadcasted_iota(jnp.int32, (rows, LANES), 1)
    first_half = (lane & 32) == 0
    rotary = c < 2
    for j in range(h.shape[1] // LANES):
        sl = slice(j * LANES, (j + 1) * LANES)
        hc = h[:, sl]
        h_ref[:, sl] = jnp.where(rotary, _rope_chunk(hc, cos_ref[...], sin_ref[...], first_half), hc)


def _s_proj(x, w_in, cos, sin):
    rows, d = x.shape
    width = w_in.shape[1]
    chunk = ATTN_WIDTH
    return pl.pallas_call(
        _s_proj_kernel,
        out_shape=jax.ShapeDtypeStruct((rows, width), F32),
        grid=(width // chunk,),
        in_specs=[
            pl.BlockSpec((rows, d), lambda c: (0, 0)),
            pl.BlockSpec((d, chunk), lambda c: (0, c)),
            pl.BlockSpec((1, LANES), lambda c: (0, 0)),
            pl.BlockSpec((1, LANES), lambda c: (0, 0)),
        ],
        out_specs=pl.BlockSpec((rows, chunk), lambda c: (0, c)),
        compiler_params=pltpu.CompilerParams(
            dimension_semantics=("arbitrary",), vmem_limit_bytes=VMEM_LIMIT),
        name="s_proj",
    )(x, w_in, cos, sin)


PAGES_PER_STEP = 64


def _column(ref, b):
    x = ref[...]
    lane = lax.broadcasted_iota(jnp.int32, x.shape, 1)
    col = jnp.sum(jnp.where(lane == b, x, 0.0), axis=1, keepdims=True)
    return col.reshape(N_HEADS, HEAD_DIM, 1)


def _s_scores_kernel(pt_ref, q_ref, *refs, page_size, n_blocks):
    page_refs = refs[:PAGES_PER_STEP]
    sc_ref, sel_ref, gs_ref = refs[PAGES_PER_STEP:]
    c = pl.program_id(1)
    pages_per_block = MOBA_BLOCK // page_size
    blocks_per_step = PAGES_PER_STEP // pages_per_block
    qb = _round_bf16(_column(q_ref, pl.program_id(0)))
    q = qb * (HEAD_DIM ** -0.5)
    for r in range(blocks_per_step):
        ksum = jnp.zeros((N_HEADS, HEAD_DIM, page_size), F32)
        for pp in range(pages_per_block):
            p = r * pages_per_block + pp
            page = page_refs[p][...]
            sc_ref[p] = jnp.sum(_round_bf16(page) * q, axis=1)
            ksum = ksum + page
        kmean = jnp.sum(ksum, axis=2, keepdims=True) * (1.0 / MOBA_BLOCK)
        gs_ref[c * blocks_per_step + r] = jnp.sum(_round_bf16(kmean) * qb, axis=1)

    @pl.when(c == pl.num_programs(1) - 1)
    def _():
        g = gs_ref[...]
        blk_f = lax.broadcasted_iota(jnp.int32, g.shape, 0).astype(F32)
        for t in range(MOBA_TOPK):
            mx = jnp.max(g, axis=0, keepdims=True)
            idx = jnp.min(jnp.where(g == mx, blk_f, float(n_blocks)), axis=0, keepdims=True)
            sel_ref[t:t + 1] = idx.astype(jnp.int32)
            g = jnp.where(blk_f == idx, -jnp.inf, g)


def _s_scores(page_table, qt, cache_kt, n_blocks):
    db, n_pages = page_table.shape
    page_size = cache_kt.shape[3]
    page_block = (None, N_HEADS, HEAD_DIM, page_size)

    def page_spec(r):
        return pl.BlockSpec(page_block, lambda b, c, pt: (pt[b, c * PAGES_PER_STEP + r], 0, 0, 0))

    return pl.pallas_call(
        functools.partial(_s_scores_kernel, page_size=page_size, n_blocks=n_blocks),
        out_shape=(jax.ShapeDtypeStruct((db, n_pages, N_HEADS, page_size), F32),
                   jax.ShapeDtypeStruct((db, MOBA_TOPK, N_HEADS, 1), jnp.int32)),
        grid_spec=pltpu.PrefetchScalarGridSpec(
            num_scalar_prefetch=1,
            grid=(db, n_pages // PAGES_PER_STEP),
            in_specs=[pl.BlockSpec(qt.shape, lambda b, c, pt: (0, 0))]
            + [page_spec(r) for r in range(PAGES_PER_STEP)],
            out_specs=(pl.BlockSpec((None, PAGES_PER_STEP, N_HEADS, page_size),
                                    lambda b, c, pt: (b, c, 0, 0)),
                       pl.BlockSpec((None, MOBA_TOPK, N_HEADS, 1), lambda b, c, pt: (b, 0, 0, 0))),
            scratch_shapes=[pltpu.VMEM((n_blocks, N_HEADS, 1), F32)],
        ),
        compiler_params=pltpu.CompilerParams(
            dimension_semantics=("arbitrary", "arbitrary"), vmem_limit_bytes=VMEM_LIMIT),
        name="s_scores",
    )(page_table, qt, *([cache_kt] * PAGES_PER_STEP))


def _s_attn_kernel(pt_ref, sel_ref, sc_ref, q_ref, kn_ref, vn_ref, cv_ref, o_ref, vbuf, sem,
                   *, pages_per_block):
    b = pl.program_id(0)
    slices_per_seq = N_HEADS * MOBA_TOPK * pages_per_block

    def slot_index(slot, h, t, r):
        return slot * slices_per_seq + (h * MOBA_TOPK + t) * pages_per_block + r

    def copies(seq, slot):
        out = []
        for h in range(N_HEADS):
            for t in range(MOBA_TOPK):
                blk = sel_ref[seq, t * N_HEADS + h]
                for r in range(pages_per_block):
                    page = pt_ref[seq, blk * pages_per_block + r]
                    out.append(pltpu.make_async_copy(
                        cv_ref.at[page, h], vbuf.at[slot_index(slot, h, t, r)], sem.at[slot]))
        return out

    @pl.when(b == 0)
    def _():
        for n, cp in enumerate(copies(0, 0)):
            cp.start(priority=n % 2)

    @pl.when(b + 1 < pl.num_programs(0))
    def _():
        for n, cp in enumerate(copies(b + 1, (b + 1) % 2)):
            cp.start(priority=n % 2)

    slot = b % 2
    for cp in copies(b, slot):
        cp.wait()

    scale = HEAD_DIM ** -0.5
    q = _round_bf16(_column(q_ref, b))
    kn = _round_bf16(_column(kn_ref, b))
    vn = _round_bf16(_column(vn_ref, b))
    for h in range(N_HEADS):
        s_self = jnp.sum(q[h] * kn[h], axis=0, keepdims=True) * scale
        scores = []
        mx = s_self
        for t in range(MOBA_TOPK):
            blk = sel_ref[b, t * N_HEADS + h]
            for r in range(pages_per_block):
                sc = sc_ref[blk * pages_per_block + r, h:h + 1, :]
                scores.append(sc)
                mx = jnp.maximum(mx, jnp.max(sc, axis=1, keepdims=True))
        p_self = jnp.exp(s_self - mx)
        ps = [jnp.exp(sc - mx) for sc in scores]
        den = p_self + jnp.sum(sum(ps), axis=1, keepdims=True)
        pv_sum = jnp.zeros(vbuf.shape[1:], F32)
        n = 0
        for t in range(MOBA_TOPK):
            for r in range(pages_per_block):
                v = _round_bf16(vbuf[slot_index(slot, h, t, r)])
                pv_sum = pv_sum + v * _round_bf16(ps[n] / den)
                n += 1
        o_ref[h] = _round_bf16(p_self / den) * vn[h] + jnp.sum(pv_sum, axis=1, keepdims=True)


def _s_attn(page_table, sel, scores, qt, knt, vnt, cache_vt):
    db, n_pages = page_table.shape
    page_size = cache_vt.shape[3]
    pages_per_block = MOBA_BLOCK // page_size
    n_slices = 2 * N_HEADS * MOBA_TOPK * pages_per_block
    per_b = pl.BlockSpec((None, N_HEADS, HEAD_DIM, 1), lambda b, pt, sl: (b, 0, 0, 0))
    whole = pl.BlockSpec(qt.shape, lambda b, pt, sl: (0, 0))
    return pl.pallas_call(
        functools.partial(_s_attn_kernel, pages_per_block=pages_per_block),
        out_shape=jax.ShapeDtypeStruct((db, N_HEADS, HEAD_DIM, 1), F32),
        grid_spec=pltpu.PrefetchScalarGridSpec(
            num_scalar_prefetch=2,
            grid=(db,),
            in_specs=[pl.BlockSpec((None, n_pages, N_HEADS, page_size),
                                   lambda b, pt, sl: (b, 0, 0, 0)),
                      whole, whole, whole, pl.BlockSpec(memory_space=pl.ANY)],
            out_specs=per_b,
            scratch_shapes=[pltpu.VMEM((n_slices, HEAD_DIM, page_size), F32),
                            pltpu.SemaphoreType.DMA((2,))],
        ),
        compiler_params=pltpu.CompilerParams(
            dimension_semantics=("arbitrary",), vmem_limit_bytes=VMEM_LIMIT),
        name="s_attn",
    )(page_table, sel, scores, qt, knt, vnt, cache_vt)


def _s_tail_kernel(x_ref, attn_ref, u_ref, ga_ref, gb_ref, sp_ref, bg_ref, wpool_ref, pscale_ref,
                   wa_ref, wb_ref, wout_ref, g1_ref, b1_ref, wr_ref, br_ref, wg_ref, wu_ref,
                   wd_ref, g2_ref, b2_ref, y_ref, x1_ref, comb_ref, acc_ref, *, alpha):
    e = pl.program_id(0)
    rows = x_ref.shape[0]
    lane = lax.broadcasted_iota(jnp.int32, (rows, LANES), 1)

    @pl.when(e == 0)
    def _():
        u = u_ref[...]
        parts = []
        for g, w in enumerate(POOL_WINDOWS):
            sl = slice(g * POOL_GROUP_WIDTH, (g + 1) * POOL_GROUP_WIDTH)
            acc = u[:, sl]
            for back in range(1, w):
                acc = acc + sp_ref[POOL_STATE - back][:, sl]
            parts.append(_pool_group(acc, u[:, sl], float(w), wpool_ref[g].astype(BF16),
                                     pscale_ref[:, sl], cast=BF16))
        pooled = jnp.concatenate(parts, axis=1)
        a = _bdot(attn_ref[...], wa_ref[...])
        bb = _bdot(pooled, wb_ref[...])
        merged = _sigmoid(ga_ref[...] + bg_ref[0:1, :]) * a + _sigmoid(gb_ref[...] + bg_ref[1:2, :]) * bb
        y = alpha * x_ref[...] + _bdot(merged, wout_ref[...])
        x1 = _layer_norm(y, g1_ref[...], b1_ref[...])
        x1_ref[...] = x1
        comb_ref[...] = _route(_bdot(x1, wr_ref[...]) + br_ref[...])
        acc_ref[...] = jnp.zeros_like(acc_ref)

    x1 = x1_ref[...]
    hg = _bdot(x1, wg_ref[...])
    hu = _bdot(x1, wu_ref[...])
    c_e = jnp.sum(jnp.where(lane == e, comb_ref[...], 0.0), axis=1, keepdims=True)
    h = hg * _sigmoid(hg) * hu * c_e
    acc_ref[...] += _bdot(h, wd_ref[...])

    @pl.when(e == N_EXPERTS - 1)
    def _():
        y_ref[...] = _layer_norm(alpha * x1_ref[...] + acc_ref[...], g2_ref[...], b2_ref[...])


def _s_tail(x, attn, u, ga, gb, sp_t, b_gate, w_pool, pool_scale, w_a, w_b, w_out, ln1_g, ln1_b,
            w_r, b_r, w_g, w_u, w_d, ln2_g, ln2_b, alpha):
    rows, d = x.shape
    f = w_g.shape[2]
    const2 = lambda e: (0, 0)
    const3 = lambda e: (0, 0, 0)
    full = lambda a: pl.BlockSpec(a.shape, const2 if a.ndim == 2 else const3)
    per_e = lambda blk: pl.BlockSpec((None,) + blk, lambda e: (e, 0, 0))
    small = [x, attn, u, ga, gb, sp_t, b_gate, w_pool, pool_scale, w_a, w_b, w_out, ln1_g, ln1_b,
             w_r, b_r]
    return pl.pallas_call(
        functools.partial(_s_tail_kernel, alpha=alpha),
        out_shape=jax.ShapeDtypeStruct((rows, d), F32),
        grid=(N_EXPERTS,),
        in_specs=[full(a) for a in small]
        + [per_e((d, f)), per_e((d, f)), per_e((f, d)), full(ln2_g), full(ln2_b)],
        out_specs=pl.BlockSpec((rows, d), const2),
        scratch_shapes=[pltpu.VMEM((rows, d), F32), pltpu.VMEM((rows, LANES), F32),
                        pltpu.VMEM((rows, d), F32)],
        compiler_params=pltpu.CompilerParams(
            dimension_semantics=("arbitrary",), vmem_limit_bytes=VMEM_LIMIT),
        name="s_tail",
    )(*small, w_g, w_u, w_d, ln2_g, ln2_b)


def _rope_tables(pos):
    inv_freq = 1.0 / (ROPE_THETA ** (jnp.arange(0, HEAD_DIM, 2, dtype=F32) / HEAD_DIM))
    ang = pos.astype(F32)[:, None] * inv_freq[None, :]
    cos = jnp.cos(ang)
    sin = jnp.sin(ang)
    cos_t = jnp.tile(cos, (1, LANES // (HEAD_DIM // 2)))
    sin_t = jnp.tile(jnp.concatenate([-sin, sin], axis=1), (1, LANES // HEAD_DIM))
    return cos_t, sin_t


def kernel(x_prompt, x_sample, cache_k, cache_v, state_pool, page_table, w_in, w_pool, pool_scale,
           w_branch_a, w_branch_b, b_gate, w_out, ln1_g, ln1_b, w_group_router, b_group_router,
           w_expert_router, b_expert_router, w_e_gate, w_e_up, w_e_down, ln2_g, ln2_b):
    depth = w_in.shape[0]
    assert depth == 1 and x_sample.shape[1] == 1
    alpha = (2 * depth) ** 0.25
    b, s, d = x_prompt.shape
    db = x_sample.shape[0]
    n_pages = page_table.shape[1]
    page_size = cache_k.shape[2]
    past_len = n_pages * page_size
    n_blocks = past_len // MOBA_BLOCK
    assert past_len % MOBA_BLOCK == 0 and s % MOBA_BLOCK == 0

    w_in0 = w_in[0]
    qkvu_cols = 3 * ATTN_WIDTH + POOL_WIDTH
    row2 = lambda a: a.reshape(1, -1)
    w_r = jnp.concatenate(
        [w_group_router[0], jnp.transpose(w_expert_router[0], (1, 0, 2)).reshape(d, N_EXPERTS),
         jnp.zeros((d, LANES - N_EXPERT_GROUPS - N_EXPERTS), F32)], axis=1)
    b_r = jnp.concatenate(
        [b_group_router[0], b_expert_router[0].reshape(-1),
         jnp.zeros((LANES - N_EXPERT_GROUPS - N_EXPERTS,), F32)]).reshape(1, LANES)

    cos_p, sin_p = _rope_tables(jnp.arange(s, dtype=jnp.int32))
    assert qkvu_cols == 4 * ATTN_WIDTH
    qt, k, v, u, kb, vt, km = _qkvu(x_prompt.reshape(b * s, d), w_in0, cos_p, sin_p, s, QKVU_TILE)
    nb = s // MOBA_BLOCK
    attn = _attn(qt, kb.reshape(b, s, ATTN_WIDTH), vt.reshape(b, nb, ATTN_WIDTH, MOBA_BLOCK),
                 km.reshape(b, nb, ATTN_WIDTH))
    u3 = u.reshape(b, s, POOL_WIDTH)
    x1 = _mix(x_prompt, attn, u3, w_in0, b_gate[0], w_pool[0], row2(pool_scale[0]), w_branch_a[0],
              w_branch_b[0], w_out[0], row2(ln1_g[0]), row2(ln1_b[0]), MIX_TILE, alpha)
    y_p = _moe(x1.reshape(b * s, d), w_r, b_r, w_e_gate[0], w_e_up[0], w_e_down[0],
               row2(ln2_g[0]), row2(ln2_b[0]), MOE_TILE, alpha)

    cos_s, sin_s = _rope_tables(jnp.full((1,), past_len, jnp.int32))
    x_s = x_sample.reshape(db, d)
    h_s = _s_proj(x_s, w_in0, cos_s, sin_s)
    q_s = h_s[:, :ATTN_WIDTH]
    k_s = h_s[:, ATTN_WIDTH:2 * ATTN_WIDTH]
    v_s = h_s[:, 2 * ATTN_WIDTH:3 * ATTN_WIDTH]
    u_s = h_s[:, 3 * ATTN_WIDTH:qkvu_cols]
    ga_s = h_s[:, qkvu_cols:qkvu_cols + d]
    gb_s = h_s[:, qkvu_cols + d:]
    cache_kt = jnp.transpose(cache_k[0], (0, 2, 3, 1))
    cache_vt = jnp.transpose(cache_v[0], (0, 2, 3, 1))
    scores, sel = _s_scores(page_table, q_s.T, cache_kt, n_blocks)
    sel2 = sel.reshape(db, MOBA_TOPK * N_HEADS)
    attn_s = _s_attn(page_table, sel2, scores, q_s.T, k_s.T, v_s.T, cache_vt).reshape(db, ATTN_WIDTH)
    sp = state_pool[0]
    y_s = _s_tail(x_s, attn_s, u_s, ga_s, gb_s, jnp.transpose(sp, (1, 0, 2)), b_gate[0], w_pool[0],
                  row2(pool_scale[0]), w_branch_a[0], w_branch_b[0], w_out[0], row2(ln1_g[0]),
                  row2(ln1_b[0]), w_r, b_r, w_e_gate[0], w_e_up[0], w_e_down[0], row2(ln2_g[0]),
                  row2(ln2_b[0]), alpha)

    heads = (N_HEADS, HEAD_DIM)
    return (
        y_p.reshape(b, s, d),
        y_s.reshape(db, 1, d),
        jnp.transpose(k.reshape((b,) + heads + (s,)), (0, 3, 1, 2))[None],
        jnp.transpose(v.reshape((b,) + heads + (s,)), (0, 3, 1, 2))[None],
        u3[:, s - POOL_STATE:, :][None],
        k_s.reshape((1, db, 1) + heads),
        v_s.reshape((1, db, 1) + heads),
        jnp.concatenate([sp[:, 1:, :], u_s[:, None, :]], axis=1)[None],
    )
```
